```python
import math
import jax, jax.numpy as jnp
from jax import lax
import numpy as np


D_MODEL = 1024
BATCH = 4
SEQ = 8192
DEPTH = 1

CHUNK = 64
EPS = 1e-6
D_MIX = D_MODEL
D_S5 = D_MIX // 2
S5_GROUP = 16
S5_GROUPS = D_S5 // S5_GROUP
S5_STATE = 64
DT_MIN = 1e-3
DT_MAX = 1e-1
D_RET = D_MIX - D_S5
RET_HEADS = 8
RET_HEAD_DIM = D_RET // RET_HEADS
ROPE_BASE = 10000.0
D_IN = D_S5 + 4 * D_RET
N_GROUPS = 4
EXPERTS_PER_GROUP = 8
N_EXPERTS = N_GROUPS * EXPERTS_PER_GROUP
TOP_K = 2
D_EXPERT = 512
MOE_BLOCK = 128

kernel_name = 'hymba_s5_retention_hiermoe_block'


def rms_norm(x, w):
    xf = x.astype(jnp.float32)
    y = xf * lax.rsqrt(jnp.mean(xf * xf, axis=-1, keepdims=True) + EPS)
    return (y * w.astype(jnp.float32)).astype(x.dtype)


def _cmul_scan_op(e1, e2):
    a1r, a1i, b1r, b1i = e1
    a2r, a2i, b2r, b2i = e2
    return (a1r * a2r - a1i * a2i,
            a1r * a2i + a1i * a2r,
            a2r * b1r - a2i * b1i + b2r,
            a2r * b1i + a2i * b1r + b2i)


def s5_mixer(u, a_re, a_im, b_re, b_im, c_re, c_im, d_skip, log_dt, w_glu, b_glu):
    bsz, seq, _ = u.shape
    n_chunks = seq // CHUNK
    f32 = jnp.float32
    lam_r = a_re.astype(f32)
    lam_i = a_im.astype(f32)
    dt = jnp.exp(log_dt.astype(f32))[:, None]
    mag = jnp.exp(lam_r * dt)
    ab_r = mag * jnp.cos(lam_i * dt)
    ab_i = mag * jnp.sin(lam_i * dt)
    den = lam_r * lam_r + lam_i * lam_i
    zr = ((ab_r - 1.0) * lam_r + ab_i * lam_i) / den
    zi = (ab_i * lam_r - (ab_r - 1.0) * lam_i) / den
    br_, bi_ = b_re.astype(f32), b_im.astype(f32)
    bb_r = zr[..., None] * br_ - zi[..., None] * bi_
    bb_i = zr[..., None] * bi_ + zi[..., None] * br_
    cr, ci = c_re.astype(f32), c_im.astype(f32)
    dd = d_skip.astype(f32).reshape(S5_GROUPS, S5_GROUP)
    a_seq_r = jnp.broadcast_to(ab_r, (CHUNK, 1, S5_GROUPS, S5_STATE))
    a_seq_i = jnp.broadcast_to(ab_i, (CHUNK, 1, S5_GROUPS, S5_STATE))
    uc = u.astype(f32).reshape(bsz, n_chunks, CHUNK, S5_GROUPS, S5_GROUP).transpose(1, 2, 0, 3, 4)

    def step(carry, u_blk):
        s0r, s0i = carry
        bu_r = jnp.einsum('tbgh,gph->tbgp', u_blk, bb_r)
        bu_i = jnp.einsum('tbgh,gph->tbgp', u_blk, bb_i)
        acr, aci, sr, si = lax.associative_scan(
            _cmul_scan_op, (a_seq_r, a_seq_i, bu_r, bu_i), axis=0)
        sr = sr + acr * s0r - aci * s0i
        si = si + acr * s0i + aci * s0r
        y = (jnp.einsum('tbgp,ghp->tbgh', sr, cr)
             - jnp.einsum('tbgp,ghp->tbgh', si, ci)
             + dd * u_blk)
        return (sr[-1], si[-1]), y

    s_init = jnp.zeros((bsz, S5_GROUPS, S5_STATE), f32)
    _, ys = lax.scan(step, (s_init, s_init), uc)
    y = ys.transpose(2, 0, 1, 3, 4).reshape(bsz, seq, D_S5)
    y = jax.nn.gelu(y)
    y = y * jax.nn.sigmoid(y @ w_glu.astype(f32) + b_glu.astype(f32))
    return y.astype(u.dtype)


def rotary(x):
    seq = x.shape[1]
    half = RET_HEAD_DIM // 2
    inv = ROPE_BASE ** (-jnp.arange(half, dtype=jnp.float32) / half)
    ang = jnp.arange(seq, dtype=jnp.float32)[:, None] * inv[None, :]
    cos = jnp.cos(ang)[None, :, None, :]
    sin = jnp.sin(ang)[None, :, None, :]
    xf = x.astype(jnp.float32)
    x1, x2 = xf[..., :half], xf[..., half:]
    return jnp.concatenate([x1 * cos - x2 * sin, x1 * sin + x2 * cos], axis=-1).astype(x.dtype)


def retention_mixer(q, k, v, g, norm_w):
    bsz, seq, _ = q.shape
    n_chunks = seq // CHUNK
    f32 = jnp.float32
    shp = (bsz, seq, RET_HEADS, RET_HEAD_DIM)
    q = rotary(q.reshape(shp))
    k = rotary(k.reshape(shp)) * (RET_HEAD_DIM ** -0.5)
    v = v.reshape(shp)
    cshp = (bsz, n_chunks, CHUNK, RET_HEADS, RET_HEAD_DIM)
    qc, kc, vc = q.reshape(cshp), k.reshape(cshp), v.reshape(cshp)
    log_gamma = jnp.log(1.0 - 2.0 ** (-5.0 - jnp.arange(RET_HEADS, dtype=f32)))
    t = jnp.arange(CHUNK, dtype=f32)
    diff = t[:, None] - t[None, :]
    decay = jnp.where(diff >= 0, jnp.exp(log_gamma[:, None, None] * jnp.maximum(diff, 0.0)), 0.0)
    scores = jnp.einsum('bcthd,bcshd->bchts', qc, kc) * decay
    inner = jnp.einsum('bchts,bcshe->bcthe', scores, vc)
    k_decay = jnp.exp(log_gamma[:, None] * (CHUNK - 1 - t)[None, :])
    kv = jnp.einsum('bcshd,bcshe,hs->cbhde', kc, vc, k_decay)
    chunk_decay = jnp.exp(log_gamma * CHUNK)[None, :, None, None]

    def step(state, kv_c):
        return state * chunk_decay + kv_c, state

    _, r_prev = lax.scan(step, jnp.zeros((bsz, RET_HEADS, RET_HEAD_DIM, RET_HEAD_DIM), f32), kv)
    q_decay = jnp.exp(log_gamma[:, None] * (t + 1.0)[None, :])
    cross = jnp.einsum('bcthd,cbhde,ht->bcthe', qc, r_prev, q_decay)
    o = (inner + cross).astype(f32).reshape(shp)
    mu = jnp.mean(o, axis=-1, keepdims=True)
    var = jnp.mean(jnp.square(o - mu), axis=-1, keepdims=True)
    o = (o - mu) * lax.rsqrt(var + EPS) * norm_w.astype(f32).reshape(RET_HEADS, RET_HEAD_DIM)
    o = o.reshape(bsz, seq, D_RET)
    return (jax.nn.silu(g.astype(f32)) * o).astype(q.dtype)


def hier_moe(hn, rg_w, rg_b, re_w, re_b, w_gate, w_up, w_down):
    bsz, seq, d = hn.shape
    n_tok = bsz * seq
    f32 = jnp.float32
    xf = hn.reshape(n_tok, d)
    g_logits = (xf @ rg_w + rg_b).astype(f32)
    g_prob = jax.nn.softmax(g_logits, axis=-1)
    g_idx = jnp.argmax(g_logits, axis=-1).astype(jnp.int32)
    g_w = jnp.take_along_axis(g_prob, g_idx[:, None], axis=1)[:, 0]
    e_logits = (xf @ re_w + re_b).astype(f32).reshape(n_tok, N_GROUPS, EXPERTS_PER_GROUP)
    e_sel = jnp.take_along_axis(e_logits, g_idx[:, None, None], axis=1)[:, 0]
    top_v, top_i = lax.top_k(e_sel, TOP_K)
    top_w = jax.nn.softmax(top_v, axis=-1) * g_w[:, None]
    eid = (g_idx[:, None] * EXPERTS_PER_GROUP + top_i).reshape(-1).astype(jnp.int32)
    tok = jnp.repeat(jnp.arange(n_tok, dtype=jnp.int32), TOP_K)
    wt = top_w.reshape(-1)
    n_asg = n_tok * TOP_K
    order = jnp.argsort(eid)
    s_e, s_tok, s_w = eid[order], tok[order], wt[order]
    counts = jnp.bincount(eid, length=N_EXPERTS)
    starts = jnp.cumsum(counts) - counts
    padded = (counts + MOE_BLOCK - 1) // MOE_BLOCK * MOE_BLOCK
    pends = jnp.cumsum(padded)
    pstarts = pends - padded
    dest = pstarts[s_e] + jnp.arange(n_asg, dtype=jnp.int32) - starts[s_e]
    n_blocks = -(-n_asg // MOE_BLOCK) + N_EXPERTS
    buf_len = n_blocks * MOE_BLOCK
    buf_tok = jnp.zeros((buf_len,), jnp.int32).at[dest].set(s_tok)
    buf_w = jnp.zeros((buf_len,), hn.dtype).at[dest].set(s_w.astype(hn.dtype))
    block_start = jnp.arange(n_blocks, dtype=jnp.int32) * MOE_BLOCK
    block_e = jnp.minimum(jnp.searchsorted(pends, block_start, side='right'), N_EXPERTS - 1)

    def expert_block(args):
        tok_b, w_b, e = args
        xb = xf[tok_b]
        hid = jax.nn.silu(xb @ w_gate[e]) * (xb @ w_up[e])
        return ((hid @ w_down[e]) * w_b[:, None]).astype(hn.dtype)

    ys = lax.map(expert_block, (buf_tok.reshape(n_blocks, MOE_BLOCK),
                                buf_w.reshape(n_blocks, MOE_BLOCK), block_e))
    out = jnp.zeros((n_tok, d), hn.dtype).at[buf_tok].add(ys.reshape(buf_len, d))
    return out.reshape(bsz, seq, d)


def setup_inputs(seed: int = 0) -> dict:
    key = jax.random.key(seed)
    ks = jax.random.split(key, 24)
    f32 = jnp.float32

    def nrm(k, shape, scale):
        return jax.random.normal(k, shape, f32) * scale

    L_ = DEPTH
    x = jax.random.normal(ks[0], (BATCH, SEQ, D_MODEL), f32)
    norm1_w = 1.0 + nrm(ks[1], (L_, D_MODEL), 0.01)
    w_in = nrm(ks[2], (L_, D_MODEL, D_IN), D_MODEL ** -0.5)
    s5_a_re = -0.5 + nrm(ks[3], (L_, S5_GROUPS, S5_STATE), 0.01)
    s5_a_im = (math.pi * jnp.arange(S5_STATE, dtype=f32))[None, None, :] + nrm(ks[4], (L_, S5_GROUPS, S5_STATE), 0.01)
    s5_b_re = nrm(ks[5], (L_, S5_GROUPS, S5_STATE, S5_GROUP), (2.0 * S5_GROUP) ** -0.5)
    s5_b_im = nrm(ks[6], (L_, S5_GROUPS, S5_STATE, S5_GROUP), (2.0 * S5_GROUP) ** -0.5)
    s5_c_re = nrm(ks[7], (L_, S5_GROUPS, S5_GROUP, S5_STATE), S5_STATE ** -0.5)
    s5_c_im = nrm(ks[8], (L_, S5_GROUPS, S5_GROUP, S5_STATE), S5_STATE ** -0.5)
    s5_d = nrm(ks[9], (L_, D_S5), 1.0)
    s5_log_dt = math.log(DT_MIN) + jax.random.uniform(ks[10], (L_, S5_GROUPS), f32) * (math.log(DT_MAX) - math.log(DT_MIN))
    s5_w_glu = nrm(ks[11], (L_, D_S5, D_S5), D_S5 ** -0.5)
    s5_b_glu = nrm(ks[12], (L_, D_S5), 0.01)
    ret_norm_w = 1.0 + nrm(ks[13], (L_, D_RET), 0.01)
    w_out = nrm(ks[14], (L_, D_MIX, D_MODEL), D_MIX ** -0.5)
    norm2_w = 1.0 + nrm(ks[15], (L_, D_MODEL), 0.01)
    router_group_w = nrm(ks[16], (L_, D_MODEL, N_GROUPS), D_MODEL ** -0.5)
    router_group_b = nrm(ks[17], (L_, N_GROUPS), 0.01)
    router_expert_w = nrm(ks[18], (L_, D_MODEL, N_EXPERTS), D_MODEL ** -0.5)
    router_expert_b = nrm(ks[19], (L_, N_EXPERTS), 0.01)
    moe_w_gate = nrm(ks[20], (L_, N_EXPERTS, D_MODEL, D_EXPERT), D_MODEL ** -0.5)
    moe_w_up = nrm(ks[21], (L_, N_EXPERTS, D_MODEL, D_EXPERT), D_MODEL ** -0.5)
    moe_w_down = nrm(ks[22], (L_, N_EXPERTS, D_EXPERT, D_MODEL), D_EXPERT ** -0.5)
    final_norm_w = 1.0 + nrm(ks[23], (D_MODEL,), 0.01)
    return {'x': x, 'norm1_w': norm1_w, 'w_in': w_in,
            's5_a_re': s5_a_re, 's5_a_im': s5_a_im, 's5_b_re': s5_b_re, 's5_b_im': s5_b_im,
            's5_c_re': s5_c_re, 's5_c_im': s5_c_im, 's5_d': s5_d, 's5_log_dt': s5_log_dt,
            's5_w_glu': s5_w_glu, 's5_b_glu': s5_b_glu, 'ret_norm_w': ret_norm_w,
            'w_out': w_out, 'norm2_w': norm2_w,
            'router_group_w': router_group_w, 'router_group_b': router_group_b,
            'router_expert_w': router_expert_w, 'router_expert_b': router_expert_b,
            'moe_w_gate': moe_w_gate, 'moe_w_up': moe_w_up, 'moe_w_down': moe_w_down,
            'final_norm_w': final_norm_w}


def reference(x, norm1_w, w_in, s5_a_re, s5_a_im, s5_b_re, s5_b_im, s5_c_re, s5_c_im,
              s5_d, s5_log_dt, s5_w_glu, s5_b_glu, ret_norm_w, w_out, norm2_w,
              router_group_w, router_group_b, router_expert_w, router_expert_b,
              moe_w_gate, moe_w_up, moe_w_down, final_norm_w):
    h = x
    for l in range(DEPTH):
        u = rms_norm(h, norm1_w[l]) @ w_in[l]
        u_s5 = u[..., :D_S5]
        q = u[..., D_S5:D_S5 + D_RET]
        k = u[..., D_S5 + D_RET:D_S5 + 2 * D_RET]
        v = u[..., D_S5 + 2 * D_RET:D_S5 + 3 * D_RET]
        g = u[..., D_S5 + 3 * D_RET:]
        y_s5 = s5_mixer(u_s5, s5_a_re[l], s5_a_im[l], s5_b_re[l], s5_b_im[l],
                        s5_c_re[l], s5_c_im[l], s5_d[l], s5_log_dt[l], s5_w_glu[l], s5_b_glu[l])
        y_ret = retention_mixer(q, k, v, g, ret_norm_w[l])
        h = h + jnp.concatenate([y_s5, y_ret], axis=-1) @ w_out[l]
        h = h + hier_moe(rms_norm(h, norm2_w[l]), router_group_w[l], router_group_b[l],
                         router_expert_w[l], router_expert_b[l],
                         moe_w_gate[l], moe_w_up[l], moe_w_down[l])
    return rms_norm(h, final_norm_w)
```

```python
import functools

import jax
import jax.numpy as jnp
from jax import lax
from jax.experimental import pallas as pl
from jax.experimental.pallas import tpu as pltpu

F32 = jnp.float32
BF16 = jnp.bfloat16

EPS = 1e-6
ROPE_BASE = 10000.0
RET_HEADS = 8
TOP_K = 2
N_GROUPS = 4
EXPERTS_PER_GROUP = 8

V7X_LANES = 128
V7X_SUBLANES = 8
V7X_VMEM_LIMIT = 56 * 1024 * 1024

TM_PROJ = 512
T_S5 = 128
S5_PAD = 8
T_RET = 128
TM_ROUTE = 512
TM_ROWS = 256
R_BLK = 256
ROUTE_LANES = 128


def _rms(x, w):
    return x * lax.rsqrt(jnp.mean(x * x, axis=-1, keepdims=True) + EPS) * w


def _cparams(sem):
    return pltpu.CompilerParams(dimension_semantics=sem, vmem_limit_bytes=V7X_VMEM_LIMIT)


def _inproj_body(x_ref, nw_ref, w_ref, u_ref):
    xn = _rms(x_ref[...], nw_ref[...]).astype(BF16)
    u_ref[...] = jnp.dot(xn, w_ref[...], preferred_element_type=F32).astype(u_ref.dtype)


def _inproj(x2, nw, w_bf):
    n, d = x2.shape
    d_in = w_bf.shape[1]
    return pl.pallas_call(
        _inproj_body,
        grid=(n // TM_PROJ,),
        in_specs=[pl.BlockSpec((TM_PROJ, d), lambda i: (i, 0)),
                  pl.BlockSpec((1, d), lambda i: (0, 0)),
                  pl.BlockSpec((d, d_in), lambda i: (0, 0))],
        out_specs=pl.BlockSpec((TM_PROJ, d_in), lambda i: (i, 0)),
        out_shape=jax.ShapeDtypeStruct((n, d_in), BF16),
        compiler_params=_cparams(("arbitrary",)),
        name="inproj",
    )(x2, nw, w_bf)


def _s5_body(u_ref, bm_ref, cm_ref, a_ref, dd_ref, wglu_ref, bglu_ref, y_ref,
             bu_scr, s_scr, st_scr, *, nb, t_len, seg):
    @pl.when(pl.program_id(0) == 0)
    def _():
        st_scr[...] = jnp.zeros_like(st_scr)

    nblk = bm_ref.shape[0]
    cw = bm_ref.shape[1]
    sw = bm_ref.shape[2]
    tiles = sw // V7X_LANES
    ht = tiles // 2
    u2 = u_ref[...].reshape(nb * t_len, nblk * cw)
    for blk in range(nblk):
        half, q = divmod(blk, 2)
        r = jnp.dot(u2[:, blk * cw:(blk + 1) * cw], bm_ref[blk], preferred_element_type=F32)
        for b in range(nb):
            for j in range(tiles):
                bu_scr[q * tiles + j, pl.ds((half * nb + b) * seg, t_len), :] = (
                    r[b * t_len:(b + 1) * t_len, j * V7X_LANES:(j + 1) * V7X_LANES])

    rows = 2 * nb
    ar = [[a_ref[0, q * ht + i] for i in range(ht)] for q in range(2)]
    ai = [[a_ref[1, q * ht + i] for i in range(ht)] for q in range(2)]
    sr = [[st_scr[q * tiles + i] for i in range(ht)] for q in range(2)]
    si = [[st_scr[q * tiles + ht + i] for i in range(ht)] for q in range(2)]
    for t in range(t_len):
        for q in range(2):
            for i in range(ht):
                jr = q * tiles + i
                ji = q * tiles + ht + i
                vr = bu_scr[jr, pl.ds(t, rows, stride=seg), :]
                vi = bu_scr[ji, pl.ds(t, rows, stride=seg), :]
                nr = ar[q][i] * sr[q][i] - ai[q][i] * si[q][i] + vr
                ni = ar[q][i] * si[q][i] + ai[q][i] * sr[q][i] + vi
                s_scr[jr, pl.ds(t, rows, stride=seg), :] = nr
                s_scr[ji, pl.ds(t, rows, stride=seg), :] = ni
                sr[q][i], si[q][i] = nr, ni
    for q in range(2):
        for i in range(ht):
            st_scr[q * tiles + i] = sr[q][i]
            st_scr[q * tiles + ht + i] = si[q][i]

    ys = []
    for blk in range(nblk):
        half, q = divmod(blk, 2)
        sb = jnp.concatenate(
            [jnp.concatenate([s_scr[q * tiles + j, pl.ds((half * nb + b) * seg, t_len), :]
                              for j in range(tiles)], axis=1) for b in range(nb)],
            axis=0).astype(BF16)
        ys.append(jnp.dot(sb, cm_ref[blk], preferred_element_type=F32))
    y = jnp.concatenate(ys, axis=1) + dd_ref[...] * u2.astype(F32)
    y = jax.nn.gelu(y)
    z = jnp.dot(y.astype(BF16), wglu_ref[...], preferred_element_type=F32) + bglu_ref[...]
    out = y * jax.nn.sigmoid(z)
    y_ref[...] = out.reshape(nb, t_len, nblk * cw).astype(y_ref.dtype)


def _s5(u3, bm, cm, a_tab, dd, wglu_bf, bglu):
    nb, seq, _ = u3.shape
    nblk, cw, sw = bm.shape
    d_s5 = nblk * cw
    seg = T_S5 + S5_PAD
    rows = 2 * nb
    assert rows == V7X_SUBLANES and nblk == 4
    body = functools.partial(_s5_body, nb=nb, t_len=T_S5, seg=seg)
    return pl.pallas_call(
        body,
        grid=(seq // T_S5,),
        in_specs=[pl.BlockSpec((nb, T_S5, d_s5), lambda c: (0, c, 0)),
                  pl.BlockSpec(bm.shape, lambda c: (0, 0, 0)),
                  pl.BlockSpec(cm.shape, lambda c: (0, 0, 0)),
                  pl.BlockSpec(a_tab.shape, lambda c: (0, 0, 0, 0)),
                  pl.BlockSpec((1, d_s5), lambda c: (0, 0)),
                  pl.BlockSpec((d_s5, d_s5), lambda c: (0, 0)),
                  pl.BlockSpec((1, d_s5), lambda c: (0, 0))],
        out_specs=pl.BlockSpec((nb, T_S5, d_s5), lambda c: (0, c, 0)),
        out_shape=jax.ShapeDtypeStruct((nb, seq, d_s5), BF16),
        scratch_shapes=[pltpu.VMEM((2 * sw // V7X_LANES, rows * seg, V7X_LANES), F32),
                        pltpu.VMEM((2 * sw // V7X_LANES, rows * seg, V7X_LANES), F32),
                        pltpu.VMEM((2 * sw // V7X_LANES, rows, V7X_LANES), F32)],
        compiler_params=_cparams(("arbitrary",)),
        name="s5",
    )(u3, bm, cm, a_tab, dd, wglu_bf, bglu)


def _s5_tables(a_re, a_im, b_re, b_im, c_re, c_im, log_dt, nb):
    g, p = a_re.shape
    hch = b_re.shape[2]
    gpb = V7X_LANES // hch
    nblk = g // gpb
    lam_r, lam_i = a_re.astype(F32), a_im.astype(F32)
    dt = jnp.exp(log_dt.astype(F32))[:, None]
    mag = jnp.exp(lam_r * dt)
    ab_r = mag * jnp.cos(lam_i * dt)
    ab_i = mag * jnp.sin(lam_i * dt)
    den = lam_r * lam_r + lam_i * lam_i
    zr = ((ab_r - 1.0) * lam_r + ab_i * lam_i) / den
    zi = (ab_i * lam_r - (ab_r - 1.0) * lam_i) / den
    br_, bi_ = b_re.astype(F32), b_im.astype(F32)
    bb_r = zr[..., None] * br_ - zi[..., None] * bi_
    bb_i = zr[..., None] * bi_ + zi[..., None] * br_
    eye = jnp.eye(gpb, dtype=F32)

    def bmat(bb):
        return jnp.einsum('bgph,gk->bghkp', bb.reshape(nblk, gpb, p, hch), eye).reshape(
            nblk, gpb * hch, gpb * p)

    def cmat(cc):
        return jnp.einsum('bghp,gk->bgpkh', cc.reshape(nblk, gpb, hch, p), eye).reshape(
            nblk, gpb * p, gpb * hch)

    bm = jnp.concatenate([bmat(bb_r), bmat(bb_i)], axis=2).astype(BF16)
    cm = jnp.concatenate([cmat(c_re.astype(F32)), -cmat(c_im.astype(F32))], axis=1).astype(BF16)
    a_tab = jnp.stack([jnp.repeat(ab_r.reshape(2, -1), nb, axis=0),
                       jnp.repeat(ab_i.reshape(2, -1), nb, axis=0)])
    a_tab = a_tab.reshape(2, 2 * nb, -1, V7X_LANES).transpose(0, 2, 1, 3)
    return bm, cm, a_tab


def _ret_body(q_ref, k_ref, v_ref, g_ref, cos_ref, sin_ref, dec_ref, qdec_ref, kdec_ref,
              cdec_ref, nw_ref, y_ref, st_scr, *, t_len, dh, scale):
    b = pl.program_id(1)

    @pl.when(pl.program_id(0) == 0)
    def _():
        st_scr[b] = jnp.zeros(st_scr.shape[1:], F32)

    width = q_ref.shape[-1]
    heads = width // dh
    reps = width // cos_ref.shape[-1]
    cos = jnp.concatenate([cos_ref[...]] * reps, axis=1)
    sin = jnp.concatenate([sin_ref[...]] * reps, axis=1)
    lane = lax.broadcasted_iota(jnp.int32, (t_len, width), 1)
    first = (lane % dh) < (dh // 2)

    def rot(x):
        x = x.astype(F32)
        partner = jnp.where(first, pltpu.roll(x, width - dh // 2, 1), pltpu.roll(x, dh // 2, 1))
        return x * cos + partner * sin

    q = rot(q_ref[...])
    k = rot(k_ref[...]) * scale
    qb, kb = q.astype(BF16), k.astype(BF16)
    qd = (q * qdec_ref[...]).astype(BF16)
    kd = (k * kdec_ref[...]).astype(BF16)
    vb = v_ref[...]
    outs = []
    for h in range(heads):
        sl = slice(h * dh, (h + 1) * dh)
        sc = lax.dot_general(qb[:, sl], kb[:, sl], (((1,), (1,)), ((), ())),
                             preferred_element_type=F32) * dec_ref[h]
        inner = jnp.dot(sc.astype(BF16), vb[:, sl], preferred_element_type=F32)
        state = st_scr[b, h]
        cross = jnp.dot(qd[:, sl], state.astype(BF16), preferred_element_type=F32)
        o = inner + cross
        mu = jnp.mean(o, axis=-1, keepdims=True)
        dlt = o - mu
        var = jnp.mean(dlt * dlt, axis=-1, keepdims=True)
        outs.append(dlt * lax.rsqrt(var + EPS))
        kv = lax.dot_general(kd[:, sl], vb[:, sl], (((0,), (0,)), ((), ())),
                             preferred_element_type=F32)
        st_scr[b, h] = state * cdec_ref[h] + kv
    o = jnp.concatenate(outs, axis=1) * nw_ref[...]
    y_ref[...] = (jax.nn.silu(g_ref[...].astype(F32)) * o).astype(y_ref.dtype)


def _retention(u3, d_s5, d_ret, norm_w):
    nb, seq, _ = u3.shape
    dh = d_ret // RET_HEADS
    half = dh // 2
    t_len = T_RET
    first_blk = d_s5 // d_ret
    assert d_s5 % d_ret == 0
    lg = jnp.log(1.0 - 2.0 ** (-5.0 - jnp.arange(RET_HEADS, dtype=F32)))
    t = jnp.arange(t_len, dtype=F32)
    diff = t[:, None] - t[None, :]
    dec = jnp.where(diff >= 0, jnp.exp(lg[:, None, None] * jnp.maximum(diff, 0.0)), 0.0)
    qdec = jnp.repeat(jnp.exp(lg[:, None] * (t + 1.0)[None, :]).T, dh, axis=1)
    kdec = jnp.repeat(jnp.exp(lg[:, None] * (t_len - 1 - t)[None, :]).T, dh, axis=1)
    cdec = jnp.broadcast_to(jnp.exp(lg * t_len)[:, None, None], (RET_HEADS, 1, dh))
    inv = ROPE_BASE ** (-jnp.arange(half, dtype=F32) / half)
    ang = jnp.arange(seq, dtype=F32)[:, None] * inv[None, :]
    reps = V7X_LANES // dh
    cos_t = jnp.tile(jnp.cos(ang), (1, 2 * reps))
    sin_t = jnp.tile(jnp.concatenate([-jnp.sin(ang), jnp.sin(ang)], axis=1), (1, reps))

    def col(j):
        return pl.BlockSpec((None, t_len, d_ret), lambda c, b: (b, c, first_blk + j))

    body = functools.partial(_ret_body, t_len=t_len, dh=dh, scale=dh ** -0.5)
    return pl.pallas_call(
        body,
        grid=(seq // t_len, nb),
        in_specs=[col(0), col(1), col(2), col(3),
                  pl.BlockSpec((t_len, V7X_LANES), lambda c, b: (c, 0)),
                  pl.BlockSpec((t_len, V7X_LANES), lambda c, b: (c, 0)),
                  pl.BlockSpec(dec.shape, lambda c, b: (0, 0, 0)),
                  pl.BlockSpec(qdec.shape, lambda c, b: (0, 0)),
                  pl.BlockSpec(kdec.shape, lambda c, b: (0, 0)),
                  pl.BlockSpec(cdec.shape, lambda c, b: (0, 0, 0)),
                  pl.BlockSpec((1, d_ret), lambda c, b: (0, 0))],
        out_specs=pl.BlockSpec((None, t_len, d_ret), lambda c, b: (b, c, 0)),
        out_shape=jax.ShapeDtypeStruct((nb, seq, d_ret), BF16),
        scratch_shapes=[pltpu.VMEM((nb, RET_HEADS, dh, dh), F32)],
        compiler_params=_cparams(("arbitrary", "arbitrary")),
        name="retention",
    )(u3, u3, u3, u3, cos_t, sin_t, dec, qdec, kdec, cdec, norm_w)


def _route_body(ys5_ref, yret_ref, x_ref, wo_ref, n2_ref, wr_ref, br_ref, tri_ref,
                h_ref, route_ref, cnt_ref, carry_scr):
    @pl.when(pl.program_id(0) == 0)
    def _():
        carry_scr[...] = jnp.zeros_like(carry_scr)

    d_s5 = ys5_ref.shape[1]
    h = (x_ref[...]
         + jnp.dot(ys5_ref[...], wo_ref[0:d_s5], preferred_element_type=F32)
         + jnp.dot(yret_ref[...], wo_ref[d_s5:], preferred_element_type=F32))
    h_ref[...] = h
    hn = _rms(h, n2_ref[...])
    logits = jnp.dot(hn, wr_ref[...], precision=lax.Precision.HIGHEST,
                     preferred_element_type=F32) + br_ref[...]
    tm = logits.shape[0]
    lane = lax.broadcasted_iota(jnp.int32, (tm, ROUTE_LANES), 1)
    lanef = lane.astype(F32)
    neg = -jnp.inf
    big = float(ROUTE_LANES)
    gl = jnp.where(lane < N_GROUPS, logits, neg)
    gmax = jnp.max(gl, axis=-1, keepdims=True)
    gidx = jnp.min(jnp.where(gl == gmax, lanef, big), axis=-1, keepdims=True)
    g_w = 1.0 / jnp.sum(jnp.exp(gl - gmax), axis=-1, keepdims=True)
    lo = N_GROUPS + EXPERTS_PER_GROUP * gidx
    el = jnp.where((lanef >= lo) & (lanef < lo + EXPERTS_PER_GROUP), logits, neg)
    v0 = jnp.max(el, axis=-1, keepdims=True)
    i0 = jnp.min(jnp.where(el == v0, lanef, big), axis=-1, keepdims=True)
    el2 = jnp.where(lanef == i0, neg, el)
    v1 = jnp.max(el2, axis=-1, keepdims=True)
    i1 = jnp.min(jnp.where(el2 == v1, lanef, big), axis=-1, keepdims=True)
    e = jnp.exp(v1 - v0)
    den = 1.0 + e
    w0 = (1.0 / den) * g_w
    w1 = (e / den) * g_w
    sel0 = lanef == i0
    sel1 = lanef == i1
    onehot = jnp.where(sel0 | sel1, 1.0, 0.0)
    before = jnp.dot(tri_ref[...], onehot.astype(BF16), preferred_element_type=F32) + carry_scr[...]
    r0 = jnp.sum(jnp.where(sel0, before, 0.0), axis=-1, keepdims=True)
    r1 = jnp.sum(jnp.where(sel1, before, 0.0), axis=-1, keepdims=True)
    carry_scr[...] += jnp.sum(onehot, axis=0, keepdims=True)
    cnt_ref[...] = carry_scr[...]
    rec = jnp.zeros((tm, ROUTE_LANES), F32)
    for j, val in enumerate((i0 - N_GROUPS, i1 - N_GROUPS, w0, w1, r0, r1)):
        rec = jnp.where(lane == j, val, rec)
    route_ref[...] = rec


def _route(ys5, yret, x2, wo_bf, n2, wr, br):
    n, d = x2.shape
    d_s5 = ys5.shape[1]
    d_ret = yret.shape[1]
    tm = TM_ROUTE
    tri = (jnp.arange(tm)[:, None] > jnp.arange(tm)[None, :]).astype(BF16)
    return pl.pallas_call(
        _route_body,
        grid=(n // tm,),
        in_specs=[pl.BlockSpec((tm, d_s5), lambda i: (i, 0)),
                  pl.BlockSpec((tm, d_ret), lambda i: (i, 0)),
                  pl.BlockSpec((tm, d), lambda i: (i, 0)),
                  pl.BlockSpec((d_s5 + d_ret, d), lambda i: (0, 0)),
                  pl.BlockSpec((1, d), lambda i: (0, 0)),
                  pl.BlockSpec((d, ROUTE_LANES), lambda i: (0, 0)),
                  pl.BlockSpec((1, ROUTE_LANES), lambda i: (0, 0)),
                  pl.BlockSpec((tm, tm), lambda i: (0, 0))],
        out_specs=[pl.BlockSpec((tm, d), lambda i: (i, 0)),
                   pl.BlockSpec((tm, ROUTE_LANES), lambda i: (i, 0)),
                   pl.BlockSpec((1, ROUTE_LANES), lambda i: (0, 0))],
        out_shape=[jax.ShapeDtypeStruct((n, d), F32),
                   jax.ShapeDtypeStruct((n, ROUTE_LANES), F32),
                   jax.ShapeDtypeStruct((1, ROUTE_LANES), F32)],
        scratch_shapes=[pltpu.VMEM((1, ROUTE_LANES), F32)],
        compiler_params=_cparams(("arbitrary",)),
        name="outproj_route",
    )(ys5, yret, x2, wo_bf, n2, wr, br, tri)


def _dispatch_body(dest_ref, h_ref, n2_ref, xs_in_ref, xs_ref, hn_scr, sem):
    del xs_in_ref
    tm = h_ref.shape[0]
    hn_scr[...] = _rms(h_ref[...], n2_ref[...])

    def row_copy(r, d):
        return pltpu.make_async_copy(hn_scr.at[pl.ds(r, 1)], xs_ref.at[pl.ds(d, 1)], sem)

    def start(r, carry):
        for k in range(TOP_K):
            row_copy(r, dest_ref[k, r]).start()
        return carry

    def wait(r, carry):
        for k in range(TOP_K):
            row_copy(r, dest_ref[k, r]).wait()
        return carry

    lax.fori_loop(0, tm, start, 0)
    lax.fori_loop(0, tm, wait, 0)


def _dispatch(dest, h, n2, xs0):
    n, d = h.shape
    tm = TM_ROWS
    return pl.pallas_call(
        _dispatch_body,
        grid=(n // tm,),
        in_specs=[pl.BlockSpec((TOP_K, tm), lambda i: (0, i), memory_space=pltpu.SMEM),
                  pl.BlockSpec((tm, d), lambda i: (i, 0)),
                  pl.BlockSpec((1, d), lambda i: (0, 0)),
                  pl.BlockSpec(memory_space=pl.ANY)],
        out_specs=pl.BlockSpec(memory_space=pl.ANY),
        out_shape=jax.ShapeDtypeStruct(xs0.shape, xs0.dtype),
        scratch_shapes=[pltpu.VMEM((tm, d), F32), pltpu.SemaphoreType.DMA(())],
        input_output_aliases={3: 0},
        compiler_params=_cparams(("arbitrary",)),
        name="dispatch",
    )(dest, h, n2, xs0)


def _expert_body(be_ref, nu_ref, xs_ref, wg_ref, wu_ref, wd_ref, ys_ref, wg_s, wu_s, wd_s):
    j = pl.program_id(0)

    @pl.when(j < nu_ref[0])
    def _():
        changed = jnp.logical_or(j == 0, be_ref[j] != be_ref[jnp.maximum(j - 1, 0)])

        @pl.when(changed)
        def _():
            wg_s[...] = wg_ref[...].astype(BF16)
            wu_s[...] = wu_ref[...].astype(BF16)
            wd_s[...] = wd_ref[...].astype(BF16)

        x = xs_ref[...].astype(BF16)
        gate = jnp.dot(x, wg_s[...], preferred_element_type=F32)
        up = jnp.dot(x, wu_s[...], preferred_element_type=F32)
        hid = (jax.nn.silu(gate) * up).astype(BF16)
        ys_ref[...] = jnp.dot(hid, wd_s[...], preferred_element_type=F32)

    @pl.when(j >= nu_ref[0])
    def _():
        ys_ref[...] = jnp.zeros_like(ys_ref)


def _experts(block_e, n_used, xs, w_gate, w_up, w_down):
    p_rows, d = xs.shape
    d_e = w_gate.shape[2]
    nblk = p_rows // R_BLK

    def row_map(j, be, nu):
        return (jnp.minimum(j, nu[0] - 1), 0)

    def w_map(j, be, nu):
        return (be[j], 0, 0)

    grid_spec = pltpu.PrefetchScalarGridSpec(
        num_scalar_prefetch=2,
        grid=(nblk,),
        in_specs=[pl.BlockSpec((R_BLK, d), row_map),
                  pl.BlockSpec((None, d, d_e), w_map),
                  pl.BlockSpec((None, d, d_e), w_map),
                  pl.BlockSpec((None, d_e, d), w_map)],
        out_specs=pl.BlockSpec((R_BLK, d), lambda j, be, nu: (j, 0)),
        scratch_shapes=[pltpu.VMEM((d, d_e), BF16), pltpu.VMEM((d, d_e), BF16),
                        pltpu.VMEM((d_e, d), BF16)])
    return pl.pallas_call(
        _expert_body,
        grid_spec=grid_spec,
        out_shape=jax.ShapeDtypeStruct((p_rows, d), F32),
        compiler_params=_cparams(("arbitrary",)),
        name="experts",
    )(block_e, n_used, xs, w_gate, w_up, w_down)


def _combine_body(dest_ref, h_ref, route_ref, fw_ref, ys_ref, out_ref, g_scr, sem):
    tm = h_ref.shape[0]

    def row_copy(k, r):
        return pltpu.make_async_copy(ys_ref.at[pl.ds(dest_ref[k, r], 1)],
                                     g_scr.at[k, pl.ds(r, 1)], sem)

    def start(r, carry):
        for k in range(TOP_K):
            row_copy(k, r).start()
        return carry

    def wait(r, carry):
        for k in range(TOP_K):
            row_copy(k, r).wait()
        return carry

    lax.fori_loop(0, tm, start, 0)
    lax.fori_loop(0, tm, wait, 0)
    rec = route_ref[...]
    hh = h_ref[...] + rec[:, 2:3] * g_scr[0] + rec[:, 3:4] * g_scr[1]
    out_ref[...] = _rms(hh, fw_ref[...])


def _combine(dest, h, route, fw, ys):
    n, d = h.shape
    tm = TM_ROWS
    return pl.pallas_call(
        _combine_body,
        grid=(n // tm,),
        in_specs=[pl.BlockSpec((TOP_K, tm), lambda i: (0, i), memory_space=pltpu.SMEM),
                  pl.BlockSpec((tm, d), lambda i: (i, 0)),
                  pl.BlockSpec((tm, ROUTE_LANES), lambda i: (i, 0)),
                  pl.BlockSpec((1, d), lambda i: (0, 0)),
                  pl.BlockSpec(memory_space=pl.ANY)],
        out_specs=pl.BlockSpec((tm, d), lambda i: (i, 0)),
        out_shape=jax.ShapeDtypeStruct((n, d), F32),
        scratch_shapes=[pltpu.VMEM((TOP_K, tm, d), F32), pltpu.SemaphoreType.DMA(())],
        compiler_params=_cparams(("arbitrary",)),
        name="combine",
    )(dest, h, route, fw, ys)


def _plan(route, counts, n_experts, n_blocks):
    eid = route[:, 0:TOP_K].astype(jnp.int32)
    rank = route[:, 4:4 + TOP_K].astype(jnp.int32)
    cnt = counts[0, N_GROUPS:N_GROUPS + n_experts].astype(jnp.int32)
    padded = (cnt + R_BLK - 1) // R_BLK * R_BLK
    pends = jnp.cumsum(padded)
    pstart = pends - padded
    dest = (pstart[eid] + rank).T
    n_used = pends[-1] // R_BLK
    blk = jnp.minimum(jnp.arange(n_blocks, dtype=jnp.int32), n_used - 1)
    block_e = jnp.minimum(jnp.searchsorted(pends, blk * R_BLK, side='right'), n_experts - 1)
    return dest.astype(jnp.int32), block_e.astype(jnp.int32), n_used.reshape(1).astype(jnp.int32)


def _layer(h3, norm1_w, w_in, s5_a_re, s5_a_im, s5_b_re, s5_b_im, s5_c_re, s5_c_im, s5_d,
           s5_log_dt, s5_w_glu, s5_b_glu, ret_norm_w, w_out, norm2_w, router_group_w,
           router_group_b, router_expert_w, router_expert_b, moe_w_gate, moe_w_up, moe_w_down,
           out_norm_w):
    nb, seq, d = h3.shape
    n = nb * seq
    d_s5 = s5_d.shape[0]
    d_ret = ret_norm_w.shape[0]
    n_experts = moe_w_gate.shape[0]
    x2 = h3.reshape(n, d)

    u = _inproj(x2, norm1_w.reshape(1, d), w_in.astype(BF16))
    u3 = u.reshape(nb, seq, -1)

    bm, cm, a_tab = _s5_tables(s5_a_re, s5_a_im, s5_b_re, s5_b_im, s5_c_re, s5_c_im, s5_log_dt, nb)
    y_s5 = _s5(u3, bm, cm, a_tab, s5_d.reshape(1, d_s5).astype(F32), s5_w_glu.astype(BF16),
               s5_b_glu.reshape(1, d_s5).astype(F32))
    y_ret = _retention(u3, d_s5, d_ret, ret_norm_w.reshape(1, d_ret).astype(F32))

    n_route = N_GROUPS + n_experts
    wr = jnp.zeros((d, ROUTE_LANES), F32).at[:, :n_route].set(
        jnp.concatenate([router_group_w, router_expert_w], axis=1).astype(F32))
    br = jnp.zeros((1, ROUTE_LANES), F32).at[0, :n_route].set(
        jnp.concatenate([router_group_b, router_expert_b]).astype(F32))
    h, route, counts = _route(y_s5.reshape(n, d_s5), y_ret.reshape(n, d_ret), x2,
                              w_out.astype(BF16), norm2_w.reshape(1, d), wr, br)

    n_blocks = (n * TOP_K) // R_BLK + n_experts
    dest, block_e, n_used = _plan(route, counts, n_experts, n_blocks)
    xs = _dispatch(dest, h, norm2_w.reshape(1, d), jnp.zeros((n_blocks * R_BLK, d), F32))
    ys = _experts(block_e, n_used, xs, moe_w_gate, moe_w_up, moe_w_down)
    out = _combine(dest, h, route, out_norm_w.reshape(1, d), ys)
    return out.reshape(nb, seq, d)


def kernel(x, norm1_w, w_in, s5_a_re, s5_a_im, s5_b_re, s5_b_im, s5_c_re, s5_c_im, s5_d, s5_log_dt, s5_w_glu, s5_b_glu, ret_norm_w, w_out, norm2_w, router_group_w, router_group_b, router_expert_w, router_expert_b, moe_w_gate, moe_w_up, moe_w_down, final_norm_w):
    depth = norm1_w.shape[0]
    assert depth == 1, "the fused final norm assumes a single layer"
    l = 0
    return _layer(x, norm1_w[l], w_in[l], s5_a_re[l], s5_a_im[l], s5_b_re[l], s5_b_im[l],
                  s5_c_re[l], s5_c_im[l], s5_d[l], s5_log_dt[l], s5_w_glu[l], s5_b_glu[l],
                  ret_norm_w[l], w_out[l], norm2_w[l], router_group_w[l], router_group_b[l],
                  router_expert_w[l], router_expert_b[l], moe_w_gate[l], moe_w_up[l],
                  moe_w_down[l], final_norm_w)
```

```python
import functools

import jax
import jax.numpy as jnp
from jax import lax
from jax.experimental import pallas as pl
from jax.experimental.pallas import tpu as pltpu

F32 = jnp.float32
BF16 = jnp.bfloat16

EPS = 1e-6
ROPE_BASE = 10000.0
RET_HEADS = 8
TOP_K = 2
N_GROUPS = 4
EXPERTS_PER_GROUP = 8

V7X_LANES = 128
V7X_SUBLANES = 8
V7X_VMEM_LIMIT = 56 * 1024 * 1024

TM_PROJ = 512
T_S5 = 128
S5_PAD = 8
T_RET = 128
TM_ROUTE = 512
TM_ROWS = 256
R_BLK = 512
DMA_UNROLL = 8
ROUTE_LANES = 128


def _rms(x, w):
    return x * lax.rsqrt(jnp.mean(x * x, axis=-1, keepdims=True) + EPS) * w


def _cparams(sem):
    return pltpu.CompilerParams(dimension_semantics=sem, vmem_limit_bytes=V7X_VMEM_LIMIT)


def _inproj_body(x_ref, nw_ref, w_ref, u_ref):
    xn = _rms(x_ref[...], nw_ref[...]).astype(BF16)
    u_ref[...] = jnp.dot(xn, w_ref[...], preferred_element_type=F32).astype(u_ref.dtype)


def _inproj(x2, nw, w_bf):
    n, d = x2.shape
    d_in = w_bf.shape[1]
    return pl.pallas_call(
        _inproj_body,
        grid=(n // TM_PROJ,),
        in_specs=[pl.BlockSpec((TM_PROJ, d), lambda i: (i, 0)),
                  pl.BlockSpec((1, d), lambda i: (0, 0)),
                  pl.BlockSpec((d, d_in), lambda i: (0, 0))],
        out_specs=pl.BlockSpec((TM_PROJ, d_in), lambda i: (i, 0)),
        out_shape=jax.ShapeDtypeStruct((n, d_in), BF16),
        compiler_params=_cparams(("arbitrary",)),
        name="inproj",
    )(x2, nw, w_bf)


def _s5_body(u_ref, bm_ref, cm_ref, a_ref, dd_ref, wglu_ref, bglu_ref, y_ref,
             bu_scr, s_scr, st_scr, *, nb, t_len, seg):
    @pl.when(pl.program_id(0) == 0)
    def _():
        st_scr[...] = jnp.zeros_like(st_scr)

    nblk = bm_ref.shape[0]
    cw = bm_ref.shape[1]
    sw = bm_ref.shape[2]
    tiles = sw // V7X_LANES
    ht = tiles // 2
    u2 = u_ref[...].reshape(nb * t_len, nblk * cw)
    for blk in range(nblk):
        half, q = divmod(blk, 2)
        r = jnp.dot(u2[:, blk * cw:(blk + 1) * cw], bm_ref[blk], preferred_element_type=F32)
        for b in range(nb):
            for j in range(tiles):
                bu_scr[q * tiles + j, pl.ds((half * nb + b) * seg, t_len), :] = (
                    r[b * t_len:(b + 1) * t_len, j * V7X_LANES:(j + 1) * V7X_LANES])

    rows = 2 * nb
    ar = [[a_ref[0, q * ht + i] for i in range(ht)] for q in range(2)]
    ai = [[a_ref[1, q * ht + i] for i in range(ht)] for q in range(2)]
    sr = [[st_scr[q * tiles + i] for i in range(ht)] for q in range(2)]
    si = [[st_scr[q * tiles + ht + i] for i in range(ht)] for q in range(2)]
    for t in range(t_len):
        for q in range(2):
            for i in range(ht):
                jr = q * tiles + i
                ji = q * tiles + ht + i
                vr = bu_scr[jr, pl.ds(t, rows, stride=seg), :]
                vi = bu_scr[ji, pl.ds(t, rows, stride=seg), :]
                nr = ar[q][i] * sr[q][i] - ai[q][i] * si[q][i] + vr
                ni = ar[q][i] * si[q][i] + ai[q][i] * sr[q][i] + vi
                s_scr[jr, pl.ds(t, rows, stride=seg), :] = nr
                s_scr[ji, pl.ds(t, rows, stride=seg), :] = ni
                sr[q][i], si[q][i] = nr, ni
    for q in range(2):
        for i in range(ht):
            st_scr[q * tiles + i] = sr[q][i]
            st_scr[q * tiles + ht + i] = si[q][i]

    ys = []
    for blk in range(nblk):
        half, q = divmod(blk, 2)
        sb = jnp.concatenate(
            [jnp.concatenate([s_scr[q * tiles + j, pl.ds((half * nb + b) * seg, t_len), :]
                              for j in range(tiles)], axis=1) for b in range(nb)],
            axis=0).astype(BF16)
        ys.append(jnp.dot(sb, cm_ref[blk], preferred_element_type=F32))
    y = jnp.concatenate(ys, axis=1) + dd_ref[...] * u2.astype(F32)
    y = jax.nn.gelu(y)
    z = jnp.dot(y.astype(BF16), wglu_ref[...], preferred_element_type=F32) + bglu_ref[...]
    out = y * jax.nn.sigmoid(z)
    y_ref[...] = out.reshape(nb, t_len, nblk * cw).astype(y_ref.dtype)


def _s5(u3, bm, cm, a_tab, dd, wglu_bf, bglu):
    nb, seq, _ = u3.shape
    nblk, cw, sw = bm.shape
    d_s5 = nblk * cw
    seg = T_S5 + S5_PAD
    rows = 2 * nb
    assert rows == V7X_SUBLANES and nblk == 4
    body = functools.partial(_s5_body, nb=nb, t_len=T_S5, seg=seg)
    return pl.pallas_call(
        body,
        grid=(seq // T_S5,),
        in_specs=[pl.BlockSpec((nb, T_S5, d_s5), lambda c: (0, c, 0)),
                  pl.BlockSpec(bm.shape, lambda c: (0, 0, 0)),
                  pl.BlockSpec(cm.shape, lambda c: (0, 0, 0)),
                  pl.BlockSpec(a_tab.shape, lambda c: (0, 0, 0, 0)),
                  pl.BlockSpec((1, d_s5), lambda c: (0, 0)),
                  pl.BlockSpec((d_s5, d_s5), lambda c: (0, 0)),
                  pl.BlockSpec((1, d_s5), lambda c: (0, 0))],
        out_specs=pl.BlockSpec((nb, T_S5, d_s5), lambda c: (0, c, 0)),
        out_shape=jax.ShapeDtypeStruct((nb, seq, d_s5), BF16),
        scratch_shapes=[pltpu.VMEM((2 * sw // V7X_LANES, rows * seg, V7X_LANES), F32),
                        pltpu.VMEM((2 * sw // V7X_LANES, rows * seg, V7X_LANES), F32),
                        pltpu.VMEM((2 * sw // V7X_LANES, rows, V7X_LANES), F32)],
        compiler_params=_cparams(("arbitrary",)),
        name="s5",
    )(u3, bm, cm, a_tab, dd, wglu_bf, bglu)


def _s5_tables(a_re, a_im, b_re, b_im, c_re, c_im, log_dt, nb):
    g, p = a_re.shape
    hch = b_re.shape[2]
    gpb = V7X_LANES // hch
    nblk = g // gpb
    lam_r, lam_i = a_re.astype(F32), a_im.astype(F32)
    dt = jnp.exp(log_dt.astype(F32))[:, None]
    mag = jnp.exp(lam_r * dt)
    ab_r = mag * jnp.cos(lam_i * dt)
    ab_i = mag * jnp.sin(lam_i * dt)
    den = lam_r * lam_r + lam_i * lam_i
    zr = ((ab_r - 1.0) * lam_r + ab_i * lam_i) / den
    zi = (ab_i * lam_r - (ab_r - 1.0) * lam_i) / den
    br_, bi_ = b_re.astype(F32), b_im.astype(F32)
    bb_r = zr[..., None] * br_ - zi[..., None] * bi_
    bb_i = zr[..., None] * bi_ + zi[..., None] * br_
    eye = jnp.eye(gpb, dtype=F32)

    def bmat(bb):
        return jnp.einsum('bgph,gk->bghkp', bb.reshape(nblk, gpb, p, hch), eye).reshape(
            nblk, gpb * hch, gpb * p)

    def cmat(cc):
        return jnp.einsum('bghp,gk->bgpkh', cc.reshape(nblk, gpb, hch, p), eye).reshape(
            nblk, gpb * p, gpb * hch)

    bm = jnp.concatenate([bmat(bb_r), bmat(bb_i)], axis=2).astype(BF16)
    cm = jnp.concatenate([cmat(c_re.astype(F32)), -cmat(c_im.astype(F32))], axis=1).astype(BF16)
    a_tab = jnp.stack([jnp.repeat(ab_r.reshape(2, -1), nb, axis=0),
                       jnp.repeat(ab_i.reshape(2, -1), nb, axis=0)])
    a_tab = a_tab.reshape(2, 2 * nb, -1, V7X_LANES).transpose(0, 2, 1, 3)
    return bm, cm, a_tab


def _ret_body(q_ref, k_ref, v_ref, g_ref, cos_ref, sin_ref, dec_ref, qdec_ref, kdec_ref,
              cdec_ref, nw_ref, y_ref, st_scr, *, t_len, dh, scale):
    b = pl.program_id(1)

    @pl.when(pl.program_id(0) == 0)
    def _():
        st_scr[b] = jnp.zeros(st_scr.shape[1:], F32)

    width = q_ref.shape[-1]
    heads = width // dh
    reps = width // cos_ref.shape[-1]
    cos = jnp.concatenate([cos_ref[...]] * reps, axis=1)
    sin = jnp.concatenate([sin_ref[...]] * reps, axis=1)
    lane = lax.broadcasted_iota(jnp.int32, (t_len, width), 1)
    first = (lane % dh) < (dh // 2)

    def rot(x):
        x = x.astype(F32)
        partner = jnp.where(first, pltpu.roll(x, width - dh // 2, 1), pltpu.roll(x, dh // 2, 1))
        return x * cos + partner * sin

    q = rot(q_ref[...])
    k = rot(k_ref[...]) * scale
    qb, kb = q.astype(BF16), k.astype(BF16)
    qd = (q * qdec_ref[...]).astype(BF16)
    kd = (k * kdec_ref[...]).astype(BF16)
    vb = v_ref[...]
    outs = []
    for h in range(heads):
        sl = slice(h * dh, (h + 1) * dh)
        sc = lax.dot_general(qb[:, sl], kb[:, sl], (((1,), (1,)), ((), ())),
                             preferred_element_type=F32) * dec_ref[h]
        inner = jnp.dot(sc.astype(BF16), vb[:, sl], preferred_element_type=F32)
        state = st_scr[b, h]
        cross = jnp.dot(qd[:, sl], state.astype(BF16), preferred_element_type=F32)
        o = inner + cross
        mu = jnp.mean(o, axis=-1, keepdims=True)
        dlt = o - mu
        var = jnp.mean(dlt * dlt, axis=-1, keepdims=True)
        outs.append(dlt * lax.rsqrt(var + EPS))
        kv = lax.dot_general(kd[:, sl], vb[:, sl], (((0,), (0,)), ((), ())),
                             preferred_element_type=F32)
        st_scr[b, h] = state * cdec_ref[h] + kv
    o = jnp.concatenate(outs, axis=1) * nw_ref[...]
    y_ref[...] = (jax.nn.silu(g_ref[...].astype(F32)) * o).astype(y_ref.dtype)


def _retention(u3, d_s5, d_ret, norm_w):
    nb, seq, _ = u3.shape
    dh = d_ret // RET_HEADS
    half = dh // 2
    t_len = T_RET
    first_blk = d_s5 // d_ret
    assert d_s5 % d_ret == 0
    lg = jnp.log(1.0 - 2.0 ** (-5.0 - jnp.arange(RET_HEADS, dtype=F32)))
    t = jnp.arange(t_len, dtype=F32)
    diff = t[:, None] - t[None, :]
    dec = jnp.where(diff >= 0, jnp.exp(lg[:, None, None] * jnp.maximum(diff, 0.0)), 0.0)
    qdec = jnp.repeat(jnp.exp(lg[:, None] * (t + 1.0)[None, :]).T, dh, axis=1)
    kdec = jnp.repeat(jnp.exp(lg[:, None] * (t_len - 1 - t)[None, :]).T, dh, axis=1)
    cdec = jnp.broadcast_to(jnp.exp(lg * t_len)[:, None, None], (RET_HEADS, 1, dh))
    inv = ROPE_BASE ** (-jnp.arange(half, dtype=F32) / half)
    ang = jnp.arange(seq, dtype=F32)[:, None] * inv[None, :]
    reps = V7X_LANES // dh
    cos_t = jnp.tile(jnp.cos(ang), (1, 2 * reps))
    sin_t = jnp.tile(jnp.concatenate([-jnp.sin(ang), jnp.sin(ang)], axis=1), (1, reps))

    def col(j):
        return pl.BlockSpec((None, t_len, d_ret), lambda c, b: (b, c, first_blk + j))

    body = functools.partial(_ret_body, t_len=t_len, dh=dh, scale=dh ** -0.5)
    return pl.pallas_call(
        body,
        grid=(seq // t_len, nb),
        in_specs=[col(0), col(1), col(2), col(3),
                  pl.BlockSpec((t_len, V7X_LANES), lambda c, b: (c, 0)),
                  pl.BlockSpec((t_len, V7X_LANES), lambda c, b: (c, 0)),
                  pl.BlockSpec(dec.shape, lambda c, b: (0, 0, 0)),
                  pl.BlockSpec(qdec.shape, lambda c, b: (0, 0)),
                  pl.BlockSpec(kdec.shape, lambda c, b: (0, 0)),
                  pl.BlockSpec(cdec.shape, lambda c, b: (0, 0, 0)),
                  pl.BlockSpec((1, d_ret), lambda c, b: (0, 0))],
        out_specs=pl.BlockSpec((None, t_len, d_ret), lambda c, b: (b, c, 0)),
        out_shape=jax.ShapeDtypeStruct((nb, seq, d_ret), BF16),
        scratch_shapes=[pltpu.VMEM((nb, RET_HEADS, dh, dh), F32)],
        compiler_params=_cparams(("arbitrary", "arbitrary")),
        name="retention",
    )(u3, u3, u3, u3, cos_t, sin_t, dec, qdec, kdec, cdec, norm_w)


def _route_body(ys5_ref, yret_ref, x_ref, wo_ref, n2_ref, wr_ref, br_ref, tri_ref,
                h_ref, route_ref, cnt_ref, carry_scr):
    @pl.when(pl.program_id(0) == 0)
    def _():
        carry_scr[...] = jnp.zeros_like(carry_scr)

    d_s5 = ys5_ref.shape[1]
    h = (x_ref[...]
         + jnp.dot(ys5_ref[...], wo_ref[0:d_s5], preferred_element_type=F32)
         + jnp.dot(yret_ref[...], wo_ref[d_s5:], preferred_element_type=F32))
    h_ref[...] = h
    hn = _rms(h, n2_ref[...])
    hi = hn.astype(BF16)
    lo = (hn - hi.astype(F32)).astype(BF16)
    p_hi = jnp.dot(hi, wr_ref[...], preferred_element_type=F32)
    p_lo = jnp.dot(lo, wr_ref[:, 0:ROUTE_LANES], preferred_element_type=F32)
    logits = p_hi[:, 0:ROUTE_LANES] + p_hi[:, ROUTE_LANES:] + p_lo + br_ref[...]
    tm = logits.shape[0]
    lane = lax.broadcasted_iota(jnp.int32, (tm, ROUTE_LANES), 1)
    lanef = lane.astype(F32)
    neg = -jnp.inf
    big = float(ROUTE_LANES)
    gl = jnp.where(lane < N_GROUPS, logits, neg)
    gmax = jnp.max(gl, axis=-1, keepdims=True)
    gidx = jnp.min(jnp.where(gl == gmax, lanef, big), axis=-1, keepdims=True)
    g_w = 1.0 / jnp.sum(jnp.exp(gl - gmax), axis=-1, keepdims=True)
    lo = N_GROUPS + EXPERTS_PER_GROUP * gidx
    el = jnp.where((lanef >= lo) & (lanef < lo + EXPERTS_PER_GROUP), logits, neg)
    v0 = jnp.max(el, axis=-1, keepdims=True)
    i0 = jnp.min(jnp.where(el == v0, lanef, big), axis=-1, keepdims=True)
    el2 = jnp.where(lanef == i0, neg, el)
    v1 = jnp.max(el2, axis=-1, keepdims=True)
    i1 = jnp.min(jnp.where(el2 == v1, lanef, big), axis=-1, keepdims=True)
    e = jnp.exp(v1 - v0)
    den = 1.0 + e
    w0 = (1.0 / den) * g_w
    w1 = (e / den) * g_w
    sel0 = lanef == i0
    sel1 = lanef == i1
    onehot = jnp.where(sel0 | sel1, 1.0, 0.0)
    before = jnp.dot(tri_ref[...], onehot.astype(BF16), preferred_element_type=F32) + carry_scr[...]
    r0 = jnp.sum(jnp.where(sel0, before, 0.0), axis=-1, keepdims=True)
    r1 = jnp.sum(jnp.where(sel1, before, 0.0), axis=-1, keepdims=True)
    carry_scr[...] += jnp.sum(onehot, axis=0, keepdims=True)
    cnt_ref[...] = carry_scr[...]
    rec = jnp.zeros((tm, ROUTE_LANES), F32)
    for j, val in enumerate((i0 - N_GROUPS, i1 - N_GROUPS, w0, w1, r0, r1)):
        rec = jnp.where(lane == j, val, rec)
    route_ref[...] = rec


def _route(ys5, yret, x2, wo_bf, n2, wr, br):
    n, d = x2.shape
    d_s5 = ys5.shape[1]
    d_ret = yret.shape[1]
    tm = TM_ROUTE
    tri = (jnp.arange(tm)[:, None] > jnp.arange(tm)[None, :]).astype(BF16)
    return pl.pallas_call(
        _route_body,
        grid=(n // tm,),
        in_specs=[pl.BlockSpec((tm, d_s5), lambda i: (i, 0)),
                  pl.BlockSpec((tm, d_ret), lambda i: (i, 0)),
                  pl.BlockSpec((tm, d), lambda i: (i, 0)),
                  pl.BlockSpec((d_s5 + d_ret, d), lambda i: (0, 0)),
                  pl.BlockSpec((1, d), lambda i: (0, 0)),
                  pl.BlockSpec((d, 2 * ROUTE_LANES), lambda i: (0, 0)),
                  pl.BlockSpec((1, ROUTE_LANES), lambda i: (0, 0)),
                  pl.BlockSpec((tm, tm), lambda i: (0, 0))],
        out_specs=[pl.BlockSpec((tm, d), lambda i: (i, 0)),
                   pl.BlockSpec((tm, ROUTE_LANES), lambda i: (i, 0)),
                   pl.BlockSpec((1, ROUTE_LANES), lambda i: (0, 0))],
        out_shape=[jax.ShapeDtypeStruct((n, d), F32),
                   jax.ShapeDtypeStruct((n, ROUTE_LANES), F32),
                   jax.ShapeDtypeStruct((1, ROUTE_LANES), F32)],
        scratch_shapes=[pltpu.VMEM((1, ROUTE_LANES), F32)],
        compiler_params=_cparams(("arbitrary",)),
        name="outproj_route",
    )(ys5, yret, x2, wo_bf, n2, wr, br, tri)


def _dispatch_body(pends_ref, padded_ref, dest_ref, h_ref, n2_ref, xs_ref,
                   hn_scr, zero_scr, sem, zsem):
    tm = h_ref.shape[0]

    @pl.when(pl.program_id(0) == 0)
    def _():
        zero_scr[...] = jnp.zeros_like(zero_scr)

        def zero_copy(e):
            first = pl.multiple_of(pends_ref[e] - R_BLK, R_BLK)
            return pltpu.make_async_copy(zero_scr, xs_ref.at[pl.ds(first, R_BLK)], zsem)

        def zstart(e, carry):
            @pl.when(padded_ref[e] > 0)
            def _():
                zero_copy(e).start()
            return carry

        def zwait(e, carry):
            @pl.when(padded_ref[e] > 0)
            def _():
                zero_copy(e).wait()
            return carry

        lax.fori_loop(0, pends_ref.shape[0], zstart, 0)
        lax.fori_loop(0, pends_ref.shape[0], zwait, 0)

    hn_scr[...] = _rms(h_ref[...], n2_ref[...])

    def start(r, carry):
        for k in range(TOP_K):
            pltpu.make_async_copy(hn_scr.at[pl.ds(r, 1)],
                                  xs_ref.at[pl.ds(dest_ref[k, r], 1)], sem).start()
        return carry

    lax.fori_loop(0, tm, start, 0, unroll=DMA_UNROLL)
    for k in range(TOP_K):
        pltpu.make_async_copy(hn_scr, xs_ref.at[pl.ds(0, tm)], sem).wait()


def _dispatch(pends, padded, dest, h, n2, p_rows):
    n, d = h.shape
    tm = TM_ROWS
    grid_spec = pltpu.PrefetchScalarGridSpec(
        num_scalar_prefetch=2,
        grid=(n // tm,),
        in_specs=[pl.BlockSpec((TOP_K, tm), lambda i, pe, pa: (0, i), memory_space=pltpu.SMEM),
                  pl.BlockSpec((tm, d), lambda i, pe, pa: (i, 0)),
                  pl.BlockSpec((1, d), lambda i, pe, pa: (0, 0))],
        out_specs=pl.BlockSpec(memory_space=pl.ANY),
        scratch_shapes=[pltpu.VMEM((tm, d), F32), pltpu.VMEM((R_BLK, d), F32),
                        pltpu.SemaphoreType.DMA(()), pltpu.SemaphoreType.DMA(())])
    return pl.pallas_call(
        _dispatch_body,
        grid_spec=grid_spec,
        out_shape=jax.ShapeDtypeStruct((p_rows, d), F32),
        compiler_params=_cparams(("arbitrary",)),
        name="dispatch",
    )(pends, padded, dest, h, n2)


def _expert_body(be_ref, nu_ref, xs_ref, wg_ref, wu_ref, wd_ref, ys_ref, wg_s, wu_s, wd_s):
    j = pl.program_id(0)

    @pl.when(j < nu_ref[0])
    def _():
        changed = jnp.logical_or(j == 0, be_ref[j] != be_ref[jnp.maximum(j - 1, 0)])

        @pl.when(changed)
        def _():
            wg_s[...] = wg_ref[...].astype(BF16)
            wu_s[...] = wu_ref[...].astype(BF16)
            wd_s[...] = wd_ref[...].astype(BF16)

        x = xs_ref[...].astype(BF16)
        gate = jnp.dot(x, wg_s[...], preferred_element_type=F32)
        up = jnp.dot(x, wu_s[...], preferred_element_type=F32)
        hid = (jax.nn.silu(gate) * up).astype(BF16)
        ys_ref[...] = jnp.dot(hid, wd_s[...], preferred_element_type=F32)

    @pl.when(j >= nu_ref[0])
    def _():
        ys_ref[...] = jnp.zeros_like(ys_ref)


def _experts(block_e, n_used, xs, w_gate, w_up, w_down):
    p_rows, d = xs.shape
    d_e = w_gate.shape[2]
    nblk = p_rows // R_BLK

    def row_map(j, be, nu):
        return (jnp.minimum(j, nu[0] - 1), 0)

    def w_map(j, be, nu):
        return (be[j], 0, 0)

    grid_spec = pltpu.PrefetchScalarGridSpec(
        num_scalar_prefetch=2,
        grid=(nblk,),
        in_specs=[pl.BlockSpec((R_BLK, d), row_map),
                  pl.BlockSpec((None, d, d_e), w_map),
                  pl.BlockSpec((None, d, d_e), w_map),
                  pl.BlockSpec((None, d_e, d), w_map)],
        out_specs=pl.BlockSpec((R_BLK, d), lambda j, be, nu: (j, 0)),
        scratch_shapes=[pltpu.VMEM((d, d_e), BF16), pltpu.VMEM((d, d_e), BF16),
                        pltpu.VMEM((d_e, d), BF16)])
    return pl.pallas_call(
        _expert_body,
        grid_spec=grid_spec,
        out_shape=jax.ShapeDtypeStruct((p_rows, d), F32),
        compiler_params=_cparams(("arbitrary",)),
        name="experts",
    )(block_e, n_used, xs, w_gate, w_up, w_down)


def _combine_body(dest_ref, h_ref, route_ref, fw_ref, ys_ref, out_ref, g_scr, sem):
    tm = h_ref.shape[0]

    def start(r, carry):
        for k in range(TOP_K):
            pltpu.make_async_copy(ys_ref.at[pl.ds(dest_ref[k, r], 1)],
                                  g_scr.at[k, pl.ds(r, 1)], sem).start()
        return carry

    lax.fori_loop(0, tm, start, 0, unroll=DMA_UNROLL)
    for k in range(TOP_K):
        pltpu.make_async_copy(ys_ref.at[pl.ds(0, tm)], g_scr.at[k], sem).wait()
    rec = route_ref[...]
    hh = h_ref[...] + rec[:, 2:3] * g_scr[0] + rec[:, 3:4] * g_scr[1]
    out_ref[...] = _rms(hh, fw_ref[...])


def _combine(dest, h, route, fw, ys):
    n, d = h.shape
    tm = TM_ROWS
    return pl.pallas_call(
        _combine_body,
        grid=(n // tm,),
        in_specs=[pl.BlockSpec((TOP_K, tm), lambda i: (0, i), memory_space=pltpu.SMEM),
                  pl.BlockSpec((tm, d), lambda i: (i, 0)),
                  pl.BlockSpec((tm, ROUTE_LANES), lambda i: (i, 0)),
                  pl.BlockSpec((1, d), lambda i: (0, 0)),
                  pl.BlockSpec(memory_space=pl.ANY)],
        out_specs=pl.BlockSpec((tm, d), lambda i: (i, 0)),
        out_shape=jax.ShapeDtypeStruct((n, d), F32),
        scratch_shapes=[pltpu.VMEM((TOP_K, tm, d), F32), pltpu.SemaphoreType.DMA(())],
        compiler_params=_cparams(("arbitrary",)),
        name="combine",
    )(dest, h, route, fw, ys)


def _plan(route, counts, n_experts, n_blocks):
    eid = route[:, 0:TOP_K].astype(jnp.int32)
    rank = route[:, 4:4 + TOP_K].astype(jnp.int32)
    cnt = counts[0, N_GROUPS:N_GROUPS + n_experts].astype(jnp.int32)
    padded = (cnt + R_BLK - 1) // R_BLK * R_BLK
    pends = jnp.cumsum(padded)
    pstart = pends - padded
    dest = (pstart[eid] + rank).T
    n_used = pends[-1] // R_BLK
    blk = jnp.minimum(jnp.arange(n_blocks, dtype=jnp.int32), n_used - 1)
    block_e = jnp.minimum(jnp.sum(pends[None, :] <= (blk * R_BLK)[:, None], axis=1), n_experts - 1)
    return (dest.astype(jnp.int32), block_e.astype(jnp.int32), n_used.reshape(1).astype(jnp.int32),
            pends.astype(jnp.int32), padded.astype(jnp.int32))


def _layer(h3, norm1_w, w_in, s5_a_re, s5_a_im, s5_b_re, s5_b_im, s5_c_re, s5_c_im, s5_d,
           s5_log_dt, s5_w_glu, s5_b_glu, ret_norm_w, w_out, norm2_w, router_group_w,
           router_group_b, router_expert_w, router_expert_b, moe_w_gate, moe_w_up, moe_w_down,
           out_norm_w):
    nb, seq, d = h3.shape
    n = nb * seq
    d_s5 = s5_d.shape[0]
    d_ret = ret_norm_w.shape[0]
    n_experts = moe_w_gate.shape[0]
    x2 = h3.reshape(n, d)

    u = _inproj(x2, norm1_w.reshape(1, d), w_in.astype(BF16))
    u3 = u.reshape(nb, seq, -1)

    bm, cm, a_tab = _s5_tables(s5_a_re, s5_a_im, s5_b_re, s5_b_im, s5_c_re, s5_c_im, s5_log_dt, nb)
    y_s5 = _s5(u3, bm, cm, a_tab, s5_d.reshape(1, d_s5).astype(F32), s5_w_glu.astype(BF16),
               s5_b_glu.reshape(1, d_s5).astype(F32))
    y_ret = _retention(u3, d_s5, d_ret, ret_norm_w.reshape(1, d_ret).astype(F32))

    n_route = N_GROUPS + n_experts
    wr = jnp.zeros((d, ROUTE_LANES), F32).at[:, :n_route].set(
        jnp.concatenate([router_group_w, router_expert_w], axis=1).astype(F32))
    br = jnp.zeros((1, ROUTE_LANES), F32).at[0, :n_route].set(
        jnp.concatenate([router_group_b, router_expert_b]).astype(F32))
    wr_hi = wr.astype(BF16)
    wr = jnp.concatenate([wr_hi, (wr - wr_hi.astype(F32)).astype(BF16)], axis=1)
    h, route, counts = _route(y_s5.reshape(n, d_s5), y_ret.reshape(n, d_ret), x2,
                              w_out.astype(BF16), norm2_w.reshape(1, d), wr, br)

    n_blocks = (n * TOP_K) // R_BLK + n_experts
    dest, block_e, n_used, pends, padded = _plan(route, counts, n_experts, n_blocks)
    xs = _dispatch(pends, padded, dest, h, norm2_w.reshape(1, d), n_blocks * R_BLK)
    ys = _experts(block_e, n_used, xs, moe_w_gate, moe_w_up, moe_w_down)
    out = _combine(dest, h, route, out_norm_w.reshape(1, d), ys)
    return out.reshape(nb, seq, d)


def kernel(x, norm1_w, w_in, s5_a_re, s5_a_im, s5_b_re, s5_b_im, s5_c_re, s5_c_im, s5_d, s5_log_dt, s5_w_glu, s5_b_glu, ret_norm_w, w_out, norm2_w, router_group_w, router_group_b, router_expert_w, router_expert_b, moe_w_gate, moe_w_up, moe_w_down, final_norm_w):
    depth = norm1_w.shape[0]
    assert depth == 1, "the fused final norm assumes a single layer"
    l = 0
    return _layer(x, norm1_w[l], w_in[l], s5_a_re[l], s5_a_im[l], s5_b_re[l], s5_b_im[l],
                  s5_c_re[l], s5_c_im[l], s5_d[l], s5_log_dt[l], s5_w_glu[l], s5_b_glu[l],
                  ret_norm_w[l], w_out[l], norm2_w[l], router_group_w[l], router_group_b[l],
                  router_expert_w[l], router_expert_b[l], moe_w_gate[l], moe_w_up[l],
                  moe_w_down[l], final_norm_w)
```

```python
import functools

import jax
import jax.numpy as jnp
from jax import lax
from jax.experimental import pallas as pl
from jax.experimental.pallas import tpu as pltpu

F32 = jnp.float32
BF16 = jnp.bfloat16

EPS = 1e-6
ROPE_BASE = 10000.0
RET_HEADS = 8
TOP_K = 2
N_GROUPS = 4
EXPERTS_PER_GROUP = 8

V7X_LANES = 128
V7X_SUBLANES = 8
V7X_VMEM_LIMIT = 56 * 1024 * 1024

TM_PROJ = 512
T_S5 = 128
S5_PAD = 8
T_RET = 128
TM_ROUTE = 512
TM_ROWS = 256
R_BLK = 512
DMA_UNROLL = 8
ROUTE_LANES = 128


def _rms(x, w):
    return x * lax.rsqrt(jnp.mean(x * x, axis=-1, keepdims=True) + EPS) * w


def _cparams(sem):
    return pltpu.CompilerParams(dimension_semantics=sem, vmem_limit_bytes=V7X_VMEM_LIMIT)


ROW_TILES = V7X_SUBLANES


def _rows_to_tiles(ref, val):
    rows = val.shape[0]
    for s in range(ROW_TILES):
        ref[pl.ds(s, rows, stride=ROW_TILES), :] = val[:, s * V7X_LANES:(s + 1) * V7X_LANES]


def _tiles_to_rows(ref, rows):
    return jnp.concatenate(
        [ref[pl.ds(s, rows, stride=ROW_TILES), :] for s in range(ROW_TILES)], axis=1)


def _tile_rows(ref, row):
    return ref.at[pl.ds(pl.multiple_of(row * ROW_TILES, ROW_TILES), ROW_TILES)]


def _inproj_body(x_ref, nw_ref, w_ref, u_ref):
    xn = _rms(x_ref[...], nw_ref[...]).astype(BF16)
    u_ref[...] = jnp.dot(xn, w_ref[...], preferred_element_type=F32).astype(u_ref.dtype)


def _inproj(x2, nw, w_bf):
    n, d = x2.shape
    d_in = w_bf.shape[1]
    return pl.pallas_call(
        _inproj_body,
        grid=(n // TM_PROJ,),
        in_specs=[pl.BlockSpec((TM_PROJ, d), lambda i: (i, 0)),
                  pl.BlockSpec((1, d), lambda i: (0, 0)),
                  pl.BlockSpec((d, d_in), lambda i: (0, 0))],
        out_specs=pl.BlockSpec((TM_PROJ, d_in), lambda i: (i, 0)),
        out_shape=jax.ShapeDtypeStruct((n, d_in), BF16),
        compiler_params=_cparams(("arbitrary",)),
        name="inproj",
    )(x2, nw, w_bf)


def _s5_body(u_ref, bm_ref, cm_ref, a_ref, dd_ref, wglu_ref, bglu_ref, y_ref,
             bu_scr, s_scr, st_scr, *, nb, t_len, seg):
    @pl.when(pl.program_id(0) == 0)
    def _():
        st_scr[...] = jnp.zeros_like(st_scr)

    nblk = bm_ref.shape[0]
    cw = bm_ref.shape[1]
    sw = bm_ref.shape[2]
    tiles = sw // V7X_LANES
    ht = tiles // 2
    u2 = u_ref[...].reshape(nb * t_len, nblk * cw)
    for blk in range(nblk):
        half, q = divmod(blk, 2)
        r = jnp.dot(u2[:, blk * cw:(blk + 1) * cw], bm_ref[blk], preferred_element_type=F32)
        for b in range(nb):
            for j in range(tiles):
                bu_scr[q * tiles + j, pl.ds((half * nb + b) * seg, t_len), :] = (
                    r[b * t_len:(b + 1) * t_len, j * V7X_LANES:(j + 1) * V7X_LANES])

    rows = 2 * nb
    ar = [[a_ref[0, q * ht + i] for i in range(ht)] for q in range(2)]
    ai = [[a_ref[1, q * ht + i] for i in range(ht)] for q in range(2)]
    sr = [[st_scr[q * tiles + i] for i in range(ht)] for q in range(2)]
    si = [[st_scr[q * tiles + ht + i] for i in range(ht)] for q in range(2)]
    for t in range(t_len):
        for q in range(2):
            for i in range(ht):
                jr = q * tiles + i
                ji = q * tiles + ht + i
                vr = bu_scr[jr, pl.ds(t, rows, stride=seg), :]
                vi = bu_scr[ji, pl.ds(t, rows, stride=seg), :]
                nr = ar[q][i] * sr[q][i] - ai[q][i] * si[q][i] + vr
                ni = ar[q][i] * si[q][i] + ai[q][i] * sr[q][i] + vi
                s_scr[jr, pl.ds(t, rows, stride=seg), :] = nr
                s_scr[ji, pl.ds(t, rows, stride=seg), :] = ni
                sr[q][i], si[q][i] = nr, ni
    for q in range(2):
        for i in range(ht):
            st_scr[q * tiles + i] = sr[q][i]
            st_scr[q * tiles + ht + i] = si[q][i]

    ys = []
    for blk in range(nblk):
        half, q = divmod(blk, 2)
        sb = jnp.concatenate(
            [jnp.concatenate([s_scr[q * tiles + j, pl.ds((half * nb + b) * seg, t_len), :]
                              for j in range(tiles)], axis=1) for b in range(nb)],
            axis=0).astype(BF16)
        ys.append(jnp.dot(sb, cm_ref[blk], preferred_element_type=F32))
    y = jnp.concatenate(ys, axis=1) + dd_ref[...] * u2.astype(F32)
    y = jax.nn.gelu(y)
    z = jnp.dot(y.astype(BF16), wglu_ref[...], preferred_element_type=F32) + bglu_ref[...]
    out = y * jax.nn.sigmoid(z)
    y_ref[...] = out.reshape(nb, t_len, nblk * cw).astype(y_ref.dtype)


def _s5(u3, bm, cm, a_tab, dd, wglu_bf, bglu):
    nb, seq, _ = u3.shape
    nblk, cw, sw = bm.shape
    d_s5 = nblk * cw
    seg = T_S5 + S5_PAD
    rows = 2 * nb
    assert rows == V7X_SUBLANES and nblk == 4
    body = functools.partial(_s5_body, nb=nb, t_len=T_S5, seg=seg)
    return pl.pallas_call(
        body,
        grid=(seq // T_S5,),
        in_specs=[pl.BlockSpec((nb, T_S5, d_s5), lambda c: (0, c, 0)),
                  pl.BlockSpec(bm.shape, lambda c: (0, 0, 0)),
                  pl.BlockSpec(cm.shape, lambda c: (0, 0, 0)),
                  pl.BlockSpec(a_tab.shape, lambda c: (0, 0, 0, 0)),
                  pl.BlockSpec((1, d_s5), lambda c: (0, 0)),
                  pl.BlockSpec((d_s5, d_s5), lambda c: (0, 0)),
                  pl.BlockSpec((1, d_s5), lambda c: (0, 0))],
        out_specs=pl.BlockSpec((nb, T_S5, d_s5), lambda c: (0, c, 0)),
        out_shape=jax.ShapeDtypeStruct((nb, seq, d_s5), BF16),
        scratch_shapes=[pltpu.VMEM((2 * sw // V7X_LANES, rows * seg, V7X_LANES), F32),
                        pltpu.VMEM((2 * sw // V7X_LANES, rows * seg, V7X_LANES), F32),
                        pltpu.VMEM((2 * sw // V7X_LANES, rows, V7X_LANES), F32)],
        compiler_params=_cparams(("arbitrary",)),
        name="s5",
    )(u3, bm, cm, a_tab, dd, wglu_bf, bglu)


def _s5_tables(a_re, a_im, b_re, b_im, c_re, c_im, log_dt, nb):
    g, p = a_re.shape
    hch = b_re.shape[2]
    gpb = V7X_LANES // hch
    nblk = g // gpb
    lam_r, lam_i = a_re.astype(F32), a_im.astype(F32)
    dt = jnp.exp(log_dt.astype(F32))[:, None]
    mag = jnp.exp(lam_r * dt)
    ab_r = mag * jnp.cos(lam_i * dt)
    ab_i = mag * jnp.sin(lam_i * dt)
    den = lam_r * lam_r + lam_i * lam_i
    zr = ((ab_r - 1.0) * lam_r + ab_i * lam_i) / den
    zi = (ab_i * lam_r - (ab_r - 1.0) * lam_i) / den
    br_, bi_ = b_re.astype(F32), b_im.astype(F32)
    bb_r = zr[..., None] * br_ - zi[..., None] * bi_
    bb_i = zr[..., None] * bi_ + zi[..., None] * br_
    eye = jnp.eye(gpb, dtype=F32)

    def bmat(bb):
        return jnp.einsum('bgph,gk->bghkp', bb.reshape(nblk, gpb, p, hch), eye).reshape(
            nblk, gpb * hch, gpb * p)

    def cmat(cc):
        return jnp.einsum('bghp,gk->bgpkh', cc.reshape(nblk, gpb, hch, p), eye).reshape(
            nblk, gpb * p, gpb * hch)

    bm = jnp.concatenate([bmat(bb_r), bmat(bb_i)], axis=2).astype(BF16)
    cm = jnp.concatenate([cmat(c_re.astype(F32)), -cmat(c_im.astype(F32))], axis=1).astype(BF16)
    a_tab = jnp.stack([jnp.repeat(ab_r.reshape(2, -1), nb, axis=0),
                       jnp.repeat(ab_i.reshape(2, -1), nb, axis=0)])
    a_tab = a_tab.reshape(2, 2 * nb, -1, V7X_LANES).transpose(0, 2, 1, 3)
    return bm, cm, a_tab


def _ret_body(q_ref, k_ref, v_ref, g_ref, cos_ref, sin_ref, dec_ref, qdec_ref, kdect_ref,
              cdec_ref, mk_ref, mv_ref, ms_ref, avg_ref, nw_ref, y_ref, st_scr, *, t_len, dh, scale):
    @pl.when(pl.program_id(0) == 0)
    def _():
        st_scr[...] = jnp.zeros_like(st_scr)

    nb = q_ref.shape[0]
    width = q_ref.shape[-1]
    heads = width // dh
    pairs = width // V7X_LANES
    reps = width // cos_ref.shape[-1]
    cos = jnp.concatenate([cos_ref[...]] * reps, axis=1)
    sin = jnp.concatenate([sin_ref[...]] * reps, axis=1)
    lane = lax.broadcasted_iota(jnp.int32, (t_len, width), 1)
    first = (lane % dh) < (dh // 2)

    def rot(x):
        x = x.astype(F32)
        partner = jnp.where(first, pltpu.roll(x, width - dh // 2, 1), pltpu.roll(x, dh // 2, 1))
        return x * cos + partner * sin

    def group_mean(x):
        hi = x.astype(BF16)
        lo = (x - hi.astype(F32)).astype(BF16)
        return (jnp.dot(hi, avg_ref[...], preferred_element_type=F32)
                + jnp.dot(lo, avg_ref[...], preferred_element_type=F32))

    for b in range(nb):
        q = rot(q_ref[b])
        k = rot(k_ref[b]) * scale
        vb = v_ref[b]
        kt = k.T
        kbd = jnp.concatenate([kt.astype(BF16)] * heads, axis=1) * mk_ref[...]
        sc = jnp.dot(q.astype(BF16), kbd, preferred_element_type=F32) * dec_ref[...]
        vbd = jnp.concatenate([vb] * heads, axis=0) * mv_ref[...]
        inner = jnp.dot(sc.astype(BF16), vbd, preferred_element_type=F32)
        qd = (q * qdec_ref[...]).astype(BF16)
        kdt = (kt * kdect_ref[...]).astype(BF16)
        crosses = []
        for p in range(pairs):
            ps = slice(p * V7X_LANES, (p + 1) * V7X_LANES)
            state = st_scr[b, p]
            crosses.append(jnp.dot(qd[:, ps], state.astype(BF16), preferred_element_type=F32))
            kv = jnp.dot(kdt[ps, :], vb[:, ps], preferred_element_type=F32)
            st_scr[b, p] = state * cdec_ref[:, ps] + kv * ms_ref[...]
        o = inner + jnp.concatenate(crosses, axis=1)
        dlt = o - group_mean(o)
        var = group_mean(dlt * dlt)
        on = dlt * lax.rsqrt(var + EPS) * nw_ref[...]
        y_ref[b] = (jax.nn.silu(g_ref[b].astype(F32)) * on).astype(y_ref.dtype)


def _retention(u3, d_s5, d_ret, norm_w):
    nb, seq, _ = u3.shape
    dh = d_ret // RET_HEADS
    half = dh // 2
    t_len = T_RET
    first_blk = d_s5 // d_ret
    assert d_s5 % d_ret == 0
    lg = jnp.log(1.0 - 2.0 ** (-5.0 - jnp.arange(RET_HEADS, dtype=F32)))
    t = jnp.arange(t_len, dtype=F32)
    diff = t[:, None] - t[None, :]
    dec = jnp.where(diff >= 0, jnp.exp(lg[:, None, None] * jnp.maximum(diff, 0.0)), 0.0)
    dec_all = dec.transpose(1, 0, 2).reshape(t_len, RET_HEADS * t_len)
    qdec = jnp.repeat(jnp.exp(lg[:, None] * (t + 1.0)[None, :]).T, dh, axis=1)
    kdect = jnp.repeat(jnp.exp(lg[:, None] * (t_len - 1 - t)[None, :]), dh, axis=0)
    cdec = jnp.repeat(jnp.exp(lg * t_len), dh)[None, :]
    head_of_lane = jnp.arange(d_ret) // dh
    head_of_col = jnp.arange(RET_HEADS * t_len) // t_len
    mask_k = (head_of_lane[:, None] == head_of_col[None, :]).astype(BF16)
    mask_v = mask_k.T
    pair_head = jnp.arange(V7X_LANES) // dh
    mask_s = (pair_head[:, None] == pair_head[None, :]).astype(F32)
    avg = (head_of_lane[:, None] == head_of_lane[None, :]).astype(F32) / dh
    assert dh & (dh - 1) == 0
    avg = avg.astype(BF16)
    inv = ROPE_BASE ** (-jnp.arange(half, dtype=F32) / half)
    ang = jnp.arange(seq, dtype=F32)[:, None] * inv[None, :]
    reps = V7X_LANES // dh
    cos_t = jnp.tile(jnp.cos(ang), (1, 2 * reps))
    sin_t = jnp.tile(jnp.concatenate([-jnp.sin(ang), jnp.sin(ang)], axis=1), (1, reps))

    def col(j):
        return pl.BlockSpec((nb, t_len, d_ret), lambda c: (0, c, first_blk + j))

    def whole(a):
        return pl.BlockSpec(a.shape, lambda c: (0,) * a.ndim)

    body = functools.partial(_ret_body, t_len=t_len, dh=dh, scale=dh ** -0.5)
    return pl.pallas_call(
        body,
        grid=(seq // t_len,),
        in_specs=[col(0), col(1), col(2), col(3),
                  pl.BlockSpec((t_len, V7X_LANES), lambda c: (c, 0)),
                  pl.BlockSpec((t_len, V7X_LANES), lambda c: (c, 0)),
                  whole(dec_all), whole(qdec), whole(kdect), whole(cdec),
                  whole(mask_k), whole(mask_v), whole(mask_s), whole(avg), whole(norm_w)],
        out_specs=pl.BlockSpec((nb, t_len, d_ret), lambda c: (0, c, 0)),
        out_shape=jax.ShapeDtypeStruct((nb, seq, d_ret), BF16),
        scratch_shapes=[pltpu.VMEM((nb, d_ret // V7X_LANES, V7X_LANES, V7X_LANES), F32)],
        compiler_params=_cparams(("arbitrary",)),
        name="retention",
    )(u3, u3, u3, u3, cos_t, sin_t, dec_all, qdec, kdect, cdec, mask_k, mask_v, mask_s, avg, norm_w)


def _route_body(ys5_ref, yret_ref, x_ref, wo_ref, n2_ref, wr_ref, br_ref, tri_ref,
                h_ref, route_ref, cnt_ref, carry_scr):
    @pl.when(pl.program_id(0) == 0)
    def _():
        carry_scr[...] = jnp.zeros_like(carry_scr)

    d_s5 = ys5_ref.shape[1]
    h = (x_ref[...]
         + jnp.dot(ys5_ref[...], wo_ref[0:d_s5], preferred_element_type=F32)
         + jnp.dot(yret_ref[...], wo_ref[d_s5:], preferred_element_type=F32))
    h_ref[...] = h
    hn = _rms(h, n2_ref[...])
    hi = hn.astype(BF16)
    lo = (hn - hi.astype(F32)).astype(BF16)
    p_hi = jnp.dot(hi, wr_ref[...], preferred_element_type=F32)
    p_lo = jnp.dot(lo, wr_ref[:, 0:ROUTE_LANES], preferred_element_type=F32)
    logits = p_hi[:, 0:ROUTE_LANES] + p_hi[:, ROUTE_LANES:] + p_lo + br_ref[...]
    tm = logits.shape[0]
    lane = lax.broadcasted_iota(jnp.int32, (tm, ROUTE_LANES), 1)
    lanef = lane.astype(F32)
    neg = -jnp.inf
    big = float(ROUTE_LANES)
    gl = jnp.where(lane < N_GROUPS, logits, neg)
    gmax = jnp.max(gl, axis=-1, keepdims=True)
    gidx = jnp.min(jnp.where(gl == gmax, lanef, big), axis=-1, keepdims=True)
    g_w = 1.0 / jnp.sum(jnp.exp(gl - gmax), axis=-1, keepdims=True)
    lo = N_GROUPS + EXPERTS_PER_GROUP * gidx
    el = jnp.where((lanef >= lo) & (lanef < lo + EXPERTS_PER_GROUP), logits, neg)
    v0 = jnp.max(el, axis=-1, keepdims=True)
    i0 = jnp.min(jnp.where(el == v0, lanef, big), axis=-1, keepdims=True)
    el2 = jnp.where(lanef == i0, neg, el)
    v1 = jnp.max(el2, axis=-1, keepdims=True)
    i1 = jnp.min(jnp.where(el2 == v1, lanef, big), axis=-1, keepdims=True)
    e = jnp.exp(v1 - v0)
    den = 1.0 + e
    w0 = (1.0 / den) * g_w
    w1 = (e / den) * g_w
    sel0 = lanef == i0
    sel1 = lanef == i1
    onehot = jnp.where(sel0 | sel1, 1.0, 0.0)
    before = jnp.dot(tri_ref[...], onehot.astype(BF16), preferred_element_type=F32) + carry_scr[...]
    r0 = jnp.sum(jnp.where(sel0, before, 0.0), axis=-1, keepdims=True)
    r1 = jnp.sum(jnp.where(sel1, before, 0.0), axis=-1, keepdims=True)
    carry_scr[...] += jnp.sum(onehot, axis=0, keepdims=True)
    cnt_ref[...] = carry_scr[...]
    rec = jnp.zeros((tm, ROUTE_LANES), F32)
    for j, val in enumerate((i0 - N_GROUPS, i1 - N_GROUPS, w0, w1, r0, r1)):
        rec = jnp.where(lane == j, val, rec)
    route_ref[...] = rec


def _route(ys5, yret, x2, wo_bf, n2, wr, br):
    n, d = x2.shape
    d_s5 = ys5.shape[1]
    d_ret = yret.shape[1]
    tm = TM_ROUTE
    tri = (jnp.arange(tm)[:, None] > jnp.arange(tm)[None, :]).astype(BF16)
    return pl.pallas_call(
        _route_body,
        grid=(n // tm,),
        in_specs=[pl.BlockSpec((tm, d_s5), lambda i: (i, 0)),
                  pl.BlockSpec((tm, d_ret), lambda i: (i, 0)),
                  pl.BlockSpec((tm, d), lambda i: (i, 0)),
                  pl.BlockSpec((d_s5 + d_ret, d), lambda i: (0, 0)),
                  pl.BlockSpec((1, d), lambda i: (0, 0)),
                  pl.BlockSpec((d, 2 * ROUTE_LANES), lambda i: (0, 0)),
                  pl.BlockSpec((1, ROUTE_LANES), lambda i: (0, 0)),
                  pl.BlockSpec((tm, tm), lambda i: (0, 0))],
        out_specs=[pl.BlockSpec((tm, d), lambda i: (i, 0)),
                   pl.BlockSpec((tm, ROUTE_LANES), lambda i: (i, 0)),
                   pl.BlockSpec((1, ROUTE_LANES), lambda i: (0, 0))],
        out_shape=[jax.ShapeDtypeStruct((n, d), F32),
                   jax.ShapeDtypeStruct((n, ROUTE_LANES), F32),
                   jax.ShapeDtypeStruct((1, ROUTE_LANES), F32)],
        scratch_shapes=[pltpu.VMEM((1, ROUTE_LANES), F32)],
        compiler_params=_cparams(("arbitrary",)),
        name="outproj_route",
    )(ys5, yret, x2, wo_bf, n2, wr, br, tri)


def _dispatch_body(pends_ref, padded_ref, dest_ref, h_ref, n2_ref, xs_ref,
                   hn_scr, zero_scr, sem, zsem):
    tm = h_ref.shape[0]

    @pl.when(pl.program_id(0) == 0)
    def _():
        zero_scr[...] = jnp.zeros_like(zero_scr)

        def zero_copy(e):
            first = pl.multiple_of((pends_ref[e] - R_BLK) * ROW_TILES, R_BLK * ROW_TILES)
            return pltpu.make_async_copy(zero_scr, xs_ref.at[pl.ds(first, R_BLK * ROW_TILES)], zsem)

        def zstart(e, carry):
            @pl.when(padded_ref[e] > 0)
            def _():
                zero_copy(e).start()
            return carry

        def zwait(e, carry):
            @pl.when(padded_ref[e] > 0)
            def _():
                zero_copy(e).wait()
            return carry

        lax.fori_loop(0, pends_ref.shape[0], zstart, 0)
        lax.fori_loop(0, pends_ref.shape[0], zwait, 0)

    _rows_to_tiles(hn_scr, _rms(h_ref[...], n2_ref[...]))

    def start(r, carry):
        for k in range(TOP_K):
            pltpu.make_async_copy(_tile_rows(hn_scr, r), _tile_rows(xs_ref, dest_ref[k, r]), sem).start()
        return carry

    lax.fori_loop(0, tm, start, 0, unroll=DMA_UNROLL)
    for k in range(TOP_K):
        pltpu.make_async_copy(hn_scr, xs_ref.at[pl.ds(0, tm * ROW_TILES)], sem).wait()


def _dispatch(pends, padded, dest, h, n2, p_rows):
    n, d = h.shape
    tm = TM_ROWS
    grid_spec = pltpu.PrefetchScalarGridSpec(
        num_scalar_prefetch=2,
        grid=(n // tm,),
        in_specs=[pl.BlockSpec((TOP_K, tm), lambda i, pe, pa: (0, i), memory_space=pltpu.SMEM),
                  pl.BlockSpec((tm, d), lambda i, pe, pa: (i, 0)),
                  pl.BlockSpec((1, d), lambda i, pe, pa: (0, 0))],
        out_specs=pl.BlockSpec(memory_space=pl.ANY),
        scratch_shapes=[pltpu.VMEM((tm * ROW_TILES, V7X_LANES), F32),
                        pltpu.VMEM((R_BLK * ROW_TILES, V7X_LANES), F32),
                        pltpu.SemaphoreType.DMA(()), pltpu.SemaphoreType.DMA(())])
    assert d == ROW_TILES * V7X_LANES
    return pl.pallas_call(
        _dispatch_body,
        grid_spec=grid_spec,
        out_shape=jax.ShapeDtypeStruct((p_rows * ROW_TILES, V7X_LANES), F32),
        compiler_params=_cparams(("arbitrary",)),
        name="dispatch",
    )(pends, padded, dest, h, n2)


def _expert_body(be_ref, nu_ref, xs_ref, wg_ref, wu_ref, wd_ref, ys_ref, wg_s, wu_s, wd_s):
    j = pl.program_id(0)

    @pl.when(j < nu_ref[0])
    def _():
        changed = jnp.logical_or(j == 0, be_ref[j] != be_ref[jnp.maximum(j - 1, 0)])

        @pl.when(changed)
        def _():
            wg_s[...] = wg_ref[...].astype(BF16)
            wu_s[...] = wu_ref[...].astype(BF16)
            wd_s[...] = wd_ref[...].astype(BF16)

        x = _tiles_to_rows(xs_ref, R_BLK).astype(BF16)
        gate = jnp.dot(x, wg_s[...], preferred_element_type=F32)
        up = jnp.dot(x, wu_s[...], preferred_element_type=F32)
        hid = (jax.nn.silu(gate) * up).astype(BF16)
        _rows_to_tiles(ys_ref, jnp.dot(hid, wd_s[...], preferred_element_type=F32))

    @pl.when(j >= nu_ref[0])
    def _():
        ys_ref[...] = jnp.zeros_like(ys_ref)


def _experts(block_e, n_used, xs, w_gate, w_up, w_down):
    d, d_e = w_gate.shape[1:]
    blk_rows = R_BLK * ROW_TILES
    nblk = xs.shape[0] // blk_rows

    def row_map(j, be, nu):
        return (jnp.maximum(jnp.minimum(j, nu[0] - 1), 0), 0)

    def w_map(j, be, nu):
        return (be[j], 0, 0)

    grid_spec = pltpu.PrefetchScalarGridSpec(
        num_scalar_prefetch=2,
        grid=(nblk,),
        in_specs=[pl.BlockSpec((blk_rows, V7X_LANES), row_map),
                  pl.BlockSpec((None, d, d_e), w_map),
                  pl.BlockSpec((None, d, d_e), w_map),
                  pl.BlockSpec((None, d_e, d), w_map)],
        out_specs=pl.BlockSpec((blk_rows, V7X_LANES), lambda j, be, nu: (j, 0)),
        scratch_shapes=[pltpu.VMEM((d, d_e), BF16), pltpu.VMEM((d, d_e), BF16),
                        pltpu.VMEM((d_e, d), BF16)])
    return pl.pallas_call(
        _expert_body,
        grid_spec=grid_spec,
        out_shape=jax.ShapeDtypeStruct(xs.shape, F32),
        compiler_params=_cparams(("arbitrary",)),
        name="experts",
    )(block_e, n_used, xs, w_gate, w_up, w_down)


def _combine_body(dest_ref, h_ref, route_ref, fw_ref, ys_ref, out_ref, g_scr, sem):
    tm = h_ref.shape[0]

    def start(r, carry):
        for k in range(TOP_K):
            pltpu.make_async_copy(_tile_rows(ys_ref, dest_ref[k, r]),
                                  _tile_rows(g_scr.at[k], r), sem).start()
        return carry

    lax.fori_loop(0, tm, start, 0, unroll=DMA_UNROLL)
    for k in range(TOP_K):
        pltpu.make_async_copy(ys_ref.at[pl.ds(0, tm * ROW_TILES)], g_scr.at[k], sem).wait()
    rec = route_ref[...]
    hh = (h_ref[...] + rec[:, 2:3] * _tiles_to_rows(g_scr.at[0], tm)
          + rec[:, 3:4] * _tiles_to_rows(g_scr.at[1], tm))
    out_ref[...] = _rms(hh, fw_ref[...])


def _combine(dest, h, route, fw, ys):
    n, d = h.shape
    tm = TM_ROWS
    return pl.pallas_call(
        _combine_body,
        grid=(n // tm,),
        in_specs=[pl.BlockSpec((TOP_K, tm), lambda i: (0, i), memory_space=pltpu.SMEM),
                  pl.BlockSpec((tm, d), lambda i: (i, 0)),
                  pl.BlockSpec((tm, ROUTE_LANES), lambda i: (i, 0)),
                  pl.BlockSpec((1, d), lambda i: (0, 0)),
                  pl.BlockSpec(memory_space=pl.ANY)],
        out_specs=pl.BlockSpec((tm, d), lambda i: (i, 0)),
        out_shape=jax.ShapeDtypeStruct((n, d), F32),
        scratch_shapes=[pltpu.VMEM((TOP_K, tm * ROW_TILES, V7X_LANES), F32),
                        pltpu.SemaphoreType.DMA(())],
        compiler_params=_cparams(("arbitrary",)),
        name="combine",
    )(dest, h, route, fw, ys)


def _plan(route, counts, n_experts, n_blocks):
    eid = route[:, 0:TOP_K].astype(jnp.int32)
    rank = route[:, 4:4 + TOP_K].astype(jnp.int32)
    cnt = counts[0, N_GROUPS:N_GROUPS + n_experts].astype(jnp.int32)
    padded = (cnt + R_BLK - 1) // R_BLK * R_BLK
    pends = jnp.cumsum(padded)
    pstart = pends - padded
    dest = (pstart[eid] + rank).T
    n_used = pends[-1] // R_BLK
    blk = jnp.minimum(jnp.arange(n_blocks, dtype=jnp.int32), n_used - 1)
    block_e = jnp.minimum(jnp.sum(pends[None, :] <= (blk * R_BLK)[:, None], axis=1), n_experts - 1)
    return (dest.astype(jnp.int32), block_e.astype(jnp.int32), n_used.reshape(1).astype(jnp.int32),
            pends.astype(jnp.int32), padded.astype(jnp.int32))


def _layer(h3, norm1_w, w_in, s5_a_re, s5_a_im, s5_b_re, s5_b_im, s5_c_re, s5_c_im, s5_d,
           s5_log_dt, s5_w_glu, s5_b_glu, ret_norm_w, w_out, norm2_w, router_group_w,
           router_group_b, router_expert_w, router_expert_b, moe_w_gate, moe_w_up, moe_w_down,
           out_norm_w):
    nb, seq, d = h3.shape
    n = nb * seq
    d_s5 = s5_d.shape[0]
    d_ret = ret_norm_w.shape[0]
    n_experts = moe_w_gate.shape[0]
    x2 = h3.reshape(n, d)

    u = _inproj(x2, norm1_w.reshape(1, d), w_in.astype(BF16))
    u3 = u.reshape(nb, seq, -1)

    bm, cm, a_tab = _s5_tables(s5_a_re, s5_a_im, s5_b_re, s5_b_im, s5_c_re, s5_c_im, s5_log_dt, nb)
    y_s5 = _s5(u3, bm, cm, a_tab, s5_d.reshape(1, d_s5).astype(F32), s5_w_glu.astype(BF16),
               s5_b_glu.reshape(1, d_s5).astype(F32))
    y_ret = _retention(u3, d_s5, d_ret, ret_norm_w.reshape(1, d_ret).astype(F32))

    n_route = N_GROUPS + n_experts
    wr = jnp.zeros((d, ROUTE_LANES), F32).at[:, :n_route].set(
        jnp.concatenate([router_group_w, router_expert_w], axis=1).astype(F32))
    br = jnp.zeros((1, ROUTE_LANES), F32).at[0, :n_route].set(
        jnp.concatenate([router_group_b, router_expert_b]).astype(F32))
    wr_hi = wr.astype(BF16)
    wr = jnp.concatenate([wr_hi, (wr - wr_hi.astype(F32)).astype(BF16)], axis=1)
    h, route, counts = _route(y_s5.reshape(n, d_s5), y_ret.reshape(n, d_ret), x2,
                              w_out.astype(BF16), norm2_w.reshape(1, d), wr, br)

    n_blocks = (n * TOP_K) // R_BLK + n_experts
    dest, block_e, n_used, pends, padded = _plan(route, counts, n_experts, n_blocks)
    xs = _dispatch(pends, padded, dest, h, norm2_w.reshape(1, d), n_blocks * R_BLK)
    ys = _experts(block_e, n_used, xs, moe_w_gate, moe_w_up, moe_w_down)
    out = _combine(dest, h, route, out_norm_w.reshape(1, d), ys)
    return out.reshape(nb, seq, d)


def kernel(x, norm1_w, w_in, s5_a_re, s5_a_im, s5_b_re, s5_b_im, s5_c_re, s5_c_im, s5_d, s5_log_dt, s5_w_glu, s5_b_glu, ret_norm_w, w_out, norm2_w, router_group_w, router_group_b, router_expert_w, router_expert_b, moe_w_gate, moe_w_up, moe_w_down, final_norm_w):
    depth = norm1_w.shape[0]
    assert depth == 1, "the fused final norm assumes a single layer"
    l = 0
    return _layer(x, norm1_w[l], w_in[l], s5_a_re[l], s5_a_im[l], s5_b_re[l], s5_b_im[l],
                  s5_c_re[l], s5_c_im[l], s5_d[l], s5_log_dt[l], s5_w_glu[l], s5_b_glu[l],
                  ret_norm_w[l], w_out[l], norm2_w[l], router_group_w[l], router_group_b[l],
                  router_expert_w[l], router_expert_b[l], moe_w_gate[l], moe_w_up[l],
                  moe_w_down[l], final_norm_w)
```

```python
import functools

import jax
import jax.numpy as jnp
from jax import lax
from jax.experimental import pallas as pl
from jax.experimental.pallas import tpu as pltpu

F32 = jnp.float32
BF16 = jnp.bfloat16

EPS = 1e-6
ROPE_BASE = 10000.0
RET_HEADS = 8
TOP_K = 2
N_GROUPS = 4
EXPERTS_PER_GROUP = 8

V7X_LANES = 128
V7X_SUBLANES = 8
V7X_VMEM_LIMIT = 56 * 1024 * 1024

TM_PROJ = 512
T_S5 = 128
S5_PAD = 8
T_RET = 128
TM_ROUTE = 512
TM_ROWS = 256
R_BLK = 512
DMA_UNROLL = 8
ROUTE_LANES = 128


def _rms(x, w):
    return x * lax.rsqrt(jnp.mean(x * x, axis=-1, keepdims=True) + EPS) * w


def _cparams(sem):
    return pltpu.CompilerParams(dimension_semantics=sem, vmem_limit_bytes=V7X_VMEM_LIMIT)


ROW_TILES = V7X_SUBLANES


def _rows_to_tiles(ref, val):
    rows = val.shape[0]
    for s in range(ROW_TILES):
        ref[pl.ds(s, rows, stride=ROW_TILES), :] = val[:, s * V7X_LANES:(s + 1) * V7X_LANES]


def _tiles_to_rows(ref, rows):
    return jnp.concatenate(
        [ref[pl.ds(s, rows, stride=ROW_TILES), :] for s in range(ROW_TILES)], axis=1)


def _tile_rows(ref, row):
    return ref.at[pl.ds(pl.multiple_of(row * ROW_TILES, ROW_TILES), ROW_TILES)]


def _inproj_body(x_ref, nw_ref, w_ref, u_ref):
    xn = _rms(x_ref[...], nw_ref[...]).astype(BF16)
    u_ref[...] = jnp.dot(xn, w_ref[...], preferred_element_type=F32).astype(u_ref.dtype)


def _inproj(x2, nw, w_bf):
    n, d = x2.shape
    d_in = w_bf.shape[1]
    return pl.pallas_call(
        _inproj_body,
        grid=(n // TM_PROJ,),
        in_specs=[pl.BlockSpec((TM_PROJ, d), lambda i: (i, 0)),
                  pl.BlockSpec((1, d), lambda i: (0, 0)),
                  pl.BlockSpec((d, d_in), lambda i: (0, 0))],
        out_specs=pl.BlockSpec((TM_PROJ, d_in), lambda i: (i, 0)),
        out_shape=jax.ShapeDtypeStruct((n, d_in), BF16),
        compiler_params=_cparams(("arbitrary",)),
        name="inproj",
    )(x2, nw, w_bf)


def _s5_body(u_ref, bm_ref, cm_ref, a_ref, dd_ref, wglu_ref, bglu_ref, y_ref,
             bu_scr, s_scr, st_scr, *, nb, t_len, seg):
    @pl.when(pl.program_id(0) == 0)
    def _():
        st_scr[...] = jnp.zeros_like(st_scr)

    nblk = bm_ref.shape[0]
    cw = bm_ref.shape[1]
    sw = bm_ref.shape[2]
    tiles = sw // V7X_LANES
    ht = tiles // 2
    u2 = u_ref[...].reshape(nb * t_len, nblk * cw)
    for blk in range(nblk):
        half, q = divmod(blk, 2)
        r = jnp.dot(u2[:, blk * cw:(blk + 1) * cw], bm_ref[blk], preferred_element_type=F32)
        for b in range(nb):
            for j in range(tiles):
                bu_scr[q * tiles + j, pl.ds((half * nb + b) * seg, t_len), :] = (
                    r[b * t_len:(b + 1) * t_len, j * V7X_LANES:(j + 1) * V7X_LANES])

    rows = 2 * nb
    ar = [[a_ref[0, q * ht + i] for i in range(ht)] for q in range(2)]
    ai = [[a_ref[1, q * ht + i] for i in range(ht)] for q in range(2)]
    sr = [[st_scr[q * tiles + i] for i in range(ht)] for q in range(2)]
    si = [[st_scr[q * tiles + ht + i] for i in range(ht)] for q in range(2)]
    for t in range(t_len):
        for q in range(2):
            for i in range(ht):
                jr = q * tiles + i
                ji = q * tiles + ht + i
                vr = bu_scr[jr, pl.ds(t, rows, stride=seg), :]
                vi = bu_scr[ji, pl.ds(t, rows, stride=seg), :]
                nr = ar[q][i] * sr[q][i] - ai[q][i] * si[q][i] + vr
                ni = ar[q][i] * si[q][i] + ai[q][i] * sr[q][i] + vi
                s_scr[jr, pl.ds(t, rows, stride=seg), :] = nr
                s_scr[ji, pl.ds(t, rows, stride=seg), :] = ni
                sr[q][i], si[q][i] = nr, ni
    for q in range(2):
        for i in range(ht):
            st_scr[q * tiles + i] = sr[q][i]
            st_scr[q * tiles + ht + i] = si[q][i]

    ys = []
    for blk in range(nblk):
        half, q = divmod(blk, 2)
        sb = jnp.concatenate(
            [jnp.concatenate([s_scr[q * tiles + j, pl.ds((half * nb + b) * seg, t_len), :]
                              for j in range(tiles)], axis=1) for b in range(nb)],
            axis=0).astype(BF16)
        ys.append(jnp.dot(sb, cm_ref[blk], preferred_element_type=F32))
    y = jnp.concatenate(ys, axis=1) + dd_ref[...] * u2.astype(F32)
    y = jax.nn.gelu(y)
    z = jnp.dot(y.astype(BF16), wglu_ref[...], preferred_element_type=F32) + bglu_ref[...]
    out = y * jax.nn.sigmoid(z)
    y_ref[...] = out.reshape(nb, t_len, nblk * cw).astype(y_ref.dtype)


def _s5(u3, bm, cm, a_tab, dd, wglu_bf, bglu):
    nb, seq, _ = u3.shape
    nblk, cw, sw = bm.shape
    d_s5 = nblk * cw
    seg = T_S5 + S5_PAD
    rows = 2 * nb
    assert rows == V7X_SUBLANES and nblk == 4
    body = functools.partial(_s5_body, nb=nb, t_len=T_S5, seg=seg)
    return pl.pallas_call(
        body,
        grid=(seq // T_S5,),
        in_specs=[pl.BlockSpec((nb, T_S5, d_s5), lambda c: (0, c, 0)),
                  pl.BlockSpec(bm.shape, lambda c: (0, 0, 0)),
                  pl.BlockSpec(cm.shape, lambda c: (0, 0, 0)),
                  pl.BlockSpec(a_tab.shape, lambda c: (0, 0, 0, 0)),
                  pl.BlockSpec((1, d_s5), lambda c: (0, 0)),
                  pl.BlockSpec((d_s5, d_s5), lambda c: (0, 0)),
                  pl.BlockSpec((1, d_s5), lambda c: (0, 0))],
        out_specs=pl.BlockSpec((nb, T_S5, d_s5), lambda c: (0, c, 0)),
        out_shape=jax.ShapeDtypeStruct((nb, seq, d_s5), BF16),
        scratch_shapes=[pltpu.VMEM((2 * sw // V7X_LANES, rows * seg, V7X_LANES), F32),
                        pltpu.VMEM((2 * sw // V7X_LANES, rows * seg, V7X_LANES), F32),
                        pltpu.VMEM((2 * sw // V7X_LANES, rows, V7X_LANES), F32)],
        compiler_params=_cparams(("arbitrary",)),
        name="s5",
    )(u3, bm, cm, a_tab, dd, wglu_bf, bglu)


def _s5_tables(a_re, a_im, b_re, b_im, c_re, c_im, log_dt, nb):
    g, p = a_re.shape
    hch = b_re.shape[2]
    gpb = V7X_LANES // hch
    nblk = g // gpb
    lam_r, lam_i = a_re.astype(F32), a_im.astype(F32)
    dt = jnp.exp(log_dt.astype(F32))[:, None]
    mag = jnp.exp(lam_r * dt)
    ab_r = mag * jnp.cos(lam_i * dt)
    ab_i = mag * jnp.sin(lam_i * dt)
    den = lam_r * lam_r + lam_i * lam_i
    zr = ((ab_r - 1.0) * lam_r + ab_i * lam_i) / den
    zi = (ab_i * lam_r - (ab_r - 1.0) * lam_i) / den
    br_, bi_ = b_re.astype(F32), b_im.astype(F32)
    bb_r = zr[..., None] * br_ - zi[..., None] * bi_
    bb_i = zr[..., None] * bi_ + zi[..., None] * br_
    eye = jnp.eye(gpb, dtype=F32)

    def bmat(bb):
        return jnp.einsum('bgph,gk->bghkp', bb.reshape(nblk, gpb, p, hch), eye).reshape(
            nblk, gpb * hch, gpb * p)

    def cmat(cc):
        return jnp.einsum('bghp,gk->bgpkh', cc.reshape(nblk, gpb, hch, p), eye).reshape(
            nblk, gpb * p, gpb * hch)

    bm = jnp.concatenate([bmat(bb_r), bmat(bb_i)], axis=2).astype(BF16)
    cm = jnp.concatenate([cmat(c_re.astype(F32)), -cmat(c_im.astype(F32))], axis=1).astype(BF16)
    a_tab = jnp.stack([jnp.repeat(ab_r.reshape(2, -1), nb, axis=0),
                       jnp.repeat(ab_i.reshape(2, -1), nb, axis=0)])
    a_tab = a_tab.reshape(2, 2 * nb, -1, V7X_LANES).transpose(0, 2, 1, 3)
    return bm, cm, a_tab


def _ret_body(q_ref, k_ref, v_ref, g_ref, cos_ref, sin_ref, dec_ref, qdec_ref, kdect_ref,
              cdec_ref, mk_ref, mv_ref, ms_ref, avg_ref, nw_ref, y_ref, st_scr, *, t_len, dh, scale):
    @pl.when(pl.program_id(0) == 0)
    def _():
        st_scr[...] = jnp.zeros_like(st_scr)

    nb = q_ref.shape[0]
    width = q_ref.shape[-1]
    heads = width // dh
    pairs = width // V7X_LANES
    reps = width // cos_ref.shape[-1]
    cos = jnp.concatenate([cos_ref[...]] * reps, axis=1)
    sin = jnp.concatenate([sin_ref[...]] * reps, axis=1)
    lane = lax.broadcasted_iota(jnp.int32, (t_len, width), 1)
    first = (lane % dh) < (dh // 2)

    def rot(x):
        x = x.astype(F32)
        partner = jnp.where(first, pltpu.roll(x, width - dh // 2, 1), pltpu.roll(x, dh // 2, 1))
        return x * cos + partner * sin

    def group_mean(x):
        hi = x.astype(BF16)
        lo = (x - hi.astype(F32)).astype(BF16)
        return (jnp.dot(hi, avg_ref[...], preferred_element_type=F32)
                + jnp.dot(lo, avg_ref[...], preferred_element_type=F32))

    for b in range(nb):
        q = rot(q_ref[b])
        k = rot(k_ref[b]) * scale
        vb = v_ref[b]
        kt = k.T
        kbd = jnp.concatenate([kt.astype(BF16)] * heads, axis=1) * mk_ref[...]
        sc = jnp.dot(q.astype(BF16), kbd, preferred_element_type=F32) * dec_ref[...]
        vbd = jnp.concatenate([vb] * heads, axis=0) * mv_ref[...]
        inner = jnp.dot(sc.astype(BF16), vbd, preferred_element_type=F32)
        qd = (q * qdec_ref[...]).astype(BF16)
        kdt = (kt * kdect_ref[...]).astype(BF16)
        crosses = []
        for p in range(pairs):
            ps = slice(p * V7X_LANES, (p + 1) * V7X_LANES)
            state = st_scr[b, p]
            crosses.append(jnp.dot(qd[:, ps], state.astype(BF16), preferred_element_type=F32))
            kv = jnp.dot(kdt[ps, :], vb[:, ps], preferred_element_type=F32)
            st_scr[b, p] = state * cdec_ref[:, ps] + kv * ms_ref[...]
        o = inner + jnp.concatenate(crosses, axis=1)
        dlt = o - group_mean(o)
        var = group_mean(dlt * dlt)
        on = dlt * lax.rsqrt(var + EPS) * nw_ref[...]
        y_ref[b] = (jax.nn.silu(g_ref[b].astype(F32)) * on).astype(y_ref.dtype)


def _retention(u3, d_s5, d_ret, norm_w):
    nb, seq, _ = u3.shape
    dh = d_ret // RET_HEADS
    half = dh // 2
    t_len = T_RET
    first_blk = d_s5 // d_ret
    assert d_s5 % d_ret == 0
    lg = jnp.log(1.0 - 2.0 ** (-5.0 - jnp.arange(RET_HEADS, dtype=F32)))
    t = jnp.arange(t_len, dtype=F32)
    diff = t[:, None] - t[None, :]
    dec = jnp.where(diff >= 0, jnp.exp(lg[:, None, None] * jnp.maximum(diff, 0.0)), 0.0)
    dec_all = dec.transpose(1, 0, 2).reshape(t_len, RET_HEADS * t_len)
    qdec = jnp.repeat(jnp.exp(lg[:, None] * (t + 1.0)[None, :]).T, dh, axis=1)
    kdect = jnp.repeat(jnp.exp(lg[:, None] * (t_len - 1 - t)[None, :]), dh, axis=0)
    cdec = jnp.repeat(jnp.exp(lg * t_len), dh)[None, :]
    head_of_lane = jnp.arange(d_ret) // dh
    head_of_col = jnp.arange(RET_HEADS * t_len) // t_len
    mask_k = (head_of_lane[:, None] == head_of_col[None, :]).astype(BF16)
    mask_v = mask_k.T
    pair_head = jnp.arange(V7X_LANES) // dh
    mask_s = (pair_head[:, None] == pair_head[None, :]).astype(F32)
    avg = (head_of_lane[:, None] == head_of_lane[None, :]).astype(F32) / dh
    assert dh & (dh - 1) == 0
    avg = avg.astype(BF16)
    inv = ROPE_BASE ** (-jnp.arange(half, dtype=F32) / half)
    ang = jnp.arange(seq, dtype=F32)[:, None] * inv[None, :]
    reps = V7X_LANES // dh
    cos_t = jnp.tile(jnp.cos(ang), (1, 2 * reps))
    sin_t = jnp.tile(jnp.concatenate([-jnp.sin(ang), jnp.sin(ang)], axis=1), (1, reps))

    def col(j):
        return pl.BlockSpec((nb, t_len, d_ret), lambda c: (0, c, first_blk + j))

    def whole(a):
        return pl.BlockSpec(a.shape, lambda c: (0,) * a.ndim)

    body = functools.partial(_ret_body, t_len=t_len, dh=dh, scale=dh ** -0.5)
    return pl.pallas_call(
        body,
        grid=(seq // t_len,),
        in_specs=[col(0), col(1), col(2), col(3),
                  pl.BlockSpec((t_len, V7X_LANES), lambda c: (c, 0)),
                  pl.BlockSpec((t_len, V7X_LANES), lambda c: (c, 0)),
                  whole(dec_all), whole(qdec), whole(kdect), whole(cdec),
                  whole(mask_k), whole(mask_v), whole(mask_s), whole(avg), whole(norm_w)],
        out_specs=pl.BlockSpec((nb, t_len, d_ret), lambda c: (0, c, 0)),
        out_shape=jax.ShapeDtypeStruct((nb, seq, d_ret), BF16),
        scratch_shapes=[pltpu.VMEM((nb, d_ret // V7X_LANES, V7X_LANES, V7X_LANES), F32)],
        compiler_params=_cparams(("arbitrary",)),
        name="retention",
    )(u3, u3, u3, u3, cos_t, sin_t, dec_all, qdec, kdect, cdec, mask_k, mask_v, mask_s, avg, norm_w)


def _route_body(ys5_ref, yret_ref, x_ref, wo_ref, n2_ref, wr_ref, br_ref, tri_ref,
                h_ref, route_ref, cnt_ref, carry_scr):
    @pl.when(pl.program_id(0) == 0)
    def _():
        carry_scr[...] = jnp.zeros_like(carry_scr)

    d_s5 = ys5_ref.shape[1]
    h = (x_ref[...]
         + jnp.dot(ys5_ref[...], wo_ref[0:d_s5], preferred_element_type=F32)
         + jnp.dot(yret_ref[...], wo_ref[d_s5:], preferred_element_type=F32))
    h_ref[...] = h
    hn = _rms(h, n2_ref[...])
    hi = hn.astype(BF16)
    lo = (hn - hi.astype(F32)).astype(BF16)
    p_hi = jnp.dot(hi, wr_ref[...], preferred_element_type=F32)
    p_lo = jnp.dot(lo, wr_ref[:, 0:ROUTE_LANES], preferred_element_type=F32)
    logits = p_hi[:, 0:ROUTE_LANES] + p_hi[:, ROUTE_LANES:] + p_lo + br_ref[...]
    tm = logits.shape[0]
    lane = lax.broadcasted_iota(jnp.int32, (tm, ROUTE_LANES), 1)
    lanef = lane.astype(F32)
    neg = -jnp.inf
    big = float(ROUTE_LANES)
    gl = jnp.where(lane < N_GROUPS, logits, neg)
    gmax = jnp.max(gl, axis=-1, keepdims=True)
    gidx = jnp.min(jnp.where(gl == gmax, lanef, big), axis=-1, keepdims=True)
    g_w = 1.0 / jnp.sum(jnp.exp(gl - gmax), axis=-1, keepdims=True)
    lo = N_GROUPS + EXPERTS_PER_GROUP * gidx
    el = jnp.where((lanef >= lo) & (lanef < lo + EXPERTS_PER_GROUP), logits, neg)
    v0 = jnp.max(el, axis=-1, keepdims=True)
    i0 = jnp.min(jnp.where(el == v0, lanef, big), axis=-1, keepdims=True)
    el2 = jnp.where(lanef == i0, neg, el)
    v1 = jnp.max(el2, axis=-1, keepdims=True)
    i1 = jnp.min(jnp.where(el2 == v1, lanef, big), axis=-1, keepdims=True)
    e = jnp.exp(v1 - v0)
    den = 1.0 + e
    w0 = (1.0 / den) * g_w
    w1 = (e / den) * g_w
    sel0 = lanef == i0
    sel1 = lanef == i1
    onehot = jnp.where(sel0 | sel1, 1.0, 0.0)
    before = jnp.dot(tri_ref[...], onehot.astype(BF16), preferred_element_type=F32) + carry_scr[...]
    r0 = jnp.sum(jnp.where(sel0, before, 0.0), axis=-1, keepdims=True)
    r1 = jnp.sum(jnp.where(sel1, before, 0.0), axis=-1, keepdims=True)
    carry_scr[...] += jnp.sum(onehot, axis=0, keepdims=True)
    cnt_ref[...] = carry_scr[...]
    rec = jnp.zeros((tm, ROUTE_LANES), F32)
    for j, val in enumerate((i0 - N_GROUPS, i1 - N_GROUPS, w0, w1, r0, r1)):
        rec = jnp.where(lane == j, val, rec)
    route_ref[...] = rec


def _route(ys5, yret, x2, wo_bf, n2, wr, br):
    n, d = x2.shape
    d_s5 = ys5.shape[1]
    d_ret = yret.shape[1]
    tm = TM_ROUTE
    tri = (jnp.arange(tm)[:, None] > jnp.arange(tm)[None, :]).astype(BF16)
    return pl.pallas_call(
        _route_body,
        grid=(n // tm,),
        in_specs=[pl.BlockSpec((tm, d_s5), lambda i: (i, 0)),
                  pl.BlockSpec((tm, d_ret), lambda i: (i, 0)),
                  pl.BlockSpec((tm, d), lambda i: (i, 0)),
                  pl.BlockSpec((d_s5 + d_ret, d), lambda i: (0, 0)),
                  pl.BlockSpec((1, d), lambda i: (0, 0)),
                  pl.BlockSpec((d, 2 * ROUTE_LANES), lambda i: (0, 0)),
                  pl.BlockSpec((1, ROUTE_LANES), lambda i: (0, 0)),
                  pl.BlockSpec((tm, tm), lambda i: (0, 0))],
        out_specs=[pl.BlockSpec((tm, d), lambda i: (i, 0)),
                   pl.BlockSpec((tm, ROUTE_LANES), lambda i: (i, 0)),
                   pl.BlockSpec((1, ROUTE_LANES), lambda i: (0, 0))],
        out_shape=[jax.ShapeDtypeStruct((n, d), F32),
                   jax.ShapeDtypeStruct((n, ROUTE_LANES), F32),
                   jax.ShapeDtypeStruct((1, ROUTE_LANES), F32)],
        scratch_shapes=[pltpu.VMEM((1, ROUTE_LANES), F32)],
        compiler_params=_cparams(("arbitrary",)),
        name="outproj_route",
    )(ys5, yret, x2, wo_bf, n2, wr, br, tri)


def _dispatch_body(pends_ref, padded_ref, dest_ref, h_ref, n2_ref, xs_ref,
                   hn_scr, zero_scr, sem, zsem):
    tm = h_ref.shape[0]

    @pl.when(pl.program_id(0) == 0)
    def _():
        zero_scr[...] = jnp.zeros_like(zero_scr)

        def zero_copy(e):
            first = pl.multiple_of((pends_ref[e] - R_BLK) * ROW_TILES, R_BLK * ROW_TILES)
            return pltpu.make_async_copy(zero_scr, xs_ref.at[pl.ds(first, R_BLK * ROW_TILES)], zsem)

        def zstart(e, carry):
            @pl.when(padded_ref[e] > 0)
            def _():
                zero_copy(e).start()
            return carry

        def zwait(e, carry):
            @pl.when(padded_ref[e] > 0)
            def _():
                zero_copy(e).wait()
            return carry

        lax.fori_loop(0, pends_ref.shape[0], zstart, 0)
        lax.fori_loop(0, pends_ref.shape[0], zwait, 0)

    step = pl.program_id(0)
    slot = step % 2
    buf = hn_scr.at[slot]
    _rows_to_tiles(buf, _rms(h_ref[...], n2_ref[...]))

    def start(r, carry):
        for k in range(TOP_K):
            pltpu.make_async_copy(_tile_rows(buf, r), _tile_rows(xs_ref, dest_ref[k, r]),
                                  sem.at[slot]).start(priority=k)
        return carry

    lax.fori_loop(0, tm, start, 0, unroll=DMA_UNROLL)

    def drain(s):
        for k in range(TOP_K):
            pltpu.make_async_copy(hn_scr.at[s], xs_ref.at[pl.ds(0, tm * ROW_TILES)], sem.at[s]).wait()

    @pl.when(step > 0)
    def _():
        drain(1 - slot)

    @pl.when(step == pl.num_programs(0) - 1)
    def _():
        drain(slot)


def _dispatch(pends, padded, dest, h, n2, p_rows):
    n, d = h.shape
    tm = TM_ROWS
    grid_spec = pltpu.PrefetchScalarGridSpec(
        num_scalar_prefetch=2,
        grid=(n // tm,),
        in_specs=[pl.BlockSpec((TOP_K, tm), lambda i, pe, pa: (0, i), memory_space=pltpu.SMEM),
                  pl.BlockSpec((tm, d), lambda i, pe, pa: (i, 0)),
                  pl.BlockSpec((1, d), lambda i, pe, pa: (0, 0))],
        out_specs=pl.BlockSpec(memory_space=pl.ANY),
        scratch_shapes=[pltpu.VMEM((2, tm * ROW_TILES, V7X_LANES), F32),
                        pltpu.VMEM((R_BLK * ROW_TILES, V7X_LANES), F32),
                        pltpu.SemaphoreType.DMA((2,)), pltpu.SemaphoreType.DMA(())])
    assert d == ROW_TILES * V7X_LANES
    return pl.pallas_call(
        _dispatch_body,
        grid_spec=grid_spec,
        out_shape=jax.ShapeDtypeStruct((p_rows * ROW_TILES, V7X_LANES), F32),
        compiler_params=_cparams(("arbitrary",)),
        name="dispatch",
    )(pends, padded, dest, h, n2)


def _expert_body(be_ref, nu_ref, xs_ref, wg_ref, wu_ref, wd_ref, ys_ref, wg_s, wu_s, wd_s):
    j = pl.program_id(0)

    @pl.when(j < nu_ref[0])
    def _():
        changed = jnp.logical_or(j == 0, be_ref[j] != be_ref[jnp.maximum(j - 1, 0)])

        @pl.when(changed)
        def _():
            wg_s[...] = wg_ref[...].astype(BF16)
            wu_s[...] = wu_ref[...].astype(BF16)
            wd_s[...] = wd_ref[...].astype(BF16)

        x = _tiles_to_rows(xs_ref, R_BLK).astype(BF16)
        gate = jnp.dot(x, wg_s[...], preferred_element_type=F32)
        up = jnp.dot(x, wu_s[...], preferred_element_type=F32)
        hid = (jax.nn.silu(gate) * up).astype(BF16)
        _rows_to_tiles(ys_ref, jnp.dot(hid, wd_s[...], preferred_element_type=F32))

    @pl.when(j >= nu_ref[0])
    def _():
        ys_ref[...] = jnp.zeros_like(ys_ref)


def _experts(block_e, n_used, xs, w_gate, w_up, w_down):
    d, d_e = w_gate.shape[1:]
    blk_rows = R_BLK * ROW_TILES
    nblk = xs.shape[0] // blk_rows

    def row_map(j, be, nu):
        return (jnp.maximum(jnp.minimum(j, nu[0] - 1), 0), 0)

    def w_map(j, be, nu):
        return (be[j], 0, 0)

    grid_spec = pltpu.PrefetchScalarGridSpec(
        num_scalar_prefetch=2,
        grid=(nblk,),
        in_specs=[pl.BlockSpec((blk_rows, V7X_LANES), row_map),
                  pl.BlockSpec((None, d, d_e), w_map),
                  pl.BlockSpec((None, d, d_e), w_map),
                  pl.BlockSpec((None, d_e, d), w_map)],
        out_specs=pl.BlockSpec((blk_rows, V7X_LANES), lambda j, be, nu: (j, 0)),
        scratch_shapes=[pltpu.VMEM((d, d_e), BF16), pltpu.VMEM((d, d_e), BF16),
                        pltpu.VMEM((d_e, d), BF16)])
    return pl.pallas_call(
        _expert_body,
        grid_spec=grid_spec,
        out_shape=jax.ShapeDtypeStruct(xs.shape, F32),
        compiler_params=_cparams(("arbitrary",)),
        name="experts",
    )(block_e, n_used, xs, w_gate, w_up, w_down)


def _combine_body(dest_ref, dest_next_ref, h_ref, route_ref, fw_ref, ys_ref, out_ref, g_scr, sem):
    tm = h_ref.shape[0]
    step = pl.program_id(0)
    slot = step % 2

    def issue(d_ref, s):
        def start(r, carry):
            for k in range(TOP_K):
                pltpu.make_async_copy(_tile_rows(ys_ref, d_ref[k, r]),
                                      _tile_rows(g_scr.at[s, k], r), sem.at[s]).start(priority=k)
            return carry

        lax.fori_loop(0, tm, start, 0, unroll=DMA_UNROLL)

    @pl.when(step == 0)
    def _():
        issue(dest_ref, slot)

    @pl.when(step < pl.num_programs(0) - 1)
    def _():
        issue(dest_next_ref, 1 - slot)

    for k in range(TOP_K):
        pltpu.make_async_copy(ys_ref.at[pl.ds(0, tm * ROW_TILES)], g_scr.at[slot, k],
                              sem.at[slot]).wait()
    rec = route_ref[...]
    hh = (h_ref[...] + rec[:, 2:3] * _tiles_to_rows(g_scr.at[slot, 0], tm)
          + rec[:, 3:4] * _tiles_to_rows(g_scr.at[slot, 1], tm))
    out_ref[...] = _rms(hh, fw_ref[...])


def _combine(dest, h, route, fw, ys):
    n, d = h.shape
    tm = TM_ROWS
    steps = n // tm
    return pl.pallas_call(
        _combine_body,
        grid=(steps,),
        in_specs=[pl.BlockSpec((TOP_K, tm), lambda i: (0, i), memory_space=pltpu.SMEM),
                  pl.BlockSpec((TOP_K, tm), lambda i: (0, jnp.minimum(i + 1, steps - 1)),
                               memory_space=pltpu.SMEM),
                  pl.BlockSpec((tm, d), lambda i: (i, 0)),
                  pl.BlockSpec((tm, ROUTE_LANES), lambda i: (i, 0)),
                  pl.BlockSpec((1, d), lambda i: (0, 0)),
                  pl.BlockSpec(memory_space=pl.ANY)],
        out_specs=pl.BlockSpec((tm, d), lambda i: (i, 0)),
        out_shape=jax.ShapeDtypeStruct((n, d), F32),
        scratch_shapes=[pltpu.VMEM((2, TOP_K, tm * ROW_TILES, V7X_LANES), F32),
                        pltpu.SemaphoreType.DMA((2,))],
        compiler_params=_cparams(("arbitrary",)),
        name="combine",
    )(dest, dest, h, route, fw, ys)


def _plan(route, counts, n_experts, n_blocks):
    eid = route[:, 0:TOP_K].astype(jnp.int32)
    rank = route[:, 4:4 + TOP_K].astype(jnp.int32)
    cnt = counts[0, N_GROUPS:N_GROUPS + n_experts].astype(jnp.int32)
    padded = (cnt + R_BLK - 1) // R_BLK * R_BLK
    pends = jnp.cumsum(padded)
    pstart = pends - padded
    dest = (pstart[eid] + rank).T
    n_used = pends[-1] // R_BLK
    blk = jnp.minimum(jnp.arange(n_blocks, dtype=jnp.int32), n_used - 1)
    block_e = jnp.minimum(jnp.sum(pends[None, :] <= (blk * R_BLK)[:, None], axis=1), n_experts - 1)
    return (dest.astype(jnp.int32), block_e.astype(jnp.int32), n_used.reshape(1).astype(jnp.int32),
            pends.astype(jnp.int32), padded.astype(jnp.int32))


def _layer(h3, norm1_w, w_in, s5_a_re, s5_a_im, s5_b_re, s5_b_im, s5_c_re, s5_c_im, s5_d,
           s5_log_dt, s5_w_glu, s5_b_glu, ret_norm_w, w_out, norm2_w, router_group_w,
           router_group_b, router_expert_w, router_expert_b, moe_w_gate, moe_w_up, moe_w_down,
           out_norm_w):
    nb, seq, d = h3.shape
    n = nb * seq
    d_s5 = s5_d.shape[0]
    d_ret = ret_norm_w.shape[0]
    n_experts = moe_w_gate.shape[0]
    x2 = h3.reshape(n, d)

    u = _inproj(x2, norm1_w.reshape(1, d), w_in.astype(BF16))
    u3 = u.reshape(nb, seq, -1)

    bm, cm, a_tab = _s5_tables(s5_a_re, s5_a_im, s5_b_re, s5_b_im, s5_c_re, s5_c_im, s5_log_dt, nb)
    y_s5 = _s5(u3, bm, cm, a_tab, s5_d.reshape(1, d_s5).astype(F32), s5_w_glu.astype(BF16),
               s5_b_glu.reshape(1, d_s5).astype(F32))
    y_ret = _retention(u3, d_s5, d_ret, ret_norm_w.reshape(1, d_ret).astype(F32))

    n_route = N_GROUPS + n_experts
    wr = jnp.zeros((d, ROUTE_LANES), F32).at[:, :n_route].set(
        jnp.concatenate([router_group_w, router_expert_w], axis=1).astype(F32))
    br = jnp.zeros((1, ROUTE_LANES), F32).at[0, :n_route].set(
        jnp.concatenate([router_group_b, router_expert_b]).astype(F32))
    wr_hi = wr.astype(BF16)
    wr = jnp.concatenate([wr_hi, (wr - wr_hi.astype(F32)).astype(BF16)], axis=1)
    h, route, counts = _route(y_s5.reshape(n, d_s5), y_ret.reshape(n, d_ret), x2,
                              w_out.astype(BF16), norm2_w.reshape(1, d), wr, br)

    n_blocks = (n * TOP_K) // R_BLK + n_experts
    dest, block_e, n_used, pends, padded = _plan(route, counts, n_experts, n_blocks)
    xs = _dispatch(pends, padded, dest, h, norm2_w.reshape(1, d), n_blocks * R_BLK)
    ys = _experts(block_e, n_used, xs, moe_w_gate, moe_w_up, moe_w_down)
    out = _combine(dest, h, route, out_norm_w.reshape(1, d), ys)
    return out.reshape(nb, seq, d)


def kernel(x, norm1_w, w_in, s5_a_re, s5_a_im, s5_b_re, s5_b_im, s5_c_re, s5_c_im, s5_d, s5_log_dt, s5_w_glu, s5_b_glu, ret_norm_w, w_out, norm2_w, router_group_w, router_group_b, router_expert_w, router_expert_b, moe_w_gate, moe_w_up, moe_w_down, final_norm_w):
    depth = norm1_w.shape[0]
    assert depth == 1, "the fused final norm assumes a single layer"
    l = 0
    return _layer(x, norm1_w[l], w_in[l], s5_a_re[l], s5_a_im[l], s5_b_re[l], s5_b_im[l],
                  s5_c_re[l], s5_c_im[l], s5_d[l], s5_log_dt[l], s5_w_glu[l], s5_b_glu[l],
                  ret_norm_w[l], w_out[l], norm2_w[l], router_group_w[l], router_group_b[l],
                  router_expert_w[l], router_expert_b[l], moe_w_gate[l], moe_w_up[l],
                  moe_w_down[l], final_norm_w)
```

```python
import functools

import jax
import jax.numpy as jnp
from jax import lax
from jax.experimental import pallas as pl
from jax.experimental.pallas import tpu as pltpu

F32 = jnp.float32
BF16 = jnp.bfloat16

EPS = 1e-6
ROPE_BASE = 10000.0
RET_HEADS = 8
TOP_K = 2
N_GROUPS = 4
EXPERTS_PER_GROUP = 8

V7X_LANES = 128
V7X_SUBLANES = 8
V7X_VMEM_LIMIT = 56 * 1024 * 1024

TM_PROJ = 512
T_S5 = 512
S5_TAU = 8
S5_PAD = 8
T_RET = 128
TM_ROUTE = 512
TM_ROWS = 256
R_BLK = 512
DMA_UNROLL = 8
ROUTE_LANES = 128


def _rms(x, w):
    return x * lax.rsqrt(jnp.mean(x * x, axis=-1, keepdims=True) + EPS) * w


def _cparams(sem):
    return pltpu.CompilerParams(dimension_semantics=sem, vmem_limit_bytes=V7X_VMEM_LIMIT)


ROW_TILES = V7X_SUBLANES


def _rows_to_tiles(ref, val):
    rows = val.shape[0]
    for s in range(ROW_TILES):
        ref[pl.ds(s, rows, stride=ROW_TILES), :] = val[:, s * V7X_LANES:(s + 1) * V7X_LANES]


def _tiles_to_rows(ref, rows):
    return jnp.concatenate(
        [ref[pl.ds(s, rows, stride=ROW_TILES), :] for s in range(ROW_TILES)], axis=1)


def _tile_rows(ref, row):
    return ref.at[pl.ds(pl.multiple_of(row * ROW_TILES, ROW_TILES), ROW_TILES)]


def _inproj_body(x_ref, nw_ref, w_ref, us5_ref, uret_ref):
    xn = _rms(x_ref[...], nw_ref[...]).astype(BF16)
    u = jnp.dot(xn, w_ref[...], preferred_element_type=F32)
    d_s5 = us5_ref.shape[1]
    us5_ref[...] = u[:, :d_s5].astype(us5_ref.dtype)
    uret_ref[...] = u[:, d_s5:].astype(uret_ref.dtype)


def _inproj(x2, nw, w_bf, d_s5):
    n, d = x2.shape
    d_in = w_bf.shape[1]
    return pl.pallas_call(
        _inproj_body,
        grid=(n // TM_PROJ,),
        in_specs=[pl.BlockSpec((TM_PROJ, d), lambda i: (i, 0)),
                  pl.BlockSpec((1, d), lambda i: (0, 0)),
                  pl.BlockSpec((d, d_in), lambda i: (0, 0))],
        out_specs=[pl.BlockSpec((TM_PROJ, d_s5), lambda i: (i, 0)),
                   pl.BlockSpec((TM_PROJ, d_in - d_s5), lambda i: (i, 0))],
        out_shape=[jax.ShapeDtypeStruct((n, d_s5), BF16),
                   jax.ShapeDtypeStruct((n, d_in - d_s5), BF16)],
        compiler_params=_cparams(("arbitrary",)),
        name="inproj",
    )(x2, nw, w_bf)


def _s5_body(u_ref, bp_ref, cp_ref, tp_ref, a_ref, dd_ref, wglu_ref, bglu_ref, y_ref,
             v_scr, sp_scr, st_scr, *, nb, nk, seg, tau):
    @pl.when(pl.program_id(0) == 0)
    def _():
        st_scr[...] = jnp.zeros_like(st_scr)

    nblk = bp_ref.shape[0]
    cw = bp_ref.shape[1] // tau
    sw = bp_ref.shape[2]
    d_s5 = nblk * cw
    tiles = sw // V7X_LANES
    ht = tiles // 2
    u2 = u_ref[...].reshape(nb * nk, tau * d_s5)

    def block_inputs(blk):
        return jnp.concatenate(
            [u2[:, j * d_s5 + blk * cw:j * d_s5 + (blk + 1) * cw] for j in range(tau)], axis=1)

    for blk in range(nblk):
        half, q = divmod(blk, 2)
        v = jnp.dot(block_inputs(blk), bp_ref[blk], preferred_element_type=F32)
        for b in range(nb):
            for j in range(tiles):
                v_scr[q * tiles + j, pl.ds((half * nb + b) * seg, nk), :] = (
                    v[b * nk:(b + 1) * nk, j * V7X_LANES:(j + 1) * V7X_LANES])

    rows = 2 * nb
    ar = [[a_ref[0, q * ht + i] for i in range(ht)] for q in range(2)]
    ai = [[a_ref[1, q * ht + i] for i in range(ht)] for q in range(2)]
    sr = [[st_scr[q * tiles + i] for i in range(ht)] for q in range(2)]
    si = [[st_scr[q * tiles + ht + i] for i in range(ht)] for q in range(2)]
    for k in range(nk):
        for q in range(2):
            for i in range(ht):
                jr = q * tiles + i
                ji = q * tiles + ht + i
                sp_scr[jr, pl.ds(k, rows, stride=seg), :] = sr[q][i]
                sp_scr[ji, pl.ds(k, rows, stride=seg), :] = si[q][i]
                vr = v_scr[jr, pl.ds(k, rows, stride=seg), :]
                vi = v_scr[ji, pl.ds(k, rows, stride=seg), :]
                nr = ar[q][i] * sr[q][i] - ai[q][i] * si[q][i] + vr
                ni = ar[q][i] * si[q][i] + ai[q][i] * sr[q][i] + vi
                sr[q][i], si[q][i] = nr, ni
    for q in range(2):
        for i in range(ht):
            st_scr[q * tiles + i] = sr[q][i]
            st_scr[q * tiles + ht + i] = si[q][i]

    yb = []
    for blk in range(nblk):
        half, q = divmod(blk, 2)
        sp = jnp.concatenate(
            [jnp.concatenate([sp_scr[q * tiles + j, pl.ds((half * nb + b) * seg, nk), :]
                              for j in range(tiles)], axis=1) for b in range(nb)],
            axis=0).astype(BF16)
        yb.append(jnp.dot(sp, cp_ref[blk], preferred_element_type=F32)
                  + jnp.dot(block_inputs(blk), tp_ref[blk], preferred_element_type=F32))
    for j in range(tau):
        y = jnp.concatenate([yb[blk][:, j * cw:(j + 1) * cw] for blk in range(nblk)], axis=1)
        y = y + dd_ref[...] * u2[:, j * d_s5:(j + 1) * d_s5].astype(F32)
        y = jax.nn.gelu(y)
        z = jnp.dot(y.astype(BF16), wglu_ref[...], preferred_element_type=F32) + bglu_ref[...]
        out = y * jax.nn.sigmoid(z)
        y_ref[:, :, j * d_s5:(j + 1) * d_s5] = out.reshape(nb, nk, d_s5).astype(y_ref.dtype)


def _s5(u_s5, nb, bp, cp, tp, a_tab, dd, wglu_bf, bglu):
    n, d_s5 = u_s5.shape
    seq = n // nb
    tau = S5_TAU
    nk = T_S5 // tau
    seg = nk + S5_PAD
    nblk, _, sw = bp.shape
    rows = 2 * nb
    assert rows == V7X_SUBLANES and nblk == 4
    folded = (nb, seq // tau, tau * d_s5)

    def whole(a):
        return pl.BlockSpec(a.shape, lambda c: (0,) * a.ndim)

    body = functools.partial(_s5_body, nb=nb, nk=nk, seg=seg, tau=tau)
    out = pl.pallas_call(
        body,
        grid=(seq // T_S5,),
        in_specs=[pl.BlockSpec((nb, nk, tau * d_s5), lambda c: (0, c, 0)),
                  whole(bp), whole(cp), whole(tp), whole(a_tab), whole(dd), whole(wglu_bf),
                  whole(bglu)],
        out_specs=pl.BlockSpec((nb, nk, tau * d_s5), lambda c: (0, c, 0)),
        out_shape=jax.ShapeDtypeStruct(folded, BF16),
        scratch_shapes=[pltpu.VMEM((2 * sw // V7X_LANES, rows * seg, V7X_LANES), F32),
                        pltpu.VMEM((2 * sw // V7X_LANES, rows * seg, V7X_LANES), F32),
                        pltpu.VMEM((2 * sw // V7X_LANES, rows, V7X_LANES), F32)],
        compiler_params=_cparams(("arbitrary",)),
        name="s5",
    )(u_s5.reshape(folded), bp, cp, tp, a_tab, dd, wglu_bf, bglu)
    return out.reshape(n, d_s5)


def _s5_tables(a_re, a_im, b_re, b_im, c_re, c_im, log_dt, nb, tau):
    hp = lax.Precision.HIGHEST
    g, p = a_re.shape
    hch = b_re.shape[2]
    gpb = V7X_LANES // hch
    nblk = g // gpb
    lam_r, lam_i = a_re.astype(F32), a_im.astype(F32)
    dt = jnp.exp(log_dt.astype(F32))[:, None]
    mag = jnp.exp(lam_r * dt)
    ab_r = mag * jnp.cos(lam_i * dt)
    ab_i = mag * jnp.sin(lam_i * dt)
    den = lam_r * lam_r + lam_i * lam_i
    zr = ((ab_r - 1.0) * lam_r + ab_i * lam_i) / den
    zi = (ab_i * lam_r - (ab_r - 1.0) * lam_i) / den
    br_, bi_ = b_re.astype(F32), b_im.astype(F32)
    bb_r = zr[..., None] * br_ - zi[..., None] * bi_
    bb_i = zr[..., None] * bi_ + zi[..., None] * br_
    cr, ci = c_re.astype(F32), c_im.astype(F32)
    eye = jnp.eye(gpb, dtype=F32)
    pr, pi = [jnp.ones_like(ab_r)], [jnp.zeros_like(ab_i)]
    for _ in range(tau):
        pr, pi = pr + [pr[-1] * ab_r - pi[-1] * ab_i], pi + [pr[-1] * ab_i + pi[-1] * ab_r]
    pw_r, pw_i = jnp.stack(pr), jnp.stack(pi)

    wr_ = jnp.stack([pr[tau - 1 - j] for j in range(tau)])[..., None]
    wi_ = jnp.stack([pi[tau - 1 - j] for j in range(tau)])[..., None]
    bp_r = wr_ * bb_r[None] - wi_ * bb_i[None]
    bp_i = wr_ * bb_i[None] + wi_ * bb_r[None]

    def bmat(x):
        return jnp.einsum('jbgph,gk->bjghkp', x.reshape(tau, nblk, gpb, p, hch), eye,
                          precision=hp).reshape(nblk, tau * gpb * hch, gpb * p)

    bp = jnp.concatenate([bmat(bp_r), bmat(bp_i)], axis=2).astype(BF16)

    qr_, qi_ = pw_r[1:, :, None, :], pw_i[1:, :, None, :]
    cp_r = cr[None] * qr_ - ci[None] * qi_
    cp_i = cr[None] * qi_ + ci[None] * qr_

    def cmat(x):
        return jnp.einsum('ibghp,gk->bgpikh', x.reshape(tau, nblk, gpb, hch, p), eye,
                          precision=hp).reshape(nblk, gpb * p, tau * gpb * hch)

    cp = jnp.concatenate([cmat(cp_r), -cmat(cp_i)], axis=1).astype(BF16)

    ab_r_ = pw_r[:tau, :, :, None] * bb_r[None] - pw_i[:tau, :, :, None] * bb_i[None]
    ab_i_ = pw_r[:tau, :, :, None] * bb_i[None] + pw_i[:tau, :, :, None] * bb_r[None]
    kd = (jnp.einsum('ghp,dgpe->dghe', cr, ab_r_, precision=hp)
          - jnp.einsum('ghp,dgpe->dghe', ci, ab_i_, precision=hp))
    kzero = jnp.zeros_like(kd[0])
    kfull = jnp.stack([jnp.stack([kd[i - j] if i >= j else kzero for i in range(tau)])
                       for j in range(tau)])
    tp = jnp.einsum('jibghe,gk->bjgeikh', kfull.reshape(tau, tau, nblk, gpb, hch, hch), eye,
                    precision=hp).reshape(nblk, tau * gpb * hch, tau * gpb * hch).astype(BF16)

    a_tab = jnp.stack([jnp.repeat(pw_r[tau].reshape(2, -1), nb, axis=0),
                       jnp.repeat(pw_i[tau].reshape(2, -1), nb, axis=0)])
    a_tab = a_tab.reshape(2, 2 * nb, -1, V7X_LANES).transpose(0, 2, 1, 3)
    return bp, cp, tp, a_tab


def _ret_body(q_ref, k_ref, v_ref, g_ref, cos_ref, sin_ref, dec_ref, qdec_ref, kdect_ref,
              cdec_ref, mk_ref, mv_ref, ms_ref, avg_ref, nw_ref, y_ref, st_scr, *, t_len, dh, scale):
    @pl.when(pl.program_id(0) == 0)
    def _():
        st_scr[...] = jnp.zeros_like(st_scr)

    nb = q_ref.shape[0]
    width = q_ref.shape[-1]
    heads = width // dh
    pairs = width // V7X_LANES
    reps = width // cos_ref.shape[-1]
    cos = jnp.concatenate([cos_ref[...]] * reps, axis=1)
    sin = jnp.concatenate([sin_ref[...]] * reps, axis=1)
    lane = lax.broadcasted_iota(jnp.int32, (t_len, width), 1)
    first = (lane % dh) < (dh // 2)

    def rot(x):
        x = x.astype(F32)
        partner = jnp.where(first, pltpu.roll(x, width - dh // 2, 1), pltpu.roll(x, dh // 2, 1))
        return x * cos + partner * sin

    def group_mean(x):
        hi = x.astype(BF16)
        lo = (x - hi.astype(F32)).astype(BF16)
        return (jnp.dot(hi, avg_ref[...], preferred_element_type=F32)
                + jnp.dot(lo, avg_ref[...], preferred_element_type=F32))

    for b in range(nb):
        q = rot(q_ref[b])
        k = rot(k_ref[b]) * scale
        vb = v_ref[b]
        kt = k.T
        kbd = jnp.concatenate([kt.astype(BF16)] * heads, axis=1) * mk_ref[...]
        sc = jnp.dot(q.astype(BF16), kbd, preferred_element_type=F32) * dec_ref[...]
        vbd = jnp.concatenate([vb] * heads, axis=0) * mv_ref[...]
        inner = jnp.dot(sc.astype(BF16), vbd, preferred_element_type=F32)
        qd = (q * qdec_ref[...]).astype(BF16)
        kdt = (kt * kdect_ref[...]).astype(BF16)
        crosses = []
        for p in range(pairs):
            ps = slice(p * V7X_LANES, (p + 1) * V7X_LANES)
            state = st_scr[b, p]
            crosses.append(jnp.dot(qd[:, ps], state.astype(BF16), preferred_element_type=F32))
            kv = jnp.dot(kdt[ps, :], vb[:, ps], preferred_element_type=F32)
            st_scr[b, p] = state * cdec_ref[:, ps] + kv * ms_ref[...]
        o = inner + jnp.concatenate(crosses, axis=1)
        dlt = o - group_mean(o)
        var = group_mean(dlt * dlt)
        on = dlt * lax.rsqrt(var + EPS) * nw_ref[...]
        y_ref[b] = (jax.nn.silu(g_ref[b].astype(F32)) * on).astype(y_ref.dtype)


def _retention(u3, d_ret, norm_w):
    nb, seq, _ = u3.shape
    dh = d_ret // RET_HEADS
    half = dh // 2
    t_len = T_RET
    lg = jnp.log(1.0 - 2.0 ** (-5.0 - jnp.arange(RET_HEADS, dtype=F32)))
    t = jnp.arange(t_len, dtype=F32)
    diff = t[:, None] - t[None, :]
    dec = jnp.where(diff >= 0, jnp.exp(lg[:, None, None] * jnp.maximum(diff, 0.0)), 0.0)
    dec_all = dec.transpose(1, 0, 2).reshape(t_len, RET_HEADS * t_len)
    qdec = jnp.repeat(jnp.exp(lg[:, None] * (t + 1.0)[None, :]).T, dh, axis=1)
    kdect = jnp.repeat(jnp.exp(lg[:, None] * (t_len - 1 - t)[None, :]), dh, axis=0)
    cdec = jnp.repeat(jnp.exp(lg * t_len), dh)[None, :]
    head_of_lane = jnp.arange(d_ret) // dh
    head_of_col = jnp.arange(RET_HEADS * t_len) // t_len
    mask_k = (head_of_lane[:, None] == head_of_col[None, :]).astype(BF16)
    mask_v = mask_k.T
    pair_head = jnp.arange(V7X_LANES) // dh
    mask_s = (pair_head[:, None] == pair_head[None, :]).astype(F32)
    avg = (head_of_lane[:, None] == head_of_lane[None, :]).astype(F32) / dh
    assert dh & (dh - 1) == 0
    avg = avg.astype(BF16)
    inv = ROPE_BASE ** (-jnp.arange(half, dtype=F32) / half)
    ang = jnp.arange(seq, dtype=F32)[:, None] * inv[None, :]
    reps = V7X_LANES // dh
    cos_t = jnp.tile(jnp.cos(ang), (1, 2 * reps))
    sin_t = jnp.tile(jnp.concatenate([-jnp.sin(ang), jnp.sin(ang)], axis=1), (1, reps))

    def col(j):
        return pl.BlockSpec((nb, t_len, d_ret), lambda c: (0, c, j))

    def whole(a):
        return pl.BlockSpec(a.shape, lambda c: (0,) * a.ndim)

    body = functools.partial(_ret_body, t_len=t_len, dh=dh, scale=dh ** -0.5)
    return pl.pallas_call(
        body,
        grid=(seq // t_len,),
        in_specs=[col(0), col(1), col(2), col(3),
                  pl.BlockSpec((t_len, V7X_LANES), lambda c: (c, 0)),
                  pl.BlockSpec((t_len, V7X_LANES), lambda c: (c, 0)),
                  whole(dec_all), whole(qdec), whole(kdect), whole(cdec),
                  whole(mask_k), whole(mask_v), whole(mask_s), whole(avg), whole(norm_w)],
        out_specs=pl.BlockSpec((nb, t_len, d_ret), lambda c: (0, c, 0)),
        out_shape=jax.ShapeDtypeStruct((nb, seq, d_ret), BF16),
        scratch_shapes=[pltpu.VMEM((nb, d_ret // V7X_LANES, V7X_LANES, V7X_LANES), F32)],
        compiler_params=_cparams(("arbitrary",)),
        name="retention",
    )(u3, u3, u3, u3, cos_t, sin_t, dec_all, qdec, kdect, cdec, mask_k, mask_v, mask_s, avg, norm_w)


def _route_body(ys5_ref, yret_ref, x_ref, wo_ref, n2_ref, wr_ref, br_ref, tri_ref,
                h_ref, route_ref, cnt_ref, carry_scr):
    @pl.when(pl.program_id(0) == 0)
    def _():
        carry_scr[...] = jnp.zeros_like(carry_scr)

    d_s5 = ys5_ref.shape[1]
    h = (x_ref[...]
         + jnp.dot(ys5_ref[...], wo_ref[0:d_s5], preferred_element_type=F32)
         + jnp.dot(yret_ref[...], wo_ref[d_s5:], preferred_element_type=F32))
    h_ref[...] = h
    hn = _rms(h, n2_ref[...])
    hi = hn.astype(BF16)
    lo = (hn - hi.astype(F32)).astype(BF16)
    p_hi = jnp.dot(hi, wr_ref[...], preferred_element_type=F32)
    p_lo = jnp.dot(lo, wr_ref[:, 0:ROUTE_LANES], preferred_element_type=F32)
    logits = p_hi[:, 0:ROUTE_LANES] + p_hi[:, ROUTE_LANES:] + p_lo + br_ref[...]
    tm = logits.shape[0]
    lane = lax.broadcasted_iota(jnp.int32, (tm, ROUTE_LANES), 1)
    lanef = lane.astype(F32)
    neg = -jnp.inf
    big = float(ROUTE_LANES)
    gl = jnp.where(lane < N_GROUPS, logits, neg)
    gmax = jnp.max(gl, axis=-1, keepdims=True)
    gidx = jnp.min(jnp.where(gl == gmax, lanef, big), axis=-1, keepdims=True)
    g_w = 1.0 / jnp.sum(jnp.exp(gl - gmax), axis=-1, keepdims=True)
    lo = N_GROUPS + EXPERTS_PER_GROUP * gidx
    el = jnp.where((lanef >= lo) & (lanef < lo + EXPERTS_PER_GROUP), logits, neg)
    v0 = jnp.max(el, axis=-1, keepdims=True)
    i0 = jnp.min(jnp.where(el == v0, lanef, big), axis=-1, keepdims=True)
    el2 = jnp.where(lanef == i0, neg, el)
    v1 = jnp.max(el2, axis=-1, keepdims=True)
    i1 = jnp.min(jnp.where(el2 == v1, lanef, big), axis=-1, keepdims=True)
    e = jnp.exp(v1 - v0)
    den = 1.0 + e
    w0 = (1.0 / den) * g_w
    w1 = (e / den) * g_w
    sel0 = lanef == i0
    sel1 = lanef == i1
    onehot = jnp.where(sel0 | sel1, 1.0, 0.0)
    before = jnp.dot(tri_ref[...], onehot.astype(BF16), preferred_element_type=F32) + carry_scr[...]
    r0 = jnp.sum(jnp.where(sel0, before, 0.0), axis=-1, keepdims=True)
    r1 = jnp.sum(jnp.where(sel1, before, 0.0), axis=-1, keepdims=True)
    carry_scr[...] += jnp.sum(onehot, axis=0, keepdims=True)
    cnt_ref[...] = carry_scr[...]
    rec = jnp.zeros((tm, ROUTE_LANES), F32)
    for j, val in enumerate((i0 - N_GROUPS, i1 - N_GROUPS, w0, w1, r0, r1)):
        rec = jnp.where(lane == j, val, rec)
    route_ref[...] = rec


def _route(ys5, yret, x2, wo_bf, n2, wr, br):
    n, d = x2.shape
    d_s5 = ys5.shape[1]
    d_ret = yret.shape[1]
    tm = TM_ROUTE
    tri = (jnp.arange(tm)[:, None] > jnp.arange(tm)[None, :]).astype(BF16)
    return pl.pallas_call(
        _route_body,
        grid=(n // tm,),
        in_specs=[pl.BlockSpec((tm, d_s5), lambda i: (i, 0)),
                  pl.BlockSpec((tm, d_ret), lambda i: (i, 0)),
                  pl.BlockSpec((tm, d), lambda i: (i, 0)),
                  pl.BlockSpec((d_s5 + d_ret, d), lambda i: (0, 0)),
                  pl.BlockSpec((1, d), lambda i: (0, 0)),
                  pl.BlockSpec((d, 2 * ROUTE_LANES), lambda i: (0, 0)),
                  pl.BlockSpec((1, ROUTE_LANES), lambda i: (0, 0)),
                  pl.BlockSpec((tm, tm), lambda i: (0, 0))],
        out_specs=[pl.BlockSpec((tm, d), lambda i: (i, 0)),
                   pl.BlockSpec((tm, ROUTE_LANES), lambda i: (i, 0)),
                   pl.BlockSpec((1, ROUTE_LANES), lambda i: (0, 0))],
        out_shape=[jax.ShapeDtypeStruct((n, d), F32),
                   jax.ShapeDtypeStruct((n, ROUTE_LANES), F32),
                   jax.ShapeDtypeStruct((1, ROUTE_LANES), F32)],
        scratch_shapes=[pltpu.VMEM((1, ROUTE_LANES), F32)],
        compiler_params=_cparams(("arbitrary",)),
        name="outproj_route",
    )(ys5, yret, x2, wo_bf, n2, wr, br, tri)


def _dispatch_body(pends_ref, padded_ref, dest_ref, h_ref, n2_ref, xs_ref,
                   hn_scr, zero_scr, sem, zsem):
    tm = h_ref.shape[0]

    @pl.when(pl.program_id(0) == 0)
    def _():
        zero_scr[...] = jnp.zeros_like(zero_scr)

        def zero_copy(e):
            first = pl.multiple_of((pends_ref[e] - R_BLK) * ROW_TILES, R_BLK * ROW_TILES)
            return pltpu.make_async_copy(zero_scr, xs_ref.at[pl.ds(first, R_BLK * ROW_TILES)], zsem)

        def zstart(e, carry):
            @pl.when(padded_ref[e] > 0)
            def _():
                zero_copy(e).start()
            return carry

        def zwait(e, carry):
            @pl.when(padded_ref[e] > 0)
            def _():
                zero_copy(e).wait()
            return carry

        lax.fori_loop(0, pends_ref.shape[0], zstart, 0)
        lax.fori_loop(0, pends_ref.shape[0], zwait, 0)

    step = pl.program_id(0)
    slot = step % 2
    buf = hn_scr.at[slot]
    _rows_to_tiles(buf, _rms(h_ref[...], n2_ref[...]))

    def start(r, carry):
        for k in range(TOP_K):
            pltpu.make_async_copy(_tile_rows(buf, r), _tile_rows(xs_ref, dest_ref[k * tm + r]),
                                  sem.at[slot]).start(priority=k)
        return carry

    lax.fori_loop(0, tm, start, 0, unroll=DMA_UNROLL)

    def drain(s):
        for k in range(TOP_K):
            pltpu.make_async_copy(hn_scr.at[s], xs_ref.at[pl.ds(0, tm * ROW_TILES)], sem.at[s]).wait()

    @pl.when(step > 0)
    def _():
        drain(1 - slot)

    @pl.when(step == pl.num_programs(0) - 1)
    def _():
        drain(slot)


def _dispatch(pends, padded, dest, h, n2, p_rows):
    n, d = h.shape
    tm = TM_ROWS
    grid_spec = pltpu.PrefetchScalarGridSpec(
        num_scalar_prefetch=2,
        grid=(n // tm,),
        in_specs=[pl.BlockSpec((TOP_K * tm,), lambda i, pe, pa: (i,), memory_space=pltpu.SMEM),
                  pl.BlockSpec((tm, d), lambda i, pe, pa: (i, 0)),
                  pl.BlockSpec((1, d), lambda i, pe, pa: (0, 0))],
        out_specs=pl.BlockSpec(memory_space=pl.ANY),
        scratch_shapes=[pltpu.VMEM((2, tm * ROW_TILES, V7X_LANES), F32),
                        pltpu.VMEM((R_BLK * ROW_TILES, V7X_LANES), F32),
                        pltpu.SemaphoreType.DMA((2,)), pltpu.SemaphoreType.DMA(())])
    assert d == ROW_TILES * V7X_LANES
    return pl.pallas_call(
        _dispatch_body,
        grid_spec=grid_spec,
        out_shape=jax.ShapeDtypeStruct((p_rows * ROW_TILES, V7X_LANES), F32),
        compiler_params=_cparams(("arbitrary",)),
        name="dispatch",
    )(pends, padded, dest, h, n2)


def _expert_body(be_ref, nu_ref, xs_ref, wg_ref, wu_ref, wd_ref, ys_ref, wg_s, wu_s, wd_s):
    j = pl.program_id(0)

    @pl.when(j < nu_ref[0])
    def _():
        changed = jnp.logical_or(j == 0, be_ref[j] != be_ref[jnp.maximum(j - 1, 0)])

        @pl.when(changed)
        def _():
            wg_s[...] = wg_ref[...].astype(BF16)
            wu_s[...] = wu_ref[...].astype(BF16)
            wd_s[...] = wd_ref[...].astype(BF16)

        x = _tiles_to_rows(xs_ref, R_BLK).astype(BF16)
        gate = jnp.dot(x, wg_s[...], preferred_element_type=F32)
        up = jnp.dot(x, wu_s[...], preferred_element_type=F32)
        hid = (jax.nn.silu(gate) * up).astype(BF16)
        _rows_to_tiles(ys_ref, jnp.dot(hid, wd_s[...], preferred_element_type=F32))

    @pl.when(j >= nu_ref[0])
    def _():
        ys_ref[...] = jnp.zeros_like(ys_ref)


def _experts(block_e, n_used, xs, w_gate, w_up, w_down):
    d, d_e = w_gate.shape[1:]
    blk_rows = R_BLK * ROW_TILES
    nblk = xs.shape[0] // blk_rows

    def row_map(j, be, nu):
        return (jnp.maximum(jnp.minimum(j, nu[0] - 1), 0), 0)

    def w_map(j, be, nu):
        return (be[j], 0, 0)

    grid_spec = pltpu.PrefetchScalarGridSpec(
        num_scalar_prefetch=2,
        grid=(nblk,),
        in_specs=[pl.BlockSpec((blk_rows, V7X_LANES), row_map),
                  pl.BlockSpec((None, d, d_e), w_map),
                  pl.BlockSpec((None, d, d_e), w_map),
                  pl.BlockSpec((None, d_e, d), w_map)],
        out_specs=pl.BlockSpec((blk_rows, V7X_LANES), lambda j, be, nu: (j, 0)),
        scratch_shapes=[pltpu.VMEM((d, d_e), BF16), pltpu.VMEM((d, d_e), BF16),
                        pltpu.VMEM((d_e, d), BF16)])
    return pl.pallas_call(
        _expert_body,
        grid_spec=grid_spec,
        out_shape=jax.ShapeDtypeStruct(xs.shape, F32),
        compiler_params=_cparams(("arbitrary",)),
        name="experts",
    )(block_e, n_used, xs, w_gate, w_up, w_down)


def _combine_body(dest_ref, dest_next_ref, h_ref, route_ref, fw_ref, ys_ref, out_ref, g_scr, sem):
    tm = h_ref.shape[0]
    step = pl.program_id(0)
    slot = step % 2

    def issue(d_ref, s):
        def start(r, carry):
            for k in range(TOP_K):
                pltpu.make_async_copy(_tile_rows(ys_ref, d_ref[k * tm + r]),
                                      _tile_rows(g_scr.at[s, k], r), sem.at[s]).start(priority=k)
            return carry

        lax.fori_loop(0, tm, start, 0, unroll=DMA_UNROLL)

    @pl.when(step == 0)
    def _():
        issue(dest_ref, slot)

    @pl.when(step < pl.num_programs(0) - 1)
    def _():
        issue(dest_next_ref, 1 - slot)

    for k in range(TOP_K):
        pltpu.make_async_copy(ys_ref.at[pl.ds(0, tm * ROW_TILES)], g_scr.at[slot, k],
                              sem.at[slot]).wait()
    rec = route_ref[...]
    hh = (h_ref[...] + rec[:, 2:3] * _tiles_to_rows(g_scr.at[slot, 0], tm)
          + rec[:, 3:4] * _tiles_to_rows(g_scr.at[slot, 1], tm))
    out_ref[...] = _rms(hh, fw_ref[...])


def _combine(dest, h, route, fw, ys):
    n, d = h.shape
    tm = TM_ROWS
    steps = n // tm
    return pl.pallas_call(
        _combine_body,
        grid=(steps,),
        in_specs=[pl.BlockSpec((TOP_K * tm,), lambda i: (i,), memory_space=pltpu.SMEM),
                  pl.BlockSpec((TOP_K * tm,), lambda i: (jnp.minimum(i + 1, steps - 1),),
                               memory_space=pltpu.SMEM),
                  pl.BlockSpec((tm, d), lambda i: (i, 0)),
                  pl.BlockSpec((tm, ROUTE_LANES), lambda i: (i, 0)),
                  pl.BlockSpec((1, d), lambda i: (0, 0)),
                  pl.BlockSpec(memory_space=pl.ANY)],
        out_specs=pl.BlockSpec((tm, d), lambda i: (i, 0)),
        out_shape=jax.ShapeDtypeStruct((n, d), F32),
        scratch_shapes=[pltpu.VMEM((2, TOP_K, tm * ROW_TILES, V7X_LANES), F32),
                        pltpu.SemaphoreType.DMA((2,))],
        compiler_params=_cparams(("arbitrary",)),
        name="combine",
    )(dest, dest, h, route, fw, ys)


def _plan(route, counts, n_experts, n_blocks):
    eid = route[:, 0:TOP_K].astype(jnp.int32)
    rank = route[:, 4:4 + TOP_K].astype(jnp.int32)
    cnt = counts[0, N_GROUPS:N_GROUPS + n_experts].astype(jnp.int32)
    padded = (cnt + R_BLK - 1) // R_BLK * R_BLK
    pends = jnp.cumsum(padded)
    pstart = pends - padded
    dest = (pstart[eid] + rank).T
    n_used = pends[-1] // R_BLK
    blk = jnp.minimum(jnp.arange(n_blocks, dtype=jnp.int32), n_used - 1)
    block_e = jnp.minimum(jnp.sum(pends[None, :] <= (blk * R_BLK)[:, None], axis=1), n_experts - 1)
    return (dest.astype(jnp.int32), block_e.astype(jnp.int32), n_used.reshape(1).astype(jnp.int32),
            pends.astype(jnp.int32), padded.astype(jnp.int32))


def _layer(h3, norm1_w, w_in, s5_a_re, s5_a_im, s5_b_re, s5_b_im, s5_c_re, s5_c_im, s5_d,
           s5_log_dt, s5_w_glu, s5_b_glu, ret_norm_w, w_out, norm2_w, router_group_w,
           router_group_b, router_expert_w, router_expert_b, moe_w_gate, moe_w_up, moe_w_down,
           out_norm_w):
    nb, seq, d = h3.shape
    n = nb * seq
    d_s5 = s5_d.shape[0]
    d_ret = ret_norm_w.shape[0]
    n_experts = moe_w_gate.shape[0]
    x2 = h3.reshape(n, d)

    u_s5, u_ret = _inproj(x2, norm1_w.reshape(1, d), w_in.astype(BF16), d_s5)

    bp, cp, tp, a_tab = _s5_tables(s5_a_re, s5_a_im, s5_b_re, s5_b_im, s5_c_re, s5_c_im,
                                   s5_log_dt, nb, S5_TAU)
    y_s5 = _s5(u_s5, nb, bp, cp, tp, a_tab, s5_d.reshape(1, d_s5).astype(F32),
               s5_w_glu.astype(BF16), s5_b_glu.reshape(1, d_s5).astype(F32))
    y_ret = _retention(u_ret.reshape(nb, seq, -1), d_ret, ret_norm_w.reshape(1, d_ret).astype(F32))

    n_route = N_GROUPS + n_experts
    wr = jnp.zeros((d, ROUTE_LANES), F32).at[:, :n_route].set(
        jnp.concatenate([router_group_w, router_expert_w], axis=1).astype(F32))
    br = jnp.zeros((1, ROUTE_LANES), F32).at[0, :n_route].set(
        jnp.concatenate([router_group_b, router_expert_b]).astype(F32))
    wr_hi = wr.astype(BF16)
    wr = jnp.concatenate([wr_hi, (wr - wr_hi.astype(F32)).astype(BF16)], axis=1)
    h, route, counts = _route(y_s5.reshape(n, d_s5), y_ret.reshape(n, d_ret), x2,
                              w_out.astype(BF16), norm2_w.reshape(1, d), wr, br)

    n_blocks = (n * TOP_K) // R_BLK + n_experts
    dest, block_e, n_used, pends, padded = _plan(route, counts, n_experts, n_blocks)
    dest = dest.reshape(TOP_K, n // TM_ROWS, TM_ROWS).transpose(1, 0, 2).reshape(-1)
    xs = _dispatch(pends, padded, dest, h, norm2_w.reshape(1, d), n_blocks * R_BLK)
    ys = _experts(block_e, n_used, xs, moe_w_gate, moe_w_up, moe_w_down)
    out = _combine(dest, h, route, out_norm_w.reshape(1, d), ys)
    return out.reshape(nb, seq, d)


def kernel(x, norm1_w, w_in, s5_a_re, s5_a_im, s5_b_re, s5_b_im, s5_c_re, s5_c_im, s5_d, s5_log_dt, s5_w_glu, s5_b_glu, ret_norm_w, w_out, norm2_w, router_group_w, router_group_b, router_expert_w, router_expert_b, moe_w_gate, moe_w_up, moe_w_down, final_norm_w):
    depth = norm1_w.shape[0]
    assert depth == 1, "the fused final norm assumes a single layer"
    l = 0
    return _layer(x, norm1_w[l], w_in[l], s5_a_re[l], s5_a_im[l], s5_b_re[l], s5_b_im[l],
                  s5_c_re[l], s5_c_im[l], s5_d[l], s5_log_dt[l], s5_w_glu[l], s5_b_glu[l],
                  ret_norm_w[l], w_out[l], norm2_w[l], router_group_w[l], router_group_b[l],
                  router_expert_w[l], router_expert_b[l], moe_w_gate[l], moe_w_up[l],
                  moe_w_down[l], final_norm_w)
```

```python
import functools

import jax
import jax.numpy as jnp
from jax import lax
from jax.experimental import pallas as pl
from jax.experimental.pallas import tpu as pltpu

F32 = jnp.float32
BF16 = jnp.bfloat16

EPS = 1e-6
ROPE_BASE = 10000.0
RET_HEADS = 8
TOP_K = 2
N_GROUPS = 4
EXPERTS_PER_GROUP = 8

V7X_LANES = 128
V7X_SUBLANES = 8
V7X_VMEM_LIMIT = 56 * 1024 * 1024

TM_PROJ = 512
T_S5 = 512
S5_TAU = 8
S5_PAD = 8
T_RET = 128
TM_ROUTE = 512
TM_ROWS = 256
R_BLK = 512
DMA_UNROLL = 8
ROUTE_LANES = 128


def _rms(x, w):
    return x * lax.rsqrt(jnp.mean(x * x, axis=-1, keepdims=True) + EPS) * w


def _cparams(sem):
    return pltpu.CompilerParams(dimension_semantics=sem, vmem_limit_bytes=V7X_VMEM_LIMIT)


ROW_TILES = V7X_SUBLANES


def _rows_to_tiles(ref, val):
    rows = val.shape[0]
    for s in range(ROW_TILES):
        ref[pl.ds(s, rows, stride=ROW_TILES), :] = val[:, s * V7X_LANES:(s + 1) * V7X_LANES]


def _tiles_to_rows(ref, rows):
    return jnp.concatenate(
        [ref[pl.ds(s, rows, stride=ROW_TILES), :] for s in range(ROW_TILES)], axis=1)


def _tile_rows(ref, row):
    return ref.at[pl.ds(pl.multiple_of(row * ROW_TILES, ROW_TILES), ROW_TILES)]


def _inproj_body(x_ref, nw_ref, w_ref, us5_ref, uret_ref):
    xn = _rms(x_ref[...], nw_ref[...]).astype(BF16)
    u = jnp.dot(xn, w_ref[...], preferred_element_type=F32)
    d_s5 = us5_ref.shape[1]
    us5_ref[...] = u[:, :d_s5].astype(us5_ref.dtype)
    uret_ref[...] = u[:, d_s5:].astype(uret_ref.dtype)


def _inproj(x2, nw, w_bf, d_s5):
    n, d = x2.shape
    d_in = w_bf.shape[1]
    return pl.pallas_call(
        _inproj_body,
        grid=(n // TM_PROJ,),
        in_specs=[pl.BlockSpec((TM_PROJ, d), lambda i: (i, 0)),
                  pl.BlockSpec((1, d), lambda i: (0, 0)),
                  pl.BlockSpec((d, d_in), lambda i: (0, 0))],
        out_specs=[pl.BlockSpec((TM_PROJ, d_s5), lambda i: (i, 0)),
                   pl.BlockSpec((TM_PROJ, d_in - d_s5), lambda i: (i, 0))],
        out_shape=[jax.ShapeDtypeStruct((n, d_s5), BF16),
                   jax.ShapeDtypeStruct((n, d_in - d_s5), BF16)],
        compiler_params=_cparams(("arbitrary",)),
        name="inproj",
    )(x2, nw, w_bf)


def _s5_body(u_ref, bp_ref, cp_ref, tp_ref, a_ref, dd_ref, wglu_ref, bglu_ref, y_ref,
             v_scr, sp_scr, st_scr, io_scr, *, nb, nk, seg, tau):
    @pl.when(pl.program_id(0) == 0)
    def _():
        st_scr[...] = jnp.zeros_like(st_scr)

    nblk = bp_ref.shape[0]
    cw = bp_ref.shape[1] // tau
    sw = bp_ref.shape[2]
    d_s5 = nblk * cw
    tiles = sw // V7X_LANES
    ht = tiles // 2
    ctiles = d_s5 // V7X_LANES
    srows = nb * nk

    u_all = u_ref[...].reshape(srows * tau, d_s5).astype(F32)
    for c in range(ctiles):
        io_scr[c] = u_all[:, c * V7X_LANES:(c + 1) * V7X_LANES]
    u_steps = [jnp.concatenate([io_scr[c, pl.ds(j, srows, stride=tau), :] for c in range(ctiles)],
                               axis=1).astype(BF16) for j in range(tau)]

    def block_inputs(blk):
        return jnp.concatenate([u_steps[j][:, blk * cw:(blk + 1) * cw] for j in range(tau)], axis=1)

    for blk in range(nblk):
        half, q = divmod(blk, 2)
        v = jnp.dot(block_inputs(blk), bp_ref[blk], preferred_element_type=F32)
        for b in range(nb):
            for j in range(tiles):
                v_scr[q * tiles + j, pl.ds((half * nb + b) * seg, nk), :] = (
                    v[b * nk:(b + 1) * nk, j * V7X_LANES:(j + 1) * V7X_LANES])

    rows = 2 * nb
    ar = [[a_ref[0, q * ht + i] for i in range(ht)] for q in range(2)]
    ai = [[a_ref[1, q * ht + i] for i in range(ht)] for q in range(2)]
    sr = [[st_scr[q * tiles + i] for i in range(ht)] for q in range(2)]
    si = [[st_scr[q * tiles + ht + i] for i in range(ht)] for q in range(2)]
    for k in range(nk):
        for q in range(2):
            for i in range(ht):
                jr = q * tiles + i
                ji = q * tiles + ht + i
                sp_scr[jr, pl.ds(k, rows, stride=seg), :] = sr[q][i]
                sp_scr[ji, pl.ds(k, rows, stride=seg), :] = si[q][i]
                vr = v_scr[jr, pl.ds(k, rows, stride=seg), :]
                vi = v_scr[ji, pl.ds(k, rows, stride=seg), :]
                nr = ar[q][i] * sr[q][i] - ai[q][i] * si[q][i] + vr
                ni = ar[q][i] * si[q][i] + ai[q][i] * sr[q][i] + vi
                sr[q][i], si[q][i] = nr, ni
    for q in range(2):
        for i in range(ht):
            st_scr[q * tiles + i] = sr[q][i]
            st_scr[q * tiles + ht + i] = si[q][i]

    yb = []
    for blk in range(nblk):
        half, q = divmod(blk, 2)
        sp = jnp.concatenate(
            [jnp.concatenate([sp_scr[q * tiles + j, pl.ds((half * nb + b) * seg, nk), :]
                              for j in range(tiles)], axis=1) for b in range(nb)],
            axis=0).astype(BF16)
        yb.append(jnp.dot(sp, cp_ref[blk], preferred_element_type=F32)
                  + jnp.dot(block_inputs(blk), tp_ref[blk], preferred_element_type=F32))
    for j in range(tau):
        y = jnp.concatenate([yb[blk][:, j * cw:(j + 1) * cw] for blk in range(nblk)], axis=1)
        y = y + dd_ref[...] * u_steps[j].astype(F32)
        y = jax.nn.gelu(y)
        z = jnp.dot(y.astype(BF16), wglu_ref[...], preferred_element_type=F32) + bglu_ref[...]
        out = y * jax.nn.sigmoid(z)
        for c in range(ctiles):
            io_scr[c, pl.ds(j, srows, stride=tau), :] = out[:, c * V7X_LANES:(c + 1) * V7X_LANES]
    y_all = jnp.concatenate([io_scr[c] for c in range(ctiles)], axis=1)
    y_ref[...] = y_all.reshape(nb, nk * tau, d_s5).astype(y_ref.dtype)


def _s5(u_s5, nb, bp, cp, tp, a_tab, dd, wglu_bf, bglu):
    n, d_s5 = u_s5.shape
    seq = n // nb
    tau = S5_TAU
    nk = T_S5 // tau
    seg = nk + S5_PAD
    nblk, _, sw = bp.shape
    rows = 2 * nb
    assert rows == V7X_SUBLANES and nblk == 4

    def whole(a):
        return pl.BlockSpec(a.shape, lambda c: (0,) * a.ndim)

    body = functools.partial(_s5_body, nb=nb, nk=nk, seg=seg, tau=tau)
    out = pl.pallas_call(
        body,
        grid=(seq // T_S5,),
        in_specs=[pl.BlockSpec((nb, T_S5, d_s5), lambda c: (0, c, 0)),
                  whole(bp), whole(cp), whole(tp), whole(a_tab), whole(dd), whole(wglu_bf),
                  whole(bglu)],
        out_specs=pl.BlockSpec((nb, T_S5, d_s5), lambda c: (0, c, 0)),
        out_shape=jax.ShapeDtypeStruct((nb, seq, d_s5), BF16),
        scratch_shapes=[pltpu.VMEM((2 * sw // V7X_LANES, rows * seg, V7X_LANES), F32),
                        pltpu.VMEM((2 * sw // V7X_LANES, rows * seg, V7X_LANES), F32),
                        pltpu.VMEM((2 * sw // V7X_LANES, rows, V7X_LANES), F32),
                        pltpu.VMEM((d_s5 // V7X_LANES, nb * T_S5, V7X_LANES), F32)],
        compiler_params=_cparams(("arbitrary",)),
        name="s5",
    )(u_s5.reshape(nb, seq, d_s5), bp, cp, tp, a_tab, dd, wglu_bf, bglu)
    return out.reshape(n, d_s5)


def _s5_tables(a_re, a_im, b_re, b_im, c_re, c_im, log_dt, nb, tau):
    hp = lax.Precision.HIGHEST
    g, p = a_re.shape
    hch = b_re.shape[2]
    gpb = V7X_LANES // hch
    nblk = g // gpb
    lam_r, lam_i = a_re.astype(F32), a_im.astype(F32)
    dt = jnp.exp(log_dt.astype(F32))[:, None]
    mag = jnp.exp(lam_r * dt)
    ab_r = mag * jnp.cos(lam_i * dt)
    ab_i = mag * jnp.sin(lam_i * dt)
    den = lam_r * lam_r + lam_i * lam_i
    zr = ((ab_r - 1.0) * lam_r + ab_i * lam_i) / den
    zi = (ab_i * lam_r - (ab_r - 1.0) * lam_i) / den
    br_, bi_ = b_re.astype(F32), b_im.astype(F32)
    bb_r = zr[..., None] * br_ - zi[..., None] * bi_
    bb_i = zr[..., None] * bi_ + zi[..., None] * br_
    cr, ci = c_re.astype(F32), c_im.astype(F32)
    pr, pi = [jnp.ones_like(ab_r)], [jnp.zeros_like(ab_i)]
    for _ in range(tau):
        pr, pi = pr + [pr[-1] * ab_r - pi[-1] * ab_i], pi + [pr[-1] * ab_i + pi[-1] * ab_r]
    pw_r, pw_i = jnp.stack(pr), jnp.stack(pi)

    def blockdiag(x):
        r, c = x.shape[-2:]
        x = jnp.tile(x.reshape(tau, nblk, gpb * r, c), (1, 1, 1, gpb))
        same = (jnp.arange(gpb * r)[:, None] // r) == (jnp.arange(gpb * c)[None, :] // c)
        return jnp.where(same, x, 0.0)

    wr_ = jnp.stack([pr[tau - 1 - j] for j in range(tau)])[:, :, None, :]
    wi_ = jnp.stack([pi[tau - 1 - j] for j in range(tau)])[:, :, None, :]
    bt_r, bt_i = jnp.swapaxes(bb_r, 1, 2)[None], jnp.swapaxes(bb_i, 1, 2)[None]
    bp_r = blockdiag(wr_ * bt_r - wi_ * bt_i)
    bp_i = blockdiag(wr_ * bt_i + wi_ * bt_r)
    bp = jnp.concatenate([jnp.concatenate([bp_r[j], bp_i[j]], axis=-1) for j in range(tau)],
                         axis=1).astype(BF16)

    qr_, qi_ = pw_r[1:, :, :, None], pw_i[1:, :, :, None]
    ct_r, ct_i = jnp.swapaxes(cr, 1, 2)[None], jnp.swapaxes(ci, 1, 2)[None]
    cp_r = blockdiag(ct_r * qr_ - ct_i * qi_)
    cp_i = blockdiag(ct_r * qi_ + ct_i * qr_)
    cp = jnp.concatenate([jnp.concatenate([cp_r[i], -cp_i[i]], axis=1) for i in range(tau)],
                         axis=2).astype(BF16)

    ab_r_ = pw_r[:tau, :, :, None] * bb_r[None] - pw_i[:tau, :, :, None] * bb_i[None]
    ab_i_ = pw_r[:tau, :, :, None] * bb_i[None] + pw_i[:tau, :, :, None] * bb_r[None]
    kd = blockdiag(jnp.einsum('ghp,dgpe->dgeh', cr, ab_r_, precision=hp)
                   - jnp.einsum('ghp,dgpe->dgeh', ci, ab_i_, precision=hp))
    kzero = jnp.zeros_like(kd[0])
    tp = jnp.concatenate(
        [jnp.concatenate([kd[i - j] if i >= j else kzero for i in range(tau)], axis=2)
         for j in range(tau)], axis=1).astype(BF16)

    a_tab = jnp.stack([jnp.repeat(pw_r[tau].reshape(2, -1), nb, axis=0),
                       jnp.repeat(pw_i[tau].reshape(2, -1), nb, axis=0)])
    a_tab = a_tab.reshape(2, 2 * nb, -1, V7X_LANES).transpose(0, 2, 1, 3)
    return bp, cp, tp, a_tab


def _ret_body(q_ref, k_ref, v_ref, g_ref, cos_ref, sin_ref, dec_ref, qdec_ref, kdect_ref,
              cdec_ref, mk_ref, mv_ref, ms_ref, avg_ref, nw_ref, y_ref, st_scr, *, t_len, dh, scale):
    @pl.when(pl.program_id(0) == 0)
    def _():
        st_scr[...] = jnp.zeros_like(st_scr)

    nb = q_ref.shape[0]
    width = q_ref.shape[-1]
    heads = width // dh
    pairs = width // V7X_LANES
    reps = width // cos_ref.shape[-1]
    cos = jnp.concatenate([cos_ref[...]] * reps, axis=1)
    sin = jnp.concatenate([sin_ref[...]] * reps, axis=1)
    lane = lax.broadcasted_iota(jnp.int32, (t_len, width), 1)
    first = (lane % dh) < (dh // 2)

    def rot(x):
        x = x.astype(F32)
        partner = jnp.where(first, pltpu.roll(x, width - dh // 2, 1), pltpu.roll(x, dh // 2, 1))
        return x * cos + partner * sin

    def group_mean(x):
        hi = x.astype(BF16)
        lo = (x - hi.astype(F32)).astype(BF16)
        return (jnp.dot(hi, avg_ref[...], preferred_element_type=F32)
                + jnp.dot(lo, avg_ref[...], preferred_element_type=F32))

    for b in range(nb):
        q = rot(q_ref[b])
        k = rot(k_ref[b]) * scale
        vb = v_ref[b]
        kt = k.T
        kbd = jnp.concatenate([kt.astype(BF16)] * heads, axis=1) * mk_ref[...]
        sc = jnp.dot(q.astype(BF16), kbd, preferred_element_type=F32) * dec_ref[...]
        vbd = jnp.concatenate([vb] * heads, axis=0) * mv_ref[...]
        inner = jnp.dot(sc.astype(BF16), vbd, preferred_element_type=F32)
        qd = (q * qdec_ref[...]).astype(BF16)
        kdt = (kt * kdect_ref[...]).astype(BF16)
        crosses = []
        for p in range(pairs):
            ps = slice(p * V7X_LANES, (p + 1) * V7X_LANES)
            state = st_scr[b, p]
            crosses.append(jnp.dot(qd[:, ps], state.astype(BF16), preferred_element_type=F32))
            kv = jnp.dot(kdt[ps, :], vb[:, ps], preferred_element_type=F32)
            st_scr[b, p] = state * cdec_ref[:, ps] + kv * ms_ref[...]
        o = inner + jnp.concatenate(crosses, axis=1)
        dlt = o - group_mean(o)
        var = group_mean(dlt * dlt)
        on = dlt * lax.rsqrt(var + EPS) * nw_ref[...]
        y_ref[b] = (jax.nn.silu(g_ref[b].astype(F32)) * on).astype(y_ref.dtype)


def _retention(u3, d_ret, norm_w):
    nb, seq, _ = u3.shape
    dh = d_ret // RET_HEADS
    half = dh // 2
    t_len = T_RET
    lg = jnp.log(1.0 - 2.0 ** (-5.0 - jnp.arange(RET_HEADS, dtype=F32)))
    t = jnp.arange(t_len, dtype=F32)
    diff = t[:, None] - t[None, :]
    dec = jnp.where(diff >= 0, jnp.exp(lg[:, None, None] * jnp.maximum(diff, 0.0)), 0.0)
    dec_all = dec.transpose(1, 0, 2).reshape(t_len, RET_HEADS * t_len)
    qdec = jnp.repeat(jnp.exp(lg[:, None] * (t + 1.0)[None, :]).T, dh, axis=1)
    kdect = jnp.repeat(jnp.exp(lg[:, None] * (t_len - 1 - t)[None, :]), dh, axis=0)
    cdec = jnp.repeat(jnp.exp(lg * t_len), dh)[None, :]
    head_of_lane = jnp.arange(d_ret) // dh
    head_of_col = jnp.arange(RET_HEADS * t_len) // t_len
    mask_k = (head_of_lane[:, None] == head_of_col[None, :]).astype(BF16)
    mask_v = mask_k.T
    pair_head = jnp.arange(V7X_LANES) // dh
    mask_s = (pair_head[:, None] == pair_head[None, :]).astype(F32)
    avg = (head_of_lane[:, None] == head_of_lane[None, :]).astype(F32) / dh
    assert dh & (dh - 1) == 0
    avg = avg.astype(BF16)
    inv = ROPE_BASE ** (-jnp.arange(half, dtype=F32) / half)
    ang = jnp.arange(seq, dtype=F32)[:, None] * inv[None, :]
    reps = V7X_LANES // dh
    cos_t = jnp.tile(jnp.cos(ang), (1, 2 * reps))
    sin_t = jnp.tile(jnp.concatenate([-jnp.sin(ang), jnp.sin(ang)], axis=1), (1, reps))

    def col(j):
        return pl.BlockSpec((nb, t_len, d_ret), lambda c: (0, c, j))

    def whole(a):
        return pl.BlockSpec(a.shape, lambda c: (0,) * a.ndim)

    body = functools.partial(_ret_body, t_len=t_len, dh=dh, scale=dh ** -0.5)
    return pl.pallas_call(
        body,
        grid=(seq // t_len,),
        in_specs=[col(0), col(1), col(2), col(3),
                  pl.BlockSpec((t_len, V7X_LANES), lambda c: (c, 0)),
                  pl.BlockSpec((t_len, V7X_LANES), lambda c: (c, 0)),
                  whole(dec_all), whole(qdec), whole(kdect), whole(cdec),
                  whole(mask_k), whole(mask_v), whole(mask_s), whole(avg), whole(norm_w)],
        out_specs=pl.BlockSpec((nb, t_len, d_ret), lambda c: (0, c, 0)),
        out_shape=jax.ShapeDtypeStruct((nb, seq, d_ret), BF16),
        scratch_shapes=[pltpu.VMEM((nb, d_ret // V7X_LANES, V7X_LANES, V7X_LANES), F32)],
        compiler_params=_cparams(("arbitrary",)),
        name="retention",
    )(u3, u3, u3, u3, cos_t, sin_t, dec_all, qdec, kdect, cdec, mask_k, mask_v, mask_s, avg, norm_w)


def _route_body(ys5_ref, yret_ref, x_ref, wo_ref, n2_ref, wr_ref, br_ref, tri_ref,
                h_ref, route_ref, cnt_ref, carry_scr):
    @pl.when(pl.program_id(0) == 0)
    def _():
        carry_scr[...] = jnp.zeros_like(carry_scr)

    d_s5 = ys5_ref.shape[1]
    h = (x_ref[...]
         + jnp.dot(ys5_ref[...], wo_ref[0:d_s5], preferred_element_type=F32)
         + jnp.dot(yret_ref[...], wo_ref[d_s5:], preferred_element_type=F32))
    h_ref[...] = h
    hn = _rms(h, n2_ref[...])
    hi = hn.astype(BF16)
    lo = (hn - hi.astype(F32)).astype(BF16)
    p_hi = jnp.dot(hi, wr_ref[...], preferred_element_type=F32)
    p_lo = jnp.dot(lo, wr_ref[:, 0:ROUTE_LANES], preferred_element_type=F32)
    logits = p_hi[:, 0:ROUTE_LANES] + p_hi[:, ROUTE_LANES:] + p_lo + br_ref[...]
    tm = logits.shape[0]
    lane = lax.broadcasted_iota(jnp.int32, (tm, ROUTE_LANES), 1)
    lanef = lane.astype(F32)
    neg = -jnp.inf
    big = float(ROUTE_LANES)
    gl = jnp.where(lane < N_GROUPS, logits, neg)
    gmax = jnp.max(gl, axis=-1, keepdims=True)
    gidx = jnp.min(jnp.where(gl == gmax, lanef, big), axis=-1, keepdims=True)
    g_w = 1.0 / jnp.sum(jnp.exp(gl - gmax), axis=-1, keepdims=True)
    lo = N_GROUPS + EXPERTS_PER_GROUP * gidx
    el = jnp.where((lanef >= lo) & (lanef < lo + EXPERTS_PER_GROUP), logits, neg)
    v0 = jnp.max(el, axis=-1, keepdims=True)
    i0 = jnp.min(jnp.where(el == v0, lanef, big), axis=-1, keepdims=True)
    el2 = jnp.where(lanef == i0, neg, el)
    v1 = jnp.max(el2, axis=-1, keepdims=True)
    i1 = jnp.min(jnp.where(el2 == v1, lanef, big), axis=-1, keepdims=True)
    e = jnp.exp(v1 - v0)
    den = 1.0 + e
    w0 = (1.0 / den) * g_w
    w1 = (e / den) * g_w
    sel0 = lanef == i0
    sel1 = lanef == i1
    onehot = jnp.where(sel0 | sel1, 1.0, 0.0)
    before = jnp.dot(tri_ref[...], onehot.astype(BF16), preferred_element_type=F32) + carry_scr[...]
    r0 = jnp.sum(jnp.where(sel0, before, 0.0), axis=-1, keepdims=True)
    r1 = jnp.sum(jnp.where(sel1, before, 0.0), axis=-1, keepdims=True)
    carry_scr[...] += jnp.sum(onehot, axis=0, keepdims=True)
    cnt_ref[...] = carry_scr[...]
    rec = jnp.zeros((tm, ROUTE_LANES), F32)
    for j, val in enumerate((i0 - N_GROUPS, i1 - N_GROUPS, w0, w1, r0, r1)):
        rec = jnp.where(lane == j, val, rec)
    route_ref[...] = rec


def _route(ys5, yret, x2, wo_bf, n2, wr, br):
    n, d = x2.shape
    d_s5 = ys5.shape[1]
    d_ret = yret.shape[1]
    tm = TM_ROUTE
    tri = (jnp.arange(tm)[:, None] > jnp.arange(tm)[None, :]).astype(BF16)
    return pl.pallas_call(
        _route_body,
        grid=(n // tm,),
        in_specs=[pl.BlockSpec((tm, d_s5), lambda i: (i, 0)),
                  pl.BlockSpec((tm, d_ret), lambda i: (i, 0)),
                  pl.BlockSpec((tm, d), lambda i: (i, 0)),
                  pl.BlockSpec((d_s5 + d_ret, d), lambda i: (0, 0)),
                  pl.BlockSpec((1, d), lambda i: (0, 0)),
                  pl.BlockSpec((d, 2 * ROUTE_LANES), lambda i: (0, 0)),
                  pl.BlockSpec((1, ROUTE_LANES), lambda i: (0, 0)),
                  pl.BlockSpec((tm, tm), lambda i: (0, 0))],
        out_specs=[pl.BlockSpec((tm, d), lambda i: (i, 0)),
                   pl.BlockSpec((tm, ROUTE_LANES), lambda i: (i, 0)),
                   pl.BlockSpec((1, ROUTE_LANES), lambda i: (0, 0))],
        out_shape=[jax.ShapeDtypeStruct((n, d), F32),
                   jax.ShapeDtypeStruct((n, ROUTE_LANES), F32),
                   jax.ShapeDtypeStruct((1, ROUTE_LANES), F32)],
        scratch_shapes=[pltpu.VMEM((1, ROUTE_LANES), F32)],
        compiler_params=_cparams(("arbitrary",)),
        name="outproj_route",
    )(ys5, yret, x2, wo_bf, n2, wr, br, tri)


def _dispatch_body(pends_ref, padded_ref, dest_ref, h_ref, n2_ref, xs_ref,
                   hn_scr, zero_scr, sem, zsem):
    tm = h_ref.shape[0]

    @pl.when(pl.program_id(0) == 0)
    def _():
        zero_scr[...] = jnp.zeros_like(zero_scr)

        def zero_copy(e):
            first = pl.multiple_of((pends_ref[e] - R_BLK) * ROW_TILES, R_BLK * ROW_TILES)
            return pltpu.make_async_copy(zero_scr, xs_ref.at[pl.ds(first, R_BLK * ROW_TILES)], zsem)

        def zstart(e, carry):
            @pl.when(padded_ref[e] > 0)
            def _():
                zero_copy(e).start()
            return carry

        def zwait(e, carry):
            @pl.when(padded_ref[e] > 0)
            def _():
                zero_copy(e).wait()
            return carry

        lax.fori_loop(0, pends_ref.shape[0], zstart, 0)
        lax.fori_loop(0, pends_ref.shape[0], zwait, 0)

    step = pl.program_id(0)
    slot = step % 2
    buf = hn_scr.at[slot]
    _rows_to_tiles(buf, _rms(h_ref[...], n2_ref[...]))

    def start(r, carry):
        for k in range(TOP_K):
            pltpu.make_async_copy(_tile_rows(buf, r), _tile_rows(xs_ref, dest_ref[k * tm + r]),
                                  sem.at[slot]).start(priority=k)
        return carry

    lax.fori_loop(0, tm, start, 0, unroll=DMA_UNROLL)

    def drain(s):
        for k in range(TOP_K):
            pltpu.make_async_copy(hn_scr.at[s], xs_ref.at[pl.ds(0, tm * ROW_TILES)], sem.at[s]).wait()

    @pl.when(step > 0)
    def _():
        drain(1 - slot)

    @pl.when(step == pl.num_programs(0) - 1)
    def _():
        drain(slot)


def _dispatch(pends, padded, dest, h, n2, p_rows):
    n, d = h.shape
    tm = TM_ROWS
    grid_spec = pltpu.PrefetchScalarGridSpec(
        num_scalar_prefetch=2,
        grid=(n // tm,),
        in_specs=[pl.BlockSpec((TOP_K * tm,), lambda i, pe, pa: (i,), memory_space=pltpu.SMEM),
                  pl.BlockSpec((tm, d), lambda i, pe, pa: (i, 0)),
                  pl.BlockSpec((1, d), lambda i, pe, pa: (0, 0))],
        out_specs=pl.BlockSpec(memory_space=pl.ANY),
        scratch_shapes=[pltpu.VMEM((2, tm * ROW_TILES, V7X_LANES), F32),
                        pltpu.VMEM((R_BLK * ROW_TILES, V7X_LANES), F32),
                        pltpu.SemaphoreType.DMA((2,)), pltpu.SemaphoreType.DMA(())])
    assert d == ROW_TILES * V7X_LANES
    return pl.pallas_call(
        _dispatch_body,
        grid_spec=grid_spec,
        out_shape=jax.ShapeDtypeStruct((p_rows * ROW_TILES, V7X_LANES), F32),
        compiler_params=_cparams(("arbitrary",)),
        name="dispatch",
    )(pends, padded, dest, h, n2)


def _expert_body(be_ref, nu_ref, xs_ref, wg_ref, wu_ref, wd_ref, ys_ref, wg_s, wu_s, wd_s):
    j = pl.program_id(0)

    @pl.when(j < nu_ref[0])
    def _():
        changed = jnp.logical_or(j == 0, be_ref[j] != be_ref[jnp.maximum(j - 1, 0)])

        @pl.when(changed)
        def _():
            wg_s[...] = wg_ref[...].astype(BF16)
            wu_s[...] = wu_ref[...].astype(BF16)
            wd_s[...] = wd_ref[...].astype(BF16)

        x = _tiles_to_rows(xs_ref, R_BLK).astype(BF16)
        gate = jnp.dot(x, wg_s[...], preferred_element_type=F32)
        up = jnp.dot(x, wu_s[...], preferred_element_type=F32)
        hid = (jax.nn.silu(gate) * up).astype(BF16)
        _rows_to_tiles(ys_ref, jnp.dot(hid, wd_s[...], preferred_element_type=F32))

    @pl.when(j >= nu_ref[0])
    def _():
        ys_ref[...] = jnp.zeros_like(ys_ref)


def _experts(block_e, n_used, xs, w_gate, w_up, w_down):
    d, d_e = w_gate.shape[1:]
    blk_rows = R_BLK * ROW_TILES
    nblk = xs.shape[0] // blk_rows

    def row_map(j, be, nu):
        return (jnp.maximum(jnp.minimum(j, nu[0] - 1), 0), 0)

    def w_map(j, be, nu):
        return (be[j], 0, 0)

    grid_spec = pltpu.PrefetchScalarGridSpec(
        num_scalar_prefetch=2,
        grid=(nblk,),
        in_specs=[pl.BlockSpec((blk_rows, V7X_LANES), row_map),
                  pl.BlockSpec((None, d, d_e), w_map),
                  pl.BlockSpec((None, d, d_e), w_map),
                  pl.BlockSpec((None, d_e, d), w_map)],
        out_specs=pl.BlockSpec((blk_rows, V7X_LANES), lambda j, be, nu: (j, 0)),
        scratch_shapes=[pltpu.VMEM((d, d_e), BF16), pltpu.VMEM((d, d_e), BF16),
                        pltpu.VMEM((d_e, d), BF16)])
    return pl.pallas_call(
        _expert_body,
        grid_spec=grid_spec,
        out_shape=jax.ShapeDtypeStruct(xs.shape, F32),
        compiler_params=_cparams(("arbitrary",)),
        name="experts",
    )(block_e, n_used, xs, w_gate, w_up, w_down)


def _combine_body(dest_ref, dest_next_ref, h_ref, route_ref, fw_ref, ys_ref, out_ref, g_scr, sem):
    tm = h_ref.shape[0]
    step = pl.program_id(0)
    slot = step % 2

    def issue(d_ref, s):
        def start(r, carry):
            for k in range(TOP_K):
                pltpu.make_async_copy(_tile_rows(ys_ref, d_ref[k * tm + r]),
                                      _tile_rows(g_scr.at[s, k], r), sem.at[s]).start(priority=k)
            return carry

        lax.fori_loop(0, tm, start, 0, unroll=DMA_UNROLL)

    @pl.when(step == 0)
    def _():
        issue(dest_ref, slot)

    @pl.when(step < pl.num_programs(0) - 1)
    def _():
        issue(dest_next_ref, 1 - slot)

    for k in range(TOP_K):
        pltpu.make_async_copy(ys_ref.at[pl.ds(0, tm * ROW_TILES)], g_scr.at[slot, k],
                              sem.at[slot]).wait()
    rec = route_ref[...]
    hh = (h_ref[...] + rec[:, 2:3] * _tiles_to_rows(g_scr.at[slot, 0], tm)
          + rec[:, 3:4] * _tiles_to_rows(g_scr.at[slot, 1], tm))
    out_ref[...] = _rms(hh, fw_ref[...])


def _combine(dest, h, route, fw, ys):
    n, d = h.shape
    tm = TM_ROWS
    steps = n // tm
    return pl.pallas_call(
        _combine_body,
        grid=(steps,),
        in_specs=[pl.BlockSpec((TOP_K * tm,), lambda i: (i,), memory_space=pltpu.SMEM),
                  pl.BlockSpec((TOP_K * tm,), lambda i: (jnp.minimum(i + 1, steps - 1),),
                               memory_space=pltpu.SMEM),
                  pl.BlockSpec((tm, d), lambda i: (i, 0)),
                  pl.BlockSpec((tm, ROUTE_LANES), lambda i: (i, 0)),
                  pl.BlockSpec((1, d), lambda i: (0, 0)),
                  pl.BlockSpec(memory_space=pl.ANY)],
        out_specs=pl.BlockSpec((tm, d), lambda i: (i, 0)),
        out_shape=jax.ShapeDtypeStruct((n, d), F32),
        scratch_shapes=[pltpu.VMEM((2, TOP_K, tm * ROW_TILES, V7X_LANES), F32),
                        pltpu.SemaphoreType.DMA((2,))],
        compiler_params=_cparams(("arbitrary",)),
        name="combine",
    )(dest, dest, h, route, fw, ys)


def _plan(route, counts, n_experts, n_blocks):
    eid = route[:, 0:TOP_K].astype(jnp.int32)
    rank = route[:, 4:4 + TOP_K].astype(jnp.int32)
    cnt = counts[0, N_GROUPS:N_GROUPS + n_experts].astype(jnp.int32)
    padded = (cnt + R_BLK - 1) // R_BLK * R_BLK
    pends = jnp.cumsum(padded)
    pstart = pends - padded
    dest = (pstart[eid] + rank).T
    n_used = pends[-1] // R_BLK
    blk = jnp.minimum(jnp.arange(n_blocks, dtype=jnp.int32), n_used - 1)
    block_e = jnp.minimum(jnp.sum(pends[None, :] <= (blk * R_BLK)[:, None], axis=1), n_experts - 1)
    return (dest.astype(jnp.int32), block_e.astype(jnp.int32), n_used.reshape(1).astype(jnp.int32),
            pends.astype(jnp.int32), padded.astype(jnp.int32))


def _layer(h3, norm1_w, w_in, s5_a_re, s5_a_im, s5_b_re, s5_b_im, s5_c_re, s5_c_im, s5_d,
           s5_log_dt, s5_w_glu, s5_b_glu, ret_norm_w, w_out, norm2_w, router_group_w,
           router_group_b, router_expert_w, router_expert_b, moe_w_gate, moe_w_up, moe_w_down,
           out_norm_w):
    nb, seq, d = h3.shape
    n = nb * seq
    d_s5 = s5_d.shape[0]
    d_ret = ret_norm_w.shape[0]
    n_experts = moe_w_gate.shape[0]
    x2 = h3.reshape(n, d)

    u_s5, u_ret = _inproj(x2, norm1_w.reshape(1, d), w_in.astype(BF16), d_s5)

    bp, cp, tp, a_tab = _s5_tables(s5_a_re, s5_a_im, s5_b_re, s5_b_im, s5_c_re, s5_c_im,
                                   s5_log_dt, nb, S5_TAU)
    y_s5 = _s5(u_s5, nb, bp, cp, tp, a_tab, s5_d.reshape(1, d_s5).astype(F32),
               s5_w_glu.astype(BF16), s5_b_glu.reshape(1, d_s5).astype(F32))
    y_ret = _retention(u_ret.reshape(nb, seq, -1), d_ret, ret_norm_w.reshape(1, d_ret).astype(F32))

    n_route = N_GROUPS + n_experts
    wr = jnp.zeros((d, ROUTE_LANES), F32).at[:, :n_route].set(
        jnp.concatenate([router_group_w, router_expert_w], axis=1).astype(F32))
    br = jnp.zeros((1, ROUTE_LANES), F32).at[0, :n_route].set(
        jnp.concatenate([router_group_b, router_expert_b]).astype(F32))
    wr_hi = wr.astype(BF16)
    wr = jnp.concatenate([wr_hi, (wr - wr_hi.astype(F32)).astype(BF16)], axis=1)
    h, route, counts = _route(y_s5.reshape(n, d_s5), y_ret.reshape(n, d_ret), x2,
                              w_out.astype(BF16), norm2_w.reshape(1, d), wr, br)

    n_blocks = (n * TOP_K) // R_BLK + n_experts
    dest, block_e, n_used, pends, padded = _plan(route, counts, n_experts, n_blocks)
    dest = dest.reshape(TOP_K, n // TM_ROWS, TM_ROWS).transpose(1, 0, 2).reshape(-1)
    xs = _dispatch(pends, padded, dest, h, norm2_w.reshape(1, d), n_blocks * R_BLK)
    ys = _experts(block_e, n_used, xs, moe_w_gate, moe_w_up, moe_w_down)
    out = _combine(dest, h, route, out_norm_w.reshape(1, d), ys)
    return out.reshape(nb, seq, d)


def kernel(x, norm1_w, w_in, s5_a_re, s5_a_im, s5_b_re, s5_b_im, s5_c_re, s5_c_im, s5_d, s5_log_dt, s5_w_glu, s5_b_glu, ret_norm_w, w_out, norm2_w, router_group_w, router_group_b, router_expert_w, router_expert_b, moe_w_gate, moe_w_up, moe_w_down, final_norm_w):
    depth = norm1_w.shape[0]
    assert depth == 1, "the fused final norm assumes a single layer"
    l = 0
    return _layer(x, norm1_w[l], w_in[l], s5_a_re[l], s5_a_im[l], s5_b_re[l], s5_b_im[l],
                  s5_c_re[l], s5_c_im[l], s5_d[l], s5_log_dt[l], s5_w_glu[l], s5_b_glu[l],
                  ret_norm_w[l], w_out[l], norm2_w[l], router_group_w[l], router_group_b[l],
                  router_expert_w[l], router_expert_b[l], moe_w_gate[l], moe_w_up[l],
                  moe_w_down[l], final_norm_w)
```

```python
import functools

import jax
import jax.numpy as jnp
from jax import lax
from jax.experimental import pallas as pl
from jax.experimental.pallas import tpu as pltpu

F32 = jnp.float32
BF16 = jnp.bfloat16

EPS = 1e-6
ROPE_BASE = 10000.0
RET_HEADS = 8
TOP_K = 2
N_GROUPS = 4
EXPERTS_PER_GROUP = 8

V7X_LANES = 128
V7X_SUBLANES = 8
V7X_VMEM_LIMIT = 56 * 1024 * 1024

TM_PROJ = 512
T_S5 = 512
S5_TAU = 8
S5_PAD = 8
T_RET = 128
TM_ROUTE = 512
TM_ROWS = 256
R_BLK = 512
DMA_UNROLL = 8
ROUTE_LANES = 128
ROUTE_FIELDS = 8


def _rms(x, w):
    return x * lax.rsqrt(jnp.mean(x * x, axis=-1, keepdims=True) + EPS) * w


def _cparams(sem):
    return pltpu.CompilerParams(dimension_semantics=sem, vmem_limit_bytes=V7X_VMEM_LIMIT)


ROW_TILES = V7X_SUBLANES


def _rows_to_tiles(ref, val):
    rows = val.shape[0]
    for s in range(ROW_TILES):
        ref[pl.ds(s, rows, stride=ROW_TILES), :] = val[:, s * V7X_LANES:(s + 1) * V7X_LANES]


def _tiles_to_rows(ref, rows):
    return jnp.concatenate(
        [ref[pl.ds(s, rows, stride=ROW_TILES), :] for s in range(ROW_TILES)], axis=1)


def _tile_rows(ref, row):
    return ref.at[pl.ds(pl.multiple_of(row * ROW_TILES, ROW_TILES), ROW_TILES)]


def _inproj_body(x_ref, nw_ref, w_ref, us5_ref, uret_ref):
    xn = _rms(x_ref[...], nw_ref[...]).astype(BF16)
    u = jnp.dot(xn, w_ref[...], preferred_element_type=F32)
    d_s5 = us5_ref.shape[1]
    us5_ref[...] = u[:, :d_s5].astype(us5_ref.dtype)
    uret_ref[...] = u[:, d_s5:].astype(uret_ref.dtype)


def _inproj(x2, nw, w_bf, d_s5):
    n, d = x2.shape
    d_in = w_bf.shape[1]
    return pl.pallas_call(
        _inproj_body,
        grid=(n // TM_PROJ,),
        in_specs=[pl.BlockSpec((TM_PROJ, d), lambda i: (i, 0)),
                  pl.BlockSpec((1, d), lambda i: (0, 0)),
                  pl.BlockSpec((d, d_in), lambda i: (0, 0))],
        out_specs=[pl.BlockSpec((TM_PROJ, d_s5), lambda i: (i, 0)),
                   pl.BlockSpec((TM_PROJ, d_in - d_s5), lambda i: (i, 0))],
        out_shape=[jax.ShapeDtypeStruct((n, d_s5), BF16),
                   jax.ShapeDtypeStruct((n, d_in - d_s5), BF16)],
        compiler_params=_cparams(("arbitrary",)),
        name="inproj",
    )(x2, nw, w_bf)


def _s5_body(u_ref, bp_ref, cp_ref, tp_ref, a_ref, dd_ref, wglu_ref, bglu_ref, y_ref,
             v_scr, sp_scr, st_scr, io_scr, *, nb, nk, seg, tau):
    @pl.when(pl.program_id(0) == 0)
    def _():
        st_scr[...] = jnp.zeros_like(st_scr)

    nblk = bp_ref.shape[0]
    cw = bp_ref.shape[1] // tau
    sw = bp_ref.shape[2]
    d_s5 = nblk * cw
    tiles = sw // V7X_LANES
    ht = tiles // 2
    ctiles = d_s5 // V7X_LANES
    srows = nb * nk

    u_all = u_ref[...].reshape(srows * tau, d_s5).astype(F32)
    for c in range(ctiles):
        io_scr[c] = u_all[:, c * V7X_LANES:(c + 1) * V7X_LANES]
    u_steps = [jnp.concatenate([io_scr[c, pl.ds(j, srows, stride=tau), :] for c in range(ctiles)],
                               axis=1).astype(BF16) for j in range(tau)]

    def block_inputs(blk):
        return jnp.concatenate([u_steps[j][:, blk * cw:(blk + 1) * cw] for j in range(tau)], axis=1)

    for blk in range(nblk):
        half, q = divmod(blk, 2)
        v = jnp.dot(block_inputs(blk), bp_ref[blk], preferred_element_type=F32)
        for b in range(nb):
            for j in range(tiles):
                v_scr[q * tiles + j, pl.ds((half * nb + b) * seg, nk), :] = (
                    v[b * nk:(b + 1) * nk, j * V7X_LANES:(j + 1) * V7X_LANES])

    rows = 2 * nb
    ar = [[a_ref[0, q * ht + i] for i in range(ht)] for q in range(2)]
    ai = [[a_ref[1, q * ht + i] for i in range(ht)] for q in range(2)]
    sr = [[st_scr[q * tiles + i] for i in range(ht)] for q in range(2)]
    si = [[st_scr[q * tiles + ht + i] for i in range(ht)] for q in range(2)]
    for k in range(nk):
        for q in range(2):
            for i in range(ht):
                jr = q * tiles + i
                ji = q * tiles + ht + i
                sp_scr[jr, pl.ds(k, rows, stride=seg), :] = sr[q][i]
                sp_scr[ji, pl.ds(k, rows, stride=seg), :] = si[q][i]
                vr = v_scr[jr, pl.ds(k, rows, stride=seg), :]
                vi = v_scr[ji, pl.ds(k, rows, stride=seg), :]
                nr = ar[q][i] * sr[q][i] - ai[q][i] * si[q][i] + vr
                ni = ar[q][i] * si[q][i] + ai[q][i] * sr[q][i] + vi
                sr[q][i], si[q][i] = nr, ni
    for q in range(2):
        for i in range(ht):
            st_scr[q * tiles + i] = sr[q][i]
            st_scr[q * tiles + ht + i] = si[q][i]

    yb = []
    for blk in range(nblk):
        half, q = divmod(blk, 2)
        sp = jnp.concatenate(
            [jnp.concatenate([sp_scr[q * tiles + j, pl.ds((half * nb + b) * seg, nk), :]
                              for j in range(tiles)], axis=1) for b in range(nb)],
            axis=0).astype(BF16)
        yb.append(jnp.dot(sp, cp_ref[blk], preferred_element_type=F32)
                  + jnp.dot(block_inputs(blk), tp_ref[blk], preferred_element_type=F32))
    for j in range(tau):
        y = jnp.concatenate([yb[blk][:, j * cw:(j + 1) * cw] for blk in range(nblk)], axis=1)
        y = y + dd_ref[...] * u_steps[j].astype(F32)
        y = jax.nn.gelu(y)
        z = jnp.dot(y.astype(BF16), wglu_ref[...], preferred_element_type=F32) + bglu_ref[...]
        out = y * jax.nn.sigmoid(z)
        for c in range(ctiles):
            io_scr[c, pl.ds(j, srows, stride=tau), :] = out[:, c * V7X_LANES:(c + 1) * V7X_LANES]
    y_all = jnp.concatenate([io_scr[c] for c in range(ctiles)], axis=1)
    y_ref[...] = y_all.reshape(nb, nk * tau, d_s5).astype(y_ref.dtype)


def _s5(u_s5, nb, bp, cp, tp, a_tab, dd, wglu_bf, bglu):
    n, d_s5 = u_s5.shape
    seq = n // nb
    tau = S5_TAU
    nk = T_S5 // tau
    seg = nk + S5_PAD
    nblk, _, sw = bp.shape
    rows = 2 * nb
    assert rows == V7X_SUBLANES and nblk == 4

    def whole(a):
        return pl.BlockSpec(a.shape, lambda c: (0,) * a.ndim)

    body = functools.partial(_s5_body, nb=nb, nk=nk, seg=seg, tau=tau)
    out = pl.pallas_call(
        body,
        grid=(seq // T_S5,),
        in_specs=[pl.BlockSpec((nb, T_S5, d_s5), lambda c: (0, c, 0)),
                  whole(bp), whole(cp), whole(tp), whole(a_tab), whole(dd), whole(wglu_bf),
                  whole(bglu)],
        out_specs=pl.BlockSpec((nb, T_S5, d_s5), lambda c: (0, c, 0)),
        out_shape=jax.ShapeDtypeStruct((nb, seq, d_s5), BF16),
        scratch_shapes=[pltpu.VMEM((2 * sw // V7X_LANES, rows * seg, V7X_LANES), F32),
                        pltpu.VMEM((2 * sw // V7X_LANES, rows * seg, V7X_LANES), F32),
                        pltpu.VMEM((2 * sw // V7X_LANES, rows, V7X_LANES), F32),
                        pltpu.VMEM((d_s5 // V7X_LANES, nb * T_S5, V7X_LANES), F32)],
        compiler_params=_cparams(("arbitrary",)),
        name="s5",
    )(u_s5.reshape(nb, seq, d_s5), bp, cp, tp, a_tab, dd, wglu_bf, bglu)
    return out.reshape(n, d_s5)


def _s5_tables(a_re, a_im, b_re, b_im, c_re, c_im, log_dt, nb, tau):
    hp = lax.Precision.HIGHEST
    g, p = a_re.shape
    hch = b_re.shape[2]
    gpb = V7X_LANES // hch
    nblk = g // gpb
    lam_r, lam_i = a_re.astype(F32), a_im.astype(F32)
    dt = jnp.exp(log_dt.astype(F32))[:, None]
    mag = jnp.exp(lam_r * dt)
    ab_r = mag * jnp.cos(lam_i * dt)
    ab_i = mag * jnp.sin(lam_i * dt)
    den = lam_r * lam_r + lam_i * lam_i
    zr = ((ab_r - 1.0) * lam_r + ab_i * lam_i) / den
    zi = (ab_i * lam_r - (ab_r - 1.0) * lam_i) / den
    br_, bi_ = b_re.astype(F32), b_im.astype(F32)
    bb_r = zr[..., None] * br_ - zi[..., None] * bi_
    bb_i = zr[..., None] * bi_ + zi[..., None] * br_
    cr, ci = c_re.astype(F32), c_im.astype(F32)
    pr, pi = [jnp.ones_like(ab_r)], [jnp.zeros_like(ab_i)]
    for _ in range(tau):
        pr, pi = pr + [pr[-1] * ab_r - pi[-1] * ab_i], pi + [pr[-1] * ab_i + pi[-1] * ab_r]
    pw_r, pw_i = jnp.stack(pr), jnp.stack(pi)

    def blockdiag(x):
        r, c = x.shape[-2:]
        x = jnp.tile(x.reshape(tau, nblk, gpb * r, c), (1, 1, 1, gpb))
        same = (jnp.arange(gpb * r)[:, None] // r) == (jnp.arange(gpb * c)[None, :] // c)
        return jnp.where(same, x, 0.0)

    wr_ = jnp.stack([pr[tau - 1 - j] for j in range(tau)])[:, :, None, :]
    wi_ = jnp.stack([pi[tau - 1 - j] for j in range(tau)])[:, :, None, :]
    bt_r, bt_i = jnp.swapaxes(bb_r, 1, 2)[None], jnp.swapaxes(bb_i, 1, 2)[None]
    bp_r = blockdiag(wr_ * bt_r - wi_ * bt_i)
    bp_i = blockdiag(wr_ * bt_i + wi_ * bt_r)
    bp = jnp.concatenate([jnp.concatenate([bp_r[j], bp_i[j]], axis=-1) for j in range(tau)],
                         axis=1).astype(BF16)

    qr_, qi_ = pw_r[1:, :, :, None], pw_i[1:, :, :, None]
    ct_r, ct_i = jnp.swapaxes(cr, 1, 2)[None], jnp.swapaxes(ci, 1, 2)[None]
    cp_r = blockdiag(ct_r * qr_ - ct_i * qi_)
    cp_i = blockdiag(ct_r * qi_ + ct_i * qr_)
    cp = jnp.concatenate([jnp.concatenate([cp_r[i], -cp_i[i]], axis=1) for i in range(tau)],
                         axis=2).astype(BF16)

    ab_r_ = pw_r[:tau, :, :, None] * bb_r[None] - pw_i[:tau, :, :, None] * bb_i[None]
    ab_i_ = pw_r[:tau, :, :, None] * bb_i[None] + pw_i[:tau, :, :, None] * bb_r[None]
    kd = blockdiag(jnp.einsum('ghp,dgpe->dgeh', cr, ab_r_, precision=hp)
                   - jnp.einsum('ghp,dgpe->dgeh', ci, ab_i_, precision=hp))
    kzero = jnp.zeros_like(kd[0])
    tp = jnp.concatenate(
        [jnp.concatenate([kd[i - j] if i >= j else kzero for i in range(tau)], axis=2)
         for j in range(tau)], axis=1).astype(BF16)

    a_tab = jnp.stack([jnp.repeat(pw_r[tau].reshape(2, -1), nb, axis=0),
                       jnp.repeat(pw_i[tau].reshape(2, -1), nb, axis=0)])
    a_tab = a_tab.reshape(2, 2 * nb, -1, V7X_LANES).transpose(0, 2, 1, 3)
    return bp, cp, tp, a_tab


def _ret_body(q_ref, k_ref, v_ref, g_ref, cos_ref, sin_ref, dec_ref, qdec_ref, kdect_ref,
              cdec_ref, ms_ref, avg_ref, nw_ref, y_ref, st_scr, *, t_len, dh, scale):
    @pl.when(pl.program_id(0) == 0)
    def _():
        st_scr[...] = jnp.zeros_like(st_scr)

    nb = q_ref.shape[0]
    width = q_ref.shape[-1]
    heads = width // dh
    pairs = width // V7X_LANES
    reps = width // cos_ref.shape[-1]
    cos = jnp.concatenate([cos_ref[...]] * reps, axis=1)
    sin = jnp.concatenate([sin_ref[...]] * reps, axis=1)
    lane = lax.broadcasted_iota(jnp.int32, (t_len, width), 1)
    first = (lane % dh) < (dh // 2)

    def rot(x):
        x = x.astype(F32)
        partner = jnp.where(first, pltpu.roll(x, width - dh // 2, 1), pltpu.roll(x, dh // 2, 1))
        return x * cos + partner * sin

    def group_mean(x):
        hi = x.astype(BF16)
        lo = (x - hi.astype(F32)).astype(BF16)
        return (jnp.dot(hi, avg_ref[...], preferred_element_type=F32)
                + jnp.dot(lo, avg_ref[...], preferred_element_type=F32))

    zero_k = jnp.zeros((dh, t_len), BF16)
    zero_v = jnp.zeros((t_len, V7X_LANES), BF16)
    low_head = lax.broadcasted_iota(jnp.int32, (t_len, V7X_LANES), 1) < dh

    for b in range(nb):
        q = rot(q_ref[b])
        k = rot(k_ref[b]) * scale
        vb = v_ref[b]
        kt = k.T
        ktb = kt.astype(BF16)
        kbd = jnp.concatenate(
            [jnp.concatenate([ktb[h * dh:(h + 1) * dh] if hh == h else zero_k for hh in range(heads)],
                             axis=0) for h in range(heads)], axis=1)
        sc = jnp.dot(q.astype(BF16), kbd, preferred_element_type=F32) * dec_ref[...]
        vbd_rows = []
        for h in range(heads):
            p = h // (V7X_LANES // dh)
            vt = vb[:, p * V7X_LANES:(p + 1) * V7X_LANES]
            keep = low_head if h % (V7X_LANES // dh) == 0 else jnp.logical_not(low_head)
            vbd_rows.append(jnp.concatenate(
                [jnp.where(keep, vt, zero_v) if pp == p else zero_v for pp in range(pairs)], axis=1))
        vbd = jnp.concatenate(vbd_rows, axis=0)
        inner = jnp.dot(sc.astype(BF16), vbd, preferred_element_type=F32)
        qd = (q * qdec_ref[...]).astype(BF16)
        kdt = (kt * kdect_ref[...]).astype(BF16)
        crosses = []
        for p in range(pairs):
            ps = slice(p * V7X_LANES, (p + 1) * V7X_LANES)
            state = st_scr[b, p]
            crosses.append(jnp.dot(qd[:, ps], state.astype(BF16), preferred_element_type=F32))
            kv = jnp.dot(kdt[ps, :], vb[:, ps], preferred_element_type=F32)
            st_scr[b, p] = state * cdec_ref[:, ps] + kv * ms_ref[...]
        o = inner + jnp.concatenate(crosses, axis=1)
        dlt = o - group_mean(o)
        var = jnp.dot((dlt * dlt).astype(BF16), avg_ref[...], preferred_element_type=F32)
        on = dlt * lax.rsqrt(var + EPS) * nw_ref[...]
        y_ref[b] = (jax.nn.silu(g_ref[b].astype(F32)) * on).astype(y_ref.dtype)


def _retention(u3, d_ret, norm_w):
    nb, seq, _ = u3.shape
    dh = d_ret // RET_HEADS
    half = dh // 2
    t_len = T_RET
    lg = jnp.log(1.0 - 2.0 ** (-5.0 - jnp.arange(RET_HEADS, dtype=F32)))
    t = jnp.arange(t_len, dtype=F32)
    diff = t[:, None] - t[None, :]
    dec = jnp.where(diff >= 0, jnp.exp(lg[:, None, None] * jnp.maximum(diff, 0.0)), 0.0)
    dec_all = dec.transpose(1, 0, 2).reshape(t_len, RET_HEADS * t_len)
    qdec = jnp.repeat(jnp.exp(lg[:, None] * (t + 1.0)[None, :]).T, dh, axis=1)
    kdect = jnp.repeat(jnp.exp(lg[:, None] * (t_len - 1 - t)[None, :]), dh, axis=0)
    cdec = jnp.repeat(jnp.exp(lg * t_len), dh)[None, :]
    head_of_lane = jnp.arange(d_ret) // dh
    pair_head = jnp.arange(V7X_LANES) // dh
    mask_s = (pair_head[:, None] == pair_head[None, :]).astype(F32)
    avg = (head_of_lane[:, None] == head_of_lane[None, :]).astype(F32) / dh
    assert dh & (dh - 1) == 0
    avg = avg.astype(BF16)
    inv = ROPE_BASE ** (-jnp.arange(half, dtype=F32) / half)
    ang = jnp.arange(seq, dtype=F32)[:, None] * inv[None, :]
    reps = V7X_LANES // dh
    cos_t = jnp.tile(jnp.cos(ang), (1, 2 * reps))
    sin_t = jnp.tile(jnp.concatenate([-jnp.sin(ang), jnp.sin(ang)], axis=1), (1, reps))

    def col(j):
        return pl.BlockSpec((nb, t_len, d_ret), lambda c: (0, c, j))

    def whole(a):
        return pl.BlockSpec(a.shape, lambda c: (0,) * a.ndim)

    body = functools.partial(_ret_body, t_len=t_len, dh=dh, scale=dh ** -0.5)
    return pl.pallas_call(
        body,
        grid=(seq // t_len,),
        in_specs=[col(0), col(1), col(2), col(3),
                  pl.BlockSpec((t_len, V7X_LANES), lambda c: (c, 0)),
                  pl.BlockSpec((t_len, V7X_LANES), lambda c: (c, 0)),
                  whole(dec_all), whole(qdec), whole(kdect), whole(cdec),
                  whole(mask_s), whole(avg), whole(norm_w)],
        out_specs=pl.BlockSpec((nb, t_len, d_ret), lambda c: (0, c, 0)),
        out_shape=jax.ShapeDtypeStruct((nb, seq, d_ret), BF16),
        scratch_shapes=[pltpu.VMEM((nb, d_ret // V7X_LANES, V7X_LANES, V7X_LANES), F32)],
        compiler_params=_cparams(("arbitrary",)),
        name="retention",
    )(u3, u3, u3, u3, cos_t, sin_t, dec_all, qdec, kdect, cdec, mask_s, avg, norm_w)


def _route_body(ys5_ref, yret_ref, x_ref, wo_ref, n2_ref, wr_ref, br_ref, tri_ref,
                h_ref, route_ref, route_t_ref, cnt_ref, carry_scr):
    @pl.when(pl.program_id(0) == 0)
    def _():
        carry_scr[...] = jnp.zeros_like(carry_scr)

    d_s5 = ys5_ref.shape[1]
    h = (x_ref[...]
         + jnp.dot(ys5_ref[...], wo_ref[0:d_s5], preferred_element_type=F32)
         + jnp.dot(yret_ref[...], wo_ref[d_s5:], preferred_element_type=F32))
    h_ref[...] = h
    hn = _rms(h, n2_ref[...])
    hi = hn.astype(BF16)
    lo = (hn - hi.astype(F32)).astype(BF16)
    p_hi = jnp.dot(hi, wr_ref[...], preferred_element_type=F32)
    p_lo = jnp.dot(lo, wr_ref[:, 0:ROUTE_LANES], preferred_element_type=F32)
    logits = p_hi[:, 0:ROUTE_LANES] + p_hi[:, ROUTE_LANES:] + p_lo + br_ref[...]
    tm = logits.shape[0]
    lane = lax.broadcasted_iota(jnp.int32, (tm, ROUTE_LANES), 1)
    lanef = lane.astype(F32)
    neg = -jnp.inf
    big = float(ROUTE_LANES)
    gl = jnp.where(lane < N_GROUPS, logits, neg)
    gmax = jnp.max(gl, axis=-1, keepdims=True)
    gidx = jnp.min(jnp.where(gl == gmax, lanef, big), axis=-1, keepdims=True)
    g_w = 1.0 / jnp.sum(jnp.exp(gl - gmax), axis=-1, keepdims=True)
    lo = N_GROUPS + EXPERTS_PER_GROUP * gidx
    el = jnp.where((lanef >= lo) & (lanef < lo + EXPERTS_PER_GROUP), logits, neg)
    v0 = jnp.max(el, axis=-1, keepdims=True)
    i0 = jnp.min(jnp.where(el == v0, lanef, big), axis=-1, keepdims=True)
    el2 = jnp.where(lanef == i0, neg, el)
    v1 = jnp.max(el2, axis=-1, keepdims=True)
    i1 = jnp.min(jnp.where(el2 == v1, lanef, big), axis=-1, keepdims=True)
    e = jnp.exp(v1 - v0)
    den = 1.0 + e
    w0 = (1.0 / den) * g_w
    w1 = (e / den) * g_w
    sel0 = lanef == i0
    sel1 = lanef == i1
    onehot = jnp.where(sel0 | sel1, 1.0, 0.0)
    before = jnp.dot(tri_ref[...], onehot.astype(BF16), preferred_element_type=F32) + carry_scr[...]
    r0 = jnp.sum(jnp.where(sel0, before, 0.0), axis=-1, keepdims=True)
    r1 = jnp.sum(jnp.where(sel1, before, 0.0), axis=-1, keepdims=True)
    carry_scr[...] += jnp.sum(onehot, axis=0, keepdims=True)
    cnt_ref[...] = carry_scr[...]
    rec = jnp.zeros((tm, ROUTE_LANES), F32)
    for j, val in enumerate((i0 - N_GROUPS, i1 - N_GROUPS, w0, w1, r0, r1)):
        rec = jnp.where(lane == j, val, rec)
    route_ref[...] = rec
    route_t_ref[...] = rec.T[0:ROUTE_FIELDS]


def _route(ys5, yret, x2, wo_bf, n2, wr, br):
    n, d = x2.shape
    d_s5 = ys5.shape[1]
    d_ret = yret.shape[1]
    tm = TM_ROUTE
    tri = (jnp.arange(tm)[:, None] > jnp.arange(tm)[None, :]).astype(BF16)
    return pl.pallas_call(
        _route_body,
        grid=(n // tm,),
        in_specs=[pl.BlockSpec((tm, d_s5), lambda i: (i, 0)),
                  pl.BlockSpec((tm, d_ret), lambda i: (i, 0)),
                  pl.BlockSpec((tm, d), lambda i: (i, 0)),
                  pl.BlockSpec((d_s5 + d_ret, d), lambda i: (0, 0)),
                  pl.BlockSpec((1, d), lambda i: (0, 0)),
                  pl.BlockSpec((d, 2 * ROUTE_LANES), lambda i: (0, 0)),
                  pl.BlockSpec((1, ROUTE_LANES), lambda i: (0, 0)),
                  pl.BlockSpec((tm, tm), lambda i: (0, 0))],
        out_specs=[pl.BlockSpec((tm, d), lambda i: (i, 0)),
                   pl.BlockSpec((tm, ROUTE_LANES), lambda i: (i, 0)),
                   pl.BlockSpec((ROUTE_FIELDS, tm), lambda i: (0, i)),
                   pl.BlockSpec((1, ROUTE_LANES), lambda i: (0, 0))],
        out_shape=[jax.ShapeDtypeStruct((n, d), F32),
                   jax.ShapeDtypeStruct((n, ROUTE_LANES), F32),
                   jax.ShapeDtypeStruct((ROUTE_FIELDS, n), F32),
                   jax.ShapeDtypeStruct((1, ROUTE_LANES), F32)],
        scratch_shapes=[pltpu.VMEM((1, ROUTE_LANES), F32)],
        compiler_params=_cparams(("arbitrary",)),
        name="outproj_route",
    )(ys5, yret, x2, wo_bf, n2, wr, br, tri)


def _dispatch_body(pends_ref, padded_ref, dest_ref, h_ref, n2_ref, xs_ref,
                   hn_scr, zero_scr, sem, zsem):
    tm = h_ref.shape[0]

    @pl.when(pl.program_id(0) == 0)
    def _():
        zero_scr[...] = jnp.zeros_like(zero_scr)

        def zero_copy(e):
            first = pl.multiple_of((pends_ref[e] - R_BLK) * ROW_TILES, R_BLK * ROW_TILES)
            return pltpu.make_async_copy(zero_scr, xs_ref.at[pl.ds(first, R_BLK * ROW_TILES)], zsem)

        def zstart(e, carry):
            @pl.when(padded_ref[e] > 0)
            def _():
                zero_copy(e).start()
            return carry

        def zwait(e, carry):
            @pl.when(padded_ref[e] > 0)
            def _():
                zero_copy(e).wait()
            return carry

        lax.fori_loop(0, pends_ref.shape[0], zstart, 0)
        lax.fori_loop(0, pends_ref.shape[0], zwait, 0)

    step = pl.program_id(0)
    slot = step % 2
    buf = hn_scr.at[slot]
    _rows_to_tiles(buf, _rms(h_ref[...], n2_ref[...]))

    def start(r, carry):
        for k in range(TOP_K):
            pltpu.make_async_copy(_tile_rows(buf, r), _tile_rows(xs_ref, dest_ref[k * tm + r]),
                                  sem.at[slot]).start(priority=k)
        return carry

    lax.fori_loop(0, tm, start, 0, unroll=DMA_UNROLL)

    def drain(s):
        for k in range(TOP_K):
            pltpu.make_async_copy(hn_scr.at[s], xs_ref.at[pl.ds(0, tm * ROW_TILES)], sem.at[s]).wait()

    @pl.when(step > 0)
    def _():
        drain(1 - slot)

    @pl.when(step == pl.num_programs(0) - 1)
    def _():
        drain(slot)


def _dispatch(pends, padded, dest, h, n2, p_rows):
    n, d = h.shape
    tm = TM_ROWS
    grid_spec = pltpu.PrefetchScalarGridSpec(
        num_scalar_prefetch=2,
        grid=(n // tm,),
        in_specs=[pl.BlockSpec((TOP_K * tm,), lambda i, pe, pa: (i,), memory_space=pltpu.SMEM),
                  pl.BlockSpec((tm, d), lambda i, pe, pa: (i, 0)),
                  pl.BlockSpec((1, d), lambda i, pe, pa: (0, 0))],
        out_specs=pl.BlockSpec(memory_space=pl.ANY),
        scratch_shapes=[pltpu.VMEM((2, tm * ROW_TILES, V7X_LANES), F32),
                        pltpu.VMEM((R_BLK * ROW_TILES, V7X_LANES), F32),
                        pltpu.SemaphoreType.DMA((2,)), pltpu.SemaphoreType.DMA(())])
    assert d == ROW_TILES * V7X_LANES
    return pl.pallas_call(
        _dispatch_body,
        grid_spec=grid_spec,
        out_shape=jax.ShapeDtypeStruct((p_rows * ROW_TILES, V7X_LANES), F32),
        compiler_params=_cparams(("arbitrary",)),
        name="dispatch",
    )(pends, padded, dest, h, n2)


def _expert_body(be_ref, nu_ref, xs_ref, wg_ref, wu_ref, wd_ref, ys_ref, wg_s, wu_s, wd_s):
    j = pl.program_id(0)

    @pl.when(j < nu_ref[0])
    def _():
        changed = jnp.logical_or(j == 0, be_ref[j] != be_ref[jnp.maximum(j - 1, 0)])

        @pl.when(changed)
        def _():
            wg_s[...] = wg_ref[...].astype(BF16)
            wu_s[...] = wu_ref[...].astype(BF16)
            wd_s[...] = wd_ref[...].astype(BF16)

        x = _tiles_to_rows(xs_ref, R_BLK).astype(BF16)
        gate = jnp.dot(x, wg_s[...], preferred_element_type=F32)
        up = jnp.dot(x, wu_s[...], preferred_element_type=F32)
        hid = (jax.nn.silu(gate) * up).astype(BF16)
        _rows_to_tiles(ys_ref, jnp.dot(hid, wd_s[...], preferred_element_type=F32))

    @pl.when(j >= nu_ref[0])
    def _():
        ys_ref[...] = jnp.zeros_like(ys_ref)


def _experts(block_e, n_used, xs, w_gate, w_up, w_down):
    d, d_e = w_gate.shape[1:]
    blk_rows = R_BLK * ROW_TILES
    nblk = xs.shape[0] // blk_rows

    def row_map(j, be, nu):
        return (jnp.maximum(jnp.minimum(j, nu[0] - 1), 0), 0)

    def w_map(j, be, nu):
        return (be[j], 0, 0)

    grid_spec = pltpu.PrefetchScalarGridSpec(
        num_scalar_prefetch=2,
        grid=(nblk,),
        in_specs=[pl.BlockSpec((blk_rows, V7X_LANES), row_map),
                  pl.BlockSpec((None, d, d_e), w_map),
                  pl.BlockSpec((None, d, d_e), w_map),
                  pl.BlockSpec((None, d_e, d), w_map)],
        out_specs=pl.BlockSpec((blk_rows, V7X_LANES), lambda j, be, nu: (j, 0)),
        scratch_shapes=[pltpu.VMEM((d, d_e), BF16), pltpu.VMEM((d, d_e), BF16),
                        pltpu.VMEM((d_e, d), BF16)])
    return pl.pallas_call(
        _expert_body,
        grid_spec=grid_spec,
        out_shape=jax.ShapeDtypeStruct(xs.shape, F32),
        compiler_params=_cparams(("arbitrary",)),
        name="experts",
    )(block_e, n_used, xs, w_gate, w_up, w_down)


def _combine_body(dest_ref, dest_next_ref, h_ref, route_ref, fw_ref, ys_ref, out_ref, g_scr, sem):
    tm = h_ref.shape[0]
    step = pl.program_id(0)
    slot = step % 2

    def issue(d_ref, s):
        def start(r, carry):
            for k in range(TOP_K):
                pltpu.make_async_copy(_tile_rows(ys_ref, d_ref[k * tm + r]),
                                      _tile_rows(g_scr.at[s, k], r), sem.at[s]).start(priority=k)
            return carry

        lax.fori_loop(0, tm, start, 0, unroll=DMA_UNROLL)

    @pl.when(step == 0)
    def _():
        issue(dest_ref, slot)

    @pl.when(step < pl.num_programs(0) - 1)
    def _():
        issue(dest_next_ref, 1 - slot)

    for k in range(TOP_K):
        pltpu.make_async_copy(ys_ref.at[pl.ds(0, tm * ROW_TILES)], g_scr.at[slot, k],
                              sem.at[slot]).wait()
    rec = route_ref[...]
    hh = (h_ref[...] + rec[:, 2:3] * _tiles_to_rows(g_scr.at[slot, 0], tm)
          + rec[:, 3:4] * _tiles_to_rows(g_scr.at[slot, 1], tm))
    out_ref[...] = _rms(hh, fw_ref[...])


def _combine(dest, h, route, fw, ys):
    n, d = h.shape
    tm = TM_ROWS
    steps = n // tm
    return pl.pallas_call(
        _combine_body,
        grid=(steps,),
        in_specs=[pl.BlockSpec((TOP_K * tm,), lambda i: (i,), memory_space=pltpu.SMEM),
                  pl.BlockSpec((TOP_K * tm,), lambda i: (jnp.minimum(i + 1, steps - 1),),
                               memory_space=pltpu.SMEM),
                  pl.BlockSpec((tm, d), lambda i: (i, 0)),
                  pl.BlockSpec((tm, ROUTE_LANES), lambda i: (i, 0)),
                  pl.BlockSpec((1, d), lambda i: (0, 0)),
                  pl.BlockSpec(memory_space=pl.ANY)],
        out_specs=pl.BlockSpec((tm, d), lambda i: (i, 0)),
        out_shape=jax.ShapeDtypeStruct((n, d), F32),
        scratch_shapes=[pltpu.VMEM((2, TOP_K, tm * ROW_TILES, V7X_LANES), F32),
                        pltpu.SemaphoreType.DMA((2,))],
        compiler_params=_cparams(("arbitrary",)),
        name="combine",
    )(dest, dest, h, route, fw, ys)


def _plan(route_t, counts, n_experts, n_blocks):
    eid = route_t[0:TOP_K].astype(jnp.int32)
    rank = route_t[4:4 + TOP_K].astype(jnp.int32)
    cnt = counts[0, N_GROUPS:N_GROUPS + n_experts].astype(jnp.int32)
    padded = (cnt + R_BLK - 1) // R_BLK * R_BLK
    pends = jnp.cumsum(padded)
    pstart = pends - padded
    dest = pstart[eid] + rank
    n_used = pends[-1] // R_BLK
    blk = jnp.minimum(jnp.arange(n_blocks, dtype=jnp.int32), n_used - 1)
    block_e = jnp.minimum(jnp.sum(pends[None, :] <= (blk * R_BLK)[:, None], axis=1), n_experts - 1)
    return (dest.astype(jnp.int32), block_e.astype(jnp.int32), n_used.reshape(1).astype(jnp.int32),
            pends.astype(jnp.int32), padded.astype(jnp.int32))


def _layer(h3, norm1_w, w_in, s5_a_re, s5_a_im, s5_b_re, s5_b_im, s5_c_re, s5_c_im, s5_d,
           s5_log_dt, s5_w_glu, s5_b_glu, ret_norm_w, w_out, norm2_w, router_group_w,
           router_group_b, router_expert_w, router_expert_b, moe_w_gate, moe_w_up, moe_w_down,
           out_norm_w):
    nb, seq, d = h3.shape
    n = nb * seq
    d_s5 = s5_d.shape[0]
    d_ret = ret_norm_w.shape[0]
    n_experts = moe_w_gate.shape[0]
    x2 = h3.reshape(n, d)

    u_s5, u_ret = _inproj(x2, norm1_w.reshape(1, d), w_in.astype(BF16), d_s5)

    bp, cp, tp, a_tab = _s5_tables(s5_a_re, s5_a_im, s5_b_re, s5_b_im, s5_c_re, s5_c_im,
                                   s5_log_dt, nb, S5_TAU)
    y_s5 = _s5(u_s5, nb, bp, cp, tp, a_tab, s5_d.reshape(1, d_s5).astype(F32),
               s5_w_glu.astype(BF16), s5_b_glu.reshape(1, d_s5).astype(F32))
    y_ret = _retention(u_ret.reshape(nb, seq, -1), d_ret, ret_norm_w.reshape(1, d_ret).astype(F32))

    n_route = N_GROUPS + n_experts
    wr = jnp.zeros((d, ROUTE_LANES), F32).at[:, :n_route].set(
        jnp.concatenate([router_group_w, router_expert_w], axis=1).astype(F32))
    br = jnp.zeros((1, ROUTE_LANES), F32).at[0, :n_route].set(
        jnp.concatenate([router_group_b, router_expert_b]).astype(F32))
    wr_hi = wr.astype(BF16)
    wr = jnp.concatenate([wr_hi, (wr - wr_hi.astype(F32)).astype(BF16)], axis=1)
    h, route, route_t, counts = _route(y_s5.reshape(n, d_s5), y_ret.reshape(n, d_ret), x2,
                                       w_out.astype(BF16), norm2_w.reshape(1, d), wr, br)

    n_blocks = (n * TOP_K) // R_BLK + n_experts
    dest, block_e, n_used, pends, padded = _plan(route_t, counts, n_experts, n_blocks)
    dest = dest.reshape(TOP_K, n // TM_ROWS, TM_ROWS).transpose(1, 0, 2).reshape(-1)
    xs = _dispatch(pends, padded, dest, h, norm2_w.reshape(1, d), n_blocks * R_BLK)
    ys = _experts(block_e, n_used, xs, moe_w_gate, moe_w_up, moe_w_down)
    out = _combine(dest, h, route, out_norm_w.reshape(1, d), ys)
    return out.reshape(nb, seq, d)


def kernel(x, norm1_w, w_in, s5_a_re, s5_a_im, s5_b_re, s5_b_im, s5_c_re, s5_c_im, s5_d, s5_log_dt, s5_w_glu, s5_b_glu, ret_norm_w, w_out, norm2_w, router_group_w, router_group_b, router_expert_w, router_expert_b, moe_w_gate, moe_w_up, moe_w_down, final_norm_w):
    depth = norm1_w.shape[0]
    assert depth == 1, "the fused final norm assumes a single layer"
    l = 0
    return _layer(x, norm1_w[l], w_in[l], s5_a_re[l], s5_a_im[l], s5_b_re[l], s5_b_im[l],
                  s5_c_re[l], s5_c_im[l], s5_d[l], s5_log_dt[l], s5_w_glu[l], s5_b_glu[l],
                  ret_norm_w[l], w_out[l], norm2_w[l], router_group_w[l], router_group_b[l],
                  router_expert_w[l], router_expert_b[l], moe_w_gate[l], moe_w_up[l],
                  moe_w_down[l], final_norm_w)
```

```python
import functools

import jax
import jax.numpy as jnp
from jax import lax
from jax.experimental import pallas as pl
from jax.experimental.pallas import tpu as pltpu

F32 = jnp.float32
BF16 = jnp.bfloat16

EPS = 1e-6
ROPE_BASE = 10000.0
RET_HEADS = 8
TOP_K = 2
N_GROUPS = 4
EXPERTS_PER_GROUP = 8

V7X_LANES = 128
V7X_SUBLANES = 8
V7X_VMEM_LIMIT = 56 * 1024 * 1024

TM_PROJ = 512
T_S5 = 512
S5_TAU = 8
S5_PAD = 8
T_RET = 128
TM_ROUTE = 512
TM_ROWS = 256
R_BLK = 512
GATHER_RING = 3
ROUTE_LANES = 128
ROUTE_FIELDS = 8


def _rms(x, w):
    return x * lax.rsqrt(jnp.mean(x * x, axis=-1, keepdims=True) + EPS) * w


def _cparams(sem):
    return pltpu.CompilerParams(dimension_semantics=sem, vmem_limit_bytes=V7X_VMEM_LIMIT)


ROW_TILES = V7X_SUBLANES


def _rows_to_tiles(ref, val):
    rows = val.shape[0]
    for s in range(ROW_TILES):
        ref[pl.ds(s, rows, stride=ROW_TILES), :] = val[:, s * V7X_LANES:(s + 1) * V7X_LANES]


def _tiles_to_rows(ref, rows):
    return jnp.concatenate(
        [ref[pl.ds(s, rows, stride=ROW_TILES), :] for s in range(ROW_TILES)], axis=1)


def _tile_rows(ref, row):
    return ref.at[pl.ds(pl.multiple_of(row * ROW_TILES, ROW_TILES), ROW_TILES)]


def _inproj_body(x_ref, nw_ref, w_ref, us5_ref, uret_ref):
    xn = _rms(x_ref[...], nw_ref[...]).astype(BF16)
    u = jnp.dot(xn, w_ref[...], preferred_element_type=F32)
    d_s5 = us5_ref.shape[1]
    us5_ref[...] = u[:, :d_s5].astype(us5_ref.dtype)
    uret_ref[...] = u[:, d_s5:].astype(uret_ref.dtype)


def _inproj(x2, nw, w_bf, d_s5):
    n, d = x2.shape
    d_in = w_bf.shape[1]
    return pl.pallas_call(
        _inproj_body,
        grid=(n // TM_PROJ,),
        in_specs=[pl.BlockSpec((TM_PROJ, d), lambda i: (i, 0)),
                  pl.BlockSpec((1, d), lambda i: (0, 0)),
                  pl.BlockSpec((d, d_in), lambda i: (0, 0))],
        out_specs=[pl.BlockSpec((TM_PROJ, d_s5), lambda i: (i, 0)),
                   pl.BlockSpec((TM_PROJ, d_in - d_s5), lambda i: (i, 0))],
        out_shape=[jax.ShapeDtypeStruct((n, d_s5), BF16),
                   jax.ShapeDtypeStruct((n, d_in - d_s5), BF16)],
        compiler_params=_cparams(("arbitrary",)),
        name="inproj",
    )(x2, nw, w_bf)


def _s5_body(u_ref, bp_ref, cp_ref, tp_ref, a_ref, dd_ref, wglu_ref, bglu_ref, y_ref,
             v_scr, sp_scr, st_scr, io_scr, *, nb, nk, seg, tau):
    @pl.when(pl.program_id(0) == 0)
    def _():
        st_scr[...] = jnp.zeros_like(st_scr)

    nblk = bp_ref.shape[0]
    cw = bp_ref.shape[1] // tau
    sw = bp_ref.shape[2]
    d_s5 = nblk * cw
    tiles = sw // V7X_LANES
    ht = tiles // 2
    ctiles = d_s5 // V7X_LANES
    srows = nb * nk

    u_all = u_ref[...].reshape(srows * tau, d_s5).astype(F32)
    for c in range(ctiles):
        io_scr[c] = u_all[:, c * V7X_LANES:(c + 1) * V7X_LANES]
    u_steps = [jnp.concatenate([io_scr[c, pl.ds(j, srows, stride=tau), :] for c in range(ctiles)],
                               axis=1).astype(BF16) for j in range(tau)]

    def block_inputs(blk):
        return jnp.concatenate([u_steps[j][:, blk * cw:(blk + 1) * cw] for j in range(tau)], axis=1)

    for blk in range(nblk):
        half, q = divmod(blk, 2)
        v = jnp.dot(block_inputs(blk), bp_ref[blk], preferred_element_type=F32)
        for b in range(nb):
            for j in range(tiles):
                v_scr[q * tiles + j, pl.ds((half * nb + b) * seg, nk), :] = (
                    v[b * nk:(b + 1) * nk, j * V7X_LANES:(j + 1) * V7X_LANES])

    rows = 2 * nb
    ar = [[a_ref[0, q * ht + i] for i in range(ht)] for q in range(2)]
    ai = [[a_ref[1, q * ht + i] for i in range(ht)] for q in range(2)]
    sr = [[st_scr[q * tiles + i] for i in range(ht)] for q in range(2)]
    si = [[st_scr[q * tiles + ht + i] for i in range(ht)] for q in range(2)]
    for k in range(nk):
        for q in range(2):
            for i in range(ht):
                jr = q * tiles + i
                ji = q * tiles + ht + i
                sp_scr[jr, pl.ds(k, rows, stride=seg), :] = sr[q][i]
                sp_scr[ji, pl.ds(k, rows, stride=seg), :] = si[q][i]
                vr = v_scr[jr, pl.ds(k, rows, stride=seg), :]
                vi = v_scr[ji, pl.ds(k, rows, stride=seg), :]
                nr = ar[q][i] * sr[q][i] - ai[q][i] * si[q][i] + vr
                ni = ar[q][i] * si[q][i] + ai[q][i] * sr[q][i] + vi
                sr[q][i], si[q][i] = nr, ni
    for q in range(2):
        for i in range(ht):
            st_scr[q * tiles + i] = sr[q][i]
            st_scr[q * tiles + ht + i] = si[q][i]

    yb = []
    for blk in range(nblk):
        half, q = divmod(blk, 2)
        sp = jnp.concatenate(
            [jnp.concatenate([sp_scr[q * tiles + j, pl.ds((half * nb + b) * seg, nk), :]
                              for j in range(tiles)], axis=1) for b in range(nb)],
            axis=0).astype(BF16)
        yb.append(jnp.dot(sp, cp_ref[blk], preferred_element_type=F32)
                  + jnp.dot(block_inputs(blk), tp_ref[blk], preferred_element_type=F32))
    for j in range(tau):
        y = jnp.concatenate([yb[blk][:, j * cw:(j + 1) * cw] for blk in range(nblk)], axis=1)
        y = y + dd_ref[...] * u_steps[j].astype(F32)
        y = jax.nn.gelu(y)
        z = jnp.dot(y.astype(BF16), wglu_ref[...], preferred_element_type=F32) + bglu_ref[...]
        out = y * jax.nn.sigmoid(z)
        for c in range(ctiles):
            io_scr[c, pl.ds(j, srows, stride=tau), :] = out[:, c * V7X_LANES:(c + 1) * V7X_LANES]
    y_all = jnp.concatenate([io_scr[c] for c in range(ctiles)], axis=1)
    y_ref[...] = y_all.reshape(nb, nk * tau, d_s5).astype(y_ref.dtype)


def _s5(u_s5, nb, bp, cp, tp, a_tab, dd, wglu_bf, bglu):
    n, d_s5 = u_s5.shape
    seq = n // nb
    tau = S5_TAU
    nk = T_S5 // tau
    seg = nk + S5_PAD
    nblk, _, sw = bp.shape
    rows = 2 * nb
    assert rows == V7X_SUBLANES and nblk == 4

    def whole(a):
        return pl.BlockSpec(a.shape, lambda c: (0,) * a.ndim)

    body = functools.partial(_s5_body, nb=nb, nk=nk, seg=seg, tau=tau)
    out = pl.pallas_call(
        body,
        grid=(seq // T_S5,),
        in_specs=[pl.BlockSpec((nb, T_S5, d_s5), lambda c: (0, c, 0)),
                  whole(bp), whole(cp), whole(tp), whole(a_tab), whole(dd), whole(wglu_bf),
                  whole(bglu)],
        out_specs=pl.BlockSpec((nb, T_S5, d_s5), lambda c: (0, c, 0)),
        out_shape=jax.ShapeDtypeStruct((nb, seq, d_s5), BF16),
        scratch_shapes=[pltpu.VMEM((2 * sw // V7X_LANES, rows * seg, V7X_LANES), F32),
                        pltpu.VMEM((2 * sw // V7X_LANES, rows * seg, V7X_LANES), F32),
                        pltpu.VMEM((2 * sw // V7X_LANES, rows, V7X_LANES), F32),
                        pltpu.VMEM((d_s5 // V7X_LANES, nb * T_S5, V7X_LANES), F32)],
        compiler_params=_cparams(("arbitrary",)),
        name="s5",
    )(u_s5.reshape(nb, seq, d_s5), bp, cp, tp, a_tab, dd, wglu_bf, bglu)
    return out.reshape(n, d_s5)


def _s5_tables(a_re, a_im, b_re, b_im, c_re, c_im, log_dt, nb, tau):
    hp = lax.Precision.HIGHEST
    g, p = a_re.shape
    hch = b_re.shape[2]
    gpb = V7X_LANES // hch
    nblk = g // gpb
    lam_r, lam_i = a_re.astype(F32), a_im.astype(F32)
    dt = jnp.exp(log_dt.astype(F32))[:, None]
    mag = jnp.exp(lam_r * dt)
    ab_r = mag * jnp.cos(lam_i * dt)
    ab_i = mag * jnp.sin(lam_i * dt)
    den = lam_r * lam_r + lam_i * lam_i
    zr = ((ab_r - 1.0) * lam_r + ab_i * lam_i) / den
    zi = (ab_i * lam_r - (ab_r - 1.0) * lam_i) / den
    br_, bi_ = b_re.astype(F32), b_im.astype(F32)
    bb_r = zr[..., None] * br_ - zi[..., None] * bi_
    bb_i = zr[..., None] * bi_ + zi[..., None] * br_
    cr, ci = c_re.astype(F32), c_im.astype(F32)
    pr, pi = [jnp.ones_like(ab_r)], [jnp.zeros_like(ab_i)]
    for _ in range(tau):
        pr, pi = pr + [pr[-1] * ab_r - pi[-1] * ab_i], pi + [pr[-1] * ab_i + pi[-1] * ab_r]
    pw_r, pw_i = jnp.stack(pr), jnp.stack(pi)

    def blockdiag(x):
        r, c = x.shape[-2:]
        x = jnp.tile(x.reshape(tau, nblk, gpb * r, c), (1, 1, 1, gpb))
        same = (jnp.arange(gpb * r)[:, None] // r) == (jnp.arange(gpb * c)[None, :] // c)
        return jnp.where(same, x, 0.0)

    wr_ = jnp.stack([pr[tau - 1 - j] for j in range(tau)])[:, :, None, :]
    wi_ = jnp.stack([pi[tau - 1 - j] for j in range(tau)])[:, :, None, :]
    bt_r, bt_i = jnp.swapaxes(bb_r, 1, 2)[None], jnp.swapaxes(bb_i, 1, 2)[None]
    bp_r = blockdiag(wr_ * bt_r - wi_ * bt_i)
    bp_i = blockdiag(wr_ * bt_i + wi_ * bt_r)
    bp = jnp.concatenate([jnp.concatenate([bp_r[j], bp_i[j]], axis=-1) for j in range(tau)],
                         axis=1).astype(BF16)

    qr_, qi_ = pw_r[1:, :, :, None], pw_i[1:, :, :, None]
    ct_r, ct_i = jnp.swapaxes(cr, 1, 2)[None], jnp.swapaxes(ci, 1, 2)[None]
    cp_r = blockdiag(ct_r * qr_ - ct_i * qi_)
    cp_i = blockdiag(ct_r * qi_ + ct_i * qr_)
    cp = jnp.concatenate([jnp.concatenate([cp_r[i], -cp_i[i]], axis=1) for i in range(tau)],
                         axis=2).astype(BF16)

    ab_r_ = pw_r[:tau, :, :, None] * bb_r[None] - pw_i[:tau, :, :, None] * bb_i[None]
    ab_i_ = pw_r[:tau, :, :, None] * bb_i[None] + pw_i[:tau, :, :, None] * bb_r[None]
    kd = blockdiag(jnp.einsum('ghp,dgpe->dgeh', cr, ab_r_, precision=hp)
                   - jnp.einsum('ghp,dgpe->dgeh', ci, ab_i_, precision=hp))
    kzero = jnp.zeros_like(kd[0])
    tp = jnp.concatenate(
        [jnp.concatenate([kd[i - j] if i >= j else kzero for i in range(tau)], axis=2)
         for j in range(tau)], axis=1).astype(BF16)

    a_tab = jnp.stack([jnp.repeat(pw_r[tau].reshape(2, -1), nb, axis=0),
                       jnp.repeat(pw_i[tau].reshape(2, -1), nb, axis=0)])
    a_tab = a_tab.reshape(2, 2 * nb, -1, V7X_LANES).transpose(0, 2, 1, 3)
    return bp, cp, tp, a_tab


def _ret_body(q_ref, k_ref, v_ref, g_ref, cos_ref, sin_ref, dec_ref, qdec_ref, kdect_ref,
              cdec_ref, ms_ref, avg_ref, nw_ref, y_ref, st_scr, *, t_len, dh, scale):
    @pl.when(pl.program_id(0) == 0)
    def _():
        st_scr[...] = jnp.zeros_like(st_scr)

    nb = q_ref.shape[0]
    width = q_ref.shape[-1]
    heads = width // dh
    pairs = width // V7X_LANES
    reps = width // cos_ref.shape[-1]
    cos = jnp.concatenate([cos_ref[...]] * reps, axis=1)
    sin = jnp.concatenate([sin_ref[...]] * reps, axis=1)
    lane = lax.broadcasted_iota(jnp.int32, (t_len, width), 1)
    first = (lane % dh) < (dh // 2)

    def rot(x):
        x = x.astype(F32)
        partner = jnp.where(first, pltpu.roll(x, width - dh // 2, 1), pltpu.roll(x, dh // 2, 1))
        return x * cos + partner * sin

    def group_mean(x):
        hi = x.astype(BF16)
        lo = (x - hi.astype(F32)).astype(BF16)
        return (jnp.dot(hi, avg_ref[...], preferred_element_type=F32)
                + jnp.dot(lo, avg_ref[...], preferred_element_type=F32))

    zero_k = jnp.zeros((dh, t_len), BF16)
    zero_v = jnp.zeros((t_len, V7X_LANES), BF16)
    low_head = lax.broadcasted_iota(jnp.int32, (t_len, V7X_LANES), 1) < dh

    for b in range(nb):
        q = rot(q_ref[b])
        k = rot(k_ref[b]) * scale
        vb = v_ref[b]
        kt = k.T
        ktb = kt.astype(BF16)
        kbd = jnp.concatenate(
            [jnp.concatenate([ktb[h * dh:(h + 1) * dh] if hh == h else zero_k for hh in range(heads)],
                             axis=0) for h in range(heads)], axis=1)
        sc = jnp.dot(q.astype(BF16), kbd, preferred_element_type=F32) * dec_ref[...]
        vbd_rows = []
        for h in range(heads):
            p = h // (V7X_LANES // dh)
            vt = vb[:, p * V7X_LANES:(p + 1) * V7X_LANES]
            keep = low_head if h % (V7X_LANES // dh) == 0 else jnp.logical_not(low_head)
            vbd_rows.append(jnp.concatenate(
                [jnp.where(keep, vt, zero_v) if pp == p else zero_v for pp in range(pairs)], axis=1))
        vbd = jnp.concatenate(vbd_rows, axis=0)
        inner = jnp.dot(sc.astype(BF16), vbd, preferred_element_type=F32)
        qd = (q * qdec_ref[...]).astype(BF16)
        kdt = (kt * kdect_ref[...]).astype(BF16)
        crosses = []
        for p in range(pairs):
            ps = slice(p * V7X_LANES, (p + 1) * V7X_LANES)
            state = st_scr[b, p]
            crosses.append(jnp.dot(qd[:, ps], state.astype(BF16), preferred_element_type=F32))
            kv = jnp.dot(kdt[ps, :], vb[:, ps], preferred_element_type=F32)
            st_scr[b, p] = state * cdec_ref[:, ps] + kv * ms_ref[...]
        o = inner + jnp.concatenate(crosses, axis=1)
        dlt = o - group_mean(o)
        var = jnp.dot((dlt * dlt).astype(BF16), avg_ref[...], preferred_element_type=F32)
        on = dlt * lax.rsqrt(var + EPS) * nw_ref[...]
        y_ref[b] = (jax.nn.silu(g_ref[b].astype(F32)) * on).astype(y_ref.dtype)


def _retention(u3, d_ret, norm_w):
    nb, seq, _ = u3.shape
    dh = d_ret // RET_HEADS
    half = dh // 2
    t_len = T_RET
    lg = jnp.log(1.0 - 2.0 ** (-5.0 - jnp.arange(RET_HEADS, dtype=F32)))
    t = jnp.arange(t_len, dtype=F32)
    diff = t[:, None] - t[None, :]
    dec = jnp.where(diff >= 0, jnp.exp(lg[:, None, None] * jnp.maximum(diff, 0.0)), 0.0)
    dec_all = dec.transpose(1, 0, 2).reshape(t_len, RET_HEADS * t_len)
    qdec = jnp.repeat(jnp.exp(lg[:, None] * (t + 1.0)[None, :]).T, dh, axis=1)
    kdect = jnp.repeat(jnp.exp(lg[:, None] * (t_len - 1 - t)[None, :]), dh, axis=0)
    cdec = jnp.repeat(jnp.exp(lg * t_len), dh)[None, :]
    head_of_lane = jnp.arange(d_ret) // dh
    pair_head = jnp.arange(V7X_LANES) // dh
    mask_s = (pair_head[:, None] == pair_head[None, :]).astype(F32)
    avg = (head_of_lane[:, None] == head_of_lane[None, :]).astype(F32) / dh
    assert dh & (dh - 1) == 0
    avg = avg.astype(BF16)
    inv = ROPE_BASE ** (-jnp.arange(half, dtype=F32) / half)
    ang = jnp.arange(seq, dtype=F32)[:, None] * inv[None, :]
    reps = V7X_LANES // dh
    cos_t = jnp.tile(jnp.cos(ang), (1, 2 * reps))
    sin_t = jnp.tile(jnp.concatenate([-jnp.sin(ang), jnp.sin(ang)], axis=1), (1, reps))

    def col(j):
        return pl.BlockSpec((nb, t_len, d_ret), lambda c: (0, c, j))

    def whole(a):
        return pl.BlockSpec(a.shape, lambda c: (0,) * a.ndim)

    body = functools.partial(_ret_body, t_len=t_len, dh=dh, scale=dh ** -0.5)
    return pl.pallas_call(
        body,
        grid=(seq // t_len,),
        in_specs=[col(0), col(1), col(2), col(3),
                  pl.BlockSpec((t_len, V7X_LANES), lambda c: (c, 0)),
                  pl.BlockSpec((t_len, V7X_LANES), lambda c: (c, 0)),
                  whole(dec_all), whole(qdec), whole(kdect), whole(cdec),
                  whole(mask_s), whole(avg), whole(norm_w)],
        out_specs=pl.BlockSpec((nb, t_len, d_ret), lambda c: (0, c, 0)),
        out_shape=jax.ShapeDtypeStruct((nb, seq, d_ret), BF16),
        scratch_shapes=[pltpu.VMEM((nb, d_ret // V7X_LANES, V7X_LANES, V7X_LANES), F32)],
        compiler_params=_cparams(("arbitrary",)),
        name="retention",
    )(u3, u3, u3, u3, cos_t, sin_t, dec_all, qdec, kdect, cdec, mask_s, avg, norm_w)


def _route_body(ys5_ref, yret_ref, x_ref, wo_ref, n2_ref, wr_ref, br_ref, tri_ref,
                h_ref, route_ref, route_t_ref, cnt_ref, carry_scr):
    @pl.when(pl.program_id(0) == 0)
    def _():
        carry_scr[...] = jnp.zeros_like(carry_scr)

    d_s5 = ys5_ref.shape[1]
    h = (x_ref[...]
         + jnp.dot(ys5_ref[...], wo_ref[0:d_s5], preferred_element_type=F32)
         + jnp.dot(yret_ref[...], wo_ref[d_s5:], preferred_element_type=F32))
    h_ref[...] = h
    hn = _rms(h, n2_ref[...])
    hi = hn.astype(BF16)
    lo = (hn - hi.astype(F32)).astype(BF16)
    p_hi = jnp.dot(hi, wr_ref[...], preferred_element_type=F32)
    p_lo = jnp.dot(lo, wr_ref[:, 0:ROUTE_LANES], preferred_element_type=F32)
    logits = p_hi[:, 0:ROUTE_LANES] + p_hi[:, ROUTE_LANES:] + p_lo + br_ref[...]
    tm = logits.shape[0]
    lane = lax.broadcasted_iota(jnp.int32, (tm, ROUTE_LANES), 1)
    lanef = lane.astype(F32)
    neg = -jnp.inf
    big = float(ROUTE_LANES)
    gl = jnp.where(lane < N_GROUPS, logits, neg)
    gmax = jnp.max(gl, axis=-1, keepdims=True)
    gidx = jnp.min(jnp.where(gl == gmax, lanef, big), axis=-1, keepdims=True)
    g_w = 1.0 / jnp.sum(jnp.exp(gl - gmax), axis=-1, keepdims=True)
    lo = N_GROUPS + EXPERTS_PER_GROUP * gidx
    el = jnp.where((lanef >= lo) & (lanef < lo + EXPERTS_PER_GROUP), logits, neg)
    v0 = jnp.max(el, axis=-1, keepdims=True)
    i0 = jnp.min(jnp.where(el == v0, lanef, big), axis=-1, keepdims=True)
    el2 = jnp.where(lanef == i0, neg, el)
    v1 = jnp.max(el2, axis=-1, keepdims=True)
    i1 = jnp.min(jnp.where(el2 == v1, lanef, big), axis=-1, keepdims=True)
    e = jnp.exp(v1 - v0)
    den = 1.0 + e
    w0 = (1.0 / den) * g_w
    w1 = (e / den) * g_w
    sel0 = lanef == i0
    sel1 = lanef == i1
    onehot = jnp.where(sel0 | sel1, 1.0, 0.0)
    before = jnp.dot(tri_ref[...], onehot.astype(BF16), preferred_element_type=F32) + carry_scr[...]
    r0 = jnp.sum(jnp.where(sel0, before, 0.0), axis=-1, keepdims=True)
    r1 = jnp.sum(jnp.where(sel1, before, 0.0), axis=-1, keepdims=True)
    carry_scr[...] += jnp.sum(onehot, axis=0, keepdims=True)
    cnt_ref[...] = carry_scr[...]
    rec = jnp.zeros((tm, ROUTE_LANES), F32)
    for j, val in enumerate((i0 - N_GROUPS, i1 - N_GROUPS, w0, w1, r0, r1)):
        rec = jnp.where(lane == j, val, rec)
    route_ref[...] = rec
    route_t_ref[...] = rec.T[0:ROUTE_FIELDS]


def _route(ys5, yret, x2, wo_bf, n2, wr, br):
    n, d = x2.shape
    d_s5 = ys5.shape[1]
    d_ret = yret.shape[1]
    tm = TM_ROUTE
    tri = (jnp.arange(tm)[:, None] > jnp.arange(tm)[None, :]).astype(BF16)
    return pl.pallas_call(
        _route_body,
        grid=(n // tm,),
        in_specs=[pl.BlockSpec((tm, d_s5), lambda i: (i, 0)),
                  pl.BlockSpec((tm, d_ret), lambda i: (i, 0)),
                  pl.BlockSpec((tm, d), lambda i: (i, 0)),
                  pl.BlockSpec((d_s5 + d_ret, d), lambda i: (0, 0)),
                  pl.BlockSpec((1, d), lambda i: (0, 0)),
                  pl.BlockSpec((d, 2 * ROUTE_LANES), lambda i: (0, 0)),
                  pl.BlockSpec((1, ROUTE_LANES), lambda i: (0, 0)),
                  pl.BlockSpec((tm, tm), lambda i: (0, 0))],
        out_specs=[pl.BlockSpec((tm, d), lambda i: (i, 0)),
                   pl.BlockSpec((tm, ROUTE_LANES), lambda i: (i, 0)),
                   pl.BlockSpec((ROUTE_FIELDS, tm), lambda i: (0, i)),
                   pl.BlockSpec((1, ROUTE_LANES), lambda i: (0, 0))],
        out_shape=[jax.ShapeDtypeStruct((n, d), F32),
                   jax.ShapeDtypeStruct((n, ROUTE_LANES), F32),
                   jax.ShapeDtypeStruct((ROUTE_FIELDS, n), F32),
                   jax.ShapeDtypeStruct((1, ROUTE_LANES), F32)],
        scratch_shapes=[pltpu.VMEM((1, ROUTE_LANES), F32)],
        compiler_params=_cparams(("arbitrary",)),
        name="outproj_route",
    )(ys5, yret, x2, wo_bf, n2, wr, br, tri)


def _dispatch_body(pends_ref, padded_ref, dest_ref, h_ref, n2_ref, xs_ref,
                   hn_scr, zero_scr, sem, zsem):
    tm = h_ref.shape[0]

    @pl.when(pl.program_id(0) == 0)
    def _():
        zero_scr[...] = jnp.zeros_like(zero_scr)

        def zero_copy(e):
            first = pl.multiple_of((pends_ref[e] - R_BLK) * ROW_TILES, R_BLK * ROW_TILES)
            return pltpu.make_async_copy(zero_scr, xs_ref.at[pl.ds(first, R_BLK * ROW_TILES)], zsem)

        def zstart(e, carry):
            @pl.when(padded_ref[e] > 0)
            def _():
                zero_copy(e).start()
            return carry

        def zwait(e, carry):
            @pl.when(padded_ref[e] > 0)
            def _():
                zero_copy(e).wait()
            return carry

        lax.fori_loop(0, pends_ref.shape[0], zstart, 0)
        lax.fori_loop(0, pends_ref.shape[0], zwait, 0)

    step = pl.program_id(0)
    slot = step % 2
    buf = hn_scr.at[slot]
    _rows_to_tiles(buf, _rms(h_ref[...], n2_ref[...]))

    for r in range(tm):
        for k in range(TOP_K):
            pltpu.make_async_copy(_tile_rows(buf, r), _tile_rows(xs_ref, dest_ref[k * tm + r]),
                                  sem.at[slot]).start(priority=k)

    def drain(s):
        for k in range(TOP_K):
            pltpu.make_async_copy(hn_scr.at[s], xs_ref.at[pl.ds(0, tm * ROW_TILES)], sem.at[s]).wait()

    @pl.when(step > 0)
    def _():
        drain(1 - slot)

    @pl.when(step == pl.num_programs(0) - 1)
    def _():
        drain(slot)


def _dispatch(pends, padded, dest, h, n2, p_rows):
    n, d = h.shape
    tm = TM_ROWS
    grid_spec = pltpu.PrefetchScalarGridSpec(
        num_scalar_prefetch=2,
        grid=(n // tm,),
        in_specs=[pl.BlockSpec((TOP_K * tm,), lambda i, pe, pa: (i,), memory_space=pltpu.SMEM),
                  pl.BlockSpec((tm, d), lambda i, pe, pa: (i, 0)),
                  pl.BlockSpec((1, d), lambda i, pe, pa: (0, 0))],
        out_specs=pl.BlockSpec(memory_space=pl.ANY),
        scratch_shapes=[pltpu.VMEM((2, tm * ROW_TILES, V7X_LANES), F32),
                        pltpu.VMEM((R_BLK * ROW_TILES, V7X_LANES), F32),
                        pltpu.SemaphoreType.DMA((2,)), pltpu.SemaphoreType.DMA(())])
    assert d == ROW_TILES * V7X_LANES
    return pl.pallas_call(
        _dispatch_body,
        grid_spec=grid_spec,
        out_shape=jax.ShapeDtypeStruct((p_rows * ROW_TILES, V7X_LANES), F32),
        compiler_params=_cparams(("arbitrary",)),
        name="dispatch",
    )(pends, padded, dest, h, n2)


def _expert_body(be_ref, nu_ref, xs_ref, wg_ref, wu_ref, wd_ref, ys_ref, wg_s, wu_s, wd_s):
    j = pl.program_id(0)

    @pl.when(j < nu_ref[0])
    def _():
        changed = jnp.logical_or(j == 0, be_ref[j] != be_ref[jnp.maximum(j - 1, 0)])

        @pl.when(changed)
        def _():
            wg_s[...] = wg_ref[...].astype(BF16)
            wu_s[...] = wu_ref[...].astype(BF16)
            wd_s[...] = wd_ref[...].astype(BF16)

        x = _tiles_to_rows(xs_ref, R_BLK).astype(BF16)
        gate = jnp.dot(x, wg_s[...], preferred_element_type=F32)
        up = jnp.dot(x, wu_s[...], preferred_element_type=F32)
        hid = (jax.nn.silu(gate) * up).astype(BF16)
        _rows_to_tiles(ys_ref, jnp.dot(hid, wd_s[...], preferred_element_type=F32))

    @pl.when(j >= nu_ref[0])
    def _():
        ys_ref[...] = jnp.zeros_like(ys_ref)


def _experts(block_e, n_used, xs, w_gate, w_up, w_down):
    d, d_e = w_gate.shape[1:]
    blk_rows = R_BLK * ROW_TILES
    nblk = xs.shape[0] // blk_rows

    def row_map(j, be, nu):
        return (jnp.maximum(jnp.minimum(j, nu[0] - 1), 0), 0)

    def w_map(j, be, nu):
        return (be[j], 0, 0)

    grid_spec = pltpu.PrefetchScalarGridSpec(
        num_scalar_prefetch=2,
        grid=(nblk,),
        in_specs=[pl.BlockSpec((blk_rows, V7X_LANES), row_map),
                  pl.BlockSpec((None, d, d_e), w_map),
                  pl.BlockSpec((None, d, d_e), w_map),
                  pl.BlockSpec((None, d_e, d), w_map)],
        out_specs=pl.BlockSpec((blk_rows, V7X_LANES), lambda j, be, nu: (j, 0)),
        scratch_shapes=[pltpu.VMEM((d, d_e), BF16), pltpu.VMEM((d, d_e), BF16),
                        pltpu.VMEM((d_e, d), BF16)])
    return pl.pallas_call(
        _expert_body,
        grid_spec=grid_spec,
        out_shape=jax.ShapeDtypeStruct(xs.shape, F32),
        compiler_params=_cparams(("arbitrary",)),
        name="experts",
    )(block_e, n_used, xs, w_gate, w_up, w_down)


def _combine_body(dest0_ref, dest1_ref, dest2_ref, h_ref, route_ref, fw_ref, ys_ref, out_ref,
                  *scratch, last):
    bufs, sem = scratch[:GATHER_RING], scratch[GATHER_RING]
    tm = h_ref.shape[0]
    step = pl.program_id(0)

    def issue(d_ref, s):
        for r in range(tm):
            for k in range(TOP_K):
                pltpu.make_async_copy(_tile_rows(ys_ref, d_ref[k * tm + r]),
                                      _tile_rows(bufs[s].at[k], r), sem.at[s]).start(priority=k)

    def drain(s):
        for k in range(TOP_K):
            pltpu.make_async_copy(ys_ref.at[pl.ds(0, tm * ROW_TILES)], bufs[s].at[k], sem.at[s]).wait()

    @pl.when(step == 0)
    def _():
        issue(dest0_ref, 0)
        issue(dest1_ref, 1)

    for s in range(GATHER_RING):
        @pl.when(step % GATHER_RING == s)
        def _():
            drain(s)
            issue(dest2_ref, (s + 2) % GATHER_RING)
            rec = route_ref[...]
            hh = (h_ref[...] + rec[:, 2:3] * _tiles_to_rows(bufs[s].at[0], tm)
                  + rec[:, 3:4] * _tiles_to_rows(bufs[s].at[1], tm))
            out_ref[...] = _rms(hh, fw_ref[...])

    @pl.when(step == last)
    def _():
        drain((last + 1) % GATHER_RING)
        drain((last + 2) % GATHER_RING)


def _combine(dest, h, route, fw, ys):
    n, d = h.shape
    tm = TM_ROWS
    steps = n // tm
    return pl.pallas_call(
        functools.partial(_combine_body, last=steps - 1),
        grid=(steps,),
        in_specs=[pl.BlockSpec((TOP_K * tm,), lambda i: (i,), memory_space=pltpu.SMEM),
                  pl.BlockSpec((TOP_K * tm,), lambda i: (jnp.minimum(i + 1, steps - 1),),
                               memory_space=pltpu.SMEM),
                  pl.BlockSpec((TOP_K * tm,), lambda i: (jnp.minimum(i + 2, steps - 1),),
                               memory_space=pltpu.SMEM),
                  pl.BlockSpec((tm, d), lambda i: (i, 0)),
                  pl.BlockSpec((tm, ROUTE_LANES), lambda i: (i, 0)),
                  pl.BlockSpec((1, d), lambda i: (0, 0)),
                  pl.BlockSpec(memory_space=pl.ANY)],
        out_specs=pl.BlockSpec((tm, d), lambda i: (i, 0)),
        out_shape=jax.ShapeDtypeStruct((n, d), F32),
        scratch_shapes=([pltpu.VMEM((TOP_K, tm * ROW_TILES, V7X_LANES), F32)] * GATHER_RING
                        + [pltpu.SemaphoreType.DMA((GATHER_RING,))]),
        compiler_params=_cparams(("arbitrary",)),
        name="combine",
    )(dest, dest, dest, h, route, fw, ys)


def _plan(route_t, counts, n_experts, n_blocks):
    eid = route_t[0:TOP_K].astype(jnp.int32)
    rank = route_t[4:4 + TOP_K].astype(jnp.int32)
    cnt = counts[0, N_GROUPS:N_GROUPS + n_experts].astype(jnp.int32)
    padded = (cnt + R_BLK - 1) // R_BLK * R_BLK
    pends = jnp.cumsum(padded)
    pstart = pends - padded
    dest = rank
    for e in range(n_experts):
        dest = dest + jnp.where(eid == e, pstart[e], 0)
    n_used = pends[-1] // R_BLK
    blk = jnp.minimum(jnp.arange(n_blocks, dtype=jnp.int32), n_used - 1)
    block_e = jnp.minimum(jnp.sum(pends[None, :] <= (blk * R_BLK)[:, None], axis=1), n_experts - 1)
    return (dest.astype(jnp.int32), block_e.astype(jnp.int32), n_used.reshape(1).astype(jnp.int32),
            pends.astype(jnp.int32), padded.astype(jnp.int32))


def _layer(h3, norm1_w, w_in, s5_a_re, s5_a_im, s5_b_re, s5_b_im, s5_c_re, s5_c_im, s5_d,
           s5_log_dt, s5_w_glu, s5_b_glu, ret_norm_w, w_out, norm2_w, router_group_w,
           router_group_b, router_expert_w, router_expert_b, moe_w_gate, moe_w_up, moe_w_down,
           out_norm_w):
    nb, seq, d = h3.shape
    n = nb * seq
    d_s5 = s5_d.shape[0]
    d_ret = ret_norm_w.shape[0]
    n_experts = moe_w_gate.shape[0]
    x2 = h3.reshape(n, d)

    u_s5, u_ret = _inproj(x2, norm1_w.reshape(1, d), w_in.astype(BF16), d_s5)

    bp, cp, tp, a_tab = _s5_tables(s5_a_re, s5_a_im, s5_b_re, s5_b_im, s5_c_re, s5_c_im,
                                   s5_log_dt, nb, S5_TAU)
    y_s5 = _s5(u_s5, nb, bp, cp, tp, a_tab, s5_d.reshape(1, d_s5).astype(F32),
               s5_w_glu.astype(BF16), s5_b_glu.reshape(1, d_s5).astype(F32))
    y_ret = _retention(u_ret.reshape(nb, seq, -1), d_ret, ret_norm_w.reshape(1, d_ret).astype(F32))

    n_route = N_GROUPS + n_experts
    wr = jnp.zeros((d, ROUTE_LANES), F32).at[:, :n_route].set(
        jnp.concatenate([router_group_w, router_expert_w], axis=1).astype(F32))
    br = jnp.zeros((1, ROUTE_LANES), F32).at[0, :n_route].set(
        jnp.concatenate([router_group_b, router_expert_b]).astype(F32))
    wr_hi = wr.astype(BF16)
    wr = jnp.concatenate([wr_hi, (wr - wr_hi.astype(F32)).astype(BF16)], axis=1)
    h, route, route_t, counts = _route(y_s5.reshape(n, d_s5), y_ret.reshape(n, d_ret), x2,
                                       w_out.astype(BF16), norm2_w.reshape(1, d), wr, br)

    n_blocks = (n * TOP_K) // R_BLK + n_experts
    dest, block_e, n_used, pends, padded = _plan(route_t, counts, n_experts, n_blocks)
    dest = dest.reshape(TOP_K, n // TM_ROWS, TM_ROWS).transpose(1, 0, 2).reshape(-1)
    xs = _dispatch(pends, padded, dest, h, norm2_w.reshape(1, d), n_blocks * R_BLK)
    ys = _experts(block_e, n_used, xs, moe_w_gate, moe_w_up, moe_w_down)
    out = _combine(dest, h, route, out_norm_w.reshape(1, d), ys)
    return out.reshape(nb, seq, d)


def kernel(x, norm1_w, w_in, s5_a_re, s5_a_im, s5_b_re, s5_b_im, s5_c_re, s5_c_im, s5_d, s5_log_dt, s5_w_glu, s5_b_glu, ret_norm_w, w_out, norm2_w, router_group_w, router_group_b, router_expert_w, router_expert_b, moe_w_gate, moe_w_up, moe_w_down, final_norm_w):
    depth = norm1_w.shape[0]
    assert depth == 1, "the fused final norm assumes a single layer"
    l = 0
    return _layer(x, norm1_w[l], w_in[l], s5_a_re[l], s5_a_im[l], s5_b_re[l], s5_b_im[l],
                  s5_c_re[l], s5_c_im[l], s5_d[l], s5_log_dt[l], s5_w_glu[l], s5_b_glu[l],
                  ret_norm_w[l], w_out[l], norm2_w[l], router_group_w[l], router_group_b[l],
                  router_expert_w[l], router_expert_b[l], moe_w_gate[l], moe_w_up[l],
                  moe_w_down[l], final_norm_w)
```

```python
import functools

import jax
import jax.numpy as jnp
from jax import lax
from jax.experimental import pallas as pl
from jax.experimental.pallas import tpu as pltpu

F32 = jnp.float32
BF16 = jnp.bfloat16

EPS = 1e-6
ROPE_BASE = 10000.0
RET_HEADS = 8
TOP_K = 2
N_GROUPS = 4
EXPERTS_PER_GROUP = 8

V7X_LANES = 128
V7X_SUBLANES = 8
V7X_VMEM_LIMIT = 56 * 1024 * 1024

TM_PROJ = 512
T_S5 = 512
S5_TAU = 8
S5_PAD = 8
T_RET = 128
TM_ROUTE = 512
TM_ROWS = 256
R_BLK = 512
GATHER_RING = 3
ROUTE_LANES = 128
ROUTE_FIELDS = 8


def _rms(x, w):
    return x * lax.rsqrt(jnp.mean(x * x, axis=-1, keepdims=True) + EPS) * w


def _cparams(sem):
    return pltpu.CompilerParams(dimension_semantics=sem, vmem_limit_bytes=V7X_VMEM_LIMIT)


ROW_TILES = V7X_SUBLANES // 2
U32 = jnp.uint32


def _pack_rows(x):
    half = x.shape[1] // 2

    def bf16_bits(v):
        return lax.bitcast_convert_type(v.astype(BF16).astype(F32), U32)

    return bf16_bits(x[:, half:]) | (bf16_bits(x[:, :half]) >> 16)


def _unpack_rows(w):
    lo = lax.bitcast_convert_type(w << 16, F32)
    hi = lax.bitcast_convert_type(w & U32(0xFFFF0000), F32)
    return jnp.concatenate([lo, hi], axis=1)


def _rows_to_tiles(ref, val):
    rows = val.shape[0]
    for s in range(ROW_TILES):
        ref[pl.ds(s, rows, stride=ROW_TILES), :] = val[:, s * V7X_LANES:(s + 1) * V7X_LANES]


def _tiles_to_rows(ref, rows):
    return jnp.concatenate(
        [ref[pl.ds(s, rows, stride=ROW_TILES), :] for s in range(ROW_TILES)], axis=1)


def _tile_rows(ref, row):
    return ref.at[pl.ds(pl.multiple_of(row * ROW_TILES, ROW_TILES), ROW_TILES)]


def _inproj_body(x_ref, nw_ref, w_ref, us5_ref, uret_ref):
    xn = _rms(x_ref[...], nw_ref[...]).astype(BF16)
    u = jnp.dot(xn, w_ref[...], preferred_element_type=F32)
    d_s5 = us5_ref.shape[1]
    us5_ref[...] = u[:, :d_s5].astype(us5_ref.dtype)
    uret_ref[...] = u[:, d_s5:].astype(uret_ref.dtype)


def _inproj(x2, nw, w_bf, d_s5):
    n, d = x2.shape
    d_in = w_bf.shape[1]
    return pl.pallas_call(
        _inproj_body,
        grid=(n // TM_PROJ,),
        in_specs=[pl.BlockSpec((TM_PROJ, d), lambda i: (i, 0)),
                  pl.BlockSpec((1, d), lambda i: (0, 0)),
                  pl.BlockSpec((d, d_in), lambda i: (0, 0))],
        out_specs=[pl.BlockSpec((TM_PROJ, d_s5), lambda i: (i, 0)),
                   pl.BlockSpec((TM_PROJ, d_in - d_s5), lambda i: (i, 0))],
        out_shape=[jax.ShapeDtypeStruct((n, d_s5), BF16),
                   jax.ShapeDtypeStruct((n, d_in - d_s5), BF16)],
        compiler_params=_cparams(("arbitrary",)),
        name="inproj",
    )(x2, nw, w_bf)


def _s5_body(u_ref, bp_ref, cp_ref, tp_ref, a_ref, dd_ref, wglu_ref, bglu_ref, y_ref,
             v_scr, sp_scr, st_scr, io_scr, *, nb, nk, seg, tau):
    @pl.when(pl.program_id(0) == 0)
    def _():
        st_scr[...] = jnp.zeros_like(st_scr)

    nblk = bp_ref.shape[0]
    cw = bp_ref.shape[1] // tau
    sw = bp_ref.shape[2]
    d_s5 = nblk * cw
    tiles = sw // V7X_LANES
    ht = tiles // 2
    ctiles = d_s5 // V7X_LANES
    srows = nb * nk

    u_all = u_ref[...].reshape(srows * tau, d_s5).astype(F32)
    for c in range(ctiles):
        io_scr[c] = u_all[:, c * V7X_LANES:(c + 1) * V7X_LANES]
    u_steps = [jnp.concatenate([io_scr[c, pl.ds(j, srows, stride=tau), :] for c in range(ctiles)],
                               axis=1).astype(BF16) for j in range(tau)]

    def block_inputs(blk):
        return jnp.concatenate([u_steps[j][:, blk * cw:(blk + 1) * cw] for j in range(tau)], axis=1)

    for blk in range(nblk):
        half, q = divmod(blk, 2)
        v = jnp.dot(block_inputs(blk), bp_ref[blk], preferred_element_type=F32)
        for b in range(nb):
            for j in range(tiles):
                v_scr[q * tiles + j, pl.ds((half * nb + b) * seg, nk), :] = (
                    v[b * nk:(b + 1) * nk, j * V7X_LANES:(j + 1) * V7X_LANES])

    rows = 2 * nb
    ar = [[a_ref[0, q * ht + i] for i in range(ht)] for q in range(2)]
    ai = [[a_ref[1, q * ht + i] for i in range(ht)] for q in range(2)]
    sr = [[st_scr[q * tiles + i] for i in range(ht)] for q in range(2)]
    si = [[st_scr[q * tiles + ht + i] for i in range(ht)] for q in range(2)]
    for k in range(nk):
        for q in range(2):
            for i in range(ht):
                jr = q * tiles + i
                ji = q * tiles + ht + i
                sp_scr[jr, pl.ds(k, rows, stride=seg), :] = sr[q][i]
                sp_scr[ji, pl.ds(k, rows, stride=seg), :] = si[q][i]
                vr = v_scr[jr, pl.ds(k, rows, stride=seg), :]
                vi = v_scr[ji, pl.ds(k, rows, stride=seg), :]
                nr = ar[q][i] * sr[q][i] - ai[q][i] * si[q][i] + vr
                ni = ar[q][i] * si[q][i] + ai[q][i] * sr[q][i] + vi
                sr[q][i], si[q][i] = nr, ni
    for q in range(2):
        for i in range(ht):
            st_scr[q * tiles + i] = sr[q][i]
            st_scr[q * tiles + ht + i] = si[q][i]

    yb = []
    for blk in range(nblk):
        half, q = divmod(blk, 2)
        sp = jnp.concatenate(
            [jnp.concatenate([sp_scr[q * tiles + j, pl.ds((half * nb + b) * seg, nk), :]
                              for j in range(tiles)], axis=1) for b in range(nb)],
            axis=0).astype(BF16)
        yb.append(jnp.dot(sp, cp_ref[blk], preferred_element_type=F32)
                  + jnp.dot(block_inputs(blk), tp_ref[blk], preferred_element_type=F32))
    for j in range(tau):
        y = jnp.concatenate([yb[blk][:, j * cw:(j + 1) * cw] for blk in range(nblk)], axis=1)
        y = y + dd_ref[...] * u_steps[j].astype(F32)
        y = jax.nn.gelu(y)
        z = jnp.dot(y.astype(BF16), wglu_ref[...], preferred_element_type=F32) + bglu_ref[...]
        out = y * jax.nn.sigmoid(z)
        for c in range(ctiles):
            io_scr[c, pl.ds(j, srows, stride=tau), :] = out[:, c * V7X_LANES:(c + 1) * V7X_LANES]
    y_all = jnp.concatenate([io_scr[c] for c in range(ctiles)], axis=1)
    y_ref[...] = y_all.reshape(nb, nk * tau, d_s5).astype(y_ref.dtype)


def _s5(u_s5, nb, bp, cp, tp, a_tab, dd, wglu_bf, bglu):
    n, d_s5 = u_s5.shape
    seq = n // nb
    tau = S5_TAU
    nk = T_S5 // tau
    seg = nk + S5_PAD
    nblk, _, sw = bp.shape
    rows = 2 * nb
    assert rows == V7X_SUBLANES and nblk == 4

    def whole(a):
        return pl.BlockSpec(a.shape, lambda c: (0,) * a.ndim)

    body = functools.partial(_s5_body, nb=nb, nk=nk, seg=seg, tau=tau)
    out = pl.pallas_call(
        body,
        grid=(seq // T_S5,),
        in_specs=[pl.BlockSpec((nb, T_S5, d_s5), lambda c: (0, c, 0)),
                  whole(bp), whole(cp), whole(tp), whole(a_tab), whole(dd), whole(wglu_bf),
                  whole(bglu)],
        out_specs=pl.BlockSpec((nb, T_S5, d_s5), lambda c: (0, c, 0)),
        out_shape=jax.ShapeDtypeStruct((nb, seq, d_s5), BF16),
        scratch_shapes=[pltpu.VMEM((2 * sw // V7X_LANES, rows * seg, V7X_LANES), F32),
                        pltpu.VMEM((2 * sw // V7X_LANES, rows * seg, V7X_LANES), F32),
                        pltpu.VMEM((2 * sw // V7X_LANES, rows, V7X_LANES), F32),
                        pltpu.VMEM((d_s5 // V7X_LANES, nb * T_S5, V7X_LANES), F32)],
        compiler_params=_cparams(("arbitrary",)),
        name="s5",
    )(u_s5.reshape(nb, seq, d_s5), bp, cp, tp, a_tab, dd, wglu_bf, bglu)
    return out.reshape(n, d_s5)


def _s5_tables(a_re, a_im, b_re, b_im, c_re, c_im, log_dt, nb, tau):
    hp = lax.Precision.HIGHEST
    g, p = a_re.shape
    hch = b_re.shape[2]
    gpb = V7X_LANES // hch
    nblk = g // gpb
    lam_r, lam_i = a_re.astype(F32), a_im.astype(F32)
    dt = jnp.exp(log_dt.astype(F32))[:, None]
    mag = jnp.exp(lam_r * dt)
    ab_r = mag * jnp.cos(lam_i * dt)
    ab_i = mag * jnp.sin(lam_i * dt)
    den = lam_r * lam_r + lam_i * lam_i
    zr = ((ab_r - 1.0) * lam_r + ab_i * lam_i) / den
    zi = (ab_i * lam_r - (ab_r - 1.0) * lam_i) / den
    br_, bi_ = b_re.astype(F32), b_im.astype(F32)
    bb_r = zr[..., None] * br_ - zi[..., None] * bi_
    bb_i = zr[..., None] * bi_ + zi[..., None] * br_
    cr, ci = c_re.astype(F32), c_im.astype(F32)
    pr, pi = [jnp.ones_like(ab_r)], [jnp.zeros_like(ab_i)]
    for _ in range(tau):
        pr, pi = pr + [pr[-1] * ab_r - pi[-1] * ab_i], pi + [pr[-1] * ab_i + pi[-1] * ab_r]
    pw_r, pw_i = jnp.stack(pr), jnp.stack(pi)

    def blockdiag(x):
        r, c = x.shape[-2:]
        x = jnp.tile(x.reshape(tau, nblk, gpb * r, c), (1, 1, 1, gpb))
        same = (jnp.arange(gpb * r)[:, None] // r) == (jnp.arange(gpb * c)[None, :] // c)
        return jnp.where(same, x, 0.0)

    wr_ = jnp.stack([pr[tau - 1 - j] for j in range(tau)])[:, :, None, :]
    wi_ = jnp.stack([pi[tau - 1 - j] for j in range(tau)])[:, :, None, :]
    bt_r, bt_i = jnp.swapaxes(bb_r, 1, 2)[None], jnp.swapaxes(bb_i, 1, 2)[None]
    bp_r = blockdiag(wr_ * bt_r - wi_ * bt_i)
    bp_i = blockdiag(wr_ * bt_i + wi_ * bt_r)
    bp = jnp.concatenate([jnp.concatenate([bp_r[j], bp_i[j]], axis=-1) for j in range(tau)],
                         axis=1).astype(BF16)

    qr_, qi_ = pw_r[1:, :, :, None], pw_i[1:, :, :, None]
    ct_r, ct_i = jnp.swapaxes(cr, 1, 2)[None], jnp.swapaxes(ci, 1, 2)[None]
    cp_r = blockdiag(ct_r * qr_ - ct_i * qi_)
    cp_i = blockdiag(ct_r * qi_ + ct_i * qr_)
    cp = jnp.concatenate([jnp.concatenate([cp_r[i], -cp_i[i]], axis=1) for i in range(tau)],
                         axis=2).astype(BF16)

    ab_r_ = pw_r[:tau, :, :, None] * bb_r[None] - pw_i[:tau, :, :, None] * bb_i[None]
    ab_i_ = pw_r[:tau, :, :, None] * bb_i[None] + pw_i[:tau, :, :, None] * bb_r[None]
    kd = blockdiag(jnp.einsum('ghp,dgpe->dgeh', cr, ab_r_, precision=hp)
                   - jnp.einsum('ghp,dgpe->dgeh', ci, ab_i_, precision=hp))
    kzero = jnp.zeros_like(kd[0])
    tp = jnp.concatenate(
        [jnp.concatenate([kd[i - j] if i >= j else kzero for i in range(tau)], axis=2)
         for j in range(tau)], axis=1).astype(BF16)

    a_tab = jnp.stack([jnp.repeat(pw_r[tau].reshape(2, -1), nb, axis=0),
                       jnp.repeat(pw_i[tau].reshape(2, -1), nb, axis=0)])
    a_tab = a_tab.reshape(2, 2 * nb, -1, V7X_LANES).transpose(0, 2, 1, 3)
    return bp, cp, tp, a_tab


def _ret_body(q_ref, k_ref, v_ref, g_ref, cos_ref, sin_ref, dec_ref, qdec_ref, kdect_ref,
              cdec_ref, ms_ref, avg_ref, nw_ref, y_ref, st_scr, *, t_len, dh, scale):
    @pl.when(pl.program_id(0) == 0)
    def _():
        st_scr[...] = jnp.zeros_like(st_scr)

    nb = q_ref.shape[0]
    width = q_ref.shape[-1]
    heads = width // dh
    pairs = width // V7X_LANES
    reps = width // cos_ref.shape[-1]
    cos = jnp.concatenate([cos_ref[...]] * reps, axis=1)
    sin = jnp.concatenate([sin_ref[...]] * reps, axis=1)
    lane = lax.broadcasted_iota(jnp.int32, (t_len, width), 1)
    first = (lane % dh) < (dh // 2)

    def rot(x):
        x = x.astype(F32)
        partner = jnp.where(first, pltpu.roll(x, width - dh // 2, 1), pltpu.roll(x, dh // 2, 1))
        return x * cos + partner * sin

    def group_mean(x):
        hi = x.astype(BF16)
        lo = (x - hi.astype(F32)).astype(BF16)
        return (jnp.dot(hi, avg_ref[...], preferred_element_type=F32)
                + jnp.dot(lo, avg_ref[...], preferred_element_type=F32))

    zero_k = jnp.zeros((dh, t_len), BF16)
    zero_v = jnp.zeros((t_len, V7X_LANES), BF16)
    low_head = lax.broadcasted_iota(jnp.int32, (t_len, V7X_LANES), 1) < dh

    for b in range(nb):
        q = rot(q_ref[b])
        k = rot(k_ref[b]) * scale
        vb = v_ref[b]
        kt = k.T
        ktb = kt.astype(BF16)
        kbd = jnp.concatenate(
            [jnp.concatenate([ktb[h * dh:(h + 1) * dh] if hh == h else zero_k for hh in range(heads)],
                             axis=0) for h in range(heads)], axis=1)
        sc = jnp.dot(q.astype(BF16), kbd, preferred_element_type=F32) * dec_ref[...]
        vbd_rows = []
        for h in range(heads):
            p = h // (V7X_LANES // dh)
            vt = vb[:, p * V7X_LANES:(p + 1) * V7X_LANES]
            keep = low_head if h % (V7X_LANES // dh) == 0 else jnp.logical_not(low_head)
            vbd_rows.append(jnp.concatenate(
                [jnp.where(keep, vt, zero_v) if pp == p else zero_v for pp in range(pairs)], axis=1))
        vbd = jnp.concatenate(vbd_rows, axis=0)
        inner = jnp.dot(sc.astype(BF16), vbd, preferred_element_type=F32)
        qd = (q * qdec_ref[...]).astype(BF16)
        kdt = (kt * kdect_ref[...]).astype(BF16)
        crosses = []
        for p in range(pairs):
            ps = slice(p * V7X_LANES, (p + 1) * V7X_LANES)
            state = st_scr[b, p]
            crosses.append(jnp.dot(qd[:, ps], state.astype(BF16), preferred_element_type=F32))
            kv = jnp.dot(kdt[ps, :], vb[:, ps], preferred_element_type=F32)
            st_scr[b, p] = state * cdec_ref[:, ps] + kv * ms_ref[...]
        o = inner + jnp.concatenate(crosses, axis=1)
        dlt = o - group_mean(o)
        var = jnp.dot((dlt * dlt).astype(BF16), avg_ref[...], preferred_element_type=F32)
        on = dlt * lax.rsqrt(var + EPS) * nw_ref[...]
        y_ref[b] = (jax.nn.silu(g_ref[b].astype(F32)) * on).astype(y_ref.dtype)


def _retention(u3, d_ret, norm_w):
    nb, seq, _ = u3.shape
    dh = d_ret // RET_HEADS
    half = dh // 2
    t_len = T_RET
    lg = jnp.log(1.0 - 2.0 ** (-5.0 - jnp.arange(RET_HEADS, dtype=F32)))
    t = jnp.arange(t_len, dtype=F32)
    diff = t[:, None] - t[None, :]
    dec = jnp.where(diff >= 0, jnp.exp(lg[:, None, None] * jnp.maximum(diff, 0.0)), 0.0)
    dec_all = dec.transpose(1, 0, 2).reshape(t_len, RET_HEADS * t_len)
    qdec = jnp.repeat(jnp.exp(lg[:, None] * (t + 1.0)[None, :]).T, dh, axis=1)
    kdect = jnp.repeat(jnp.exp(lg[:, None] * (t_len - 1 - t)[None, :]), dh, axis=0)
    cdec = jnp.repeat(jnp.exp(lg * t_len), dh)[None, :]
    head_of_lane = jnp.arange(d_ret) // dh
    pair_head = jnp.arange(V7X_LANES) // dh
    mask_s = (pair_head[:, None] == pair_head[None, :]).astype(F32)
    avg = (head_of_lane[:, None] == head_of_lane[None, :]).astype(F32) / dh
    assert dh & (dh - 1) == 0
    avg = avg.astype(BF16)
    inv = ROPE_BASE ** (-jnp.arange(half, dtype=F32) / half)
    ang = jnp.arange(seq, dtype=F32)[:, None] * inv[None, :]
    reps = V7X_LANES // dh
    cos_t = jnp.tile(jnp.cos(ang), (1, 2 * reps))
    sin_t = jnp.tile(jnp.concatenate([-jnp.sin(ang), jnp.sin(ang)], axis=1), (1, reps))

    def col(j):
        return pl.BlockSpec((nb, t_len, d_ret), lambda c: (0, c, j))

    def whole(a):
        return pl.BlockSpec(a.shape, lambda c: (0,) * a.ndim)

    body = functools.partial(_ret_body, t_len=t_len, dh=dh, scale=dh ** -0.5)
    return pl.pallas_call(
        body,
        grid=(seq // t_len,),
        in_specs=[col(0), col(1), col(2), col(3),
                  pl.BlockSpec((t_len, V7X_LANES), lambda c: (c, 0)),
                  pl.BlockSpec((t_len, V7X_LANES), lambda c: (c, 0)),
                  whole(dec_all), whole(qdec), whole(kdect), whole(cdec),
                  whole(mask_s), whole(avg), whole(norm_w)],
        out_specs=pl.BlockSpec((nb, t_len, d_ret), lambda c: (0, c, 0)),
        out_shape=jax.ShapeDtypeStruct((nb, seq, d_ret), BF16),
        scratch_shapes=[pltpu.VMEM((nb, d_ret // V7X_LANES, V7X_LANES, V7X_LANES), F32)],
        compiler_params=_cparams(("arbitrary",)),
        name="retention",
    )(u3, u3, u3, u3, cos_t, sin_t, dec_all, qdec, kdect, cdec, mask_s, avg, norm_w)


def _route_body(ys5_ref, yret_ref, x_ref, wo_ref, n2_ref, wr_ref, br_ref, tri_ref,
                h_ref, route_ref, route_t_ref, cnt_ref, carry_scr):
    @pl.when(pl.program_id(0) == 0)
    def _():
        carry_scr[...] = jnp.zeros_like(carry_scr)

    d_s5 = ys5_ref.shape[1]
    h = (x_ref[...]
         + jnp.dot(ys5_ref[...], wo_ref[0:d_s5], preferred_element_type=F32)
         + jnp.dot(yret_ref[...], wo_ref[d_s5:], preferred_element_type=F32))
    h_ref[...] = h
    hn = _rms(h, n2_ref[...])
    hi = hn.astype(BF16)
    lo = (hn - hi.astype(F32)).astype(BF16)
    p_hi = jnp.dot(hi, wr_ref[...], preferred_element_type=F32)
    p_lo = jnp.dot(lo, wr_ref[:, 0:ROUTE_LANES], preferred_element_type=F32)
    logits = p_hi[:, 0:ROUTE_LANES] + p_hi[:, ROUTE_LANES:] + p_lo + br_ref[...]
    tm = logits.shape[0]
    lane = lax.broadcasted_iota(jnp.int32, (tm, ROUTE_LANES), 1)
    lanef = lane.astype(F32)
    neg = -jnp.inf
    big = float(ROUTE_LANES)
    gl = jnp.where(lane < N_GROUPS, logits, neg)
    gmax = jnp.max(gl, axis=-1, keepdims=True)
    gidx = jnp.min(jnp.where(gl == gmax, lanef, big), axis=-1, keepdims=True)
    g_w = 1.0 / jnp.sum(jnp.exp(gl - gmax), axis=-1, keepdims=True)
    lo = N_GROUPS + EXPERTS_PER_GROUP * gidx
    el = jnp.where((lanef >= lo) & (lanef < lo + EXPERTS_PER_GROUP), logits, neg)
    v0 = jnp.max(el, axis=-1, keepdims=True)
    i0 = jnp.min(jnp.where(el == v0, lanef, big), axis=-1, keepdims=True)
    el2 = jnp.where(lanef == i0, neg, el)
    v1 = jnp.max(el2, axis=-1, keepdims=True)
    i1 = jnp.min(jnp.where(el2 == v1, lanef, big), axis=-1, keepdims=True)
    e = jnp.exp(v1 - v0)
    den = 1.0 + e
    w0 = (1.0 / den) * g_w
    w1 = (e / den) * g_w
    sel0 = lanef == i0
    sel1 = lanef == i1
    onehot = jnp.where(sel0 | sel1, 1.0, 0.0)
    before = jnp.dot(tri_ref[...], onehot.astype(BF16), preferred_element_type=F32) + carry_scr[...]
    r0 = jnp.sum(jnp.where(sel0, before, 0.0), axis=-1, keepdims=True)
    r1 = jnp.sum(jnp.where(sel1, before, 0.0), axis=-1, keepdims=True)
    carry_scr[...] += jnp.sum(onehot, axis=0, keepdims=True)
    cnt_ref[...] = carry_scr[...]
    rec = jnp.zeros((tm, ROUTE_LANES), F32)
    for j, val in enumerate((i0 - N_GROUPS, i1 - N_GROUPS, w0, w1, r0, r1)):
        rec = jnp.where(lane == j, val, rec)
    route_ref[...] = rec
    route_t_ref[...] = rec.T[0:ROUTE_FIELDS]


def _route(ys5, yret, x2, wo_bf, n2, wr, br):
    n, d = x2.shape
    d_s5 = ys5.shape[1]
    d_ret = yret.shape[1]
    tm = TM_ROUTE
    tri = (jnp.arange(tm)[:, None] > jnp.arange(tm)[None, :]).astype(BF16)
    return pl.pallas_call(
        _route_body,
        grid=(n // tm,),
        in_specs=[pl.BlockSpec((tm, d_s5), lambda i: (i, 0)),
                  pl.BlockSpec((tm, d_ret), lambda i: (i, 0)),
                  pl.BlockSpec((tm, d), lambda i: (i, 0)),
                  pl.BlockSpec((d_s5 + d_ret, d), lambda i: (0, 0)),
                  pl.BlockSpec((1, d), lambda i: (0, 0)),
                  pl.BlockSpec((d, 2 * ROUTE_LANES), lambda i: (0, 0)),
                  pl.BlockSpec((1, ROUTE_LANES), lambda i: (0, 0)),
                  pl.BlockSpec((tm, tm), lambda i: (0, 0))],
        out_specs=[pl.BlockSpec((tm, d), lambda i: (i, 0)),
                   pl.BlockSpec((tm, ROUTE_LANES), lambda i: (i, 0)),
                   pl.BlockSpec((ROUTE_FIELDS, tm), lambda i: (0, i)),
                   pl.BlockSpec((1, ROUTE_LANES), lambda i: (0, 0))],
        out_shape=[jax.ShapeDtypeStruct((n, d), F32),
                   jax.ShapeDtypeStruct((n, ROUTE_LANES), F32),
                   jax.ShapeDtypeStruct((ROUTE_FIELDS, n), F32),
                   jax.ShapeDtypeStruct((1, ROUTE_LANES), F32)],
        scratch_shapes=[pltpu.VMEM((1, ROUTE_LANES), F32)],
        compiler_params=_cparams(("arbitrary",)),
        name="outproj_route",
    )(ys5, yret, x2, wo_bf, n2, wr, br, tri)


def _dispatch_body(pends_ref, padded_ref, dest_ref, h_ref, n2_ref, xs_ref,
                   hn_scr, zero_scr, sem, zsem):
    tm = h_ref.shape[0]

    @pl.when(pl.program_id(0) == 0)
    def _():
        zero_scr[...] = jnp.zeros_like(zero_scr)

        def zero_copy(e):
            first = pl.multiple_of((pends_ref[e] - R_BLK) * ROW_TILES, R_BLK * ROW_TILES)
            return pltpu.make_async_copy(zero_scr, xs_ref.at[pl.ds(first, R_BLK * ROW_TILES)], zsem)

        def zstart(e, carry):
            @pl.when(padded_ref[e] > 0)
            def _():
                zero_copy(e).start()
            return carry

        def zwait(e, carry):
            @pl.when(padded_ref[e] > 0)
            def _():
                zero_copy(e).wait()
            return carry

        lax.fori_loop(0, pends_ref.shape[0], zstart, 0)
        lax.fori_loop(0, pends_ref.shape[0], zwait, 0)

    step = pl.program_id(0)
    slot = step % 2
    buf = hn_scr.at[slot]
    _rows_to_tiles(buf, _pack_rows(_rms(h_ref[...], n2_ref[...])))

    for r in range(tm):
        for k in range(TOP_K):
            pltpu.make_async_copy(_tile_rows(buf, r), _tile_rows(xs_ref, dest_ref[k * tm + r]),
                                  sem.at[slot]).start(priority=k)

    def drain(s):
        for k in range(TOP_K):
            pltpu.make_async_copy(hn_scr.at[s], xs_ref.at[pl.ds(0, tm * ROW_TILES)], sem.at[s]).wait()

    @pl.when(step > 0)
    def _():
        drain(1 - slot)

    @pl.when(step == pl.num_programs(0) - 1)
    def _():
        drain(slot)


def _dispatch(pends, padded, dest, h, n2, p_rows):
    n, d = h.shape
    tm = TM_ROWS
    grid_spec = pltpu.PrefetchScalarGridSpec(
        num_scalar_prefetch=2,
        grid=(n // tm,),
        in_specs=[pl.BlockSpec((TOP_K * tm,), lambda i, pe, pa: (i,), memory_space=pltpu.SMEM),
                  pl.BlockSpec((tm, d), lambda i, pe, pa: (i, 0)),
                  pl.BlockSpec((1, d), lambda i, pe, pa: (0, 0))],
        out_specs=pl.BlockSpec(memory_space=pl.ANY),
        scratch_shapes=[pltpu.VMEM((2, tm * ROW_TILES, V7X_LANES), U32),
                        pltpu.VMEM((R_BLK * ROW_TILES, V7X_LANES), U32),
                        pltpu.SemaphoreType.DMA((2,)), pltpu.SemaphoreType.DMA(())])
    assert d == 2 * ROW_TILES * V7X_LANES
    return pl.pallas_call(
        _dispatch_body,
        grid_spec=grid_spec,
        out_shape=jax.ShapeDtypeStruct((p_rows * ROW_TILES, V7X_LANES), U32),
        compiler_params=_cparams(("arbitrary",)),
        name="dispatch",
    )(pends, padded, dest, h, n2)


def _expert_body(be_ref, nu_ref, xs_ref, wg_ref, wu_ref, wd_ref, ys_ref, wg_s, wu_s, wd_s):
    j = pl.program_id(0)

    @pl.when(j < nu_ref[0])
    def _():
        changed = jnp.logical_or(j == 0, be_ref[j] != be_ref[jnp.maximum(j - 1, 0)])

        @pl.when(changed)
        def _():
            wg_s[...] = wg_ref[...].astype(BF16)
            wu_s[...] = wu_ref[...].astype(BF16)
            wd_s[...] = wd_ref[...].astype(BF16)

        x = _unpack_rows(_tiles_to_rows(xs_ref, R_BLK)).astype(BF16)
        gate = jnp.dot(x, wg_s[...], preferred_element_type=F32)
        up = jnp.dot(x, wu_s[...], preferred_element_type=F32)
        hid = (jax.nn.silu(gate) * up).astype(BF16)
        _rows_to_tiles(ys_ref, _pack_rows(jnp.dot(hid, wd_s[...], preferred_element_type=F32)))

    @pl.when(j >= nu_ref[0])
    def _():
        ys_ref[...] = jnp.zeros_like(ys_ref)


def _experts(block_e, n_used, xs, w_gate, w_up, w_down):
    d, d_e = w_gate.shape[1:]
    blk_rows = R_BLK * ROW_TILES
    nblk = xs.shape[0] // blk_rows

    def row_map(j, be, nu):
        return (jnp.maximum(jnp.minimum(j, nu[0] - 1), 0), 0)

    def w_map(j, be, nu):
        return (be[j], 0, 0)

    grid_spec = pltpu.PrefetchScalarGridSpec(
        num_scalar_prefetch=2,
        grid=(nblk,),
        in_specs=[pl.BlockSpec((blk_rows, V7X_LANES), row_map),
                  pl.BlockSpec((None, d, d_e), w_map),
                  pl.BlockSpec((None, d, d_e), w_map),
                  pl.BlockSpec((None, d_e, d), w_map)],
        out_specs=pl.BlockSpec((blk_rows, V7X_LANES), lambda j, be, nu: (j, 0)),
        scratch_shapes=[pltpu.VMEM((d, d_e), BF16), pltpu.VMEM((d, d_e), BF16),
                        pltpu.VMEM((d_e, d), BF16)])
    return pl.pallas_call(
        _expert_body,
        grid_spec=grid_spec,
        out_shape=jax.ShapeDtypeStruct(xs.shape, xs.dtype),
        compiler_params=_cparams(("arbitrary",)),
        name="experts",
    )(block_e, n_used, xs, w_gate, w_up, w_down)


def _combine_body(dest0_ref, dest1_ref, dest2_ref, h_ref, route_ref, fw_ref, ys_ref, out_ref,
                  *scratch, last):
    bufs, sem = scratch[:GATHER_RING], scratch[GATHER_RING]
    tm = h_ref.shape[0]
    step = pl.program_id(0)

    def issue(d_ref, s):
        for r in range(tm):
            for k in range(TOP_K):
                pltpu.make_async_copy(_tile_rows(ys_ref, d_ref[k * tm + r]),
                                      _tile_rows(bufs[s].at[k], r), sem.at[s]).start(priority=k)

    def drain(s):
        for k in range(TOP_K):
            pltpu.make_async_copy(ys_ref.at[pl.ds(0, tm * ROW_TILES)], bufs[s].at[k], sem.at[s]).wait()

    @pl.when(step == 0)
    def _():
        issue(dest0_ref, 0)
        issue(dest1_ref, 1)

    for s in range(GATHER_RING):
        @pl.when(step % GATHER_RING == s)
        def _():
            drain(s)
            issue(dest2_ref, (s + 2) % GATHER_RING)
            rec = route_ref[...]
            hh = (h_ref[...] + rec[:, 2:3] * _unpack_rows(_tiles_to_rows(bufs[s].at[0], tm))
                  + rec[:, 3:4] * _unpack_rows(_tiles_to_rows(bufs[s].at[1], tm)))
            out_ref[...] = _rms(hh, fw_ref[...])

    @pl.when(step == last)
    def _():
        drain((last + 1) % GATHER_RING)
        drain((last + 2) % GATHER_RING)


def _combine(dest, h, route, fw, ys):
    n, d = h.shape
    tm = TM_ROWS
    steps = n // tm
    return pl.pallas_call(
        functools.partial(_combine_body, last=steps - 1),
        grid=(steps,),
        in_specs=[pl.BlockSpec((TOP_K * tm,), lambda i: (i,), memory_space=pltpu.SMEM),
                  pl.BlockSpec((TOP_K * tm,), lambda i: (jnp.minimum(i + 1, steps - 1),),
                               memory_space=pltpu.SMEM),
                  pl.BlockSpec((TOP_K * tm,), lambda i: (jnp.minimum(i + 2, steps - 1),),
                               memory_space=pltpu.SMEM),
                  pl.BlockSpec((tm, d), lambda i: (i, 0)),
                  pl.BlockSpec((tm, ROUTE_LANES), lambda i: (i, 0)),
                  pl.BlockSpec((1, d), lambda i: (0, 0)),
                  pl.BlockSpec(memory_space=pl.ANY)],
        out_specs=pl.BlockSpec((tm, d), lambda i: (i, 0)),
        out_shape=jax.ShapeDtypeStruct((n, d), F32),
        scratch_shapes=([pltpu.VMEM((TOP_K, tm * ROW_TILES, V7X_LANES), U32)] * GATHER_RING
                        + [pltpu.SemaphoreType.DMA((GATHER_RING,))]),
        compiler_params=_cparams(("arbitrary",)),
        name="combine",
    )(dest, dest, dest, h, route, fw, ys)


def _plan(route_t, counts, n_experts, n_blocks):
    eid = route_t[0:TOP_K].astype(jnp.int32)
    rank = route_t[4:4 + TOP_K].astype(jnp.int32)
    cnt = counts[0, N_GROUPS:N_GROUPS + n_experts].astype(jnp.int32)
    padded = (cnt + R_BLK - 1) // R_BLK * R_BLK
    pends = jnp.cumsum(padded)
    pstart = pends - padded
    dest = rank
    for e in range(n_experts):
        dest = dest + jnp.where(eid == e, pstart[e], 0)
    n_used = pends[-1] // R_BLK
    blk = jnp.minimum(jnp.arange(n_blocks, dtype=jnp.int32), n_used - 1)
    block_e = jnp.minimum(jnp.sum(pends[None, :] <= (blk * R_BLK)[:, None], axis=1), n_experts - 1)
    return (dest.astype(jnp.int32), block_e.astype(jnp.int32), n_used.reshape(1).astype(jnp.int32),
            pends.astype(jnp.int32), padded.astype(jnp.int32))


def _layer(h3, norm1_w, w_in, s5_a_re, s5_a_im, s5_b_re, s5_b_im, s5_c_re, s5_c_im, s5_d,
           s5_log_dt, s5_w_glu, s5_b_glu, ret_norm_w, w_out, norm2_w, router_group_w,
           router_group_b, router_expert_w, router_expert_b, moe_w_gate, moe_w_up, moe_w_down,
           out_norm_w):
    nb, seq, d = h3.shape
    n = nb * seq
    d_s5 = s5_d.shape[0]
    d_ret = ret_norm_w.shape[0]
    n_experts = moe_w_gate.shape[0]
    x2 = h3.reshape(n, d)

    u_s5, u_ret = _inproj(x2, norm1_w.reshape(1, d), w_in.astype(BF16), d_s5)

    bp, cp, tp, a_tab = _s5_tables(s5_a_re, s5_a_im, s5_b_re, s5_b_im, s5_c_re, s5_c_im,
                                   s5_log_dt, nb, S5_TAU)
    y_s5 = _s5(u_s5, nb, bp, cp, tp, a_tab, s5_d.reshape(1, d_s5).astype(F32),
               s5_w_glu.astype(BF16), s5_b_glu.reshape(1, d_s5).astype(F32))
    y_ret = _retention(u_ret.reshape(nb, seq, -1), d_ret, ret_norm_w.reshape(1, d_ret).astype(F32))

    n_route = N_GROUPS + n_experts
    wr = jnp.zeros((d, ROUTE_LANES), F32).at[:, :n_route].set(
        jnp.concatenate([router_group_w, router_expert_w], axis=1).astype(F32))
    br = jnp.zeros((1, ROUTE_LANES), F32).at[0, :n_route].set(
        jnp.concatenate([router_group_b, router_expert_b]).astype(F32))
    wr_hi = wr.astype(BF16)
    wr = jnp.concatenate([wr_hi, (wr - wr_hi.astype(F32)).astype(BF16)], axis=1)
    h, route, route_t, counts = _route(y_s5.reshape(n, d_s5), y_ret.reshape(n, d_ret), x2,
                                       w_out.astype(BF16), norm2_w.reshape(1, d), wr, br)

    n_blocks = (n * TOP_K) // R_BLK + n_experts
    dest, block_e, n_used, pends, padded = _plan(route_t, counts, n_experts, n_blocks)
    dest = dest.reshape(TOP_K, n // TM_ROWS, TM_ROWS).transpose(1, 0, 2).reshape(-1)
    xs = _dispatch(pends, padded, dest, h, norm2_w.reshape(1, d), n_blocks * R_BLK)
    ys = _experts(block_e, n_used, xs, moe_w_gate, moe_w_up, moe_w_down)
    out = _combine(dest, h, route, out_norm_w.reshape(1, d), ys)
    return out.reshape(nb, seq, d)


def kernel(x, norm1_w, w_in, s5_a_re, s5_a_im, s5_b_re, s5_b_im, s5_c_re, s5_c_im, s5_d, s5_log_dt, s5_w_glu, s5_b_glu, ret_norm_w, w_out, norm2_w, router_group_w, router_group_b, router_expert_w, router_expert_b, moe_w_gate, moe_w_up, moe_w_down, final_norm_w):
    depth = norm1_w.shape[0]
    assert depth == 1, "the fused final norm assumes a single layer"
    l = 0
    return _layer(x, norm1_w[l], w_in[l], s5_a_re[l], s5_a_im[l], s5_b_re[l], s5_b_im[l],
                  s5_c_re[l], s5_c_im[l], s5_d[l], s5_log_dt[l], s5_w_glu[l], s5_b_glu[l],
                  ret_norm_w[l], w_out[l], norm2_w[l], router_group_w[l], router_group_b[l],
                  router_expert_w[l], router_expert_b[l], moe_w_gate[l], moe_w_up[l],
                  moe_w_down[l], final_norm_w)
```

```python
import functools

import jax
import jax.numpy as jnp
from jax import lax
from jax.experimental import pallas as pl
from jax.experimental.pallas import tpu as pltpu

F32 = jnp.float32
BF16 = jnp.bfloat16

EPS = 1e-6
ROPE_BASE = 10000.0
RET_HEADS = 8
TOP_K = 2
N_GROUPS = 4
EXPERTS_PER_GROUP = 8

V7X_LANES = 128
V7X_SUBLANES = 8
V7X_VMEM_LIMIT = 56 * 1024 * 1024

T_S5 = 512
S5_TAU = 8
S5_PAD = 8
T_RET = 128
TM_ROUTE = 512
TM_ROWS = 256
R_BLK = 512
GATHER_RING = 3
ROUTE_LANES = 128
ROUTE_FIELDS = 8


def _rms(x, w):
    return x * lax.rsqrt(jnp.mean(x * x, axis=-1, keepdims=True) + EPS) * w


def _cparams(sem):
    return pltpu.CompilerParams(dimension_semantics=sem, vmem_limit_bytes=V7X_VMEM_LIMIT)


ROW_TILES = V7X_SUBLANES // 2
U32 = jnp.uint32


def _pack_rows(x):
    half = x.shape[1] // 2

    def bf16_bits(v):
        return lax.bitcast_convert_type(v.astype(BF16).astype(F32), U32)

    return bf16_bits(x[:, half:]) | (bf16_bits(x[:, :half]) >> 16)


def _unpack_rows(w):
    lo = lax.bitcast_convert_type(w << 16, F32)
    hi = lax.bitcast_convert_type(w & U32(0xFFFF0000), F32)
    return jnp.concatenate([lo, hi], axis=1)


def _rows_to_tiles(ref, val):
    rows = val.shape[0]
    for s in range(ROW_TILES):
        ref[pl.ds(s, rows, stride=ROW_TILES), :] = val[:, s * V7X_LANES:(s + 1) * V7X_LANES]


def _tiles_to_rows(ref, rows):
    return jnp.concatenate(
        [ref[pl.ds(s, rows, stride=ROW_TILES), :] for s in range(ROW_TILES)], axis=1)


def _tile_rows(ref, row):
    return ref.at[pl.ds(pl.multiple_of(row * ROW_TILES, ROW_TILES), ROW_TILES)]


def _s5_body(u_ref, bp_ref, cp_ref, tp_ref, a_ref, dd_ref, wglu_ref, bglu_ref, y_ref,
             v_scr, sp_scr, st_scr, io_scr, *, nb, nk, seg, tau):
    @pl.when(pl.program_id(0) == 0)
    def _():
        st_scr[...] = jnp.zeros_like(st_scr)

    nblk = bp_ref.shape[0]
    cw = bp_ref.shape[1] // tau
    sw = bp_ref.shape[2]
    d_s5 = nblk * cw
    tiles = sw // V7X_LANES
    ht = tiles // 2
    ctiles = d_s5 // V7X_LANES
    srows = nb * nk

    u_all = u_ref[...].reshape(srows * tau, d_s5).astype(F32)
    for c in range(ctiles):
        io_scr[c] = u_all[:, c * V7X_LANES:(c + 1) * V7X_LANES]
    u_steps = [jnp.concatenate([io_scr[c, pl.ds(j, srows, stride=tau), :] for c in range(ctiles)],
                               axis=1).astype(BF16) for j in range(tau)]

    def block_inputs(blk):
        return jnp.concatenate([u_steps[j][:, blk * cw:(blk + 1) * cw] for j in range(tau)], axis=1)

    for blk in range(nblk):
        half, q = divmod(blk, 2)
        v = jnp.dot(block_inputs(blk), bp_ref[blk], preferred_element_type=F32)
        for b in range(nb):
            for j in range(tiles):
                v_scr[q * tiles + j, pl.ds((half * nb + b) * seg, nk), :] = (
                    v[b * nk:(b + 1) * nk, j * V7X_LANES:(j + 1) * V7X_LANES])

    rows = 2 * nb
    ar = [[a_ref[0, q * ht + i] for i in range(ht)] for q in range(2)]
    ai = [[a_ref[1, q * ht + i] for i in range(ht)] for q in range(2)]
    sr = [[st_scr[q * tiles + i] for i in range(ht)] for q in range(2)]
    si = [[st_scr[q * tiles + ht + i] for i in range(ht)] for q in range(2)]
    for k in range(nk):
        for q in range(2):
            for i in range(ht):
                jr = q * tiles + i
                ji = q * tiles + ht + i
                sp_scr[jr, pl.ds(k, rows, stride=seg), :] = sr[q][i]
                sp_scr[ji, pl.ds(k, rows, stride=seg), :] = si[q][i]
                vr = v_scr[jr, pl.ds(k, rows, stride=seg), :]
                vi = v_scr[ji, pl.ds(k, rows, stride=seg), :]
                nr = ar[q][i] * sr[q][i] - ai[q][i] * si[q][i] + vr
                ni = ar[q][i] * si[q][i] + ai[q][i] * sr[q][i] + vi
                sr[q][i], si[q][i] = nr, ni
    for q in range(2):
        for i in range(ht):
            st_scr[q * tiles + i] = sr[q][i]
            st_scr[q * tiles + ht + i] = si[q][i]

    yb = []
    for blk in range(nblk):
        half, q = divmod(blk, 2)
        sp = jnp.concatenate(
            [jnp.concatenate([sp_scr[q * tiles + j, pl.ds((half * nb + b) * seg, nk), :]
                              for j in range(tiles)], axis=1) for b in range(nb)],
            axis=0).astype(BF16)
        yb.append(jnp.dot(sp, cp_ref[blk], preferred_element_type=F32)
                  + jnp.dot(block_inputs(blk), tp_ref[blk], preferred_element_type=F32))
    for j in range(tau):
        y = jnp.concatenate([yb[blk][:, j * cw:(j + 1) * cw] for blk in range(nblk)], axis=1)
        y = y + dd_ref[...] * u_steps[j].astype(F32)
        y = jax.nn.gelu(y)
        z = jnp.dot(y.astype(BF16), wglu_ref[...], preferred_element_type=F32) + bglu_ref[...]
        out = y * jax.nn.sigmoid(z)
        for c in range(ctiles):
            io_scr[c, pl.ds(j, srows, stride=tau), :] = out[:, c * V7X_LANES:(c + 1) * V7X_LANES]
    y_all = jnp.concatenate([io_scr[c] for c in range(ctiles)], axis=1)
    y_ref[...] = y_all.reshape(nb, nk * tau, d_s5).astype(y_ref.dtype)


def _s5(u_s5, nb, bp, cp, tp, a_tab, dd, wglu_bf, bglu):
    n, d_s5 = u_s5.shape
    seq = n // nb
    tau = S5_TAU
    nk = T_S5 // tau
    seg = nk + S5_PAD
    nblk, _, sw = bp.shape
    rows = 2 * nb
    assert rows == V7X_SUBLANES and nblk == 4

    def whole(a):
        return pl.BlockSpec(a.shape, lambda c: (0,) * a.ndim)

    body = functools.partial(_s5_body, nb=nb, nk=nk, seg=seg, tau=tau)
    out = pl.pallas_call(
        body,
        grid=(seq // T_S5,),
        in_specs=[pl.BlockSpec((nb, T_S5, d_s5), lambda c: (0, c, 0)),
                  whole(bp), whole(cp), whole(tp), whole(a_tab), whole(dd), whole(wglu_bf),
                  whole(bglu)],
        out_specs=pl.BlockSpec((nb, T_S5, d_s5), lambda c: (0, c, 0)),
        out_shape=jax.ShapeDtypeStruct((nb, seq, d_s5), BF16),
        scratch_shapes=[pltpu.VMEM((2 * sw // V7X_LANES, rows * seg, V7X_LANES), F32),
                        pltpu.VMEM((2 * sw // V7X_LANES, rows * seg, V7X_LANES), F32),
                        pltpu.VMEM((2 * sw // V7X_LANES, rows, V7X_LANES), F32),
                        pltpu.VMEM((d_s5 // V7X_LANES, nb * T_S5, V7X_LANES), F32)],
        compiler_params=_cparams(("arbitrary",)),
        name="s5",
    )(u_s5.reshape(nb, seq, d_s5), bp, cp, tp, a_tab, dd, wglu_bf, bglu)
    return out.reshape(n, d_s5)


def _s5_tables(a_re, a_im, b_re, b_im, c_re, c_im, log_dt, nb, tau):
    hp = lax.Precision.HIGHEST
    g, p = a_re.shape
    hch = b_re.shape[2]
    gpb = V7X_LANES // hch
    nblk = g // gpb
    lam_r, lam_i = a_re.astype(F32), a_im.astype(F32)
    dt = jnp.exp(log_dt.astype(F32))[:, None]
    mag = jnp.exp(lam_r * dt)
    ab_r = mag * jnp.cos(lam_i * dt)
    ab_i = mag * jnp.sin(lam_i * dt)
    den = lam_r * lam_r + lam_i * lam_i
    zr = ((ab_r - 1.0) * lam_r + ab_i * lam_i) / den
    zi = (ab_i * lam_r - (ab_r - 1.0) * lam_i) / den
    br_, bi_ = b_re.astype(F32), b_im.astype(F32)
    bb_r = zr[..., None] * br_ - zi[..., None] * bi_
    bb_i = zr[..., None] * bi_ + zi[..., None] * br_
    cr, ci = c_re.astype(F32), c_im.astype(F32)
    pr, pi = [jnp.ones_like(ab_r)], [jnp.zeros_like(ab_i)]
    for _ in range(tau):
        pr, pi = pr + [pr[-1] * ab_r - pi[-1] * ab_i], pi + [pr[-1] * ab_i + pi[-1] * ab_r]
    pw_r, pw_i = jnp.stack(pr), jnp.stack(pi)

    def blockdiag(x):
        r, c = x.shape[-2:]
        x = jnp.tile(x.reshape(tau, nblk, gpb * r, c), (1, 1, 1, gpb))
        same = (jnp.arange(gpb * r)[:, None] // r) == (jnp.arange(gpb * c)[None, :] // c)
        return jnp.where(same, x, 0.0)

    wr_ = jnp.stack([pr[tau - 1 - j] for j in range(tau)])[:, :, None, :]
    wi_ = jnp.stack([pi[tau - 1 - j] for j in range(tau)])[:, :, None, :]
    bt_r, bt_i = jnp.swapaxes(bb_r, 1, 2)[None], jnp.swapaxes(bb_i, 1, 2)[None]
    bp_r = blockdiag(wr_ * bt_r - wi_ * bt_i)
    bp_i = blockdiag(wr_ * bt_i + wi_ * bt_r)
    bp = jnp.concatenate([jnp.concatenate([bp_r[j], bp_i[j]], axis=-1) for j in range(tau)],
                         axis=1).astype(BF16)

    qr_, qi_ = pw_r[1:, :, :, None], pw_i[1:, :, :, None]
    ct_r, ct_i = jnp.swapaxes(cr, 1, 2)[None], jnp.swapaxes(ci, 1, 2)[None]
    cp_r = blockdiag(ct_r * qr_ - ct_i * qi_)
    cp_i = blockdiag(ct_r * qi_ + ct_i * qr_)
    cp = jnp.concatenate([jnp.concatenate([cp_r[i], -cp_i[i]], axis=1) for i in range(tau)],
                         axis=2).astype(BF16)

    ab_r_ = pw_r[:tau, :, :, None] * bb_r[None] - pw_i[:tau, :, :, None] * bb_i[None]
    ab_i_ = pw_r[:tau, :, :, None] * bb_i[None] + pw_i[:tau, :, :, None] * bb_r[None]
    kd = blockdiag(jnp.einsum('ghp,dgpe->dgeh', cr, ab_r_, precision=hp)
                   - jnp.einsum('ghp,dgpe->dgeh', ci, ab_i_, precision=hp))
    kzero = jnp.zeros_like(kd[0])
    tp = jnp.concatenate(
        [jnp.concatenate([kd[i - j] if i >= j else kzero for i in range(tau)], axis=2)
         for j in range(tau)], axis=1).astype(BF16)

    a_tab = jnp.stack([jnp.repeat(pw_r[tau].reshape(2, -1), nb, axis=0),
                       jnp.repeat(pw_i[tau].reshape(2, -1), nb, axis=0)])
    a_tab = a_tab.reshape(2, 2 * nb, -1, V7X_LANES).transpose(0, 2, 1, 3)
    return bp, cp, tp, a_tab


def _proj_ret_body(x_ref, n1_ref, w_ref, cos_ref, sin_ref, dec_ref, qdec_ref, kdect_ref,
                   cdec_ref, ms_ref, avg_ref, nw_ref, us5_ref, y_ref, u_even, u_odd, st_scr,
                   *, t_len, dh, scale):
    step = pl.program_id(0)
    nb = x_ref.shape[0]
    width = y_ref.shape[-1]
    d_s5 = us5_ref.shape[-1]

    @pl.when(step == 0)
    def _():
        u_odd[...] = jnp.zeros_like(u_odd)
        st_scr[...] = jnp.zeros_like(st_scr)

    def project(u_out):
        xn = _rms(x_ref[...].reshape(nb * t_len, x_ref.shape[-1]), n1_ref[...]).astype(BF16)
        u = jnp.dot(xn, w_ref[...], preferred_element_type=F32)
        us5_ref[...] = u[:, :d_s5].reshape(nb, t_len, d_s5).astype(us5_ref.dtype)
        u_out[...] = u[:, d_s5:].astype(u_out.dtype)

    for parity, (u_out, u_in) in enumerate(((u_even, u_odd), (u_odd, u_even))):
        @pl.when(step % 2 == parity)
        def _():
            project(u_out)
            _ret_chunk(u_in, cos_ref, sin_ref, dec_ref, qdec_ref, kdect_ref, cdec_ref, ms_ref, avg_ref,
                       nw_ref, y_ref, st_scr, t_len=t_len, dh=dh, scale=scale)


def _ret_chunk(u_ref, cos_ref, sin_ref, dec_ref, qdec_ref, kdect_ref, cdec_ref, ms_ref, avg_ref,
               nw_ref, y_ref, st_scr, *, t_len, dh, scale):
    nb = y_ref.shape[0]
    width = y_ref.shape[-1]
    heads = width // dh
    pairs = width // V7X_LANES
    reps = width // cos_ref.shape[-1]
    cos = jnp.concatenate([cos_ref[...]] * reps, axis=1)
    sin = jnp.concatenate([sin_ref[...]] * reps, axis=1)
    lane = lax.broadcasted_iota(jnp.int32, (t_len, width), 1)
    first = (lane % dh) < (dh // 2)

    def rot(x):
        x = x.astype(F32)
        partner = jnp.where(first, pltpu.roll(x, width - dh // 2, 1), pltpu.roll(x, dh // 2, 1))
        return x * cos + partner * sin

    def group_mean(x):
        hi = x.astype(BF16)
        lo = (x - hi.astype(F32)).astype(BF16)
        return (jnp.dot(hi, avg_ref[...], preferred_element_type=F32)
                + jnp.dot(lo, avg_ref[...], preferred_element_type=F32))

    zero_k = jnp.zeros((dh, t_len), BF16)
    zero_v = jnp.zeros((t_len, V7X_LANES), BF16)
    low_head = lax.broadcasted_iota(jnp.int32, (t_len, V7X_LANES), 1) < dh

    for b in range(nb):
        rows = slice(b * t_len, (b + 1) * t_len)
        q = rot(u_ref[rows, 0:width])
        k = rot(u_ref[rows, width:2 * width]) * scale
        vb = u_ref[rows, 2 * width:3 * width]
        kt = k.T
        ktb = kt.astype(BF16)
        kbd = jnp.concatenate(
            [jnp.concatenate([ktb[h * dh:(h + 1) * dh] if hh == h else zero_k for hh in range(heads)],
                             axis=0) for h in range(heads)], axis=1)
        sc = jnp.dot(q.astype(BF16), kbd, preferred_element_type=F32) * dec_ref[...]
        vbd_rows = []
        for h in range(heads):
            p = h // (V7X_LANES // dh)
            vt = vb[:, p * V7X_LANES:(p + 1) * V7X_LANES]
            keep = low_head if h % (V7X_LANES // dh) == 0 else jnp.logical_not(low_head)
            vbd_rows.append(jnp.concatenate(
                [jnp.where(keep, vt, zero_v) if pp == p else zero_v for pp in range(pairs)], axis=1))
        vbd = jnp.concatenate(vbd_rows, axis=0)
        inner = jnp.dot(sc.astype(BF16), vbd, preferred_element_type=F32)
        qd = (q * qdec_ref[...]).astype(BF16)
        kdt = (kt * kdect_ref[...]).astype(BF16)
        crosses = []
        for p in range(pairs):
            ps = slice(p * V7X_LANES, (p + 1) * V7X_LANES)
            state = st_scr[b, p]
            crosses.append(jnp.dot(qd[:, ps], state.astype(BF16), preferred_element_type=F32))
            kv = jnp.dot(kdt[ps, :], vb[:, ps], preferred_element_type=F32)
            st_scr[b, p] = state * cdec_ref[:, ps] + kv * ms_ref[...]
        o = inner + jnp.concatenate(crosses, axis=1)
        dlt = o - group_mean(o)
        var = jnp.dot((dlt * dlt).astype(BF16), avg_ref[...], preferred_element_type=F32)
        on = dlt * lax.rsqrt(var + EPS) * nw_ref[...]
        gate = u_ref[rows, 3 * width:4 * width].astype(F32)
        y_ref[b] = (jax.nn.silu(gate) * on).astype(y_ref.dtype)


def _proj_retention(x3, n1, w_bf, d_s5, d_ret, norm_w):
    nb, seq, d = x3.shape
    dh = d_ret // RET_HEADS
    half = dh // 2
    t_len = T_RET
    lg = jnp.log(1.0 - 2.0 ** (-5.0 - jnp.arange(RET_HEADS, dtype=F32)))
    t = jnp.arange(t_len, dtype=F32)
    diff = t[:, None] - t[None, :]
    dec = jnp.where(diff >= 0, jnp.exp(lg[:, None, None] * jnp.maximum(diff, 0.0)), 0.0)
    dec_all = dec.transpose(1, 0, 2).reshape(t_len, RET_HEADS * t_len)
    qdec = jnp.repeat(jnp.exp(lg[:, None] * (t + 1.0)[None, :]).T, dh, axis=1)
    kdect = jnp.repeat(jnp.exp(lg[:, None] * (t_len - 1 - t)[None, :]), dh, axis=0)
    cdec = jnp.repeat(jnp.exp(lg * t_len), dh)[None, :]
    head_of_lane = jnp.arange(d_ret) // dh
    pair_head = jnp.arange(V7X_LANES) // dh
    mask_s = (pair_head[:, None] == pair_head[None, :]).astype(F32)
    avg = (head_of_lane[:, None] == head_of_lane[None, :]).astype(F32) / dh
    assert dh & (dh - 1) == 0
    avg = avg.astype(BF16)
    inv = ROPE_BASE ** (-jnp.arange(half, dtype=F32) / half)
    ang = jnp.arange(seq, dtype=F32)[:, None] * inv[None, :]
    reps = V7X_LANES // dh
    cos_t = jnp.tile(jnp.cos(ang), (1, 2 * reps))
    sin_t = jnp.tile(jnp.concatenate([-jnp.sin(ang), jnp.sin(ang)], axis=1), (1, reps))

    def whole(a):
        return pl.BlockSpec(a.shape, lambda c: (0,) * a.ndim)

    nc = seq // t_len

    def proj_chunk(c):
        return jnp.minimum(c, nc - 1)

    def ret_chunk(c):
        return jnp.maximum(c - 1, 0)

    body = functools.partial(_proj_ret_body, t_len=t_len, dh=dh, scale=dh ** -0.5)
    return pl.pallas_call(
        body,
        grid=(nc + 1,),
        in_specs=[pl.BlockSpec((nb, t_len, d), lambda c: (0, proj_chunk(c), 0)),
                  whole(n1), whole(w_bf),
                  pl.BlockSpec((t_len, V7X_LANES), lambda c: (ret_chunk(c), 0)),
                  pl.BlockSpec((t_len, V7X_LANES), lambda c: (ret_chunk(c), 0)),
                  whole(dec_all), whole(qdec), whole(kdect), whole(cdec),
                  whole(mask_s), whole(avg), whole(norm_w)],
        out_specs=[pl.BlockSpec((nb, t_len, d_s5), lambda c: (0, proj_chunk(c), 0)),
                   pl.BlockSpec((nb, t_len, d_ret), lambda c: (0, ret_chunk(c), 0))],
        out_shape=[jax.ShapeDtypeStruct((nb, seq, d_s5), BF16),
                   jax.ShapeDtypeStruct((nb, seq, d_ret), BF16)],
        scratch_shapes=[pltpu.VMEM((nb * t_len, 4 * d_ret), BF16),
                        pltpu.VMEM((nb * t_len, 4 * d_ret), BF16),
                        pltpu.VMEM((nb, d_ret // V7X_LANES, V7X_LANES, V7X_LANES), F32)],
        compiler_params=_cparams(("arbitrary",)),
        name="inproj_retention",
    )(x3, n1, w_bf, cos_t, sin_t, dec_all, qdec, kdect, cdec, mask_s, avg, norm_w)


def _route_body(ys5_ref, yret_ref, x_ref, wo_ref, n2_ref, wr_ref, br_ref, tri_ref,
                h_ref, route_ref, route_t_ref, cnt_ref, carry_scr):
    @pl.when(pl.program_id(0) == 0)
    def _():
        carry_scr[...] = jnp.zeros_like(carry_scr)

    d_s5 = ys5_ref.shape[1]
    h = (x_ref[...]
         + jnp.dot(ys5_ref[...], wo_ref[0:d_s5], preferred_element_type=F32)
         + jnp.dot(yret_ref[...], wo_ref[d_s5:], preferred_element_type=F32))
    h_ref[...] = h
    hn = _rms(h, n2_ref[...])
    hi = hn.astype(BF16)
    lo = (hn - hi.astype(F32)).astype(BF16)
    p_hi = jnp.dot(hi, wr_ref[...], preferred_element_type=F32)
    p_lo = jnp.dot(lo, wr_ref[:, 0:ROUTE_LANES], preferred_element_type=F32)
    logits = p_hi[:, 0:ROUTE_LANES] + p_hi[:, ROUTE_LANES:] + p_lo + br_ref[...]
    tm = logits.shape[0]
    lane = lax.broadcasted_iota(jnp.int32, (tm, ROUTE_LANES), 1)
    lanef = lane.astype(F32)
    neg = -jnp.inf
    big = float(ROUTE_LANES)
    gl = jnp.where(lane < N_GROUPS, logits, neg)
    gmax = jnp.max(gl, axis=-1, keepdims=True)
    gidx = jnp.min(jnp.where(gl == gmax, lanef, big), axis=-1, keepdims=True)
    g_w = 1.0 / jnp.sum(jnp.exp(gl - gmax), axis=-1, keepdims=True)
    lo = N_GROUPS + EXPERTS_PER_GROUP * gidx
    el = jnp.where((lanef >= lo) & (lanef < lo + EXPERTS_PER_GROUP), logits, neg)
    v0 = jnp.max(el, axis=-1, keepdims=True)
    i0 = jnp.min(jnp.where(el == v0, lanef, big), axis=-1, keepdims=True)
    el2 = jnp.where(lanef == i0, neg, el)
    v1 = jnp.max(el2, axis=-1, keepdims=True)
    i1 = jnp.min(jnp.where(el2 == v1, lanef, big), axis=-1, keepdims=True)
    e = jnp.exp(v1 - v0)
    den = 1.0 + e
    w0 = (1.0 / den) * g_w
    w1 = (e / den) * g_w
    sel0 = lanef == i0
    sel1 = lanef == i1
    onehot = jnp.where(sel0 | sel1, 1.0, 0.0)
    before = jnp.dot(tri_ref[...], onehot.astype(BF16), preferred_element_type=F32) + carry_scr[...]
    r0 = jnp.sum(jnp.where(sel0, before, 0.0), axis=-1, keepdims=True)
    r1 = jnp.sum(jnp.where(sel1, before, 0.0), axis=-1, keepdims=True)
    carry_scr[...] += jnp.sum(onehot, axis=0, keepdims=True)
    cnt_ref[...] = carry_scr[...]
    rec = jnp.zeros((tm, ROUTE_LANES), F32)
    for j, val in enumerate((i0 - N_GROUPS, i1 - N_GROUPS, w0, w1, r0, r1)):
        rec = jnp.where(lane == j, val, rec)
    route_ref[...] = rec
    route_t_ref[...] = rec.T[0:ROUTE_FIELDS]


def _route(ys5, yret, x2, wo_bf, n2, wr, br):
    n, d = x2.shape
    d_s5 = ys5.shape[1]
    d_ret = yret.shape[1]
    tm = TM_ROUTE
    tri = (jnp.arange(tm)[:, None] > jnp.arange(tm)[None, :]).astype(BF16)
    return pl.pallas_call(
        _route_body,
        grid=(n // tm,),
        in_specs=[pl.BlockSpec((tm, d_s5), lambda i: (i, 0)),
                  pl.BlockSpec((tm, d_ret), lambda i: (i, 0)),
                  pl.BlockSpec((tm, d), lambda i: (i, 0)),
                  pl.BlockSpec((d_s5 + d_ret, d), lambda i: (0, 0)),
                  pl.BlockSpec((1, d), lambda i: (0, 0)),
                  pl.BlockSpec((d, 2 * ROUTE_LANES), lambda i: (0, 0)),
                  pl.BlockSpec((1, ROUTE_LANES), lambda i: (0, 0)),
                  pl.BlockSpec((tm, tm), lambda i: (0, 0))],
        out_specs=[pl.BlockSpec((tm, d), lambda i: (i, 0)),
                   pl.BlockSpec((tm, ROUTE_LANES), lambda i: (i, 0)),
                   pl.BlockSpec((ROUTE_FIELDS, tm), lambda i: (0, i)),
                   pl.BlockSpec((1, ROUTE_LANES), lambda i: (0, 0))],
        out_shape=[jax.ShapeDtypeStruct((n, d), F32),
                   jax.ShapeDtypeStruct((n, ROUTE_LANES), F32),
                   jax.ShapeDtypeStruct((ROUTE_FIELDS, n), F32),
                   jax.ShapeDtypeStruct((1, ROUTE_LANES), F32)],
        scratch_shapes=[pltpu.VMEM((1, ROUTE_LANES), F32)],
        compiler_params=_cparams(("arbitrary",)),
        name="outproj_route",
    )(ys5, yret, x2, wo_bf, n2, wr, br, tri)


def _dispatch_body(pends_ref, padded_ref, dest_ref, h_ref, n2_ref, xs_ref,
                   hn_scr, zero_scr, sem, zsem):
    tm = h_ref.shape[0]

    @pl.when(pl.program_id(0) == 0)
    def _():
        zero_scr[...] = jnp.zeros_like(zero_scr)

        def zero_copy(e):
            first = pl.multiple_of((pends_ref[e] - R_BLK) * ROW_TILES, R_BLK * ROW_TILES)
            return pltpu.make_async_copy(zero_scr, xs_ref.at[pl.ds(first, R_BLK * ROW_TILES)], zsem)

        def zstart(e, carry):
            @pl.when(padded_ref[e] > 0)
            def _():
                zero_copy(e).start()
            return carry

        def zwait(e, carry):
            @pl.when(padded_ref[e] > 0)
            def _():
                zero_copy(e).wait()
            return carry

        lax.fori_loop(0, pends_ref.shape[0], zstart, 0)
        lax.fori_loop(0, pends_ref.shape[0], zwait, 0)

    step = pl.program_id(0)
    slot = step % 2
    buf = hn_scr.at[slot]
    _rows_to_tiles(buf, _pack_rows(_rms(h_ref[...], n2_ref[...])))

    for r in range(tm):
        for k in range(TOP_K):
            pltpu.make_async_copy(_tile_rows(buf, r), _tile_rows(xs_ref, dest_ref[k * tm + r]),
                                  sem.at[slot]).start(priority=k)

    def drain(s):
        for k in range(TOP_K):
            pltpu.make_async_copy(hn_scr.at[s], xs_ref.at[pl.ds(0, tm * ROW_TILES)], sem.at[s]).wait()

    @pl.when(step > 0)
    def _():
        drain(1 - slot)

    @pl.when(step == pl.num_programs(0) - 1)
    def _():
        drain(slot)


def _dispatch(pends, padded, dest, h, n2, p_rows):
    n, d = h.shape
    tm = TM_ROWS
    grid_spec = pltpu.PrefetchScalarGridSpec(
        num_scalar_prefetch=2,
        grid=(n // tm,),
        in_specs=[pl.BlockSpec((TOP_K * tm,), lambda i, pe, pa: (i,), memory_space=pltpu.SMEM),
                  pl.BlockSpec((tm, d), lambda i, pe, pa: (i, 0)),
                  pl.BlockSpec((1, d), lambda i, pe, pa: (0, 0))],
        out_specs=pl.BlockSpec(memory_space=pl.ANY),
        scratch_shapes=[pltpu.VMEM((2, tm * ROW_TILES, V7X_LANES), U32),
                        pltpu.VMEM((R_BLK * ROW_TILES, V7X_LANES), U32),
                        pltpu.SemaphoreType.DMA((2,)), pltpu.SemaphoreType.DMA(())])
    assert d == 2 * ROW_TILES * V7X_LANES
    return pl.pallas_call(
        _dispatch_body,
        grid_spec=grid_spec,
        out_shape=jax.ShapeDtypeStruct((p_rows * ROW_TILES, V7X_LANES), U32),
        compiler_params=_cparams(("arbitrary",)),
        name="dispatch",
    )(pends, padded, dest, h, n2)


def _expert_body(be_ref, nu_ref, xs_ref, wg_ref, wu_ref, wd_ref, ys_ref, wg_s, wu_s, wd_s):
    j = pl.program_id(0)

    @pl.when(j < nu_ref[0])
    def _():
        changed = jnp.logical_or(j == 0, be_ref[j] != be_ref[jnp.maximum(j - 1, 0)])

        @pl.when(changed)
        def _():
            wg_s[...] = wg_ref[...].astype(BF16)
            wu_s[...] = wu_ref[...].astype(BF16)
            wd_s[...] = wd_ref[...].astype(BF16)

        x = _unpack_rows(_tiles_to_rows(xs_ref, R_BLK)).astype(BF16)
        gate = jnp.dot(x, wg_s[...], preferred_element_type=F32)
        up = jnp.dot(x, wu_s[...], preferred_element_type=F32)
        hid = (jax.nn.silu(gate) * up).astype(BF16)
        _rows_to_tiles(ys_ref, _pack_rows(jnp.dot(hid, wd_s[...], preferred_element_type=F32)))

    @pl.when(j >= nu_ref[0])
    def _():
        ys_ref[...] = jnp.zeros_like(ys_ref)


def _experts(block_e, n_used, xs, w_gate, w_up, w_down):
    d, d_e = w_gate.shape[1:]
    blk_rows = R_BLK * ROW_TILES
    nblk = xs.shape[0] // blk_rows

    def row_map(j, be, nu):
        return (jnp.maximum(jnp.minimum(j, nu[0] - 1), 0), 0)

    def w_map(j, be, nu):
        return (be[j], 0, 0)

    grid_spec = pltpu.PrefetchScalarGridSpec(
        num_scalar_prefetch=2,
        grid=(nblk,),
        in_specs=[pl.BlockSpec((blk_rows, V7X_LANES), row_map),
                  pl.BlockSpec((None, d, d_e), w_map),
                  pl.BlockSpec((None, d, d_e), w_map),
                  pl.BlockSpec((None, d_e, d), w_map)],
        out_specs=pl.BlockSpec((blk_rows, V7X_LANES), lambda j, be, nu: (j, 0)),
        scratch_shapes=[pltpu.VMEM((d, d_e), BF16), pltpu.VMEM((d, d_e), BF16),
                        pltpu.VMEM((d_e, d), BF16)])
    return pl.pallas_call(
        _expert_body,
        grid_spec=grid_spec,
        out_shape=jax.ShapeDtypeStruct(xs.shape, xs.dtype),
        compiler_params=_cparams(("arbitrary",)),
        name="experts",
    )(block_e, n_used, xs, w_gate, w_up, w_down)


def _combine_body(dest0_ref, dest1_ref, dest2_ref, h_ref, route_ref, fw_ref, ys_ref, out_ref,
                  *scratch, last):
    bufs, sem = scratch[:GATHER_RING], scratch[GATHER_RING]
    tm = h_ref.shape[0]
    step = pl.program_id(0)

    def issue(d_ref, s):
        for r in range(tm):
            for k in range(TOP_K):
                pltpu.make_async_copy(_tile_rows(ys_ref, d_ref[k * tm + r]),
                                      _tile_rows(bufs[s].at[k], r), sem.at[s]).start(priority=k)

    def drain(s):
        for k in range(TOP_K):
            pltpu.make_async_copy(ys_ref.at[pl.ds(0, tm * ROW_TILES)], bufs[s].at[k], sem.at[s]).wait()

    @pl.when(step == 0)
    def _():
        issue(dest0_ref, 0)
        issue(dest1_ref, 1)

    for s in range(GATHER_RING):
        @pl.when(step % GATHER_RING == s)
        def _():
            drain(s)
            issue(dest2_ref, (s + 2) % GATHER_RING)
            rec = route_ref[...]
            hh = (h_ref[...] + rec[:, 2:3] * _unpack_rows(_tiles_to_rows(bufs[s].at[0], tm))
                  + rec[:, 3:4] * _unpack_rows(_tiles_to_rows(bufs[s].at[1], tm)))
            out_ref[...] = _rms(hh, fw_ref[...])

    @pl.when(step == last)
    def _():
        drain((last + 1) % GATHER_RING)
        drain((last + 2) % GATHER_RING)


def _combine(dest, h, route, fw, ys):
    n, d = h.shape
    tm = TM_ROWS
    steps = n // tm
    return pl.pallas_call(
        functools.partial(_combine_body, last=steps - 1),
        grid=(steps,),
        in_specs=[pl.BlockSpec((TOP_K * tm,), lambda i: (i,), memory_space=pltpu.SMEM),
                  pl.BlockSpec((TOP_K * tm,), lambda i: (jnp.minimum(i + 1, steps - 1),),
                               memory_space=pltpu.SMEM),
                  pl.BlockSpec((TOP_K * tm,), lambda i: (jnp.minimum(i + 2, steps - 1),),
                               memory_space=pltpu.SMEM),
                  pl.BlockSpec((tm, d), lambda i: (i, 0)),
                  pl.BlockSpec((tm, ROUTE_LANES), lambda i: (i, 0)),
                  pl.BlockSpec((1, d), lambda i: (0, 0)),
                  pl.BlockSpec(memory_space=pl.ANY)],
        out_specs=pl.BlockSpec((tm, d), lambda i: (i, 0)),
        out_shape=jax.ShapeDtypeStruct((n, d), F32),
        scratch_shapes=([pltpu.VMEM((TOP_K, tm * ROW_TILES, V7X_LANES), U32)] * GATHER_RING
                        + [pltpu.SemaphoreType.DMA((GATHER_RING,))]),
        compiler_params=_cparams(("arbitrary",)),
        name="combine",
    )(dest, dest, dest, h, route, fw, ys)


def _plan(route_t, counts, n_experts, n_blocks):
    eid = route_t[0:TOP_K].astype(jnp.int32)
    rank = route_t[4:4 + TOP_K].astype(jnp.int32)
    cnt = counts[0, N_GROUPS:N_GROUPS + n_experts].astype(jnp.int32)
    padded = (cnt + R_BLK - 1) // R_BLK * R_BLK
    pends = jnp.cumsum(padded)
    pstart = pends - padded
    dest = rank
    for e in range(n_experts):
        dest = dest + jnp.where(eid == e, pstart[e], 0)
    n_used = pends[-1] // R_BLK
    blk = jnp.minimum(jnp.arange(n_blocks, dtype=jnp.int32), n_used - 1)
    block_e = jnp.minimum(jnp.sum(pends[None, :] <= (blk * R_BLK)[:, None], axis=1), n_experts - 1)
    return (dest.astype(jnp.int32), block_e.astype(jnp.int32), n_used.reshape(1).astype(jnp.int32),
            pends.astype(jnp.int32), padded.astype(jnp.int32))


def _layer(h3, norm1_w, w_in, s5_a_re, s5_a_im, s5_b_re, s5_b_im, s5_c_re, s5_c_im, s5_d,
           s5_log_dt, s5_w_glu, s5_b_glu, ret_norm_w, w_out, norm2_w, router_group_w,
           router_group_b, router_expert_w, router_expert_b, moe_w_gate, moe_w_up, moe_w_down,
           out_norm_w):
    nb, seq, d = h3.shape
    n = nb * seq
    d_s5 = s5_d.shape[0]
    d_ret = ret_norm_w.shape[0]
    n_experts = moe_w_gate.shape[0]
    x2 = h3.reshape(n, d)

    u_s5, y_ret = _proj_retention(h3, norm1_w.reshape(1, d), w_in.astype(BF16), d_s5, d_ret,
                                  ret_norm_w.reshape(1, d_ret).astype(F32))

    bp, cp, tp, a_tab = _s5_tables(s5_a_re, s5_a_im, s5_b_re, s5_b_im, s5_c_re, s5_c_im,
                                   s5_log_dt, nb, S5_TAU)
    y_s5 = _s5(u_s5.reshape(n, d_s5), nb, bp, cp, tp, a_tab, s5_d.reshape(1, d_s5).astype(F32),
               s5_w_glu.astype(BF16), s5_b_glu.reshape(1, d_s5).astype(F32))

    n_route = N_GROUPS + n_experts
    wr = jnp.zeros((d, ROUTE_LANES), F32).at[:, :n_route].set(
        jnp.concatenate([router_group_w, router_expert_w], axis=1).astype(F32))
    br = jnp.zeros((1, ROUTE_LANES), F32).at[0, :n_route].set(
        jnp.concatenate([router_group_b, router_expert_b]).astype(F32))
    wr_hi = wr.astype(BF16)
    wr = jnp.concatenate([wr_hi, (wr - wr_hi.astype(F32)).astype(BF16)], axis=1)
    h, route, route_t, counts = _route(y_s5.reshape(n, d_s5), y_ret.reshape(n, d_ret), x2,
                                       w_out.astype(BF16), norm2_w.reshape(1, d), wr, br)

    n_blocks = (n * TOP_K) // R_BLK + n_experts
    dest, block_e, n_used, pends, padded = _plan(route_t, counts, n_experts, n_blocks)
    dest = dest.reshape(TOP_K, n // TM_ROWS, TM_ROWS).transpose(1, 0, 2).reshape(-1)
    xs = _dispatch(pends, padded, dest, h, norm2_w.reshape(1, d), n_blocks * R_BLK)
    ys = _experts(block_e, n_used, xs, moe_w_gate, moe_w_up, moe_w_down)
    out = _combine(dest, h, route, out_norm_w.reshape(1, d), ys)
    return out.reshape(nb, seq, d)


def kernel(x, norm1_w, w_in, s5_a_re, s5_a_im, s5_b_re, s5_b_im, s5_c_re, s5_c_im, s5_d, s5_log_dt, s5_w_glu, s5_b_glu, ret_norm_w, w_out, norm2_w, router_group_w, router_group_b, router_expert_w, router_expert_b, moe_w_gate, moe_w_up, moe_w_down, final_norm_w):
    depth = norm1_w.shape[0]
    assert depth == 1, "the fused final norm assumes a single layer"
    l = 0
    return _layer(x, norm1_w[l], w_in[l], s5_a_re[l], s5_a_im[l], s5_b_re[l], s5_b_im[l],
                  s5_c_re[l], s5_c_im[l], s5_d[l], s5_log_dt[l], s5_w_glu[l], s5_b_glu[l],
                  ret_norm_w[l], w_out[l], norm2_w[l], router_group_w[l], router_group_b[l],
                  router_expert_w[l], router_expert_b[l], moe_w_gate[l], moe_w_up[l],
                  moe_w_down[l], final_norm_w)
```

```python
import functools

import jax
import jax.numpy as jnp
from jax import lax
from jax.experimental import pallas as pl
from jax.experimental.pallas import tpu as pltpu

F32 = jnp.float32
BF16 = jnp.bfloat16

EPS = 1e-6
ROPE_BASE = 10000.0
RET_HEADS = 8
TOP_K = 2
N_GROUPS = 4
EXPERTS_PER_GROUP = 8

V7X_LANES = 128
V7X_SUBLANES = 8
V7X_VMEM_LIMIT = 56 * 1024 * 1024

T_S5 = 512
S5_TAU = 8
S5_PAD = 8
T_RET = 128
TM_ROUTE = 512
TM_ROWS = 256
R_BLK = 512
GATHER_RING = 3
ROUTE_LANES = 128
ROUTE_FIELDS = 8


def _rms(x, w):
    return x * lax.rsqrt(jnp.mean(x * x, axis=-1, keepdims=True) + EPS) * w


def _cparams(sem):
    return pltpu.CompilerParams(dimension_semantics=sem, vmem_limit_bytes=V7X_VMEM_LIMIT)


ROW_TILES = V7X_SUBLANES // 2
U32 = jnp.uint32


def _pack_rows(x):
    half = x.shape[1] // 2

    def bf16_bits(v):
        return lax.bitcast_convert_type(v.astype(BF16).astype(F32), U32)

    return bf16_bits(x[:, half:]) | (bf16_bits(x[:, :half]) >> 16)


def _unpack_rows(w):
    lo = lax.bitcast_convert_type(w << 16, F32)
    hi = lax.bitcast_convert_type(w & U32(0xFFFF0000), F32)
    return jnp.concatenate([lo, hi], axis=1)


def _rows_to_tiles(ref, val):
    rows = val.shape[0]
    for s in range(ROW_TILES):
        ref[pl.ds(s, rows, stride=ROW_TILES), :] = val[:, s * V7X_LANES:(s + 1) * V7X_LANES]


def _tiles_to_rows(ref, rows):
    return jnp.concatenate(
        [ref[pl.ds(s, rows, stride=ROW_TILES), :] for s in range(ROW_TILES)], axis=1)


def _tile_rows(ref, row):
    return ref.at[pl.ds(pl.multiple_of(row * ROW_TILES, ROW_TILES), ROW_TILES)]


def _s5_body(u_ref, bp_ref, cp_ref, tp_ref, a_ref, dd_ref, wglu_ref, bglu_ref, y_ref,
             v_scr, sp_scr, st_scr, io_scr, *, nb, nk, seg, tau):
    @pl.when(pl.program_id(0) == 0)
    def _():
        st_scr[...] = jnp.zeros_like(st_scr)

    nblk = bp_ref.shape[0]
    cw = bp_ref.shape[1] // tau
    sw = bp_ref.shape[2]
    d_s5 = nblk * cw
    tiles = sw // V7X_LANES
    ht = tiles // 2
    ctiles = d_s5 // V7X_LANES
    srows = nb * nk

    u_all = u_ref[...].reshape(srows * tau, d_s5).astype(F32)
    for c in range(ctiles):
        io_scr[c] = u_all[:, c * V7X_LANES:(c + 1) * V7X_LANES]
    u_steps = [jnp.concatenate([io_scr[c, pl.ds(j, srows, stride=tau), :] for c in range(ctiles)],
                               axis=1).astype(BF16) for j in range(tau)]

    def block_inputs(blk):
        return jnp.concatenate([u_steps[j][:, blk * cw:(blk + 1) * cw] for j in range(tau)], axis=1)

    for blk in range(nblk):
        half, q = divmod(blk, 2)
        v = jnp.dot(block_inputs(blk), bp_ref[blk], preferred_element_type=F32)
        for b in range(nb):
            for j in range(tiles):
                v_scr[q * tiles + j, pl.ds((half * nb + b) * seg, nk), :] = (
                    v[b * nk:(b + 1) * nk, j * V7X_LANES:(j + 1) * V7X_LANES])

    rows = 2 * nb
    ar = [[a_ref[0, q * ht + i] for i in range(ht)] for q in range(2)]
    ai = [[a_ref[1, q * ht + i] for i in range(ht)] for q in range(2)]
    sr = [[st_scr[q * tiles + i] for i in range(ht)] for q in range(2)]
    si = [[st_scr[q * tiles + ht + i] for i in range(ht)] for q in range(2)]
    for k in range(nk):
        for q in range(2):
            for i in range(ht):
                jr = q * tiles + i
                ji = q * tiles + ht + i
                sp_scr[jr, pl.ds(k, rows, stride=seg), :] = sr[q][i]
                sp_scr[ji, pl.ds(k, rows, stride=seg), :] = si[q][i]
                vr = v_scr[jr, pl.ds(k, rows, stride=seg), :]
                vi = v_scr[ji, pl.ds(k, rows, stride=seg), :]
                nr = ar[q][i] * sr[q][i] - ai[q][i] * si[q][i] + vr
                ni = ar[q][i] * si[q][i] + ai[q][i] * sr[q][i] + vi
                sr[q][i], si[q][i] = nr, ni
    for q in range(2):
        for i in range(ht):
            st_scr[q * tiles + i] = sr[q][i]
            st_scr[q * tiles + ht + i] = si[q][i]

    yb = []
    for blk in range(nblk):
        half, q = divmod(blk, 2)
        sp = jnp.concatenate(
            [jnp.concatenate([sp_scr[q * tiles + j, pl.ds((half * nb + b) * seg, nk), :]
                              for j in range(tiles)], axis=1) for b in range(nb)],
            axis=0).astype(BF16)
        yb.append(jnp.dot(sp, cp_ref[blk], preferred_element_type=F32)
                  + jnp.dot(block_inputs(blk), tp_ref[blk], preferred_element_type=F32))
    for j in range(tau):
        y = jnp.concatenate([yb[blk][:, j * cw:(j + 1) * cw] for blk in range(nblk)], axis=1)
        y = y + dd_ref[...] * u_steps[j].astype(F32)
        y = jax.nn.gelu(y)
        z = jnp.dot(y.astype(BF16), wglu_ref[...], preferred_element_type=F32) + bglu_ref[...]
        out = y * jax.nn.sigmoid(z)
        for c in range(ctiles):
            io_scr[c, pl.ds(j, srows, stride=tau), :] = out[:, c * V7X_LANES:(c + 1) * V7X_LANES]
    y_all = jnp.concatenate([io_scr[c] for c in range(ctiles)], axis=1)
    y_ref[...] = y_all.reshape(nb, nk * tau, d_s5).astype(y_ref.dtype)


def _s5(u_s5, nb, bp, cp, tp, a_tab, dd, wglu_bf, bglu):
    n, d_s5 = u_s5.shape
    seq = n // nb
    tau = S5_TAU
    nk = T_S5 // tau
    seg = nk + S5_PAD
    nblk, _, sw = bp.shape
    rows = 2 * nb
    assert rows == V7X_SUBLANES and nblk == 4

    def whole(a):
        return pl.BlockSpec(a.shape, lambda c: (0,) * a.ndim)

    body = functools.partial(_s5_body, nb=nb, nk=nk, seg=seg, tau=tau)
    out = pl.pallas_call(
        body,
        grid=(seq // T_S5,),
        in_specs=[pl.BlockSpec((nb, T_S5, d_s5), lambda c: (0, c, 0)),
                  whole(bp), whole(cp), whole(tp), whole(a_tab), whole(dd), whole(wglu_bf),
                  whole(bglu)],
        out_specs=pl.BlockSpec((nb, T_S5, d_s5), lambda c: (0, c, 0)),
        out_shape=jax.ShapeDtypeStruct((nb, seq, d_s5), BF16),
        scratch_shapes=[pltpu.VMEM((2 * sw // V7X_LANES, rows * seg, V7X_LANES), F32),
                        pltpu.VMEM((2 * sw // V7X_LANES, rows * seg, V7X_LANES), F32),
                        pltpu.VMEM((2 * sw // V7X_LANES, rows, V7X_LANES), F32),
                        pltpu.VMEM((d_s5 // V7X_LANES, nb * T_S5, V7X_LANES), F32)],
        compiler_params=_cparams(("arbitrary",)),
        name="s5",
    )(u_s5.reshape(nb, seq, d_s5), bp, cp, tp, a_tab, dd, wglu_bf, bglu)
    return out.reshape(n, d_s5)


def _s5_tables(a_re, a_im, b_re, b_im, c_re, c_im, log_dt, nb, tau):
    hp = lax.Precision.HIGHEST
    g, p = a_re.shape
    hch = b_re.shape[2]
    gpb = V7X_LANES // hch
    nblk = g // gpb
    lam_r, lam_i = a_re.astype(F32), a_im.astype(F32)
    dt = jnp.exp(log_dt.astype(F32))[:, None]
    mag = jnp.exp(lam_r * dt)
    ab_r = mag * jnp.cos(lam_i * dt)
    ab_i = mag * jnp.sin(lam_i * dt)
    den = lam_r * lam_r + lam_i * lam_i
    zr = ((ab_r - 1.0) * lam_r + ab_i * lam_i) / den
    zi = (ab_i * lam_r - (ab_r - 1.0) * lam_i) / den
    br_, bi_ = b_re.astype(F32), b_im.astype(F32)
    bb_r = zr[..., None] * br_ - zi[..., None] * bi_
    bb_i = zr[..., None] * bi_ + zi[..., None] * br_
    cr, ci = c_re.astype(F32), c_im.astype(F32)
    pr, pi = [jnp.ones_like(ab_r)], [jnp.zeros_like(ab_i)]
    for _ in range(tau):
        pr, pi = pr + [pr[-1] * ab_r - pi[-1] * ab_i], pi + [pr[-1] * ab_i + pi[-1] * ab_r]
    pw_r, pw_i = jnp.stack(pr), jnp.stack(pi)

    def blockdiag(x):
        r, c = x.shape[-2:]
        x = jnp.tile(x.reshape(tau, nblk, gpb * r, c), (1, 1, 1, gpb))
        same = (jnp.arange(gpb * r)[:, None] // r) == (jnp.arange(gpb * c)[None, :] // c)
        return jnp.where(same, x, 0.0)

    wr_ = jnp.stack([pr[tau - 1 - j] for j in range(tau)])[:, :, None, :]
    wi_ = jnp.stack([pi[tau - 1 - j] for j in range(tau)])[:, :, None, :]
    bt_r, bt_i = jnp.swapaxes(bb_r, 1, 2)[None], jnp.swapaxes(bb_i, 1, 2)[None]
    bp_r = blockdiag(wr_ * bt_r - wi_ * bt_i)
    bp_i = blockdiag(wr_ * bt_i + wi_ * bt_r)
    bp = jnp.concatenate([jnp.concatenate([bp_r[j], bp_i[j]], axis=-1) for j in range(tau)],
                         axis=1).astype(BF16)

    qr_, qi_ = pw_r[1:, :, :, None], pw_i[1:, :, :, None]
    ct_r, ct_i = jnp.swapaxes(cr, 1, 2)[None], jnp.swapaxes(ci, 1, 2)[None]
    cp_r = blockdiag(ct_r * qr_ - ct_i * qi_)
    cp_i = blockdiag(ct_r * qi_ + ct_i * qr_)
    cp = jnp.concatenate([jnp.concatenate([cp_r[i], -cp_i[i]], axis=1) for i in range(tau)],
                         axis=2).astype(BF16)

    ab_r_ = pw_r[:tau, :, :, None] * bb_r[None] - pw_i[:tau, :, :, None] * bb_i[None]
    ab_i_ = pw_r[:tau, :, :, None] * bb_i[None] + pw_i[:tau, :, :, None] * bb_r[None]
    kd = blockdiag(jnp.einsum('ghp,dgpe->dgeh', cr, ab_r_, precision=hp)
                   - jnp.einsum('ghp,dgpe->dgeh', ci, ab_i_, precision=hp))
    kzero = jnp.zeros_like(kd[0])
    tp = jnp.concatenate(
        [jnp.concatenate([kd[i - j] if i >= j else kzero for i in range(tau)], axis=2)
         for j in range(tau)], axis=1).astype(BF16)

    a_tab = jnp.stack([jnp.repeat(pw_r[tau].reshape(2, -1), nb, axis=0),
                       jnp.repeat(pw_i[tau].reshape(2, -1), nb, axis=0)])
    a_tab = a_tab.reshape(2, 2 * nb, -1, V7X_LANES).transpose(0, 2, 1, 3)
    return bp, cp, tp, a_tab


def _proj_ret_body(x_ref, n1_ref, w_ref, cos_ref, sin_ref, dec_ref, qdec_ref, kdect_ref,
                   cdec_ref, ms_ref, avg_ref, nw_ref, us5_ref, y_ref, u_even, u_odd, st_scr,
                   *, t_len, dh, scale):
    step = pl.program_id(0)
    nb = x_ref.shape[0]
    width = y_ref.shape[-1]
    d_s5 = us5_ref.shape[-1]

    @pl.when(step == 0)
    def _():
        u_odd[...] = jnp.zeros_like(u_odd)
        st_scr[...] = jnp.zeros_like(st_scr)

    def project(u_out):
        xn = _rms(x_ref[...].reshape(nb * t_len, x_ref.shape[-1]), n1_ref[...]).astype(BF16)
        u = jnp.dot(xn, w_ref[...], preferred_element_type=F32)
        us5_ref[...] = u[:, :d_s5].reshape(nb, t_len, d_s5).astype(us5_ref.dtype)
        u_out[...] = u[:, d_s5:].astype(u_out.dtype)

    for parity, (u_out, u_in) in enumerate(((u_even, u_odd), (u_odd, u_even))):
        @pl.when(step % 2 == parity)
        def _():
            project(u_out)
            _ret_chunk(u_in, cos_ref, sin_ref, dec_ref, qdec_ref, kdect_ref, cdec_ref, ms_ref, avg_ref,
                       nw_ref, y_ref, st_scr, t_len=t_len, dh=dh, scale=scale)


def _ret_chunk(u_ref, cos_ref, sin_ref, dec_ref, qdec_ref, kdect_ref, cdec_ref, ms_ref, avg_ref,
               nw_ref, y_ref, st_scr, *, t_len, dh, scale):
    nb = y_ref.shape[0]
    width = y_ref.shape[-1]
    heads = width // dh
    pairs = width // V7X_LANES
    reps = width // cos_ref.shape[-1]
    cos = jnp.concatenate([cos_ref[...]] * reps, axis=1)
    sin = jnp.concatenate([sin_ref[...]] * reps, axis=1)
    lane = lax.broadcasted_iota(jnp.int32, (t_len, width), 1)
    first = (lane % dh) < (dh // 2)

    def rot(x):
        x = x.astype(F32)
        partner = jnp.where(first, pltpu.roll(x, width - dh // 2, 1), pltpu.roll(x, dh // 2, 1))
        return x * cos + partner * sin

    def group_mean(x):
        hi = x.astype(BF16)
        lo = (x - hi.astype(F32)).astype(BF16)
        return (jnp.dot(hi, avg_ref[...], preferred_element_type=F32)
                + jnp.dot(lo, avg_ref[...], preferred_element_type=F32))

    zero_k = jnp.zeros((dh, t_len), BF16)
    zero_v = jnp.zeros((t_len, V7X_LANES), BF16)
    low_head = lax.broadcasted_iota(jnp.int32, (t_len, V7X_LANES), 1) < dh

    for b in range(nb):
        rows = slice(b * t_len, (b + 1) * t_len)
        q = rot(u_ref[rows, 0:width])
        k = rot(u_ref[rows, width:2 * width]) * scale
        vb = u_ref[rows, 2 * width:3 * width]
        kt = k.T
        ktb = kt.astype(BF16)
        kbd = jnp.concatenate(
            [jnp.concatenate([ktb[h * dh:(h + 1) * dh] if hh == h else zero_k for hh in range(heads)],
                             axis=0) for h in range(heads)], axis=1)
        sc = jnp.dot(q.astype(BF16), kbd, preferred_element_type=F32) * dec_ref[...]
        vbd_rows = []
        for h in range(heads):
            p = h // (V7X_LANES // dh)
            vt = vb[:, p * V7X_LANES:(p + 1) * V7X_LANES]
            keep = low_head if h % (V7X_LANES // dh) == 0 else jnp.logical_not(low_head)
            vbd_rows.append(jnp.concatenate(
                [jnp.where(keep, vt, zero_v) if pp == p else zero_v for pp in range(pairs)], axis=1))
        vbd = jnp.concatenate(vbd_rows, axis=0)
        inner = jnp.dot(sc.astype(BF16), vbd, preferred_element_type=F32)
        qd = (q * qdec_ref[...]).astype(BF16)
        kdt = (kt * kdect_ref[...]).astype(BF16)
        crosses = []
        for p in range(pairs):
            ps = slice(p * V7X_LANES, (p + 1) * V7X_LANES)
            state = st_scr[b, p]
            crosses.append(jnp.dot(qd[:, ps], state.astype(BF16), preferred_element_type=F32))
            kv = jnp.dot(kdt[ps, :], vb[:, ps], preferred_element_type=F32)
            st_scr[b, p] = state * cdec_ref[:, ps] + kv * ms_ref[...]
        o = inner + jnp.concatenate(crosses, axis=1)
        dlt = o - group_mean(o)
        var = jnp.dot((dlt * dlt).astype(BF16), avg_ref[...], preferred_element_type=F32)
        on = dlt * lax.rsqrt(var + EPS) * nw_ref[...]
        gate = u_ref[rows, 3 * width:4 * width].astype(F32)
        y_ref[b] = (jax.nn.silu(gate) * on).astype(y_ref.dtype)


def _proj_retention(x3, n1, w_bf, d_s5, d_ret, norm_w):
    nb, seq, d = x3.shape
    dh = d_ret // RET_HEADS
    half = dh // 2
    t_len = T_RET
    lg = jnp.log(1.0 - 2.0 ** (-5.0 - jnp.arange(RET_HEADS, dtype=F32)))
    t = jnp.arange(t_len, dtype=F32)
    diff = t[:, None] - t[None, :]
    dec = jnp.where(diff >= 0, jnp.exp(lg[:, None, None] * jnp.maximum(diff, 0.0)), 0.0)
    dec_all = dec.transpose(1, 0, 2).reshape(t_len, RET_HEADS * t_len)
    qdec = jnp.repeat(jnp.exp(lg[:, None] * (t + 1.0)[None, :]).T, dh, axis=1)
    kdect = jnp.repeat(jnp.exp(lg[:, None] * (t_len - 1 - t)[None, :]), dh, axis=0)
    cdec = jnp.repeat(jnp.exp(lg * t_len), dh)[None, :]
    head_of_lane = jnp.arange(d_ret) // dh
    pair_head = jnp.arange(V7X_LANES) // dh
    mask_s = (pair_head[:, None] == pair_head[None, :]).astype(F32)
    avg = (head_of_lane[:, None] == head_of_lane[None, :]).astype(F32) / dh
    assert dh & (dh - 1) == 0
    avg = avg.astype(BF16)
    inv = ROPE_BASE ** (-jnp.arange(half, dtype=F32) / half)
    ang = jnp.arange(seq, dtype=F32)[:, None] * inv[None, :]
    reps = V7X_LANES // dh
    cos_t = jnp.tile(jnp.cos(ang), (1, 2 * reps))
    sin_t = jnp.tile(jnp.concatenate([-jnp.sin(ang), jnp.sin(ang)], axis=1), (1, reps))

    def whole(a):
        return pl.BlockSpec(a.shape, lambda c: (0,) * a.ndim)

    nc = seq // t_len

    def proj_chunk(c):
        return jnp.minimum(c, nc - 1)

    def ret_chunk(c):
        return jnp.maximum(c - 1, 0)

    body = functools.partial(_proj_ret_body, t_len=t_len, dh=dh, scale=dh ** -0.5)
    return pl.pallas_call(
        body,
        grid=(nc + 1,),
        in_specs=[pl.BlockSpec((nb, t_len, d), lambda c: (0, proj_chunk(c), 0)),
                  whole(n1), whole(w_bf),
                  pl.BlockSpec((t_len, V7X_LANES), lambda c: (ret_chunk(c), 0)),
                  pl.BlockSpec((t_len, V7X_LANES), lambda c: (ret_chunk(c), 0)),
                  whole(dec_all), whole(qdec), whole(kdect), whole(cdec),
                  whole(mask_s), whole(avg), whole(norm_w)],
        out_specs=[pl.BlockSpec((nb, t_len, d_s5), lambda c: (0, proj_chunk(c), 0)),
                   pl.BlockSpec((nb, t_len, d_ret), lambda c: (0, ret_chunk(c), 0))],
        out_shape=[jax.ShapeDtypeStruct((nb, seq, d_s5), BF16),
                   jax.ShapeDtypeStruct((nb, seq, d_ret), BF16)],
        scratch_shapes=[pltpu.VMEM((nb * t_len, 4 * d_ret), BF16),
                        pltpu.VMEM((nb * t_len, 4 * d_ret), BF16),
                        pltpu.VMEM((nb, d_ret // V7X_LANES, V7X_LANES, V7X_LANES), F32)],
        compiler_params=_cparams(("arbitrary",)),
        name="inproj_retention",
    )(x3, n1, w_bf, cos_t, sin_t, dec_all, qdec, kdect, cdec, mask_s, avg, norm_w)


def _route_body(ys5_ref, yret_ref, x_ref, wo_ref, n2_ref, wr_ref, br_ref, tri_ref,
                h_ref, route_ref, route_t_ref, cnt_ref, carry_scr):
    @pl.when(pl.program_id(0) == 0)
    def _():
        carry_scr[...] = jnp.zeros_like(carry_scr)

    d_s5 = ys5_ref.shape[1]
    h = (x_ref[...]
         + jnp.dot(ys5_ref[...], wo_ref[0:d_s5], preferred_element_type=F32)
         + jnp.dot(yret_ref[...], wo_ref[d_s5:], preferred_element_type=F32))
    h_ref[...] = h
    hn = _rms(h, n2_ref[...])
    hi = hn.astype(BF16)
    lo = (hn - hi.astype(F32)).astype(BF16)
    p_hi = jnp.dot(hi, wr_ref[...], preferred_element_type=F32)
    p_lo = jnp.dot(lo, wr_ref[:, 0:ROUTE_LANES], preferred_element_type=F32)
    logits = p_hi[:, 0:ROUTE_LANES] + p_hi[:, ROUTE_LANES:] + p_lo + br_ref[...]
    tm = logits.shape[0]
    lane = lax.broadcasted_iota(jnp.int32, (tm, ROUTE_LANES), 1)
    lanef = lane.astype(F32)
    neg = -jnp.inf
    big = float(ROUTE_LANES)
    gl = jnp.where(lane < N_GROUPS, logits, neg)
    gmax = jnp.max(gl, axis=-1, keepdims=True)
    gidx = jnp.min(jnp.where(gl == gmax, lanef, big), axis=-1, keepdims=True)
    g_w = 1.0 / jnp.sum(jnp.exp(gl - gmax), axis=-1, keepdims=True)
    lo = N_GROUPS + EXPERTS_PER_GROUP * gidx
    el = jnp.where((lanef >= lo) & (lanef < lo + EXPERTS_PER_GROUP), logits, neg)
    v0 = jnp.max(el, axis=-1, keepdims=True)
    i0 = jnp.min(jnp.where(el == v0, lanef, big), axis=-1, keepdims=True)
    el2 = jnp.where(lanef == i0, neg, el)
    v1 = jnp.max(el2, axis=-1, keepdims=True)
    i1 = jnp.min(jnp.where(el2 == v1, lanef, big), axis=-1, keepdims=True)
    e = jnp.exp(v1 - v0)
    den = 1.0 + e
    w0 = (1.0 / den) * g_w
    w1 = (e / den) * g_w
    sel0 = lanef == i0
    sel1 = lanef == i1
    onehot = jnp.where(sel0 | sel1, 1.0, 0.0)
    before = jnp.dot(tri_ref[...], onehot.astype(BF16), preferred_element_type=F32) + carry_scr[...]
    r0 = jnp.sum(jnp.where(sel0, before, 0.0), axis=-1, keepdims=True)
    r1 = jnp.sum(jnp.where(sel1, before, 0.0), axis=-1, keepdims=True)
    carry_scr[...] += jnp.sum(onehot, axis=0, keepdims=True)
    cnt_ref[...] = carry_scr[...]
    rec = jnp.zeros((tm, ROUTE_LANES), F32)
    for j, val in enumerate((i0 - N_GROUPS, i1 - N_GROUPS, w0, w1, r0, r1)):
        rec = jnp.where(lane == j, val, rec)
    route_ref[...] = rec
    route_t_ref[...] = rec.T[0:ROUTE_FIELDS]


def _route(ys5, yret, x2, wo_bf, n2, wr, br):
    n, d = x2.shape
    d_s5 = ys5.shape[1]
    d_ret = yret.shape[1]
    tm = TM_ROUTE
    tri = (jnp.arange(tm)[:, None] > jnp.arange(tm)[None, :]).astype(BF16)
    return pl.pallas_call(
        _route_body,
        grid=(n // tm,),
        in_specs=[pl.BlockSpec((tm, d_s5), lambda i: (i, 0)),
                  pl.BlockSpec((tm, d_ret), lambda i: (i, 0)),
                  pl.BlockSpec((tm, d), lambda i: (i, 0)),
                  pl.BlockSpec((d_s5 + d_ret, d), lambda i: (0, 0)),
                  pl.BlockSpec((1, d), lambda i: (0, 0)),
                  pl.BlockSpec((d, 2 * ROUTE_LANES), lambda i: (0, 0)),
                  pl.BlockSpec((1, ROUTE_LANES), lambda i: (0, 0)),
                  pl.BlockSpec((tm, tm), lambda i: (0, 0))],
        out_specs=[pl.BlockSpec((tm, d), lambda i: (i, 0)),
                   pl.BlockSpec((tm, ROUTE_LANES), lambda i: (i, 0)),
                   pl.BlockSpec((ROUTE_FIELDS, tm), lambda i: (0, i)),
                   pl.BlockSpec((1, ROUTE_LANES), lambda i: (0, 0))],
        out_shape=[jax.ShapeDtypeStruct((n, d), F32),
                   jax.ShapeDtypeStruct((n, ROUTE_LANES), F32),
                   jax.ShapeDtypeStruct((ROUTE_FIELDS, n), F32),
                   jax.ShapeDtypeStruct((1, ROUTE_LANES), F32)],
        scratch_shapes=[pltpu.VMEM((1, ROUTE_LANES), F32)],
        compiler_params=_cparams(("arbitrary",)),
        name="outproj_route",
    )(ys5, yret, x2, wo_bf, n2, wr, br, tri)


def _dispatch_body(pends_ref, padded_ref, dest_ref, h_ref, n2_ref, xs_ref,
                   hn_scr, zero_scr, sem, zsem):
    tm = h_ref.shape[0]

    @pl.when(pl.program_id(0) == 0)
    def _():
        zero_scr[...] = jnp.zeros_like(zero_scr)

        def zero_copy(e):
            first = pl.multiple_of((pends_ref[e] - R_BLK) * ROW_TILES, R_BLK * ROW_TILES)
            return pltpu.make_async_copy(zero_scr, xs_ref.at[pl.ds(first, R_BLK * ROW_TILES)], zsem)

        def zstart(e, carry):
            @pl.when(padded_ref[e] > 0)
            def _():
                zero_copy(e).start()
            return carry

        def zwait(e, carry):
            @pl.when(padded_ref[e] > 0)
            def _():
                zero_copy(e).wait()
            return carry

        lax.fori_loop(0, pends_ref.shape[0], zstart, 0)
        lax.fori_loop(0, pends_ref.shape[0], zwait, 0)

    step = pl.program_id(0)
    slot = step % 2
    buf = hn_scr.at[slot]
    _rows_to_tiles(buf, _pack_rows(_rms(h_ref[...], n2_ref[...])))

    for r in range(tm):
        for k in range(TOP_K):
            pltpu.make_async_copy(_tile_rows(buf, r), _tile_rows(xs_ref, dest_ref[k * tm + r]),
                                  sem.at[slot]).start(priority=k)

    def drain(s):
        for k in range(TOP_K):
            pltpu.make_async_copy(hn_scr.at[s], xs_ref.at[pl.ds(0, tm * ROW_TILES)], sem.at[s]).wait()

    @pl.when(step > 0)
    def _():
        drain(1 - slot)

    @pl.when(step == pl.num_programs(0) - 1)
    def _():
        drain(slot)


def _dispatch(pends, padded, dest, h, n2, p_rows):
    n, d = h.shape
    tm = TM_ROWS
    grid_spec = pltpu.PrefetchScalarGridSpec(
        num_scalar_prefetch=2,
        grid=(n // tm,),
        in_specs=[pl.BlockSpec((TOP_K * tm,), lambda i, pe, pa: (i,), memory_space=pltpu.SMEM),
                  pl.BlockSpec((tm, d), lambda i, pe, pa: (i, 0)),
                  pl.BlockSpec((1, d), lambda i, pe, pa: (0, 0))],
        out_specs=pl.BlockSpec(memory_space=pl.ANY),
        scratch_shapes=[pltpu.VMEM((2, tm * ROW_TILES, V7X_LANES), U32),
                        pltpu.VMEM((R_BLK * ROW_TILES, V7X_LANES), U32),
                        pltpu.SemaphoreType.DMA((2,)), pltpu.SemaphoreType.DMA(())])
    assert d == 2 * ROW_TILES * V7X_LANES
    return pl.pallas_call(
        _dispatch_body,
        grid_spec=grid_spec,
        out_shape=jax.ShapeDtypeStruct((p_rows * ROW_TILES, V7X_LANES), U32),
        compiler_params=_cparams(("arbitrary",)),
        name="dispatch",
    )(pends, padded, dest, h, n2)


def _expert_body(first_ref, count_ref, xs_ref, wg_ref, wu_ref, wd_ref, ys_ref,
                 wg_s, wu_s, wd_s, x_buf, y_buf, in_sem, out_sem):
    e = pl.program_id(0)
    n = count_ref[e]
    blk = R_BLK * ROW_TILES

    def rows_of(i):
        return pl.ds(pl.multiple_of((first_ref[e] + i) * blk, blk), blk)

    def in_copy(i, slot):
        return pltpu.make_async_copy(xs_ref.at[rows_of(i)], x_buf.at[slot], in_sem.at[slot])

    def out_copy(i, slot):
        return pltpu.make_async_copy(y_buf.at[slot], ys_ref.at[rows_of(i)], out_sem.at[slot])

    @pl.when(n > 0)
    def _():
        in_copy(0, 0).start()
        wg_s[...] = wg_ref[...].astype(BF16)
        wu_s[...] = wu_ref[...].astype(BF16)
        wd_s[...] = wd_ref[...].astype(BF16)

        def block(i, carry):
            slot = i % 2
            in_copy(i, slot).wait()

            @pl.when(i + 1 < n)
            def _():
                in_copy(i + 1, 1 - slot).start()

            @pl.when(i >= 2)
            def _():
                out_copy(i - 2, slot).wait()

            x = _unpack_rows(_tiles_to_rows(x_buf.at[slot], R_BLK)).astype(BF16)
            gate = jnp.dot(x, wg_s[...], preferred_element_type=F32)
            up = jnp.dot(x, wu_s[...], preferred_element_type=F32)
            hid = (jax.nn.silu(gate) * up).astype(BF16)
            _rows_to_tiles(y_buf.at[slot],
                           _pack_rows(jnp.dot(hid, wd_s[...], preferred_element_type=F32)))
            out_copy(i, slot).start()
            return carry

        lax.fori_loop(0, n, block, 0)

        @pl.when(n >= 2)
        def _():
            out_copy(n - 2, n % 2).wait()

        out_copy(n - 1, (n - 1) % 2).wait()


def _experts(first_blk, n_blk, xs, w_gate, w_up, w_down):
    n_experts, d, d_e = w_gate.shape
    blk_rows = R_BLK * ROW_TILES

    def w_map(e, first, count):
        return (e, 0, 0)

    grid_spec = pltpu.PrefetchScalarGridSpec(
        num_scalar_prefetch=2,
        grid=(n_experts,),
        in_specs=[pl.BlockSpec(memory_space=pl.ANY),
                  pl.BlockSpec((None, d, d_e), w_map),
                  pl.BlockSpec((None, d, d_e), w_map),
                  pl.BlockSpec((None, d_e, d), w_map)],
        out_specs=pl.BlockSpec(memory_space=pl.ANY),
        scratch_shapes=[pltpu.VMEM((d, d_e), BF16), pltpu.VMEM((d, d_e), BF16),
                        pltpu.VMEM((d_e, d), BF16),
                        pltpu.VMEM((2, blk_rows, V7X_LANES), xs.dtype),
                        pltpu.VMEM((2, blk_rows, V7X_LANES), xs.dtype),
                        pltpu.SemaphoreType.DMA((2,)), pltpu.SemaphoreType.DMA((2,))])
    return pl.pallas_call(
        _expert_body,
        grid_spec=grid_spec,
        out_shape=jax.ShapeDtypeStruct(xs.shape, xs.dtype),
        compiler_params=_cparams(("arbitrary",)),
        name="experts",
    )(first_blk, n_blk, xs, w_gate, w_up, w_down)


def _combine_body(dest0_ref, dest1_ref, dest2_ref, h_ref, route_ref, fw_ref, ys_ref, out_ref,
                  *scratch, last):
    bufs, sem = scratch[:GATHER_RING], scratch[GATHER_RING]
    tm = h_ref.shape[0]
    step = pl.program_id(0)

    def issue(d_ref, s):
        for r in range(tm):
            for k in range(TOP_K):
                pltpu.make_async_copy(_tile_rows(ys_ref, d_ref[k * tm + r]),
                                      _tile_rows(bufs[s].at[k], r), sem.at[s]).start(priority=k)

    def drain(s):
        for k in range(TOP_K):
            pltpu.make_async_copy(ys_ref.at[pl.ds(0, tm * ROW_TILES)], bufs[s].at[k], sem.at[s]).wait()

    @pl.when(step == 0)
    def _():
        issue(dest0_ref, 0)
        issue(dest1_ref, 1)

    for s in range(GATHER_RING):
        @pl.when(step % GATHER_RING == s)
        def _():
            drain(s)
            issue(dest2_ref, (s + 2) % GATHER_RING)
            rec = route_ref[...]
            hh = (h_ref[...] + rec[:, 2:3] * _unpack_rows(_tiles_to_rows(bufs[s].at[0], tm))
                  + rec[:, 3:4] * _unpack_rows(_tiles_to_rows(bufs[s].at[1], tm)))
            out_ref[...] = _rms(hh, fw_ref[...])

    @pl.when(step == last)
    def _():
        drain((last + 1) % GATHER_RING)
        drain((last + 2) % GATHER_RING)


def _combine(dest, h, route, fw, ys):
    n, d = h.shape
    tm = TM_ROWS
    steps = n // tm
    return pl.pallas_call(
        functools.partial(_combine_body, last=steps - 1),
        grid=(steps,),
        in_specs=[pl.BlockSpec((TOP_K * tm,), lambda i: (i,), memory_space=pltpu.SMEM),
                  pl.BlockSpec((TOP_K * tm,), lambda i: (jnp.minimum(i + 1, steps - 1),),
                               memory_space=pltpu.SMEM),
                  pl.BlockSpec((TOP_K * tm,), lambda i: (jnp.minimum(i + 2, steps - 1),),
                               memory_space=pltpu.SMEM),
                  pl.BlockSpec((tm, d), lambda i: (i, 0)),
                  pl.BlockSpec((tm, ROUTE_LANES), lambda i: (i, 0)),
                  pl.BlockSpec((1, d), lambda i: (0, 0)),
                  pl.BlockSpec(memory_space=pl.ANY)],
        out_specs=pl.BlockSpec((tm, d), lambda i: (i, 0)),
        out_shape=jax.ShapeDtypeStruct((n, d), F32),
        scratch_shapes=([pltpu.VMEM((TOP_K, tm * ROW_TILES, V7X_LANES), U32)] * GATHER_RING
                        + [pltpu.SemaphoreType.DMA((GATHER_RING,))]),
        compiler_params=_cparams(("arbitrary",)),
        name="combine",
    )(dest, dest, dest, h, route, fw, ys)


def _plan(route_t, counts, n_experts):
    eid = route_t[0:TOP_K].astype(jnp.int32)
    rank = route_t[4:4 + TOP_K].astype(jnp.int32)
    cnt = counts[0, N_GROUPS:N_GROUPS + n_experts].astype(jnp.int32)
    padded = (cnt + R_BLK - 1) // R_BLK * R_BLK
    pends = jnp.cumsum(padded)
    pstart = pends - padded
    dest = rank
    for e in range(n_experts):
        dest = dest + jnp.where(eid == e, pstart[e], 0)
    return (dest.astype(jnp.int32), pends.astype(jnp.int32), padded.astype(jnp.int32),
            (pstart // R_BLK).astype(jnp.int32), (padded // R_BLK).astype(jnp.int32))


def _layer(h3, norm1_w, w_in, s5_a_re, s5_a_im, s5_b_re, s5_b_im, s5_c_re, s5_c_im, s5_d,
           s5_log_dt, s5_w_glu, s5_b_glu, ret_norm_w, w_out, norm2_w, router_group_w,
           router_group_b, router_expert_w, router_expert_b, moe_w_gate, moe_w_up, moe_w_down,
           out_norm_w):
    nb, seq, d = h3.shape
    n = nb * seq
    d_s5 = s5_d.shape[0]
    d_ret = ret_norm_w.shape[0]
    n_experts = moe_w_gate.shape[0]
    x2 = h3.reshape(n, d)

    u_s5, y_ret = _proj_retention(h3, norm1_w.reshape(1, d), w_in.astype(BF16), d_s5, d_ret,
                                  ret_norm_w.reshape(1, d_ret).astype(F32))

    bp, cp, tp, a_tab = _s5_tables(s5_a_re, s5_a_im, s5_b_re, s5_b_im, s5_c_re, s5_c_im,
                                   s5_log_dt, nb, S5_TAU)
    y_s5 = _s5(u_s5.reshape(n, d_s5), nb, bp, cp, tp, a_tab, s5_d.reshape(1, d_s5).astype(F32),
               s5_w_glu.astype(BF16), s5_b_glu.reshape(1, d_s5).astype(F32))

    n_route = N_GROUPS + n_experts
    wr = jnp.zeros((d, ROUTE_LANES), F32).at[:, :n_route].set(
        jnp.concatenate([router_group_w, router_expert_w], axis=1).astype(F32))
    br = jnp.zeros((1, ROUTE_LANES), F32).at[0, :n_route].set(
        jnp.concatenate([router_group_b, router_expert_b]).astype(F32))
    wr_hi = wr.astype(BF16)
    wr = jnp.concatenate([wr_hi, (wr - wr_hi.astype(F32)).astype(BF16)], axis=1)
    h, route, route_t, counts = _route(y_s5.reshape(n, d_s5), y_ret.reshape(n, d_ret), x2,
                                       w_out.astype(BF16), norm2_w.reshape(1, d), wr, br)

    n_blocks = (n * TOP_K) // R_BLK + n_experts
    dest, pends, padded, first_blk, n_blk = _plan(route_t, counts, n_experts)
    dest = dest.reshape(TOP_K, n // TM_ROWS, TM_ROWS).transpose(1, 0, 2).reshape(-1)
    xs = _dispatch(pends, padded, dest, h, norm2_w.reshape(1, d), n_blocks * R_BLK)
    ys = _experts(first_blk, n_blk, xs, moe_w_gate, moe_w_up, moe_w_down)
    out = _combine(dest, h, route, out_norm_w.reshape(1, d), ys)
    return out.reshape(nb, seq, d)


def kernel(x, norm1_w, w_in, s5_a_re, s5_a_im, s5_b_re, s5_b_im, s5_c_re, s5_c_im, s5_d, s5_log_dt, s5_w_glu, s5_b_glu, ret_norm_w, w_out, norm2_w, router_group_w, router_group_b, router_expert_w, router_expert_b, moe_w_gate, moe_w_up, moe_w_down, final_norm_w):
    depth = norm1_w.shape[0]
    assert depth == 1, "the fused final norm assumes a single layer"
    l = 0
    return _layer(x, norm1_w[l], w_in[l], s5_a_re[l], s5_a_im[l], s5_b_re[l], s5_b_im[l],
                  s5_c_re[l], s5_c_im[l], s5_d[l], s5_log_dt[l], s5_w_glu[l], s5_b_glu[l],
                  ret_norm_w[l], w_out[l], norm2_w[l], router_group_w[l], router_group_b[l],
                  router_expert_w[l], router_expert_b[l], moe_w_gate[l], moe_w_up[l],
                  moe_w_down[l], final_norm_w)
```

```python
import functools

import jax
import jax.numpy as jnp
from jax import lax
from jax.experimental import pallas as pl
from jax.experimental.pallas import tpu as pltpu

F32 = jnp.float32
BF16 = jnp.bfloat16

EPS = 1e-6
ROPE_BASE = 10000.0
RET_HEADS = 8
TOP_K = 2
N_GROUPS = 4
EXPERTS_PER_GROUP = 8

V7X_LANES = 128
V7X_SUBLANES = 8
V7X_VMEM_LIMIT = 56 * 1024 * 1024

T_S5 = 512
S5_TAU = 8
S5_PAD = 8
T_RET = 128
TM_ROUTE = 512
TM_ROWS = 256
R_BLK = 512
GATHER_RING = 3
ROUTE_LANES = 128
ROUTE_FIELDS = 8
ROUTE_EXPERT_ROW = 8


def _rms(x, w):
    return x * lax.rsqrt(jnp.mean(x * x, axis=-1, keepdims=True) + EPS) * w


def _cparams(sem):
    return pltpu.CompilerParams(dimension_semantics=sem, vmem_limit_bytes=V7X_VMEM_LIMIT)


ROW_TILES = V7X_SUBLANES // 2
U32 = jnp.uint32


def _pack_rows(x):
    half = x.shape[1] // 2

    def bf16_bits(v):
        return lax.bitcast_convert_type(v.astype(BF16).astype(F32), U32)

    return bf16_bits(x[:, half:]) | (bf16_bits(x[:, :half]) >> 16)


def _unpack_rows(w):
    lo = lax.bitcast_convert_type(w << 16, F32)
    hi = lax.bitcast_convert_type(w & U32(0xFFFF0000), F32)
    return jnp.concatenate([lo, hi], axis=1)


def _rows_to_tiles(ref, val):
    rows = val.shape[0]
    for s in range(ROW_TILES):
        ref[pl.ds(s, rows, stride=ROW_TILES), :] = val[:, s * V7X_LANES:(s + 1) * V7X_LANES]


def _tiles_to_rows(ref, rows):
    return jnp.concatenate(
        [ref[pl.ds(s, rows, stride=ROW_TILES), :] for s in range(ROW_TILES)], axis=1)


def _tile_rows(ref, row):
    return ref.at[pl.ds(pl.multiple_of(row * ROW_TILES, ROW_TILES), ROW_TILES)]


def _s5_body(u_ref, bp_ref, cp_ref, tp_ref, a_ref, dd_ref, wglu_ref, bglu_ref, y_ref,
             v_scr, sp_scr, st_scr, io_scr, *, nb, nk, seg, tau):
    @pl.when(pl.program_id(0) == 0)
    def _():
        st_scr[...] = jnp.zeros_like(st_scr)

    nblk = bp_ref.shape[0]
    cw = bp_ref.shape[1] // tau
    sw = bp_ref.shape[2]
    d_s5 = nblk * cw
    tiles = sw // V7X_LANES
    ht = tiles // 2
    ctiles = d_s5 // V7X_LANES
    srows = nb * nk

    u_all = u_ref[...].reshape(srows * tau, d_s5).astype(F32)
    for c in range(ctiles):
        io_scr[c] = u_all[:, c * V7X_LANES:(c + 1) * V7X_LANES]
    u_steps = [jnp.concatenate([io_scr[c, pl.ds(j, srows, stride=tau), :] for c in range(ctiles)],
                               axis=1).astype(BF16) for j in range(tau)]

    def block_inputs(blk):
        return jnp.concatenate([u_steps[j][:, blk * cw:(blk + 1) * cw] for j in range(tau)], axis=1)

    for blk in range(nblk):
        half, q = divmod(blk, 2)
        v = jnp.dot(block_inputs(blk), bp_ref[blk], preferred_element_type=F32)
        for b in range(nb):
            for j in range(tiles):
                v_scr[q * tiles + j, pl.ds((half * nb + b) * seg, nk), :] = (
                    v[b * nk:(b + 1) * nk, j * V7X_LANES:(j + 1) * V7X_LANES])

    rows = 2 * nb
    ar = [[a_ref[0, q * ht + i] for i in range(ht)] for q in range(2)]
    ai = [[a_ref[1, q * ht + i] for i in range(ht)] for q in range(2)]
    sr = [[st_scr[q * tiles + i] for i in range(ht)] for q in range(2)]
    si = [[st_scr[q * tiles + ht + i] for i in range(ht)] for q in range(2)]
    for k in range(nk):
        for q in range(2):
            for i in range(ht):
                jr = q * tiles + i
                ji = q * tiles + ht + i
                sp_scr[jr, pl.ds(k, rows, stride=seg), :] = sr[q][i]
                sp_scr[ji, pl.ds(k, rows, stride=seg), :] = si[q][i]
                vr = v_scr[jr, pl.ds(k, rows, stride=seg), :]
                vi = v_scr[ji, pl.ds(k, rows, stride=seg), :]
                nr = ar[q][i] * sr[q][i] - ai[q][i] * si[q][i] + vr
                ni = ar[q][i] * si[q][i] + ai[q][i] * sr[q][i] + vi
                sr[q][i], si[q][i] = nr, ni
    for q in range(2):
        for i in range(ht):
            st_scr[q * tiles + i] = sr[q][i]
            st_scr[q * tiles + ht + i] = si[q][i]

    yb = []
    for blk in range(nblk):
        half, q = divmod(blk, 2)
        sp = jnp.concatenate(
            [jnp.concatenate([sp_scr[q * tiles + j, pl.ds((half * nb + b) * seg, nk), :]
                              for j in range(tiles)], axis=1) for b in range(nb)],
            axis=0).astype(BF16)
        yb.append(jnp.dot(sp, cp_ref[blk], preferred_element_type=F32)
                  + jnp.dot(block_inputs(blk), tp_ref[blk], preferred_element_type=F32))
    for j in range(tau):
        y = jnp.concatenate([yb[blk][:, j * cw:(j + 1) * cw] for blk in range(nblk)], axis=1)
        y = y + dd_ref[...] * u_steps[j].astype(F32)
        y = jax.nn.gelu(y)
        z = jnp.dot(y.astype(BF16), wglu_ref[...], preferred_element_type=F32) + bglu_ref[...]
        out = y * jax.nn.sigmoid(z)
        for c in range(ctiles):
            io_scr[c, pl.ds(j, srows, stride=tau), :] = out[:, c * V7X_LANES:(c + 1) * V7X_LANES]
    y_all = jnp.concatenate([io_scr[c] for c in range(ctiles)], axis=1)
    y_ref[...] = y_all.reshape(nb, nk * tau, d_s5).astype(y_ref.dtype)


def _s5(u_s5, nb, bp, cp, tp, a_tab, dd, wglu_bf, bglu):
    n, d_s5 = u_s5.shape
    seq = n // nb
    tau = S5_TAU
    nk = T_S5 // tau
    seg = nk + S5_PAD
    nblk, _, sw = bp.shape
    rows = 2 * nb
    assert rows == V7X_SUBLANES and nblk == 4

    def whole(a):
        return pl.BlockSpec(a.shape, lambda c: (0,) * a.ndim)

    body = functools.partial(_s5_body, nb=nb, nk=nk, seg=seg, tau=tau)
    out = pl.pallas_call(
        body,
        grid=(seq // T_S5,),
        in_specs=[pl.BlockSpec((nb, T_S5, d_s5), lambda c: (0, c, 0)),
                  whole(bp), whole(cp), whole(tp), whole(a_tab), whole(dd), whole(wglu_bf),
                  whole(bglu)],
        out_specs=pl.BlockSpec((nb, T_S5, d_s5), lambda c: (0, c, 0)),
        out_shape=jax.ShapeDtypeStruct((nb, seq, d_s5), BF16),
        scratch_shapes=[pltpu.VMEM((2 * sw // V7X_LANES, rows * seg, V7X_LANES), F32),
                        pltpu.VMEM((2 * sw // V7X_LANES, rows * seg, V7X_LANES), F32),
                        pltpu.VMEM((2 * sw // V7X_LANES, rows, V7X_LANES), F32),
                        pltpu.VMEM((d_s5 // V7X_LANES, nb * T_S5, V7X_LANES), F32)],
        compiler_params=_cparams(("arbitrary",)),
        name="s5",
    )(u_s5.reshape(nb, seq, d_s5), bp, cp, tp, a_tab, dd, wglu_bf, bglu)
    return out.reshape(n, d_s5)


def _s5_tables(a_re, a_im, b_re, b_im, c_re, c_im, log_dt, nb, tau):
    hp = lax.Precision.HIGHEST
    g, p = a_re.shape
    hch = b_re.shape[2]
    gpb = V7X_LANES // hch
    nblk = g // gpb
    lam_r, lam_i = a_re.astype(F32), a_im.astype(F32)
    dt = jnp.exp(log_dt.astype(F32))[:, None]
    mag = jnp.exp(lam_r * dt)
    ab_r = mag * jnp.cos(lam_i * dt)
    ab_i = mag * jnp.sin(lam_i * dt)
    den = lam_r * lam_r + lam_i * lam_i
    zr = ((ab_r - 1.0) * lam_r + ab_i * lam_i) / den
    zi = (ab_i * lam_r - (ab_r - 1.0) * lam_i) / den
    br_, bi_ = b_re.astype(F32), b_im.astype(F32)
    bb_r = zr[..., None] * br_ - zi[..., None] * bi_
    bb_i = zr[..., None] * bi_ + zi[..., None] * br_
    cr, ci = c_re.astype(F32), c_im.astype(F32)
    pr, pi = [jnp.ones_like(ab_r)], [jnp.zeros_like(ab_i)]
    for _ in range(tau):
        pr, pi = pr + [pr[-1] * ab_r - pi[-1] * ab_i], pi + [pr[-1] * ab_i + pi[-1] * ab_r]
    pw_r, pw_i = jnp.stack(pr), jnp.stack(pi)

    def blockdiag(x):
        r, c = x.shape[-2:]
        x = jnp.tile(x.reshape(tau, nblk, gpb * r, c), (1, 1, 1, gpb))
        same = (jnp.arange(gpb * r)[:, None] // r) == (jnp.arange(gpb * c)[None, :] // c)
        return jnp.where(same, x, 0.0)

    wr_ = jnp.stack([pr[tau - 1 - j] for j in range(tau)])[:, :, None, :]
    wi_ = jnp.stack([pi[tau - 1 - j] for j in range(tau)])[:, :, None, :]
    bt_r, bt_i = jnp.swapaxes(bb_r, 1, 2)[None], jnp.swapaxes(bb_i, 1, 2)[None]
    bp_r = blockdiag(wr_ * bt_r - wi_ * bt_i)
    bp_i = blockdiag(wr_ * bt_i + wi_ * bt_r)
    bp = jnp.concatenate([jnp.concatenate([bp_r[j], bp_i[j]], axis=-1) for j in range(tau)],
                         axis=1).astype(BF16)

    qr_, qi_ = pw_r[1:, :, :, None], pw_i[1:, :, :, None]
    ct_r, ct_i = jnp.swapaxes(cr, 1, 2)[None], jnp.swapaxes(ci, 1, 2)[None]
    cp_r = blockdiag(ct_r * qr_ - ct_i * qi_)
    cp_i = blockdiag(ct_r * qi_ + ct_i * qr_)
    cp = jnp.concatenate([jnp.concatenate([cp_r[i], -cp_i[i]], axis=1) for i in range(tau)],
                         axis=2).astype(BF16)

    ab_r_ = pw_r[:tau, :, :, None] * bb_r[None] - pw_i[:tau, :, :, None] * bb_i[None]
    ab_i_ = pw_r[:tau, :, :, None] * bb_i[None] + pw_i[:tau, :, :, None] * bb_r[None]
    kd = blockdiag(jnp.einsum('ghp,dgpe->dgeh', cr, ab_r_, precision=hp)
                   - jnp.einsum('ghp,dgpe->dgeh', ci, ab_i_, precision=hp))
    kzero = jnp.zeros_like(kd[0])
    tp = jnp.concatenate(
        [jnp.concatenate([kd[i - j] if i >= j else kzero for i in range(tau)], axis=2)
         for j in range(tau)], axis=1).astype(BF16)

    a_tab = jnp.stack([jnp.repeat(pw_r[tau].reshape(2, -1), nb, axis=0),
                       jnp.repeat(pw_i[tau].reshape(2, -1), nb, axis=0)])
    a_tab = a_tab.reshape(2, 2 * nb, -1, V7X_LANES).transpose(0, 2, 1, 3)
    return bp, cp, tp, a_tab


def _proj_ret_body(x_ref, n1_ref, w_ref, cos_ref, sin_ref, dec_ref, qdec_ref, kdect_ref,
                   cdec_ref, ms_ref, avg_ref, nw_ref, us5_ref, y_ref, u_even, u_odd, st_scr,
                   *, t_len, dh, scale):
    step = pl.program_id(0)
    nb = x_ref.shape[0]
    width = y_ref.shape[-1]
    d_s5 = us5_ref.shape[-1]

    @pl.when(step == 0)
    def _():
        u_odd[...] = jnp.zeros_like(u_odd)
        st_scr[...] = jnp.zeros_like(st_scr)

    def project(u_out):
        xn = _rms(x_ref[...].reshape(nb * t_len, x_ref.shape[-1]), n1_ref[...]).astype(BF16)
        u = jnp.dot(xn, w_ref[...], preferred_element_type=F32)
        us5_ref[...] = u[:, :d_s5].reshape(nb, t_len, d_s5).astype(us5_ref.dtype)
        u_out[...] = u[:, d_s5:].astype(u_out.dtype)

    for parity, (u_out, u_in) in enumerate(((u_even, u_odd), (u_odd, u_even))):
        @pl.when(step % 2 == parity)
        def _():
            project(u_out)
            _ret_chunk(u_in, cos_ref, sin_ref, dec_ref, qdec_ref, kdect_ref, cdec_ref, ms_ref, avg_ref,
                       nw_ref, y_ref, st_scr, t_len=t_len, dh=dh, scale=scale)


def _ret_chunk(u_ref, cos_ref, sin_ref, dec_ref, qdec_ref, kdect_ref, cdec_ref, ms_ref, avg_ref,
               nw_ref, y_ref, st_scr, *, t_len, dh, scale):
    nb = y_ref.shape[0]
    width = y_ref.shape[-1]
    heads = width // dh
    pairs = width // V7X_LANES
    reps = width // cos_ref.shape[-1]
    cos = jnp.concatenate([cos_ref[...]] * reps, axis=1)
    sin = jnp.concatenate([sin_ref[...]] * reps, axis=1)
    lane = lax.broadcasted_iota(jnp.int32, (t_len, width), 1)
    first = (lane % dh) < (dh // 2)

    def rot(x):
        x = x.astype(F32)
        partner = jnp.where(first, pltpu.roll(x, width - dh // 2, 1), pltpu.roll(x, dh // 2, 1))
        return x * cos + partner * sin

    def group_mean(x):
        return jnp.dot(x.astype(BF16), avg_ref[...], preferred_element_type=F32)

    zero_k = jnp.zeros((dh, t_len), BF16)
    zero_v = jnp.zeros((t_len, V7X_LANES), BF16)
    low_head = lax.broadcasted_iota(jnp.int32, (t_len, V7X_LANES), 1) < dh

    for b in range(nb):
        rows = slice(b * t_len, (b + 1) * t_len)
        q = rot(u_ref[rows, 0:width])
        k = rot(u_ref[rows, width:2 * width]) * scale
        vb = u_ref[rows, 2 * width:3 * width]
        kt = k.T
        ktb = kt.astype(BF16)
        qb = q.astype(BF16)
        inners = []
        for p in range(pairs):
            ps = slice(p * V7X_LANES, (p + 1) * V7X_LANES)
            k_lo, k_hi = ktb[p * V7X_LANES:p * V7X_LANES + dh], ktb[p * V7X_LANES + dh:(p + 1) * V7X_LANES]
            kbd = jnp.concatenate([jnp.concatenate([k_lo, zero_k], axis=0),
                                   jnp.concatenate([zero_k, k_hi], axis=0)], axis=1)
            sc = (jnp.dot(qb[:, ps], kbd, preferred_element_type=F32)
                  * dec_ref[:, 2 * p * t_len:2 * (p + 1) * t_len])
            vt = vb[:, ps]
            vbd = jnp.concatenate([jnp.where(low_head, vt, zero_v),
                                   jnp.where(low_head, zero_v, vt)], axis=0)
            inners.append(jnp.dot(sc.astype(BF16), vbd, preferred_element_type=F32))
        inner = jnp.concatenate(inners, axis=1)
        qd = (q * qdec_ref[...]).astype(BF16)
        kdt = (kt * kdect_ref[...]).astype(BF16)
        crosses = []
        for p in range(pairs):
            ps = slice(p * V7X_LANES, (p + 1) * V7X_LANES)
            state = st_scr[b, p]
            crosses.append(jnp.dot(qd[:, ps], state.astype(BF16), preferred_element_type=F32))
            kv = jnp.dot(kdt[ps, :], vb[:, ps], preferred_element_type=F32)
            st_scr[b, p] = state * cdec_ref[:, ps] + kv * ms_ref[...]
        o = inner + jnp.concatenate(crosses, axis=1)
        dlt = o - group_mean(o)
        var = jnp.dot((dlt * dlt).astype(BF16), avg_ref[...], preferred_element_type=F32)
        on = dlt * lax.rsqrt(var + EPS) * nw_ref[...]
        gate = u_ref[rows, 3 * width:4 * width].astype(F32)
        y_ref[b] = (jax.nn.silu(gate) * on).astype(y_ref.dtype)


def _proj_retention(x3, n1, w_bf, d_s5, d_ret, norm_w):
    nb, seq, d = x3.shape
    dh = d_ret // RET_HEADS
    half = dh // 2
    t_len = T_RET
    lg = jnp.log(1.0 - 2.0 ** (-5.0 - jnp.arange(RET_HEADS, dtype=F32)))
    t = jnp.arange(t_len, dtype=F32)
    diff = t[:, None] - t[None, :]
    dec = jnp.where(diff >= 0, jnp.exp(lg[:, None, None] * jnp.maximum(diff, 0.0)), 0.0)
    dec_all = dec.transpose(1, 0, 2).reshape(t_len, RET_HEADS * t_len)
    qdec = jnp.repeat(jnp.exp(lg[:, None] * (t + 1.0)[None, :]).T, dh, axis=1)
    kdect = jnp.repeat(jnp.exp(lg[:, None] * (t_len - 1 - t)[None, :]), dh, axis=0)
    cdec = jnp.repeat(jnp.exp(lg * t_len), dh)[None, :]
    head_of_lane = jnp.arange(d_ret) // dh
    pair_head = jnp.arange(V7X_LANES) // dh
    mask_s = (pair_head[:, None] == pair_head[None, :]).astype(F32)
    avg = (head_of_lane[:, None] == head_of_lane[None, :]).astype(F32) / dh
    assert dh & (dh - 1) == 0
    avg = avg.astype(BF16)
    inv = ROPE_BASE ** (-jnp.arange(half, dtype=F32) / half)
    ang = jnp.arange(seq, dtype=F32)[:, None] * inv[None, :]
    reps = V7X_LANES // dh
    cos_t = jnp.tile(jnp.cos(ang), (1, 2 * reps))
    sin_t = jnp.tile(jnp.concatenate([-jnp.sin(ang), jnp.sin(ang)], axis=1), (1, reps))

    def whole(a):
        return pl.BlockSpec(a.shape, lambda c: (0,) * a.ndim)

    nc = seq // t_len

    def proj_chunk(c):
        return jnp.minimum(c, nc - 1)

    def ret_chunk(c):
        return jnp.maximum(c - 1, 0)

    body = functools.partial(_proj_ret_body, t_len=t_len, dh=dh, scale=dh ** -0.5)
    return pl.pallas_call(
        body,
        grid=(nc + 1,),
        in_specs=[pl.BlockSpec((nb, t_len, d), lambda c: (0, proj_chunk(c), 0)),
                  whole(n1), whole(w_bf),
                  pl.BlockSpec((t_len, V7X_LANES), lambda c: (ret_chunk(c), 0)),
                  pl.BlockSpec((t_len, V7X_LANES), lambda c: (ret_chunk(c), 0)),
                  whole(dec_all), whole(qdec), whole(kdect), whole(cdec),
                  whole(mask_s), whole(avg), whole(norm_w)],
        out_specs=[pl.BlockSpec((nb, t_len, d_s5), lambda c: (0, proj_chunk(c), 0)),
                   pl.BlockSpec((nb, t_len, d_ret), lambda c: (0, ret_chunk(c), 0))],
        out_shape=[jax.ShapeDtypeStruct((nb, seq, d_s5), BF16),
                   jax.ShapeDtypeStruct((nb, seq, d_ret), BF16)],
        scratch_shapes=[pltpu.VMEM((nb * t_len, 4 * d_ret), BF16),
                        pltpu.VMEM((nb * t_len, 4 * d_ret), BF16),
                        pltpu.VMEM((nb, d_ret // V7X_LANES, V7X_LANES, V7X_LANES), F32)],
        compiler_params=_cparams(("arbitrary",)),
        name="inproj_retention",
    )(x3, n1, w_bf, cos_t, sin_t, dec_all, qdec, kdect, cdec, mask_s, avg, norm_w)


def _route_body(ys5_ref, yret_ref, x_ref, wo_ref, n2_ref, wr_ref, br_ref, tri_ref,
                h_ref, route_ref, route_t_ref, cnt_ref, carry_scr):
    @pl.when(pl.program_id(0) == 0)
    def _():
        carry_scr[...] = jnp.zeros_like(carry_scr)

    d_s5 = ys5_ref.shape[1]
    h = (x_ref[...]
         + jnp.dot(ys5_ref[...], wo_ref[0:d_s5], preferred_element_type=F32)
         + jnp.dot(yret_ref[...], wo_ref[d_s5:], preferred_element_type=F32))
    h_ref[...] = h
    hn = _rms(h, n2_ref[...])
    hi = hn.astype(BF16)
    lo = (hn - hi.astype(F32)).astype(BF16)
    p_hi = jnp.dot(hi, wr_ref[...], preferred_element_type=F32)
    p_lo = jnp.dot(lo, wr_ref[:, 0:ROUTE_LANES], preferred_element_type=F32)
    logits = p_hi[:, 0:ROUTE_LANES] + p_hi[:, ROUTE_LANES:] + p_lo + br_ref[...]
    tm = logits.shape[0]
    lt = logits.T
    epg = EXPERTS_PER_GROUP
    row = lax.broadcasted_iota(jnp.int32, (epg, tm), 0)
    rowf = row.astype(F32)
    neg = -jnp.inf
    big = float(epg)
    gl = jnp.where(row < N_GROUPS, lt[0:epg], neg)
    gmax = jnp.max(gl, axis=0, keepdims=True)
    gidx = jnp.min(jnp.where(gl == gmax, rowf, big), axis=0, keepdims=True)
    g_w = 1.0 / jnp.sum(jnp.exp(gl - gmax), axis=0, keepdims=True)
    el = jnp.full((epg, tm), neg, F32)
    for g in range(N_GROUPS):
        el = jnp.where(gidx == g, lt[ROUTE_EXPERT_ROW + g * epg:ROUTE_EXPERT_ROW + (g + 1) * epg], el)
    v0 = jnp.max(el, axis=0, keepdims=True)
    i0 = jnp.min(jnp.where(el == v0, rowf, big), axis=0, keepdims=True)
    el2 = jnp.where(rowf == i0, neg, el)
    v1 = jnp.max(el2, axis=0, keepdims=True)
    i1 = jnp.min(jnp.where(el2 == v1, rowf, big), axis=0, keepdims=True)
    e = jnp.exp(v1 - v0)
    den = 1.0 + e
    w0 = (1.0 / den) * g_w
    w1 = (e / den) * g_w
    eid0 = gidx * epg + i0
    eid1 = gidx * epg + i1
    n_exp = N_GROUPS * epg
    erow = lax.broadcasted_iota(jnp.int32, (n_exp, tm), 0).astype(F32)
    sel0 = erow == eid0
    sel1 = erow == eid1
    onehot = jnp.where(sel0 | sel1, 1.0, 0.0)
    before = jnp.dot(onehot.astype(BF16), tri_ref[...], preferred_element_type=F32) + carry_scr[...]
    r0 = jnp.sum(jnp.where(sel0, before, 0.0), axis=0, keepdims=True)
    r1 = jnp.sum(jnp.where(sel1, before, 0.0), axis=0, keepdims=True)
    carry_scr[...] += jnp.sum(onehot, axis=1, keepdims=True)
    cnt_ref[...] = jnp.broadcast_to(carry_scr[...], cnt_ref.shape)
    rec_t = jnp.zeros((ROUTE_FIELDS, tm), F32)
    for j, val in enumerate((eid0, eid1, w0, w1, r0, r1)):
        rec_t = jnp.where(row == j, val, rec_t)
    route_t_ref[...] = rec_t
    route_ref[...] = jnp.concatenate(
        [rec_t, jnp.zeros((ROUTE_LANES - ROUTE_FIELDS, tm), F32)], axis=0).T


def _route(ys5, yret, x2, wo_bf, n2, wr, br):
    n, d = x2.shape
    d_s5 = ys5.shape[1]
    d_ret = yret.shape[1]
    tm = TM_ROUTE
    n_exp = N_GROUPS * EXPERTS_PER_GROUP
    tri = (jnp.arange(tm)[:, None] < jnp.arange(tm)[None, :]).astype(BF16)
    return pl.pallas_call(
        _route_body,
        grid=(n // tm,),
        in_specs=[pl.BlockSpec((tm, d_s5), lambda i: (i, 0)),
                  pl.BlockSpec((tm, d_ret), lambda i: (i, 0)),
                  pl.BlockSpec((tm, d), lambda i: (i, 0)),
                  pl.BlockSpec((d_s5 + d_ret, d), lambda i: (0, 0)),
                  pl.BlockSpec((1, d), lambda i: (0, 0)),
                  pl.BlockSpec((d, 2 * ROUTE_LANES), lambda i: (0, 0)),
                  pl.BlockSpec((1, ROUTE_LANES), lambda i: (0, 0)),
                  pl.BlockSpec((tm, tm), lambda i: (0, 0))],
        out_specs=[pl.BlockSpec((tm, d), lambda i: (i, 0)),
                   pl.BlockSpec((tm, ROUTE_LANES), lambda i: (i, 0)),
                   pl.BlockSpec((ROUTE_FIELDS, tm), lambda i: (0, i)),
                   pl.BlockSpec((n_exp, ROUTE_LANES), lambda i: (0, 0))],
        out_shape=[jax.ShapeDtypeStruct((n, d), F32),
                   jax.ShapeDtypeStruct((n, ROUTE_LANES), F32),
                   jax.ShapeDtypeStruct((ROUTE_FIELDS, n), F32),
                   jax.ShapeDtypeStruct((n_exp, ROUTE_LANES), F32)],
        scratch_shapes=[pltpu.VMEM((n_exp, 1), F32)],
        compiler_params=_cparams(("arbitrary",)),
        name="outproj_route",
    )(ys5, yret, x2, wo_bf, n2, wr, br, tri)


def _dispatch_body(pends_ref, padded_ref, dest_ref, h_ref, n2_ref, xs_ref,
                   hn_scr, zero_scr, sem, zsem):
    tm = h_ref.shape[0]

    @pl.when(pl.program_id(0) == 0)
    def _():
        zero_scr[...] = jnp.zeros_like(zero_scr)

        def zero_copy(e):
            first = pl.multiple_of((pends_ref[e] - R_BLK) * ROW_TILES, R_BLK * ROW_TILES)
            return pltpu.make_async_copy(zero_scr, xs_ref.at[pl.ds(first, R_BLK * ROW_TILES)], zsem)

        def zstart(e, carry):
            @pl.when(padded_ref[e] > 0)
            def _():
                zero_copy(e).start()
            return carry

        def zwait(e, carry):
            @pl.when(padded_ref[e] > 0)
            def _():
                zero_copy(e).wait()
            return carry

        lax.fori_loop(0, pends_ref.shape[0], zstart, 0)
        lax.fori_loop(0, pends_ref.shape[0], zwait, 0)

    step = pl.program_id(0)
    slot = step % 2
    buf = hn_scr.at[slot]
    _rows_to_tiles(buf, _pack_rows(_rms(h_ref[...], n2_ref[...])))

    for r in range(tm):
        for k in range(TOP_K):
            pltpu.make_async_copy(_tile_rows(buf, r), _tile_rows(xs_ref, dest_ref[k * tm + r]),
                                  sem.at[slot]).start(priority=k)

    def drain(s):
        for k in range(TOP_K):
            pltpu.make_async_copy(hn_scr.at[s], xs_ref.at[pl.ds(0, tm * ROW_TILES)], sem.at[s]).wait()

    @pl.when(step > 0)
    def _():
        drain(1 - slot)

    @pl.when(step == pl.num_programs(0) - 1)
    def _():
        drain(slot)


def _dispatch(pends, padded, dest, h, n2, p_rows):
    n, d = h.shape
    tm = TM_ROWS
    grid_spec = pltpu.PrefetchScalarGridSpec(
        num_scalar_prefetch=2,
        grid=(n // tm,),
        in_specs=[pl.BlockSpec((TOP_K * tm,), lambda i, pe, pa: (i,), memory_space=pltpu.SMEM),
                  pl.BlockSpec((tm, d), lambda i, pe, pa: (i, 0)),
                  pl.BlockSpec((1, d), lambda i, pe, pa: (0, 0))],
        out_specs=pl.BlockSpec(memory_space=pl.ANY),
        scratch_shapes=[pltpu.VMEM((2, tm * ROW_TILES, V7X_LANES), U32),
                        pltpu.VMEM((R_BLK * ROW_TILES, V7X_LANES), U32),
                        pltpu.SemaphoreType.DMA((2,)), pltpu.SemaphoreType.DMA(())])
    assert d == 2 * ROW_TILES * V7X_LANES
    return pl.pallas_call(
        _dispatch_body,
        grid_spec=grid_spec,
        out_shape=jax.ShapeDtypeStruct((p_rows * ROW_TILES, V7X_LANES), U32),
        compiler_params=_cparams(("arbitrary",)),
        name="dispatch",
    )(pends, padded, dest, h, n2)


def _expert_body(be_ref, nu_ref, xs_ref, wg_ref, wu_ref, wd_ref, ys_ref, wg_s, wu_s, wd_s):
    j = pl.program_id(0)

    @pl.when(j < nu_ref[0])
    def _():
        changed = jnp.logical_or(j == 0, be_ref[j] != be_ref[jnp.maximum(j - 1, 0)])

        @pl.when(changed)
        def _():
            wg_s[...] = wg_ref[...].astype(BF16)
            wu_s[...] = wu_ref[...].astype(BF16)
            wd_s[...] = wd_ref[...].astype(BF16)

        x = _unpack_rows(_tiles_to_rows(xs_ref, R_BLK)).astype(BF16)
        gate = jnp.dot(x, wg_s[...], preferred_element_type=F32)
        up = jnp.dot(x, wu_s[...], preferred_element_type=F32)
        hid = (jax.nn.silu(gate) * up).astype(BF16)
        _rows_to_tiles(ys_ref, _pack_rows(jnp.dot(hid, wd_s[...], preferred_element_type=F32)))

    @pl.when(j >= nu_ref[0])
    def _():
        ys_ref[...] = jnp.zeros_like(ys_ref)


def _experts(block_e, n_used, xs, w_gate, w_up, w_down):
    d, d_e = w_gate.shape[1:]
    blk_rows = R_BLK * ROW_TILES
    nblk = xs.shape[0] // blk_rows

    def row_map(j, be, nu):
        return (jnp.maximum(jnp.minimum(j, nu[0] - 1), 0), 0)

    def w_map(j, be, nu):
        return (be[j], 0, 0)

    grid_spec = pltpu.PrefetchScalarGridSpec(
        num_scalar_prefetch=2,
        grid=(nblk,),
        in_specs=[pl.BlockSpec((blk_rows, V7X_LANES), row_map),
                  pl.BlockSpec((None, d, d_e), w_map),
                  pl.BlockSpec((None, d, d_e), w_map),
                  pl.BlockSpec((None, d_e, d), w_map)],
        out_specs=pl.BlockSpec((blk_rows, V7X_LANES), lambda j, be, nu: (j, 0)),
        scratch_shapes=[pltpu.VMEM((d, d_e), BF16), pltpu.VMEM((d, d_e), BF16),
                        pltpu.VMEM((d_e, d), BF16)])
    return pl.pallas_call(
        _expert_body,
        grid_spec=grid_spec,
        out_shape=jax.ShapeDtypeStruct(xs.shape, xs.dtype),
        compiler_params=_cparams(("arbitrary",)),
        name="experts",
    )(block_e, n_used, xs, w_gate, w_up, w_down)


def _combine_body(dest0_ref, dest1_ref, dest2_ref, h_ref, route_ref, fw_ref, ys_ref, out_ref,
                  *scratch, last):
    bufs, sem = scratch[:GATHER_RING], scratch[GATHER_RING]
    tm = h_ref.shape[0]
    step = pl.program_id(0)

    def issue(d_ref, s):
        for r in range(tm):
            for k in range(TOP_K):
                pltpu.make_async_copy(_tile_rows(ys_ref, d_ref[k * tm + r]),
                                      _tile_rows(bufs[s].at[k], r), sem.at[s]).start(priority=k)

    def drain(s):
        for k in range(TOP_K):
            pltpu.make_async_copy(ys_ref.at[pl.ds(0, tm * ROW_TILES)], bufs[s].at[k], sem.at[s]).wait()

    @pl.when(step == 0)
    def _():
        issue(dest0_ref, 0)
        issue(dest1_ref, 1)

    for s in range(GATHER_RING):
        @pl.when(step % GATHER_RING == s)
        def _():
            drain(s)
            issue(dest2_ref, (s + 2) % GATHER_RING)
            rec = route_ref[...]
            hh = (h_ref[...] + rec[:, 2:3] * _unpack_rows(_tiles_to_rows(bufs[s].at[0], tm))
                  + rec[:, 3:4] * _unpack_rows(_tiles_to_rows(bufs[s].at[1], tm)))
            out_ref[...] = _rms(hh, fw_ref[...])

    @pl.when(step == last)
    def _():
        drain((last + 1) % GATHER_RING)
        drain((last + 2) % GATHER_RING)


def _combine(dest, h, route, fw, ys):
    n, d = h.shape
    tm = TM_ROWS
    steps = n // tm
    return pl.pallas_call(
        functools.partial(_combine_body, last=steps - 1),
        grid=(steps,),
        in_specs=[pl.BlockSpec((TOP_K * tm,), lambda i: (i,), memory_space=pltpu.SMEM),
                  pl.BlockSpec((TOP_K * tm,), lambda i: (jnp.minimum(i + 1, steps - 1),),
                               memory_space=pltpu.SMEM),
                  pl.BlockSpec((TOP_K * tm,), lambda i: (jnp.minimum(i + 2, steps - 1),),
                               memory_space=pltpu.SMEM),
                  pl.BlockSpec((tm, d), lambda i: (i, 0)),
                  pl.BlockSpec((tm, ROUTE_LANES), lambda i: (i, 0)),
                  pl.BlockSpec((1, d), lambda i: (0, 0)),
                  pl.BlockSpec(memory_space=pl.ANY)],
        out_specs=pl.BlockSpec((tm, d), lambda i: (i, 0)),
        out_shape=jax.ShapeDtypeStruct((n, d), F32),
        scratch_shapes=([pltpu.VMEM((TOP_K, tm * ROW_TILES, V7X_LANES), U32)] * GATHER_RING
                        + [pltpu.SemaphoreType.DMA((GATHER_RING,))]),
        compiler_params=_cparams(("arbitrary",)),
        name="combine",
    )(dest, dest, dest, h, route, fw, ys)


def _plan(route_t, counts, n_experts, n_blocks):
    eid = route_t[0:TOP_K].astype(jnp.int32)
    rank = route_t[4:4 + TOP_K].astype(jnp.int32)
    cnt = counts[:, 0].astype(jnp.int32)
    padded = (cnt + R_BLK - 1) // R_BLK * R_BLK
    pends = jnp.cumsum(padded)
    pstart = pends - padded
    dest = rank
    for e in range(n_experts):
        dest = dest + jnp.where(eid == e, pstart[e], 0)
    n_used = pends[-1] // R_BLK
    blk = jnp.minimum(jnp.arange(n_blocks, dtype=jnp.int32), n_used - 1)
    block_e = jnp.minimum(jnp.sum(pends[None, :] <= (blk * R_BLK)[:, None], axis=1), n_experts - 1)
    return (dest.astype(jnp.int32), block_e.astype(jnp.int32), n_used.reshape(1).astype(jnp.int32),
            pends.astype(jnp.int32), padded.astype(jnp.int32))


def _layer(h3, norm1_w, w_in, s5_a_re, s5_a_im, s5_b_re, s5_b_im, s5_c_re, s5_c_im, s5_d,
           s5_log_dt, s5_w_glu, s5_b_glu, ret_norm_w, w_out, norm2_w, router_group_w,
           router_group_b, router_expert_w, router_expert_b, moe_w_gate, moe_w_up, moe_w_down,
           out_norm_w):
    nb, seq, d = h3.shape
    n = nb * seq
    d_s5 = s5_d.shape[0]
    d_ret = ret_norm_w.shape[0]
    n_experts = moe_w_gate.shape[0]
    x2 = h3.reshape(n, d)

    u_s5, y_ret = _proj_retention(h3, norm1_w.reshape(1, d), w_in.astype(BF16), d_s5, d_ret,
                                  ret_norm_w.reshape(1, d_ret).astype(F32))

    bp, cp, tp, a_tab = _s5_tables(s5_a_re, s5_a_im, s5_b_re, s5_b_im, s5_c_re, s5_c_im,
                                   s5_log_dt, nb, S5_TAU)
    y_s5 = _s5(u_s5.reshape(n, d_s5), nb, bp, cp, tp, a_tab, s5_d.reshape(1, d_s5).astype(F32),
               s5_w_glu.astype(BF16), s5_b_glu.reshape(1, d_s5).astype(F32))

    assert n_experts == N_GROUPS * EXPERTS_PER_GROUP
    wr = (jnp.zeros((d, ROUTE_LANES), F32).at[:, :N_GROUPS].set(router_group_w.astype(F32))
          .at[:, ROUTE_EXPERT_ROW:ROUTE_EXPERT_ROW + n_experts].set(router_expert_w.astype(F32)))
    br = (jnp.zeros((1, ROUTE_LANES), F32).at[0, :N_GROUPS].set(router_group_b.astype(F32))
          .at[0, ROUTE_EXPERT_ROW:ROUTE_EXPERT_ROW + n_experts].set(router_expert_b.astype(F32)))
    wr_hi = wr.astype(BF16)
    wr = jnp.concatenate([wr_hi, (wr - wr_hi.astype(F32)).astype(BF16)], axis=1)
    h, route, route_t, counts = _route(y_s5.reshape(n, d_s5), y_ret.reshape(n, d_ret), x2,
                                       w_out.astype(BF16), norm2_w.reshape(1, d), wr, br)

    n_blocks = (n * TOP_K) // R_BLK + n_experts
    dest, block_e, n_used, pends, padded = _plan(route_t, counts, n_experts, n_blocks)
    dest = dest.reshape(TOP_K, n // TM_ROWS, TM_ROWS).transpose(1, 0, 2).reshape(-1)
    xs = _dispatch(pends, padded, dest, h, norm2_w.reshape(1, d), n_blocks * R_BLK)
    ys = _experts(block_e, n_used, xs, moe_w_gate, moe_w_up, moe_w_down)
    out = _combine(dest, h, route, out_norm_w.reshape(1, d), ys)
    return out.reshape(nb, seq, d)


def kernel(x, norm1_w, w_in, s5_a_re, s5_a_im, s5_b_re, s5_b_im, s5_c_re, s5_c_im, s5_d, s5_log_dt, s5_w_glu, s5_b_glu, ret_norm_w, w_out, norm2_w, router_group_w, router_group_b, router_expert_w, router_expert_b, moe_w_gate, moe_w_up, moe_w_down, final_norm_w):
    depth = norm1_w.shape[0]
    assert depth == 1, "the fused final norm assumes a single layer"
    l = 0
    return _layer(x, norm1_w[l], w_in[l], s5_a_re[l], s5_a_im[l], s5_b_re[l], s5_b_im[l],
                  s5_c_re[l], s5_c_im[l], s5_d[l], s5_log_dt[l], s5_w_glu[l], s5_b_glu[l],
                  ret_norm_w[l], w_out[l], norm2_w[l], router_group_w[l], router_group_b[l],
                  router_expert_w[l], router_expert_b[l], moe_w_gate[l], moe_w_up[l],
                  moe_w_down[l], final_norm_w)
```

```python
import functools

import jax
import jax.numpy as jnp
from jax import lax
from jax.experimental import pallas as pl
from jax.experimental.pallas import tpu as pltpu

F32 = jnp.float32
BF16 = jnp.bfloat16

EPS = 1e-6
ROPE_BASE = 10000.0
RET_HEADS = 8
TOP_K = 2
N_GROUPS = 4
EXPERTS_PER_GROUP = 8

V7X_LANES = 128
V7X_SUBLANES = 8
V7X_VMEM_LIMIT = 56 * 1024 * 1024

T_S5 = 512
S5_TAU = 8
S5_PAD = 8
T_RET = 128
TM_ROUTE = 512
TM_ROWS = 256
R_BLK = 512
GATHER_RING = 3
ROUTE_LANES = 128
ROUTE_FIELDS = 8
ROUTE_EXPERT_ROW = 8


def _rms(x, w):
    return x * lax.rsqrt(jnp.mean(x * x, axis=-1, keepdims=True) + EPS) * w


def _cparams(sem):
    return pltpu.CompilerParams(dimension_semantics=sem, vmem_limit_bytes=V7X_VMEM_LIMIT)


ROW_TILES = V7X_SUBLANES // 2
U32 = jnp.uint32


def _pack_rows(x):
    half = x.shape[1] // 2

    def bf16_bits(v):
        return lax.bitcast_convert_type(v.astype(BF16).astype(F32), U32)

    return bf16_bits(x[:, half:]) | (bf16_bits(x[:, :half]) >> 16)


def _unpack_rows(w):
    lo = lax.bitcast_convert_type(w << 16, F32)
    hi = lax.bitcast_convert_type(w & U32(0xFFFF0000), F32)
    return jnp.concatenate([lo, hi], axis=1)


def _rows_to_tiles(ref, val):
    rows = val.shape[0]
    for s in range(ROW_TILES):
        ref[pl.ds(s, rows, stride=ROW_TILES), :] = val[:, s * V7X_LANES:(s + 1) * V7X_LANES]


def _tiles_to_rows(ref, rows):
    return jnp.concatenate(
        [ref[pl.ds(s, rows, stride=ROW_TILES), :] for s in range(ROW_TILES)], axis=1)


def _tile_rows(ref, row):
    return ref.at[pl.ds(pl.multiple_of(row * ROW_TILES, ROW_TILES), ROW_TILES)]


def _s5_body(u_ref, bp_ref, cp_ref, tp_ref, a_ref, dd_ref, wglu_ref, bglu_ref, y_ref,
             v_scr, sp_scr, st_scr, io_scr, *, nb, nk, seg, tau):
    @pl.when(pl.program_id(0) == 0)
    def _():
        st_scr[...] = jnp.zeros_like(st_scr)

    nblk = bp_ref.shape[0]
    cw = bp_ref.shape[1] // tau
    sw = bp_ref.shape[2]
    d_s5 = nblk * cw
    tiles = sw // V7X_LANES
    ht = tiles // 2
    ctiles = d_s5 // V7X_LANES
    srows = nb * nk

    u_all = u_ref[...].reshape(srows * tau, d_s5).astype(F32)
    for c in range(ctiles):
        io_scr[c] = u_all[:, c * V7X_LANES:(c + 1) * V7X_LANES]
    u_steps = [jnp.concatenate([io_scr[c, pl.ds(j, srows, stride=tau), :] for c in range(ctiles)],
                               axis=1).astype(BF16) for j in range(tau)]

    def block_inputs(blk):
        return jnp.concatenate([u_steps[j][:, blk * cw:(blk + 1) * cw] for j in range(tau)], axis=1)

    for blk in range(nblk):
        half, q = divmod(blk, 2)
        v = jnp.dot(block_inputs(blk), bp_ref[blk], preferred_element_type=F32)
        for b in range(nb):
            for j in range(tiles):
                v_scr[q * tiles + j, pl.ds((half * nb + b) * seg, nk), :] = (
                    v[b * nk:(b + 1) * nk, j * V7X_LANES:(j + 1) * V7X_LANES])

    rows = 2 * nb
    ar = [[a_ref[0, q * ht + i] for i in range(ht)] for q in range(2)]
    ai = [[a_ref[1, q * ht + i] for i in range(ht)] for q in range(2)]
    sr = [[st_scr[q * tiles + i] for i in range(ht)] for q in range(2)]
    si = [[st_scr[q * tiles + ht + i] for i in range(ht)] for q in range(2)]
    for k in range(nk):
        for q in range(2):
            for i in range(ht):
                jr = q * tiles + i
                ji = q * tiles + ht + i
                sp_scr[jr, pl.ds(k, rows, stride=seg), :] = sr[q][i]
                sp_scr[ji, pl.ds(k, rows, stride=seg), :] = si[q][i]
                vr = v_scr[jr, pl.ds(k, rows, stride=seg), :]
                vi = v_scr[ji, pl.ds(k, rows, stride=seg), :]
                nr = ar[q][i] * sr[q][i] - ai[q][i] * si[q][i] + vr
                ni = ar[q][i] * si[q][i] + ai[q][i] * sr[q][i] + vi
                sr[q][i], si[q][i] = nr, ni
    for q in range(2):
        for i in range(ht):
            st_scr[q * tiles + i] = sr[q][i]
            st_scr[q * tiles + ht + i] = si[q][i]

    yb = []
    for blk in range(nblk):
        half, q = divmod(blk, 2)
        sp = jnp.concatenate(
            [jnp.concatenate([sp_scr[q * tiles + j, pl.ds((half * nb + b) * seg, nk), :]
                              for j in range(tiles)], axis=1) for b in range(nb)],
            axis=0).astype(BF16)
        yb.append(jnp.dot(sp, cp_ref[blk], preferred_element_type=F32)
                  + jnp.dot(block_inputs(blk), tp_ref[blk], preferred_element_type=F32))
    for j in range(tau):
        y = jnp.concatenate([yb[blk][:, j * cw:(j + 1) * cw] for blk in range(nblk)], axis=1)
        y = y + dd_ref[...] * u_steps[j].astype(F32)
        y = jax.nn.gelu(y)
        z = jnp.dot(y.astype(BF16), wglu_ref[...], preferred_element_type=F32) + bglu_ref[...]
        out = y * jax.nn.sigmoid(z)
        for c in range(ctiles):
            io_scr[c, pl.ds(j, srows, stride=tau), :] = out[:, c * V7X_LANES:(c + 1) * V7X_LANES]
    y_all = jnp.concatenate([io_scr[c] for c in range(ctiles)], axis=1)
    y_ref[...] = y_all.reshape(nb, nk * tau, d_s5).astype(y_ref.dtype)


def _s5(u_s5, nb, bp, cp, tp, a_tab, dd, wglu_bf, bglu):
    n, d_s5 = u_s5.shape
    seq = n // nb
    tau = S5_TAU
    nk = T_S5 // tau
    seg = nk + S5_PAD
    nblk, _, sw = bp.shape
    rows = 2 * nb
    assert rows == V7X_SUBLANES and nblk == 4

    def whole(a):
        return pl.BlockSpec(a.shape, lambda c: (0,) * a.ndim)

    body = functools.partial(_s5_body, nb=nb, nk=nk, seg=seg, tau=tau)
    out = pl.pallas_call(
        body,
        grid=(seq // T_S5,),
        in_specs=[pl.BlockSpec((nb, T_S5, d_s5), lambda c: (0, c, 0)),
                  whole(bp), whole(cp), whole(tp), whole(a_tab), whole(dd), whole(wglu_bf),
                  whole(bglu)],
        out_specs=pl.BlockSpec((nb, T_S5, d_s5), lambda c: (0, c, 0)),
        out_shape=jax.ShapeDtypeStruct((nb, seq, d_s5), BF16),
        scratch_shapes=[pltpu.VMEM((2 * sw // V7X_LANES, rows * seg, V7X_LANES), F32),
                        pltpu.VMEM((2 * sw // V7X_LANES, rows * seg, V7X_LANES), F32),
                        pltpu.VMEM((2 * sw // V7X_LANES, rows, V7X_LANES), F32),
                        pltpu.VMEM((d_s5 // V7X_LANES, nb * T_S5, V7X_LANES), F32)],
        compiler_params=_cparams(("arbitrary",)),
        name="s5",
    )(u_s5.reshape(nb, seq, d_s5), bp, cp, tp, a_tab, dd, wglu_bf, bglu)
    return out.reshape(n, d_s5)


def _s5_tables(a_re, a_im, b_re, b_im, c_re, c_im, log_dt, nb, tau):
    hp = lax.Precision.HIGHEST
    g, p = a_re.shape
    hch = b_re.shape[2]
    gpb = V7X_LANES // hch
    nblk = g // gpb
    lam_r, lam_i = a_re.astype(F32), a_im.astype(F32)
    dt = jnp.exp(log_dt.astype(F32))[:, None]
    mag = jnp.exp(lam_r * dt)
    ab_r = mag * jnp.cos(lam_i * dt)
    ab_i = mag * jnp.sin(lam_i * dt)
    den = lam_r * lam_r + lam_i * lam_i
    zr = ((ab_r - 1.0) * lam_r + ab_i * lam_i) / den
    zi = (ab_i * lam_r - (ab_r - 1.0) * lam_i) / den
    br_, bi_ = b_re.astype(F32), b_im.astype(F32)
    bb_r = zr[..., None] * br_ - zi[..., None] * bi_
    bb_i = zr[..., None] * bi_ + zi[..., None] * br_
    cr, ci = c_re.astype(F32), c_im.astype(F32)
    pr, pi = [jnp.ones_like(ab_r)], [jnp.zeros_like(ab_i)]
    for _ in range(tau):
        pr, pi = pr + [pr[-1] * ab_r - pi[-1] * ab_i], pi + [pr[-1] * ab_i + pi[-1] * ab_r]
    pw_r, pw_i = jnp.stack(pr), jnp.stack(pi)

    def blockdiag(x):
        r, c = x.shape[-2:]
        x = jnp.tile(x.reshape(tau, nblk, gpb * r, c), (1, 1, 1, gpb))
        same = (jnp.arange(gpb * r)[:, None] // r) == (jnp.arange(gpb * c)[None, :] // c)
        return jnp.where(same, x, 0.0)

    wr_ = jnp.stack([pr[tau - 1 - j] for j in range(tau)])[:, :, None, :]
    wi_ = jnp.stack([pi[tau - 1 - j] for j in range(tau)])[:, :, None, :]
    bt_r, bt_i = jnp.swapaxes(bb_r, 1, 2)[None], jnp.swapaxes(bb_i, 1, 2)[None]
    bp_r = blockdiag(wr_ * bt_r - wi_ * bt_i)
    bp_i = blockdiag(wr_ * bt_i + wi_ * bt_r)
    bp = jnp.concatenate([jnp.concatenate([bp_r[j], bp_i[j]], axis=-1) for j in range(tau)],
                         axis=1).astype(BF16)

    qr_, qi_ = pw_r[1:, :, :, None], pw_i[1:, :, :, None]
    ct_r, ct_i = jnp.swapaxes(cr, 1, 2)[None], jnp.swapaxes(ci, 1, 2)[None]
    cp_r = blockdiag(ct_r * qr_ - ct_i * qi_)
    cp_i = blockdiag(ct_r * qi_ + ct_i * qr_)
    cp = jnp.concatenate([jnp.concatenate([cp_r[i], -cp_i[i]], axis=1) for i in range(tau)],
                         axis=2).astype(BF16)

    ab_r_ = pw_r[:tau, :, :, None] * bb_r[None] - pw_i[:tau, :, :, None] * bb_i[None]
    ab_i_ = pw_r[:tau, :, :, None] * bb_i[None] + pw_i[:tau, :, :, None] * bb_r[None]
    kd = blockdiag(jnp.einsum('ghp,dgpe->dgeh', cr, ab_r_, precision=hp)
                   - jnp.einsum('ghp,dgpe->dgeh', ci, ab_i_, precision=hp))
    kzero = jnp.zeros_like(kd[0])
    tp = jnp.concatenate(
        [jnp.concatenate([kd[i - j] if i >= j else kzero for i in range(tau)], axis=2)
         for j in range(tau)], axis=1).astype(BF16)

    a_tab = jnp.stack([jnp.repeat(pw_r[tau].reshape(2, -1), nb, axis=0),
                       jnp.repeat(pw_i[tau].reshape(2, -1), nb, axis=0)])
    a_tab = a_tab.reshape(2, 2 * nb, -1, V7X_LANES).transpose(0, 2, 1, 3)
    return bp, cp, tp, a_tab


def _proj_ret_body(x_ref, n1_ref, w_ref, cos_ref, sin_ref, dec_ref, qdec_ref, kdect_ref,
                   cdec_ref, ms_ref, avg_ref, nw_ref, wg_ref, wu_ref, wd_ref,
                   us5_ref, y_ref, wg_out, wu_out, wd_out, u_even, u_odd, st_scr,
                   *, t_len, dh, scale):
    wg_out[...] = wg_ref[...].astype(wg_out.dtype)
    wu_out[...] = wu_ref[...].astype(wu_out.dtype)
    wd_out[...] = wd_ref[...].astype(wd_out.dtype)

    step = pl.program_id(0)
    nb = x_ref.shape[0]
    width = y_ref.shape[-1]
    d_s5 = us5_ref.shape[-1]

    @pl.when(step == 0)
    def _():
        u_odd[...] = jnp.zeros_like(u_odd)
        st_scr[...] = jnp.zeros_like(st_scr)

    def project(u_out):
        xn = _rms(x_ref[...].reshape(nb * t_len, x_ref.shape[-1]), n1_ref[...]).astype(BF16)
        u = jnp.dot(xn, w_ref[...], preferred_element_type=F32)
        us5_ref[...] = u[:, :d_s5].reshape(nb, t_len, d_s5).astype(us5_ref.dtype)
        u_out[...] = u[:, d_s5:].astype(u_out.dtype)

    for parity, (u_out, u_in) in enumerate(((u_even, u_odd), (u_odd, u_even))):
        @pl.when(step % 2 == parity)
        def _():
            project(u_out)
            _ret_chunk(u_in, cos_ref, sin_ref, dec_ref, qdec_ref, kdect_ref, cdec_ref, ms_ref, avg_ref,
                       nw_ref, y_ref, st_scr, t_len=t_len, dh=dh, scale=scale)


def _ret_chunk(u_ref, cos_ref, sin_ref, dec_ref, qdec_ref, kdect_ref, cdec_ref, ms_ref, avg_ref,
               nw_ref, y_ref, st_scr, *, t_len, dh, scale):
    nb = y_ref.shape[0]
    width = y_ref.shape[-1]
    heads = width // dh
    pairs = width // V7X_LANES
    reps = width // cos_ref.shape[-1]
    cos = jnp.concatenate([cos_ref[...]] * reps, axis=1)
    sin = jnp.concatenate([sin_ref[...]] * reps, axis=1)
    lane = lax.broadcasted_iota(jnp.int32, (t_len, width), 1)
    first = (lane % dh) < (dh // 2)

    def rot(x):
        x = x.astype(F32)
        partner = jnp.where(first, pltpu.roll(x, width - dh // 2, 1), pltpu.roll(x, dh // 2, 1))
        return x * cos + partner * sin

    def group_mean(x):
        return jnp.dot(x.astype(BF16), avg_ref[...], preferred_element_type=F32)

    zero_k = jnp.zeros((dh, t_len), BF16)
    zero_v = jnp.zeros((t_len, V7X_LANES), BF16)
    low_head = lax.broadcasted_iota(jnp.int32, (t_len, V7X_LANES), 1) < dh

    for b in range(nb):
        rows = slice(b * t_len, (b + 1) * t_len)
        q = rot(u_ref[rows, 0:width])
        k = rot(u_ref[rows, width:2 * width]) * scale
        vb = u_ref[rows, 2 * width:3 * width]
        kt = k.T
        ktb = kt.astype(BF16)
        qb = q.astype(BF16)
        inners = []
        for p in range(pairs):
            ps = slice(p * V7X_LANES, (p + 1) * V7X_LANES)
            k_lo, k_hi = ktb[p * V7X_LANES:p * V7X_LANES + dh], ktb[p * V7X_LANES + dh:(p + 1) * V7X_LANES]
            kbd = jnp.concatenate([jnp.concatenate([k_lo, zero_k], axis=0),
                                   jnp.concatenate([zero_k, k_hi], axis=0)], axis=1)
            sc = (jnp.dot(qb[:, ps], kbd, preferred_element_type=F32)
                  * dec_ref[:, 2 * p * t_len:2 * (p + 1) * t_len])
            vt = vb[:, ps]
            vbd = jnp.concatenate([jnp.where(low_head, vt, zero_v),
                                   jnp.where(low_head, zero_v, vt)], axis=0)
            inners.append(jnp.dot(sc.astype(BF16), vbd, preferred_element_type=F32))
        inner = jnp.concatenate(inners, axis=1)
        qd = (q * qdec_ref[...]).astype(BF16)
        kdt = (kt * kdect_ref[...]).astype(BF16)
        crosses = []
        for p in range(pairs):
            ps = slice(p * V7X_LANES, (p + 1) * V7X_LANES)
            state = st_scr[b, p]
            crosses.append(jnp.dot(qd[:, ps], state.astype(BF16), preferred_element_type=F32))
            kv = jnp.dot(kdt[ps, :], vb[:, ps], preferred_element_type=F32)
            st_scr[b, p] = state * cdec_ref[:, ps] + kv * ms_ref[...]
        o = inner + jnp.concatenate(crosses, axis=1)
        dlt = o - group_mean(o)
        var = jnp.dot((dlt * dlt).astype(BF16), avg_ref[...], preferred_element_type=F32)
        on = dlt * lax.rsqrt(var + EPS) * nw_ref[...]
        gate = u_ref[rows, 3 * width:4 * width].astype(F32)
        y_ref[b] = (jax.nn.silu(gate) * on).astype(y_ref.dtype)


def _proj_retention(x3, n1, w_bf, d_s5, d_ret, norm_w, moe_w):
    nb, seq, d = x3.shape
    dh = d_ret // RET_HEADS
    half = dh // 2
    t_len = T_RET
    lg = jnp.log(1.0 - 2.0 ** (-5.0 - jnp.arange(RET_HEADS, dtype=F32)))
    t = jnp.arange(t_len, dtype=F32)
    diff = t[:, None] - t[None, :]
    dec = jnp.where(diff >= 0, jnp.exp(lg[:, None, None] * jnp.maximum(diff, 0.0)), 0.0)
    dec_all = dec.transpose(1, 0, 2).reshape(t_len, RET_HEADS * t_len)
    qdec = jnp.repeat(jnp.exp(lg[:, None] * (t + 1.0)[None, :]).T, dh, axis=1)
    kdect = jnp.repeat(jnp.exp(lg[:, None] * (t_len - 1 - t)[None, :]), dh, axis=0)
    cdec = jnp.repeat(jnp.exp(lg * t_len), dh)[None, :]
    head_of_lane = jnp.arange(d_ret) // dh
    pair_head = jnp.arange(V7X_LANES) // dh
    mask_s = (pair_head[:, None] == pair_head[None, :]).astype(F32)
    avg = (head_of_lane[:, None] == head_of_lane[None, :]).astype(F32) / dh
    assert dh & (dh - 1) == 0
    avg = avg.astype(BF16)
    inv = ROPE_BASE ** (-jnp.arange(half, dtype=F32) / half)
    ang = jnp.arange(seq, dtype=F32)[:, None] * inv[None, :]
    reps = V7X_LANES // dh
    cos_t = jnp.tile(jnp.cos(ang), (1, 2 * reps))
    sin_t = jnp.tile(jnp.concatenate([-jnp.sin(ang), jnp.sin(ang)], axis=1), (1, reps))

    def whole(a):
        return pl.BlockSpec(a.shape, lambda c: (0,) * a.ndim)

    nc = seq // t_len

    def proj_chunk(c):
        return jnp.minimum(c, nc - 1)

    def ret_chunk(c):
        return jnp.maximum(c - 1, 0)

    def sliced(w):
        e, rows, cols = w.shape
        if nc >= e:
            parts = nc // e
            assert nc % e == 0 and rows % (parts * V7X_SUBLANES) == 0
            return w.reshape(e * parts, rows // parts, cols), 1
        assert e % nc == 0
        return w, e // nc

    moe_in, moe_specs, moe_shapes = [], [], []
    for w in moe_w:
        ws, per_step = sliced(w)
        blk = (per_step,) + ws.shape[1:]
        moe_in.append(ws)
        moe_specs.append(pl.BlockSpec(blk, lambda c: (proj_chunk(c), 0, 0)))
        moe_shapes.append(jax.ShapeDtypeStruct(ws.shape, BF16))

    body = functools.partial(_proj_ret_body, t_len=t_len, dh=dh, scale=dh ** -0.5)
    outs = pl.pallas_call(
        body,
        grid=(nc + 1,),
        in_specs=[pl.BlockSpec((nb, t_len, d), lambda c: (0, proj_chunk(c), 0)),
                  whole(n1), whole(w_bf),
                  pl.BlockSpec((t_len, V7X_LANES), lambda c: (ret_chunk(c), 0)),
                  pl.BlockSpec((t_len, V7X_LANES), lambda c: (ret_chunk(c), 0)),
                  whole(dec_all), whole(qdec), whole(kdect), whole(cdec),
                  whole(mask_s), whole(avg), whole(norm_w)] + moe_specs,
        out_specs=[pl.BlockSpec((nb, t_len, d_s5), lambda c: (0, proj_chunk(c), 0)),
                   pl.BlockSpec((nb, t_len, d_ret), lambda c: (0, ret_chunk(c), 0))] + moe_specs,
        out_shape=[jax.ShapeDtypeStruct((nb, seq, d_s5), BF16),
                   jax.ShapeDtypeStruct((nb, seq, d_ret), BF16)] + moe_shapes,
        scratch_shapes=[pltpu.VMEM((nb * t_len, 4 * d_ret), BF16),
                        pltpu.VMEM((nb * t_len, 4 * d_ret), BF16),
                        pltpu.VMEM((nb, d_ret // V7X_LANES, V7X_LANES, V7X_LANES), F32)],
        compiler_params=_cparams(("arbitrary",)),
        name="inproj_retention",
    )(x3, n1, w_bf, cos_t, sin_t, dec_all, qdec, kdect, cdec, mask_s, avg, norm_w, *moe_in)
    u_s5, y_ret = outs[:2]
    moe_bf = [o.reshape(w.shape) for o, w in zip(outs[2:], moe_w)]
    return u_s5, y_ret, moe_bf


def _route_body(ys5_ref, yret_ref, x_ref, wo_ref, n2_ref, wr_ref, br_ref, tri_ref,
                h_ref, route_ref, route_t_ref, cnt_ref, carry_scr):
    @pl.when(pl.program_id(0) == 0)
    def _():
        carry_scr[...] = jnp.zeros_like(carry_scr)

    d_s5 = ys5_ref.shape[1]
    h = (x_ref[...]
         + jnp.dot(ys5_ref[...], wo_ref[0:d_s5], preferred_element_type=F32)
         + jnp.dot(yret_ref[...], wo_ref[d_s5:], preferred_element_type=F32))
    h_ref[...] = h
    hn = _rms(h, n2_ref[...])
    hi = hn.astype(BF16)
    lo = (hn - hi.astype(F32)).astype(BF16)
    p_hi = jnp.dot(hi, wr_ref[...], preferred_element_type=F32)
    p_lo = jnp.dot(lo, wr_ref[:, 0:ROUTE_LANES], preferred_element_type=F32)
    logits = p_hi[:, 0:ROUTE_LANES] + p_hi[:, ROUTE_LANES:] + p_lo + br_ref[...]
    tm = logits.shape[0]
    lt = logits.T
    epg = EXPERTS_PER_GROUP
    row = lax.broadcasted_iota(jnp.int32, (epg, tm), 0)
    rowf = row.astype(F32)
    neg = -jnp.inf
    big = float(epg)
    gl = jnp.where(row < N_GROUPS, lt[0:epg], neg)
    gmax = jnp.max(gl, axis=0, keepdims=True)
    gidx = jnp.min(jnp.where(gl == gmax, rowf, big), axis=0, keepdims=True)
    g_w = 1.0 / jnp.sum(jnp.exp(gl - gmax), axis=0, keepdims=True)
    el = jnp.full((epg, tm), neg, F32)
    for g in range(N_GROUPS):
        el = jnp.where(gidx == g, lt[ROUTE_EXPERT_ROW + g * epg:ROUTE_EXPERT_ROW + (g + 1) * epg], el)
    v0 = jnp.max(el, axis=0, keepdims=True)
    i0 = jnp.min(jnp.where(el == v0, rowf, big), axis=0, keepdims=True)
    el2 = jnp.where(rowf == i0, neg, el)
    v1 = jnp.max(el2, axis=0, keepdims=True)
    i1 = jnp.min(jnp.where(el2 == v1, rowf, big), axis=0, keepdims=True)
    e = jnp.exp(v1 - v0)
    den = 1.0 + e
    w0 = (1.0 / den) * g_w
    w1 = (e / den) * g_w
    eid0 = gidx * epg + i0
    eid1 = gidx * epg + i1
    n_exp = N_GROUPS * epg
    erow = lax.broadcasted_iota(jnp.int32, (n_exp, tm), 0).astype(F32)
    sel0 = erow == eid0
    sel1 = erow == eid1
    onehot = jnp.where(sel0 | sel1, 1.0, 0.0)
    before = jnp.dot(onehot.astype(BF16), tri_ref[...], preferred_element_type=F32) + carry_scr[...]
    r0 = jnp.sum(jnp.where(sel0, before, 0.0), axis=0, keepdims=True)
    r1 = jnp.sum(jnp.where(sel1, before, 0.0), axis=0, keepdims=True)
    carry_scr[...] += jnp.sum(onehot, axis=1, keepdims=True)
    cnt_ref[...] = jnp.broadcast_to(carry_scr[...], cnt_ref.shape)
    rec_t = jnp.zeros((ROUTE_FIELDS, tm), F32)
    for j, val in enumerate((eid0, eid1, w0, w1, r0, r1)):
        rec_t = jnp.where(row == j, val, rec_t)
    route_t_ref[...] = rec_t
    route_ref[...] = jnp.concatenate(
        [rec_t, jnp.zeros((ROUTE_LANES - ROUTE_FIELDS, tm), F32)], axis=0).T


def _route(ys5, yret, x2, wo_bf, n2, wr, br):
    n, d = x2.shape
    d_s5 = ys5.shape[1]
    d_ret = yret.shape[1]
    tm = TM_ROUTE
    n_exp = N_GROUPS * EXPERTS_PER_GROUP
    tri = (jnp.arange(tm)[:, None] < jnp.arange(tm)[None, :]).astype(BF16)
    return pl.pallas_call(
        _route_body,
        grid=(n // tm,),
        in_specs=[pl.BlockSpec((tm, d_s5), lambda i: (i, 0)),
                  pl.BlockSpec((tm, d_ret), lambda i: (i, 0)),
                  pl.BlockSpec((tm, d), lambda i: (i, 0)),
                  pl.BlockSpec((d_s5 + d_ret, d), lambda i: (0, 0)),
                  pl.BlockSpec((1, d), lambda i: (0, 0)),
                  pl.BlockSpec((d, 2 * ROUTE_LANES), lambda i: (0, 0)),
                  pl.BlockSpec((1, ROUTE_LANES), lambda i: (0, 0)),
                  pl.BlockSpec((tm, tm), lambda i: (0, 0))],
        out_specs=[pl.BlockSpec((tm, d), lambda i: (i, 0)),
                   pl.BlockSpec((tm, ROUTE_LANES), lambda i: (i, 0)),
                   pl.BlockSpec((ROUTE_FIELDS, tm), lambda i: (0, i)),
                   pl.BlockSpec((n_exp, ROUTE_LANES), lambda i: (0, 0))],
        out_shape=[jax.ShapeDtypeStruct((n, d), F32),
                   jax.ShapeDtypeStruct((n, ROUTE_LANES), F32),
                   jax.ShapeDtypeStruct((ROUTE_FIELDS, n), F32),
                   jax.ShapeDtypeStruct((n_exp, ROUTE_LANES), F32)],
        scratch_shapes=[pltpu.VMEM((n_exp, 1), F32)],
        compiler_params=_cparams(("arbitrary",)),
        name="outproj_route",
    )(ys5, yret, x2, wo_bf, n2, wr, br, tri)


def _dispatch_body(pends_ref, padded_ref, dest_ref, h_ref, n2_ref, xs_ref,
                   hn_scr, zero_scr, sem, zsem):
    tm = h_ref.shape[0]

    @pl.when(pl.program_id(0) == 0)
    def _():
        zero_scr[...] = jnp.zeros_like(zero_scr)

        def zero_copy(e):
            first = pl.multiple_of((pends_ref[e] - R_BLK) * ROW_TILES, R_BLK * ROW_TILES)
            return pltpu.make_async_copy(zero_scr, xs_ref.at[pl.ds(first, R_BLK * ROW_TILES)], zsem)

        def zstart(e, carry):
            @pl.when(padded_ref[e] > 0)
            def _():
                zero_copy(e).start()
            return carry

        def zwait(e, carry):
            @pl.when(padded_ref[e] > 0)
            def _():
                zero_copy(e).wait()
            return carry

        lax.fori_loop(0, pends_ref.shape[0], zstart, 0)
        lax.fori_loop(0, pends_ref.shape[0], zwait, 0)

    step = pl.program_id(0)
    slot = step % 2
    buf = hn_scr.at[slot]
    _rows_to_tiles(buf, _pack_rows(_rms(h_ref[...], n2_ref[...])))

    for r in range(tm):
        for k in range(TOP_K):
            pltpu.make_async_copy(_tile_rows(buf, r), _tile_rows(xs_ref, dest_ref[k * tm + r]),
                                  sem.at[slot]).start(priority=k)

    def drain(s):
        for k in range(TOP_K):
            pltpu.make_async_copy(hn_scr.at[s], xs_ref.at[pl.ds(0, tm * ROW_TILES)], sem.at[s]).wait()

    @pl.when(step > 0)
    def _():
        drain(1 - slot)

    @pl.when(step == pl.num_programs(0) - 1)
    def _():
        drain(slot)


def _dispatch(pends, padded, dest, h, n2, p_rows):
    n, d = h.shape
    tm = TM_ROWS
    grid_spec = pltpu.PrefetchScalarGridSpec(
        num_scalar_prefetch=2,
        grid=(n // tm,),
        in_specs=[pl.BlockSpec((TOP_K * tm,), lambda i, pe, pa: (i,), memory_space=pltpu.SMEM),
                  pl.BlockSpec((tm, d), lambda i, pe, pa: (i, 0)),
                  pl.BlockSpec((1, d), lambda i, pe, pa: (0, 0))],
        out_specs=pl.BlockSpec(memory_space=pl.ANY),
        scratch_shapes=[pltpu.VMEM((2, tm * ROW_TILES, V7X_LANES), U32),
                        pltpu.VMEM((R_BLK * ROW_TILES, V7X_LANES), U32),
                        pltpu.SemaphoreType.DMA((2,)), pltpu.SemaphoreType.DMA(())])
    assert d == 2 * ROW_TILES * V7X_LANES
    return pl.pallas_call(
        _dispatch_body,
        grid_spec=grid_spec,
        out_shape=jax.ShapeDtypeStruct((p_rows * ROW_TILES, V7X_LANES), U32),
        compiler_params=_cparams(("arbitrary",)),
        name="dispatch",
    )(pends, padded, dest, h, n2)


def _expert_body(be_ref, nu_ref, xs_ref, wg_ref, wu_ref, wd_ref, ys_ref):
    j = pl.program_id(0)

    @pl.when(j < nu_ref[0])
    def _():
        x = _unpack_rows(_tiles_to_rows(xs_ref, R_BLK)).astype(BF16)
        gate = jnp.dot(x, wg_ref[...], preferred_element_type=F32)
        up = jnp.dot(x, wu_ref[...], preferred_element_type=F32)
        hid = (jax.nn.silu(gate) * up).astype(BF16)
        _rows_to_tiles(ys_ref, _pack_rows(jnp.dot(hid, wd_ref[...], preferred_element_type=F32)))

    @pl.when(j >= nu_ref[0])
    def _():
        ys_ref[...] = jnp.zeros_like(ys_ref)


def _experts(block_e, n_used, xs, w_gate, w_up, w_down):
    d, d_e = w_gate.shape[1:]
    blk_rows = R_BLK * ROW_TILES
    nblk = xs.shape[0] // blk_rows

    def row_map(j, be, nu):
        return (jnp.maximum(jnp.minimum(j, nu[0] - 1), 0), 0)

    def w_map(j, be, nu):
        return (be[j], 0, 0)

    grid_spec = pltpu.PrefetchScalarGridSpec(
        num_scalar_prefetch=2,
        grid=(nblk,),
        in_specs=[pl.BlockSpec((blk_rows, V7X_LANES), row_map),
                  pl.BlockSpec((None, d, d_e), w_map),
                  pl.BlockSpec((None, d, d_e), w_map),
                  pl.BlockSpec((None, d_e, d), w_map)],
        out_specs=pl.BlockSpec((blk_rows, V7X_LANES), lambda j, be, nu: (j, 0)))
    return pl.pallas_call(
        _expert_body,
        grid_spec=grid_spec,
        out_shape=jax.ShapeDtypeStruct(xs.shape, xs.dtype),
        compiler_params=_cparams(("arbitrary",)),
        name="experts",
    )(block_e, n_used, xs, w_gate, w_up, w_down)


def _combine_body(dest0_ref, dest1_ref, dest2_ref, h_ref, route_ref, fw_ref, ys_ref, out_ref,
                  *scratch, last):
    bufs, sem = scratch[:GATHER_RING], scratch[GATHER_RING]
    tm = h_ref.shape[0]
    step = pl.program_id(0)

    def issue(d_ref, s):
        for r in range(tm):
            for k in range(TOP_K):
                pltpu.make_async_copy(_tile_rows(ys_ref, d_ref[k * tm + r]),
                                      _tile_rows(bufs[s].at[k], r), sem.at[s]).start(priority=k)

    def drain(s):
        for k in range(TOP_K):
            pltpu.make_async_copy(ys_ref.at[pl.ds(0, tm * ROW_TILES)], bufs[s].at[k], sem.at[s]).wait()

    @pl.when(step == 0)
    def _():
        issue(dest0_ref, 0)
        issue(dest1_ref, 1)

    for s in range(GATHER_RING):
        @pl.when(step % GATHER_RING == s)
        def _():
            drain(s)
            issue(dest2_ref, (s + 2) % GATHER_RING)
            rec = route_ref[...]
            hh = (h_ref[...] + rec[:, 2:3] * _unpack_rows(_tiles_to_rows(bufs[s].at[0], tm))
                  + rec[:, 3:4] * _unpack_rows(_tiles_to_rows(bufs[s].at[1], tm)))
            out_ref[...] = _rms(hh, fw_ref[...])

    @pl.when(step == last)
    def _():
        drain((last + 1) % GATHER_RING)
        drain((last + 2) % GATHER_RING)


def _combine(dest, h, route, fw, ys):
    n, d = h.shape
    tm = TM_ROWS
    steps = n // tm
    return pl.pallas_call(
        functools.partial(_combine_body, last=steps - 1),
        grid=(steps,),
        in_specs=[pl.BlockSpec((TOP_K * tm,), lambda i: (i,), memory_space=pltpu.SMEM),
                  pl.BlockSpec((TOP_K * tm,), lambda i: (jnp.minimum(i + 1, steps - 1),),
                               memory_space=pltpu.SMEM),
                  pl.BlockSpec((TOP_K * tm,), lambda i: (jnp.minimum(i + 2, steps - 1),),
                               memory_space=pltpu.SMEM),
                  pl.BlockSpec((tm, d), lambda i: (i, 0)),
                  pl.BlockSpec((tm, ROUTE_LANES), lambda i: (i, 0)),
                  pl.BlockSpec((1, d), lambda i: (0, 0)),
                  pl.BlockSpec(memory_space=pl.ANY)],
        out_specs=pl.BlockSpec((tm, d), lambda i: (i, 0)),
        out_shape=jax.ShapeDtypeStruct((n, d), F32),
        scratch_shapes=([pltpu.VMEM((TOP_K, tm * ROW_TILES, V7X_LANES), U32)] * GATHER_RING
                        + [pltpu.SemaphoreType.DMA((GATHER_RING,))]),
        compiler_params=_cparams(("arbitrary",)),
        name="combine",
    )(dest, dest, dest, h, route, fw, ys)


def _plan(route_t, counts, n_experts, n_blocks):
    eid = route_t[0:TOP_K].astype(jnp.int32)
    rank = route_t[4:4 + TOP_K].astype(jnp.int32)
    cnt = counts[:, 0].astype(jnp.int32)
    padded = (cnt + R_BLK - 1) // R_BLK * R_BLK
    pends = jnp.cumsum(padded)
    pstart = pends - padded
    dest = rank
    for e in range(n_experts):
        dest = dest + jnp.where(eid == e, pstart[e], 0)
    n_used = pends[-1] // R_BLK
    blk = jnp.minimum(jnp.arange(n_blocks, dtype=jnp.int32), n_used - 1)
    block_e = jnp.minimum(jnp.sum(pends[None, :] <= (blk * R_BLK)[:, None], axis=1), n_experts - 1)
    return (dest.astype(jnp.int32), block_e.astype(jnp.int32), n_used.reshape(1).astype(jnp.int32),
            pends.astype(jnp.int32), padded.astype(jnp.int32))


def _layer(h3, norm1_w, w_in, s5_a_re, s5_a_im, s5_b_re, s5_b_im, s5_c_re, s5_c_im, s5_d,
           s5_log_dt, s5_w_glu, s5_b_glu, ret_norm_w, w_out, norm2_w, router_group_w,
           router_group_b, router_expert_w, router_expert_b, moe_w_gate, moe_w_up, moe_w_down,
           out_norm_w):
    nb, seq, d = h3.shape
    n = nb * seq
    d_s5 = s5_d.shape[0]
    d_ret = ret_norm_w.shape[0]
    n_experts = moe_w_gate.shape[0]
    x2 = h3.reshape(n, d)

    u_s5, y_ret, moe_bf = _proj_retention(h3, norm1_w.reshape(1, d), w_in.astype(BF16), d_s5, d_ret,
                                          ret_norm_w.reshape(1, d_ret).astype(F32),
                                          (moe_w_gate, moe_w_up, moe_w_down))

    bp, cp, tp, a_tab = _s5_tables(s5_a_re, s5_a_im, s5_b_re, s5_b_im, s5_c_re, s5_c_im,
                                   s5_log_dt, nb, S5_TAU)
    y_s5 = _s5(u_s5.reshape(n, d_s5), nb, bp, cp, tp, a_tab, s5_d.reshape(1, d_s5).astype(F32),
               s5_w_glu.astype(BF16), s5_b_glu.reshape(1, d_s5).astype(F32))

    assert n_experts == N_GROUPS * EXPERTS_PER_GROUP
    wr = (jnp.zeros((d, ROUTE_LANES), F32).at[:, :N_GROUPS].set(router_group_w.astype(F32))
          .at[:, ROUTE_EXPERT_ROW:ROUTE_EXPERT_ROW + n_experts].set(router_expert_w.astype(F32)))
    br = (jnp.zeros((1, ROUTE_LANES), F32).at[0, :N_GROUPS].set(router_group_b.astype(F32))
          .at[0, ROUTE_EXPERT_ROW:ROUTE_EXPERT_ROW + n_experts].set(router_expert_b.astype(F32)))
    wr_hi = wr.astype(BF16)
    wr = jnp.concatenate([wr_hi, (wr - wr_hi.astype(F32)).astype(BF16)], axis=1)
    h, route, route_t, counts = _route(y_s5.reshape(n, d_s5), y_ret.reshape(n, d_ret), x2,
                                       w_out.astype(BF16), norm2_w.reshape(1, d), wr, br)

    n_blocks = (n * TOP_K) // R_BLK + n_experts
    dest, block_e, n_used, pends, padded = _plan(route_t, counts, n_experts, n_blocks)
    dest = dest.reshape(TOP_K, n // TM_ROWS, TM_ROWS).transpose(1, 0, 2).reshape(-1)
    xs = _dispatch(pends, padded, dest, h, norm2_w.reshape(1, d), n_blocks * R_BLK)
    ys = _experts(block_e, n_used, xs, *moe_bf)
    out = _combine(dest, h, route, out_norm_w.reshape(1, d), ys)
    return out.reshape(nb, seq, d)


def kernel(x, norm1_w, w_in, s5_a_re, s5_a_im, s5_b_re, s5_b_im, s5_c_re, s5_c_im, s5_d, s5_log_dt, s5_w_glu, s5_b_glu, ret_norm_w, w_out, norm2_w, router_group_w, router_group_b, router_expert_w, router_expert_b, moe_w_gate, moe_w_up, moe_w_down, final_norm_w):
    depth = norm1_w.shape[0]
    assert depth == 1, "the fused final norm assumes a single layer"
    l = 0
    return _layer(x, norm1_w[l], w_in[l], s5_a_re[l], s5_a_im[l], s5_b_re[l], s5_b_im[l],
                  s5_c_re[l], s5_c_im[l], s5_d[l], s5_log_dt[l], s5_w_glu[l], s5_b_glu[l],
                  ret_norm_w[l], w_out[l], norm2_w[l], router_group_w[l], router_group_b[l],
                  router_expert_w[l], router_expert_b[l], moe_w_gate[l], moe_w_up[l],
                  moe_w_down[l], final_norm_w)
```

```python
import functools

import jax
import jax.numpy as jnp
from jax import lax
from jax.experimental import pallas as pl
from jax.experimental.pallas import tpu as pltpu

F32 = jnp.float32
BF16 = jnp.bfloat16

EPS = 1e-6
ROPE_BASE = 10000.0
RET_HEADS = 8
TOP_K = 2
N_GROUPS = 4
EXPERTS_PER_GROUP = 8

V7X_LANES = 128
V7X_SUBLANES = 8
V7X_VMEM_LIMIT = 56 * 1024 * 1024

T_S5 = 512
S5_TAU = 8
S5_PAD = 8
T_RET = 128
TM_ROUTE = 512
TM_ROWS = 256
R_BLK = 512
GATHER_RING = 3
ROUTE_LANES = 128
ROUTE_FIELDS = 8
ROUTE_EXPERT_ROW = 8


def _rms(x, w):
    return x * lax.rsqrt(jnp.mean(x * x, axis=-1, keepdims=True) + EPS) * w


def _cparams(sem):
    return pltpu.CompilerParams(dimension_semantics=sem, vmem_limit_bytes=V7X_VMEM_LIMIT)


ROW_TILES = V7X_SUBLANES // 2
U32 = jnp.uint32


def _pack_rows(x):
    half = x.shape[1] // 2

    def bf16_bits(v):
        return lax.bitcast_convert_type(v.astype(BF16).astype(F32), U32)

    return bf16_bits(x[:, half:]) | (bf16_bits(x[:, :half]) >> 16)


def _unpack_rows(w):
    lo = lax.bitcast_convert_type(w << 16, F32)
    hi = lax.bitcast_convert_type(w & U32(0xFFFF0000), F32)
    return jnp.concatenate([lo, hi], axis=1)


def _rows_to_tiles(ref, val):
    rows = val.shape[0]
    for s in range(ROW_TILES):
        ref[pl.ds(s, rows, stride=ROW_TILES), :] = val[:, s * V7X_LANES:(s + 1) * V7X_LANES]


def _tiles_to_rows(ref, rows):
    return jnp.concatenate(
        [ref[pl.ds(s, rows, stride=ROW_TILES), :] for s in range(ROW_TILES)], axis=1)


def _tile_rows(ref, row):
    return ref.at[pl.ds(pl.multiple_of(row * ROW_TILES, ROW_TILES), ROW_TILES)]


def _s5_body(u_ref, bp_ref, cp_ref, tp_ref, a_ref, dd_ref, wglu_ref, bglu_ref, y_ref,
             v_scr, sp_scr, st_scr, io_scr, *, nb, nk, seg, tau):
    @pl.when(pl.program_id(0) == 0)
    def _():
        st_scr[...] = jnp.zeros_like(st_scr)

    nblk = bp_ref.shape[0]
    cw = bp_ref.shape[1] // tau
    sw = bp_ref.shape[2]
    d_s5 = nblk * cw
    tiles = sw // V7X_LANES
    ht = tiles // 2
    ctiles = d_s5 // V7X_LANES
    srows = nb * nk

    u_all = u_ref[...].reshape(srows * tau, d_s5).astype(F32)
    for c in range(ctiles):
        io_scr[c] = u_all[:, c * V7X_LANES:(c + 1) * V7X_LANES]
    u_steps = [jnp.concatenate([io_scr[c, pl.ds(j, srows, stride=tau), :] for c in range(ctiles)],
                               axis=1) for j in range(tau)]
    u_blocks = [jnp.concatenate([u_steps[j][:, blk * cw:(blk + 1) * cw] for j in range(tau)],
                                axis=1).astype(BF16) for blk in range(nblk)]

    def block_inputs(blk):
        return u_blocks[blk]

    for blk in range(nblk):
        half, q = divmod(blk, 2)
        v = jnp.dot(block_inputs(blk), bp_ref[blk], preferred_element_type=F32)
        for b in range(nb):
            for j in range(tiles):
                v_scr[q * tiles + j, pl.ds((half * nb + b) * seg, nk), :] = (
                    v[b * nk:(b + 1) * nk, j * V7X_LANES:(j + 1) * V7X_LANES])

    rows = 2 * nb
    ar = [[a_ref[0, q * ht + i] for i in range(ht)] for q in range(2)]
    ai = [[a_ref[1, q * ht + i] for i in range(ht)] for q in range(2)]
    sr = [[st_scr[q * tiles + i] for i in range(ht)] for q in range(2)]
    si = [[st_scr[q * tiles + ht + i] for i in range(ht)] for q in range(2)]
    for k in range(nk):
        for q in range(2):
            for i in range(ht):
                jr = q * tiles + i
                ji = q * tiles + ht + i
                sp_scr[jr, pl.ds(k, rows, stride=seg), :] = sr[q][i]
                sp_scr[ji, pl.ds(k, rows, stride=seg), :] = si[q][i]
                vr = v_scr[jr, pl.ds(k, rows, stride=seg), :]
                vi = v_scr[ji, pl.ds(k, rows, stride=seg), :]
                nr = ar[q][i] * sr[q][i] - ai[q][i] * si[q][i] + vr
                ni = ar[q][i] * si[q][i] + ai[q][i] * sr[q][i] + vi
                sr[q][i], si[q][i] = nr, ni
    for q in range(2):
        for i in range(ht):
            st_scr[q * tiles + i] = sr[q][i]
            st_scr[q * tiles + ht + i] = si[q][i]

    yb = []
    for blk in range(nblk):
        half, q = divmod(blk, 2)
        sp = jnp.concatenate(
            [jnp.concatenate([sp_scr[q * tiles + j, pl.ds((half * nb + b) * seg, nk), :]
                              for j in range(tiles)], axis=1) for b in range(nb)],
            axis=0).astype(BF16)
        yb.append(jnp.dot(sp, cp_ref[blk], preferred_element_type=F32)
                  + jnp.dot(block_inputs(blk), tp_ref[blk], preferred_element_type=F32))
    for j in range(tau):
        y = jnp.concatenate([yb[blk][:, j * cw:(j + 1) * cw] for blk in range(nblk)], axis=1)
        y = y + dd_ref[...] * u_steps[j]
        y = jax.nn.gelu(y)
        z = jnp.dot(y.astype(BF16), wglu_ref[...], preferred_element_type=F32) + bglu_ref[...]
        out = y * jax.nn.sigmoid(z)
        for c in range(ctiles):
            io_scr[c, pl.ds(j, srows, stride=tau), :] = out[:, c * V7X_LANES:(c + 1) * V7X_LANES]
    y_all = jnp.concatenate([io_scr[c] for c in range(ctiles)], axis=1)
    y_ref[...] = y_all.reshape(nb, nk * tau, d_s5).astype(y_ref.dtype)


def _s5(u_s5, nb, bp, cp, tp, a_tab, dd, wglu_bf, bglu):
    n, d_s5 = u_s5.shape
    seq = n // nb
    tau = S5_TAU
    nk = T_S5 // tau
    seg = nk + S5_PAD
    nblk, _, sw = bp.shape
    rows = 2 * nb
    assert rows == V7X_SUBLANES and nblk == 4

    def whole(a):
        return pl.BlockSpec(a.shape, lambda c: (0,) * a.ndim)

    body = functools.partial(_s5_body, nb=nb, nk=nk, seg=seg, tau=tau)
    out = pl.pallas_call(
        body,
        grid=(seq // T_S5,),
        in_specs=[pl.BlockSpec((nb, T_S5, d_s5), lambda c: (0, c, 0)),
                  whole(bp), whole(cp), whole(tp), whole(a_tab), whole(dd), whole(wglu_bf),
                  whole(bglu)],
        out_specs=pl.BlockSpec((nb, T_S5, d_s5), lambda c: (0, c, 0)),
        out_shape=jax.ShapeDtypeStruct((nb, seq, d_s5), BF16),
        scratch_shapes=[pltpu.VMEM((2 * sw // V7X_LANES, rows * seg, V7X_LANES), F32),
                        pltpu.VMEM((2 * sw // V7X_LANES, rows * seg, V7X_LANES), F32),
                        pltpu.VMEM((2 * sw // V7X_LANES, rows, V7X_LANES), F32),
                        pltpu.VMEM((d_s5 // V7X_LANES, nb * T_S5, V7X_LANES), F32)],
        compiler_params=_cparams(("arbitrary",)),
        name="s5",
    )(u_s5.reshape(nb, seq, d_s5), bp, cp, tp, a_tab, dd, wglu_bf, bglu)
    return out.reshape(n, d_s5)


def _s5_tables(a_re, a_im, b_re, b_im, c_re, c_im, log_dt, nb, tau):
    hp = lax.Precision.HIGHEST
    g, p = a_re.shape
    hch = b_re.shape[2]
    gpb = V7X_LANES // hch
    nblk = g // gpb
    lam_r, lam_i = a_re.astype(F32), a_im.astype(F32)
    dt = jnp.exp(log_dt.astype(F32))[:, None]
    mag = jnp.exp(lam_r * dt)
    ab_r = mag * jnp.cos(lam_i * dt)
    ab_i = mag * jnp.sin(lam_i * dt)
    den = lam_r * lam_r + lam_i * lam_i
    zr = ((ab_r - 1.0) * lam_r + ab_i * lam_i) / den
    zi = (ab_i * lam_r - (ab_r - 1.0) * lam_i) / den
    br_, bi_ = b_re.astype(F32), b_im.astype(F32)
    bb_r = zr[..., None] * br_ - zi[..., None] * bi_
    bb_i = zr[..., None] * bi_ + zi[..., None] * br_
    cr, ci = c_re.astype(F32), c_im.astype(F32)
    pr, pi = [jnp.ones_like(ab_r)], [jnp.zeros_like(ab_i)]
    for _ in range(tau):
        pr, pi = pr + [pr[-1] * ab_r - pi[-1] * ab_i], pi + [pr[-1] * ab_i + pi[-1] * ab_r]
    pw_r, pw_i = jnp.stack(pr), jnp.stack(pi)

    def blockdiag(x):
        r, c = x.shape[-2:]
        x = jnp.tile(x.reshape(tau, nblk, gpb * r, c), (1, 1, 1, gpb))
        same = (jnp.arange(gpb * r)[:, None] // r) == (jnp.arange(gpb * c)[None, :] // c)
        return jnp.where(same, x, 0.0)

    wr_ = jnp.stack([pr[tau - 1 - j] for j in range(tau)])[:, :, None, :]
    wi_ = jnp.stack([pi[tau - 1 - j] for j in range(tau)])[:, :, None, :]
    bt_r, bt_i = jnp.swapaxes(bb_r, 1, 2)[None], jnp.swapaxes(bb_i, 1, 2)[None]
    bp_r = blockdiag(wr_ * bt_r - wi_ * bt_i)
    bp_i = blockdiag(wr_ * bt_i + wi_ * bt_r)
    bp = jnp.concatenate([jnp.concatenate([bp_r[j], bp_i[j]], axis=-1) for j in range(tau)],
                         axis=1).astype(BF16)

    qr_, qi_ = pw_r[1:, :, :, None], pw_i[1:, :, :, None]
    ct_r, ct_i = jnp.swapaxes(cr, 1, 2)[None], jnp.swapaxes(ci, 1, 2)[None]
    cp_r = blockdiag(ct_r * qr_ - ct_i * qi_)
    cp_i = blockdiag(ct_r * qi_ + ct_i * qr_)
    cp = jnp.concatenate([jnp.concatenate([cp_r[i], -cp_i[i]], axis=1) for i in range(tau)],
                         axis=2).astype(BF16)

    ab_r_ = pw_r[:tau, :, :, None] * bb_r[None] - pw_i[:tau, :, :, None] * bb_i[None]
    ab_i_ = pw_r[:tau, :, :, None] * bb_i[None] + pw_i[:tau, :, :, None] * bb_r[None]
    kd = blockdiag(jnp.einsum('ghp,dgpe->dgeh', cr, ab_r_, precision=hp)
                   - jnp.einsum('ghp,dgpe->dgeh', ci, ab_i_, precision=hp))
    kzero = jnp.zeros_like(kd[0])
    tp = jnp.concatenate(
        [jnp.concatenate([kd[i - j] if i >= j else kzero for i in range(tau)], axis=2)
         for j in range(tau)], axis=1).astype(BF16)

    a_tab = jnp.stack([jnp.repeat(pw_r[tau].reshape(2, -1), nb, axis=0),
                       jnp.repeat(pw_i[tau].reshape(2, -1), nb, axis=0)])
    a_tab = a_tab.reshape(2, 2 * nb, -1, V7X_LANES).transpose(0, 2, 1, 3)
    return bp, cp, tp, a_tab


def _proj_ret_body(x_ref, n1_ref, w_ref, cos_ref, sin_ref, dec_ref, qdec_ref, kdect_ref,
                   cdec_ref, ms_ref, avg_ref, nw_ref, wg_ref, wu_ref, wd_ref,
                   us5_ref, y_ref, wg_out, wu_out, wd_out, u_even, u_odd, st_scr,
                   *, t_len, dh, scale):
    wg_out[...] = wg_ref[...].astype(wg_out.dtype)
    wu_out[...] = wu_ref[...].astype(wu_out.dtype)
    wd_out[...] = wd_ref[...].astype(wd_out.dtype)

    step = pl.program_id(0)
    nb = x_ref.shape[0]
    width = y_ref.shape[-1]
    d_s5 = us5_ref.shape[-1]

    @pl.when(step == 0)
    def _():
        u_odd[...] = jnp.zeros_like(u_odd)
        st_scr[...] = jnp.zeros_like(st_scr)

    def project(u_out):
        xn = _rms(x_ref[...].reshape(nb * t_len, x_ref.shape[-1]), n1_ref[...]).astype(BF16)
        u = jnp.dot(xn, w_ref[...], preferred_element_type=F32)
        us5_ref[...] = u[:, :d_s5].reshape(nb, t_len, d_s5).astype(us5_ref.dtype)
        u_out[...] = u[:, d_s5:].astype(u_out.dtype)

    for parity, (u_out, u_in) in enumerate(((u_even, u_odd), (u_odd, u_even))):
        @pl.when(step % 2 == parity)
        def _():
            project(u_out)
            _ret_chunk(u_in, cos_ref, sin_ref, dec_ref, qdec_ref, kdect_ref, cdec_ref, ms_ref, avg_ref,
                       nw_ref, y_ref, st_scr, t_len=t_len, dh=dh, scale=scale)


def _ret_chunk(u_ref, cos_ref, sin_ref, dec_ref, qdec_ref, kdect_ref, cdec_ref, ms_ref, avg_ref,
               nw_ref, y_ref, st_scr, *, t_len, dh, scale):
    nb = y_ref.shape[0]
    width = y_ref.shape[-1]
    heads = width // dh
    pairs = width // V7X_LANES
    reps = width // cos_ref.shape[-1]
    cos = jnp.concatenate([cos_ref[...]] * reps, axis=1)
    sin = jnp.concatenate([sin_ref[...]] * reps, axis=1)
    lane = lax.broadcasted_iota(jnp.int32, (t_len, width), 1)
    first = (lane % dh) < (dh // 2)

    def rot(x):
        x = x.astype(F32)
        partner = jnp.where(first, pltpu.roll(x, width - dh // 2, 1), pltpu.roll(x, dh // 2, 1))
        return x * cos + partner * sin

    def group_mean(x):
        return jnp.dot(x.astype(BF16), avg_ref[...], preferred_element_type=F32)

    zero_k = jnp.zeros((dh, t_len), BF16)
    zero_v = jnp.zeros((t_len, V7X_LANES), BF16)
    low_head = lax.broadcasted_iota(jnp.int32, (t_len, V7X_LANES), 1) < dh

    outs = []
    for b in range(nb):
        rows = slice(b * t_len, (b + 1) * t_len)
        q = rot(u_ref[rows, 0:width])
        k = rot(u_ref[rows, width:2 * width]) * scale
        vb = u_ref[rows, 2 * width:3 * width]
        kt = k.T
        ktb = kt.astype(BF16)
        qb = q.astype(BF16)
        inners = []
        for p in range(pairs):
            ps = slice(p * V7X_LANES, (p + 1) * V7X_LANES)
            k_lo, k_hi = ktb[p * V7X_LANES:p * V7X_LANES + dh], ktb[p * V7X_LANES + dh:(p + 1) * V7X_LANES]
            kbd = jnp.concatenate([jnp.concatenate([k_lo, zero_k], axis=0),
                                   jnp.concatenate([zero_k, k_hi], axis=0)], axis=1)
            sc = (jnp.dot(qb[:, ps], kbd, preferred_element_type=F32)
                  * dec_ref[:, 2 * p * t_len:2 * (p + 1) * t_len])
            vt = vb[:, ps]
            vbd = jnp.concatenate([jnp.where(low_head, vt, zero_v),
                                   jnp.where(low_head, zero_v, vt)], axis=0)
            inners.append(jnp.dot(sc.astype(BF16), vbd, preferred_element_type=F32))
        inner = jnp.concatenate(inners, axis=1)
        qd = (q * qdec_ref[...]).astype(BF16)
        kdt = (kt * kdect_ref[...]).astype(BF16)
        crosses = []
        for p in range(pairs):
            ps = slice(p * V7X_LANES, (p + 1) * V7X_LANES)
            state = st_scr[b, p]
            crosses.append(jnp.dot(qd[:, ps], state.astype(BF16), preferred_element_type=F32))
            kv = jnp.dot(kdt[ps, :], vb[:, ps], preferred_element_type=F32)
            st_scr[b, p] = state * cdec_ref[:, ps] + kv * ms_ref[...]
        outs.append(inner + jnp.concatenate(crosses, axis=1))

    o = jnp.concatenate(outs, axis=0)
    dlt = o - group_mean(o)
    on = dlt * lax.rsqrt(group_mean(dlt * dlt) + EPS) * nw_ref[...]
    gate = u_ref[:, 3 * width:4 * width].astype(F32)
    y_ref[...] = (jax.nn.silu(gate) * on).reshape(nb, t_len, width).astype(y_ref.dtype)


def _proj_retention(x3, n1, w_bf, d_s5, d_ret, norm_w, moe_w):
    nb, seq, d = x3.shape
    dh = d_ret // RET_HEADS
    half = dh // 2
    t_len = T_RET
    lg = jnp.log(1.0 - 2.0 ** (-5.0 - jnp.arange(RET_HEADS, dtype=F32)))
    t = jnp.arange(t_len, dtype=F32)
    diff = t[:, None] - t[None, :]
    dec = jnp.where(diff >= 0, jnp.exp(lg[:, None, None] * jnp.maximum(diff, 0.0)), 0.0)
    dec_all = dec.transpose(1, 0, 2).reshape(t_len, RET_HEADS * t_len)
    qdec = jnp.repeat(jnp.exp(lg[:, None] * (t + 1.0)[None, :]).T, dh, axis=1)
    kdect = jnp.repeat(jnp.exp(lg[:, None] * (t_len - 1 - t)[None, :]), dh, axis=0)
    cdec = jnp.repeat(jnp.exp(lg * t_len), dh)[None, :]
    head_of_lane = jnp.arange(d_ret) // dh
    pair_head = jnp.arange(V7X_LANES) // dh
    mask_s = (pair_head[:, None] == pair_head[None, :]).astype(F32)
    avg = (head_of_lane[:, None] == head_of_lane[None, :]).astype(F32) / dh
    assert dh & (dh - 1) == 0
    avg = avg.astype(BF16)
    inv = ROPE_BASE ** (-jnp.arange(half, dtype=F32) / half)
    ang = jnp.arange(seq, dtype=F32)[:, None] * inv[None, :]
    reps = V7X_LANES // dh
    cos_t = jnp.tile(jnp.cos(ang), (1, 2 * reps))
    sin_t = jnp.tile(jnp.concatenate([-jnp.sin(ang), jnp.sin(ang)], axis=1), (1, reps))

    def whole(a):
        return pl.BlockSpec(a.shape, lambda c: (0,) * a.ndim)

    nc = seq // t_len

    def proj_chunk(c):
        return jnp.minimum(c, nc - 1)

    def ret_chunk(c):
        return jnp.maximum(c - 1, 0)

    def sliced(w):
        e, rows, cols = w.shape
        if nc >= e:
            parts = nc // e
            assert nc % e == 0 and rows % (parts * V7X_SUBLANES) == 0
            return w.reshape(e * parts, rows // parts, cols), 1
        assert e % nc == 0
        return w, e // nc

    moe_in, moe_specs, moe_shapes = [], [], []
    for w in moe_w:
        ws, per_step = sliced(w)
        blk = (per_step,) + ws.shape[1:]
        moe_in.append(ws)
        moe_specs.append(pl.BlockSpec(blk, lambda c: (proj_chunk(c), 0, 0)))
        moe_shapes.append(jax.ShapeDtypeStruct(ws.shape, BF16))

    body = functools.partial(_proj_ret_body, t_len=t_len, dh=dh, scale=dh ** -0.5)
    outs = pl.pallas_call(
        body,
        grid=(nc + 1,),
        in_specs=[pl.BlockSpec((nb, t_len, d), lambda c: (0, proj_chunk(c), 0)),
                  whole(n1), whole(w_bf),
                  pl.BlockSpec((t_len, V7X_LANES), lambda c: (ret_chunk(c), 0)),
                  pl.BlockSpec((t_len, V7X_LANES), lambda c: (ret_chunk(c), 0)),
                  whole(dec_all), whole(qdec), whole(kdect), whole(cdec),
                  whole(mask_s), whole(avg), whole(norm_w)] + moe_specs,
        out_specs=[pl.BlockSpec((nb, t_len, d_s5), lambda c: (0, proj_chunk(c), 0)),
                   pl.BlockSpec((nb, t_len, d_ret), lambda c: (0, ret_chunk(c), 0))] + moe_specs,
        out_shape=[jax.ShapeDtypeStruct((nb, seq, d_s5), BF16),
                   jax.ShapeDtypeStruct((nb, seq, d_ret), BF16)] + moe_shapes,
        scratch_shapes=[pltpu.VMEM((nb * t_len, 4 * d_ret), BF16),
                        pltpu.VMEM((nb * t_len, 4 * d_ret), BF16),
                        pltpu.VMEM((nb, d_ret // V7X_LANES, V7X_LANES, V7X_LANES), F32)],
        compiler_params=_cparams(("arbitrary",)),
        name="inproj_retention",
    )(x3, n1, w_bf, cos_t, sin_t, dec_all, qdec, kdect, cdec, mask_s, avg, norm_w, *moe_in)
    u_s5, y_ret = outs[:2]
    moe_bf = [o.reshape(w.shape) for o, w in zip(outs[2:], moe_w)]
    return u_s5, y_ret, moe_bf


def _route_body(ys5_ref, yret_ref, x_ref, wo_ref, n2_ref, wr_ref, br_ref, tri_ref,
                h_ref, route_ref, route_t_ref, cnt_ref, carry_scr):
    @pl.when(pl.program_id(0) == 0)
    def _():
        carry_scr[...] = jnp.zeros_like(carry_scr)

    d_s5 = ys5_ref.shape[1]
    h = (x_ref[...]
         + jnp.dot(ys5_ref[...], wo_ref[0:d_s5], preferred_element_type=F32)
         + jnp.dot(yret_ref[...], wo_ref[d_s5:], preferred_element_type=F32))
    h_ref[...] = h
    hn = _rms(h, n2_ref[...])
    hi = hn.astype(BF16)
    lo = (hn - hi.astype(F32)).astype(BF16)
    p_hi = jnp.dot(hi, wr_ref[...], preferred_element_type=F32)
    p_lo = jnp.dot(lo, wr_ref[:, 0:ROUTE_LANES], preferred_element_type=F32)
    logits = p_hi[:, 0:ROUTE_LANES] + p_hi[:, ROUTE_LANES:] + p_lo + br_ref[...]
    tm = logits.shape[0]
    lt = logits.T
    epg = EXPERTS_PER_GROUP
    row = lax.broadcasted_iota(jnp.int32, (epg, tm), 0)
    rowf = row.astype(F32)
    neg = -jnp.inf
    big = float(epg)
    gl = jnp.where(row < N_GROUPS, lt[0:epg], neg)
    gmax = jnp.max(gl, axis=0, keepdims=True)
    gidx = jnp.min(jnp.where(gl == gmax, rowf, big), axis=0, keepdims=True)
    g_w = 1.0 / jnp.sum(jnp.exp(gl - gmax), axis=0, keepdims=True)
    el = jnp.full((epg, tm), neg, F32)
    for g in range(N_GROUPS):
        el = jnp.where(gidx == g, lt[ROUTE_EXPERT_ROW + g * epg:ROUTE_EXPERT_ROW + (g + 1) * epg], el)
    v0 = jnp.max(el, axis=0, keepdims=True)
    i0 = jnp.min(jnp.where(el == v0, rowf, big), axis=0, keepdims=True)
    el2 = jnp.where(rowf == i0, neg, el)
    v1 = jnp.max(el2, axis=0, keepdims=True)
    i1 = jnp.min(jnp.where(el2 == v1, rowf, big), axis=0, keepdims=True)
    e = jnp.exp(v1 - v0)
    den = 1.0 + e
    w0 = (1.0 / den) * g_w
    w1 = (e / den) * g_w
    eid0 = gidx * epg + i0
    eid1 = gidx * epg + i1
    n_exp = N_GROUPS * epg
    erow = lax.broadcasted_iota(jnp.int32, (n_exp, tm), 0).astype(F32)
    sel0 = erow == eid0
    sel1 = erow == eid1
    onehot = jnp.where(sel0 | sel1, 1.0, 0.0)
    before = jnp.dot(onehot.astype(BF16), tri_ref[...], preferred_element_type=F32) + carry_scr[...]
    r0 = jnp.sum(jnp.where(sel0, before, 0.0), axis=0, keepdims=True)
    r1 = jnp.sum(jnp.where(sel1, before, 0.0), axis=0, keepdims=True)
    carry_scr[...] += jnp.sum(onehot, axis=1, keepdims=True)
    cnt_ref[...] = jnp.broadcast_to(carry_scr[...], cnt_ref.shape)
    rec_t = jnp.zeros((ROUTE_FIELDS, tm), F32)
    for j, val in enumerate((eid0, eid1, w0, w1, r0, r1)):
        rec_t = jnp.where(row == j, val, rec_t)
    route_t_ref[...] = rec_t
    route_ref[...] = jnp.concatenate(
        [rec_t, jnp.zeros((ROUTE_LANES - ROUTE_FIELDS, tm), F32)], axis=0).T


def _route(ys5, yret, x2, wo_bf, n2, wr, br):
    n, d = x2.shape
    d_s5 = ys5.shape[1]
    d_ret = yret.shape[1]
    tm = TM_ROUTE
    n_exp = N_GROUPS * EXPERTS_PER_GROUP
    tri = (jnp.arange(tm)[:, None] < jnp.arange(tm)[None, :]).astype(BF16)
    return pl.pallas_call(
        _route_body,
        grid=(n // tm,),
        in_specs=[pl.BlockSpec((tm, d_s5), lambda i: (i, 0)),
                  pl.BlockSpec((tm, d_ret), lambda i: (i, 0)),
                  pl.BlockSpec((tm, d), lambda i: (i, 0)),
                  pl.BlockSpec((d_s5 + d_ret, d), lambda i: (0, 0)),
                  pl.BlockSpec((1, d), lambda i: (0, 0)),
                  pl.BlockSpec((d, 2 * ROUTE_LANES), lambda i: (0, 0)),
                  pl.BlockSpec((1, ROUTE_LANES), lambda i: (0, 0)),
                  pl.BlockSpec((tm, tm), lambda i: (0, 0))],
        out_specs=[pl.BlockSpec((tm, d), lambda i: (i, 0)),
                   pl.BlockSpec((tm, ROUTE_LANES), lambda i: (i, 0)),
                   pl.BlockSpec((ROUTE_FIELDS, tm), lambda i: (0, i)),
                   pl.BlockSpec((n_exp, ROUTE_LANES), lambda i: (0, 0))],
        out_shape=[jax.ShapeDtypeStruct((n, d), F32),
                   jax.ShapeDtypeStruct((n, ROUTE_LANES), F32),
                   jax.ShapeDtypeStruct((ROUTE_FIELDS, n), F32),
                   jax.ShapeDtypeStruct((n_exp, ROUTE_LANES), F32)],
        scratch_shapes=[pltpu.VMEM((n_exp, 1), F32)],
        compiler_params=_cparams(("arbitrary",)),
        name="outproj_route",
    )(ys5, yret, x2, wo_bf, n2, wr, br, tri)


def _dispatch_body(pends_ref, padded_ref, dest_ref, h_ref, n2_ref, xs_ref,
                   hn_scr, zero_scr, sem, zsem):
    tm = h_ref.shape[0]

    @pl.when(pl.program_id(0) == 0)
    def _():
        zero_scr[...] = jnp.zeros_like(zero_scr)

        def zero_copy(e):
            first = pl.multiple_of((pends_ref[e] - R_BLK) * ROW_TILES, R_BLK * ROW_TILES)
            return pltpu.make_async_copy(zero_scr, xs_ref.at[pl.ds(first, R_BLK * ROW_TILES)], zsem)

        def zstart(e, carry):
            @pl.when(padded_ref[e] > 0)
            def _():
                zero_copy(e).start()
            return carry

        def zwait(e, carry):
            @pl.when(padded_ref[e] > 0)
            def _():
                zero_copy(e).wait()
            return carry

        lax.fori_loop(0, pends_ref.shape[0], zstart, 0)
        lax.fori_loop(0, pends_ref.shape[0], zwait, 0)

    step = pl.program_id(0)
    slot = step % 2
    buf = hn_scr.at[slot]
    _rows_to_tiles(buf, _pack_rows(_rms(h_ref[...], n2_ref[...])))

    for r in range(tm):
        for k in range(TOP_K):
            pltpu.make_async_copy(_tile_rows(buf, r), _tile_rows(xs_ref, dest_ref[k * tm + r]),
                                  sem.at[slot]).start(priority=k)

    def drain(s):
        for k in range(TOP_K):
            pltpu.make_async_copy(hn_scr.at[s], xs_ref.at[pl.ds(0, tm * ROW_TILES)], sem.at[s]).wait()

    @pl.when(step > 0)
    def _():
        drain(1 - slot)

    @pl.when(step == pl.num_programs(0) - 1)
    def _():
        drain(slot)


def _dispatch(pends, padded, dest, h, n2, p_rows):
    n, d = h.shape
    tm = TM_ROWS
    grid_spec = pltpu.PrefetchScalarGridSpec(
        num_scalar_prefetch=2,
        grid=(n // tm,),
        in_specs=[pl.BlockSpec((TOP_K * tm,), lambda i, pe, pa: (i,), memory_space=pltpu.SMEM),
                  pl.BlockSpec((tm, d), lambda i, pe, pa: (i, 0)),
                  pl.BlockSpec((1, d), lambda i, pe, pa: (0, 0))],
        out_specs=pl.BlockSpec(memory_space=pl.ANY),
        scratch_shapes=[pltpu.VMEM((2, tm * ROW_TILES, V7X_LANES), U32),
                        pltpu.VMEM((R_BLK * ROW_TILES, V7X_LANES), U32),
                        pltpu.SemaphoreType.DMA((2,)), pltpu.SemaphoreType.DMA(())])
    assert d == 2 * ROW_TILES * V7X_LANES
    return pl.pallas_call(
        _dispatch_body,
        grid_spec=grid_spec,
        out_shape=jax.ShapeDtypeStruct((p_rows * ROW_TILES, V7X_LANES), U32),
        compiler_params=_cparams(("arbitrary",)),
        name="dispatch",
    )(pends, padded, dest, h, n2)


def _expert_body(be_ref, nu_ref, xs_ref, wg_ref, wu_ref, wd_ref, ys_ref):
    j = pl.program_id(0)

    @pl.when(j < nu_ref[0])
    def _():
        x = _unpack_rows(_tiles_to_rows(xs_ref, R_BLK)).astype(BF16)
        gate = jnp.dot(x, wg_ref[...], preferred_element_type=F32)
        up = jnp.dot(x, wu_ref[...], preferred_element_type=F32)
        hid = (jax.nn.silu(gate) * up).astype(BF16)
        _rows_to_tiles(ys_ref, _pack_rows(jnp.dot(hid, wd_ref[...], preferred_element_type=F32)))

    @pl.when(j >= nu_ref[0])
    def _():
        ys_ref[...] = jnp.zeros_like(ys_ref)


def _experts(block_e, n_used, xs, w_gate, w_up, w_down):
    d, d_e = w_gate.shape[1:]
    blk_rows = R_BLK * ROW_TILES
    nblk = xs.shape[0] // blk_rows

    def row_map(j, be, nu):
        return (jnp.maximum(jnp.minimum(j, nu[0] - 1), 0), 0)

    def w_map(j, be, nu):
        return (be[j], 0, 0)

    grid_spec = pltpu.PrefetchScalarGridSpec(
        num_scalar_prefetch=2,
        grid=(nblk,),
        in_specs=[pl.BlockSpec((blk_rows, V7X_LANES), row_map),
                  pl.BlockSpec((None, d, d_e), w_map),
                  pl.BlockSpec((None, d, d_e), w_map),
                  pl.BlockSpec((None, d_e, d), w_map)],
        out_specs=pl.BlockSpec((blk_rows, V7X_LANES), lambda j, be, nu: (j, 0)))
    return pl.pallas_call(
        _expert_body,
        grid_spec=grid_spec,
        out_shape=jax.ShapeDtypeStruct(xs.shape, xs.dtype),
        compiler_params=_cparams(("arbitrary",)),
        name="experts",
    )(block_e, n_used, xs, w_gate, w_up, w_down)


def _combine_body(dest0_ref, dest1_ref, dest2_ref, h_ref, route_ref, fw_ref, ys_ref, out_ref,
                  *scratch, last):
    bufs, sem = scratch[:GATHER_RING], scratch[GATHER_RING]
    tm = h_ref.shape[0]
    step = pl.program_id(0)

    def issue(d_ref, s):
        for r in range(tm):
            for k in range(TOP_K):
                pltpu.make_async_copy(_tile_rows(ys_ref, d_ref[k * tm + r]),
                                      _tile_rows(bufs[s].at[k], r), sem.at[s]).start(priority=k)

    def drain(s):
        for k in range(TOP_K):
            pltpu.make_async_copy(ys_ref.at[pl.ds(0, tm * ROW_TILES)], bufs[s].at[k], sem.at[s]).wait()

    @pl.when(step == 0)
    def _():
        issue(dest0_ref, 0)
        issue(dest1_ref, 1)

    for s in range(GATHER_RING):
        @pl.when(step % GATHER_RING == s)
        def _():
            drain(s)
            issue(dest2_ref, (s + 2) % GATHER_RING)
            rec = route_ref[...]
            hh = (h_ref[...] + rec[:, 2:3] * _unpack_rows(_tiles_to_rows(bufs[s].at[0], tm))
                  + rec[:, 3:4] * _unpack_rows(_tiles_to_rows(bufs[s].at[1], tm)))
            out_ref[...] = _rms(hh, fw_ref[...])

    @pl.when(step == last)
    def _():
        drain((last + 1) % GATHER_RING)
        drain((last + 2) % GATHER_RING)


def _combine(dest, h, route, fw, ys):
    n, d = h.shape
    tm = TM_ROWS
    steps = n // tm
    return pl.pallas_call(
        functools.partial(_combine_body, last=steps - 1),
        grid=(steps,),
        in_specs=[pl.BlockSpec((TOP_K * tm,), lambda i: (i,), memory_space=pltpu.SMEM),
                  pl.BlockSpec((TOP_K * tm,), lambda i: (jnp.minimum(i + 1, steps - 1),),
                               memory_space=pltpu.SMEM),
                  pl.BlockSpec((TOP_K * tm,), lambda i: (jnp.minimum(i + 2, steps - 1),),
                               memory_space=pltpu.SMEM),
                  pl.BlockSpec((tm, d), lambda i: (i, 0)),
                  pl.BlockSpec((tm, ROUTE_LANES), lambda i: (i, 0)),
                  pl.BlockSpec((1, d), lambda i: (0, 0)),
                  pl.BlockSpec(memory_space=pl.ANY)],
        out_specs=pl.BlockSpec((tm, d), lambda i: (i, 0)),
        out_shape=jax.ShapeDtypeStruct((n, d), F32),
        scratch_shapes=([pltpu.VMEM((TOP_K, tm * ROW_TILES, V7X_LANES), U32)] * GATHER_RING
                        + [pltpu.SemaphoreType.DMA((GATHER_RING,))]),
        compiler_params=_cparams(("arbitrary",)),
        name="combine",
    )(dest, dest, dest, h, route, fw, ys)


def _plan(route_t, counts, n_experts, n_blocks):
    eid = route_t[0:TOP_K].astype(jnp.int32).reshape(-1, V7X_LANES)
    rank = route_t[4:4 + TOP_K].astype(jnp.int32).reshape(-1, V7X_LANES)
    cnt = counts[:, 0].astype(jnp.int32)
    padded = (cnt + R_BLK - 1) // R_BLK * R_BLK
    pends = jnp.cumsum(padded)
    pstart = pends - padded
    dest = rank
    for e in range(n_experts):
        dest = dest + jnp.where(eid == e, pstart[e], 0)
    n_used = pends[-1] // R_BLK
    blk = jnp.minimum(jnp.arange(n_blocks, dtype=jnp.int32), n_used - 1)
    block_e = jnp.minimum(jnp.sum(pends[None, :] <= (blk * R_BLK)[:, None], axis=1), n_experts - 1)
    return (dest.astype(jnp.int32), block_e.astype(jnp.int32), n_used.reshape(1).astype(jnp.int32),
            pends.astype(jnp.int32), padded.astype(jnp.int32))


def _layer(h3, norm1_w, w_in, s5_a_re, s5_a_im, s5_b_re, s5_b_im, s5_c_re, s5_c_im, s5_d,
           s5_log_dt, s5_w_glu, s5_b_glu, ret_norm_w, w_out, norm2_w, router_group_w,
           router_group_b, router_expert_w, router_expert_b, moe_w_gate, moe_w_up, moe_w_down,
           out_norm_w):
    nb, seq, d = h3.shape
    n = nb * seq
    d_s5 = s5_d.shape[0]
    d_ret = ret_norm_w.shape[0]
    n_experts = moe_w_gate.shape[0]
    x2 = h3.reshape(n, d)

    u_s5, y_ret, moe_bf = _proj_retention(h3, norm1_w.reshape(1, d), w_in.astype(BF16), d_s5, d_ret,
                                          ret_norm_w.reshape(1, d_ret).astype(F32),
                                          (moe_w_gate, moe_w_up, moe_w_down))

    bp, cp, tp, a_tab = _s5_tables(s5_a_re, s5_a_im, s5_b_re, s5_b_im, s5_c_re, s5_c_im,
                                   s5_log_dt, nb, S5_TAU)
    y_s5 = _s5(u_s5.reshape(n, d_s5), nb, bp, cp, tp, a_tab, s5_d.reshape(1, d_s5).astype(F32),
               s5_w_glu.astype(BF16), s5_b_glu.reshape(1, d_s5).astype(F32))

    assert n_experts == N_GROUPS * EXPERTS_PER_GROUP
    wr = (jnp.zeros((d, ROUTE_LANES), F32).at[:, :N_GROUPS].set(router_group_w.astype(F32))
          .at[:, ROUTE_EXPERT_ROW:ROUTE_EXPERT_ROW + n_experts].set(router_expert_w.astype(F32)))
    br = (jnp.zeros((1, ROUTE_LANES), F32).at[0, :N_GROUPS].set(router_group_b.astype(F32))
          .at[0, ROUTE_EXPERT_ROW:ROUTE_EXPERT_ROW + n_experts].set(router_expert_b.astype(F32)))
    wr_hi = wr.astype(BF16)
    wr = jnp.concatenate([wr_hi, (wr - wr_hi.astype(F32)).astype(BF16)], axis=1)
    h, route, route_t, counts = _route(y_s5.reshape(n, d_s5), y_ret.reshape(n, d_ret), x2,
                                       w_out.astype(BF16), norm2_w.reshape(1, d), wr, br)

    n_blocks = (n * TOP_K) // R_BLK + n_experts
    dest, block_e, n_used, pends, padded = _plan(route_t, counts, n_experts, n_blocks)
    dest = dest.reshape(TOP_K, n // TM_ROWS, TM_ROWS).transpose(1, 0, 2).reshape(-1)
    xs = _dispatch(pends, padded, dest, h, norm2_w.reshape(1, d), n_blocks * R_BLK)
    ys = _experts(block_e, n_used, xs, *moe_bf)
    out = _combine(dest, h, route, out_norm_w.reshape(1, d), ys)
    return out.reshape(nb, seq, d)


def kernel(x, norm1_w, w_in, s5_a_re, s5_a_im, s5_b_re, s5_b_im, s5_c_re, s5_c_im, s5_d, s5_log_dt, s5_w_glu, s5_b_glu, ret_norm_w, w_out, norm2_w, router_group_w, router_group_b, router_expert_w, router_expert_b, moe_w_gate, moe_w_up, moe_w_down, final_norm_w):
    depth = norm1_w.shape[0]
    assert depth == 1, "the fused final norm assumes a single layer"
    l = 0
    return _layer(x, norm1_w[l], w_in[l], s5_a_re[l], s5_a_im[l], s5_b_re[l], s5_b_im[l],
                  s5_c_re[l], s5_c_im[l], s5_d[l], s5_log_dt[l], s5_w_glu[l], s5_b_glu[l],
                  ret_norm_w[l], w_out[l], norm2_w[l], router_group_w[l], router_group_b[l],
                  router_expert_w[l], router_expert_b[l], moe_w_gate[l], moe_w_up[l],
                  moe_w_down[l], final_norm_w)
```

```python
import functools

import jax
import jax.numpy as jnp
from jax import lax
from jax.experimental import pallas as pl
from jax.experimental.pallas import tpu as pltpu

F32 = jnp.float32
BF16 = jnp.bfloat16

EPS = 1e-6
ROPE_BASE = 10000.0
RET_HEADS = 8
TOP_K = 2
N_GROUPS = 4
EXPERTS_PER_GROUP = 8

V7X_LANES = 128
V7X_SUBLANES = 8
V7X_VMEM_LIMIT = 56 * 1024 * 1024

T_S5 = 512
S5_TAU = 8
S5_PAD = 8
T_RET = 128
TM_ROUTE = 512
TM_ROWS = 256
R_BLK = 512
GATHER_RING = 3
RUN_CHUNK = 64
RUN_TABLE = 128
ROUTE_LANES = 128
ROUTE_FIELDS = 8
ROUTE_EXPERT_ROW = 8


def _rms(x, w):
    return x * lax.rsqrt(jnp.mean(x * x, axis=-1, keepdims=True) + EPS) * w


def _cparams(sem):
    return pltpu.CompilerParams(dimension_semantics=sem, vmem_limit_bytes=V7X_VMEM_LIMIT)


ROW_TILES = V7X_SUBLANES // 2
U32 = jnp.uint32


def _pack_rows(x):
    half = x.shape[1] // 2

    def bf16_bits(v):
        return lax.bitcast_convert_type(v.astype(BF16).astype(F32), U32)

    return bf16_bits(x[:, half:]) | (bf16_bits(x[:, :half]) >> 16)


def _unpack_rows(w):
    lo = lax.bitcast_convert_type(w << 16, F32)
    hi = lax.bitcast_convert_type(w & U32(0xFFFF0000), F32)
    return jnp.concatenate([lo, hi], axis=1)


def _rows_to_tiles(ref, val):
    rows = val.shape[0]
    for s in range(ROW_TILES):
        ref[pl.ds(s, rows, stride=ROW_TILES), :] = val[:, s * V7X_LANES:(s + 1) * V7X_LANES]


def _tiles_to_rows(ref, rows):
    return jnp.concatenate(
        [ref[pl.ds(s, rows, stride=ROW_TILES), :] for s in range(ROW_TILES)], axis=1)


def _tile_rows(ref, row):
    return ref.at[pl.ds(pl.multiple_of(row * ROW_TILES, ROW_TILES), ROW_TILES)]


def _s5_body(u_ref, bp_ref, cp_ref, tp_ref, a_ref, dd_ref, wglu_ref, bglu_ref, y_ref,
             v_scr, sp_scr, st_scr, io_scr, *, nb, nk, seg, tau):
    @pl.when(pl.program_id(0) == 0)
    def _():
        st_scr[...] = jnp.zeros_like(st_scr)

    nblk = bp_ref.shape[0]
    cw = bp_ref.shape[1] // tau
    sw = bp_ref.shape[2]
    d_s5 = nblk * cw
    tiles = sw // V7X_LANES
    ht = tiles // 2
    ctiles = d_s5 // V7X_LANES
    srows = nb * nk

    u_all = u_ref[...].reshape(srows * tau, d_s5).astype(F32)
    for c in range(ctiles):
        io_scr[c] = u_all[:, c * V7X_LANES:(c + 1) * V7X_LANES]
    u_steps = [jnp.concatenate([io_scr[c, pl.ds(j, srows, stride=tau), :] for c in range(ctiles)],
                               axis=1) for j in range(tau)]
    u_blocks = [jnp.concatenate([u_steps[j][:, blk * cw:(blk + 1) * cw] for j in range(tau)],
                                axis=1).astype(BF16) for blk in range(nblk)]

    def block_inputs(blk):
        return u_blocks[blk]

    for blk in range(nblk):
        half, q = divmod(blk, 2)
        v = jnp.dot(block_inputs(blk), bp_ref[blk], preferred_element_type=F32)
        for b in range(nb):
            for j in range(tiles):
                v_scr[q * tiles + j, pl.ds((half * nb + b) * seg, nk), :] = (
                    v[b * nk:(b + 1) * nk, j * V7X_LANES:(j + 1) * V7X_LANES])

    rows = 2 * nb
    ar = [[a_ref[0, q * ht + i] for i in range(ht)] for q in range(2)]
    ai = [[a_ref[1, q * ht + i] for i in range(ht)] for q in range(2)]
    sr = [[st_scr[q * tiles + i] for i in range(ht)] for q in range(2)]
    si = [[st_scr[q * tiles + ht + i] for i in range(ht)] for q in range(2)]
    for k in range(nk):
        for q in range(2):
            for i in range(ht):
                jr = q * tiles + i
                ji = q * tiles + ht + i
                sp_scr[jr, pl.ds(k, rows, stride=seg), :] = sr[q][i]
                sp_scr[ji, pl.ds(k, rows, stride=seg), :] = si[q][i]
                vr = v_scr[jr, pl.ds(k, rows, stride=seg), :]
                vi = v_scr[ji, pl.ds(k, rows, stride=seg), :]
                nr = ar[q][i] * sr[q][i] - ai[q][i] * si[q][i] + vr
                ni = ar[q][i] * si[q][i] + ai[q][i] * sr[q][i] + vi
                sr[q][i], si[q][i] = nr, ni
    for q in range(2):
        for i in range(ht):
            st_scr[q * tiles + i] = sr[q][i]
            st_scr[q * tiles + ht + i] = si[q][i]

    yb = []
    for blk in range(nblk):
        half, q = divmod(blk, 2)
        sp = jnp.concatenate(
            [jnp.concatenate([sp_scr[q * tiles + j, pl.ds((half * nb + b) * seg, nk), :]
                              for j in range(tiles)], axis=1) for b in range(nb)],
            axis=0).astype(BF16)
        yb.append(jnp.dot(sp, cp_ref[blk], preferred_element_type=F32)
                  + jnp.dot(block_inputs(blk), tp_ref[blk], preferred_element_type=F32))
    for j in range(tau):
        y = jnp.concatenate([yb[blk][:, j * cw:(j + 1) * cw] for blk in range(nblk)], axis=1)
        y = y + dd_ref[...] * u_steps[j]
        y = jax.nn.gelu(y)
        z = jnp.dot(y.astype(BF16), wglu_ref[...], preferred_element_type=F32) + bglu_ref[...]
        out = y * jax.nn.sigmoid(z)
        for c in range(ctiles):
            io_scr[c, pl.ds(j, srows, stride=tau), :] = out[:, c * V7X_LANES:(c + 1) * V7X_LANES]
    y_all = jnp.concatenate([io_scr[c] for c in range(ctiles)], axis=1)
    y_ref[...] = y_all.reshape(nb, nk * tau, d_s5).astype(y_ref.dtype)


def _s5(u_s5, nb, bp, cp, tp, a_tab, dd, wglu_bf, bglu):
    n, d_s5 = u_s5.shape
    seq = n // nb
    tau = S5_TAU
    nk = T_S5 // tau
    seg = nk + S5_PAD
    nblk, _, sw = bp.shape
    rows = 2 * nb
    assert rows == V7X_SUBLANES and nblk == 4

    def whole(a):
        return pl.BlockSpec(a.shape, lambda c: (0,) * a.ndim)

    body = functools.partial(_s5_body, nb=nb, nk=nk, seg=seg, tau=tau)
    out = pl.pallas_call(
        body,
        grid=(seq // T_S5,),
        in_specs=[pl.BlockSpec((nb, T_S5, d_s5), lambda c: (0, c, 0)),
                  whole(bp), whole(cp), whole(tp), whole(a_tab), whole(dd), whole(wglu_bf),
                  whole(bglu)],
        out_specs=pl.BlockSpec((nb, T_S5, d_s5), lambda c: (0, c, 0)),
        out_shape=jax.ShapeDtypeStruct((nb, seq, d_s5), BF16),
        scratch_shapes=[pltpu.VMEM((2 * sw // V7X_LANES, rows * seg, V7X_LANES), F32),
                        pltpu.VMEM((2 * sw // V7X_LANES, rows * seg, V7X_LANES), F32),
                        pltpu.VMEM((2 * sw // V7X_LANES, rows, V7X_LANES), F32),
                        pltpu.VMEM((d_s5 // V7X_LANES, nb * T_S5, V7X_LANES), F32)],
        compiler_params=_cparams(("arbitrary",)),
        name="s5",
    )(u_s5.reshape(nb, seq, d_s5), bp, cp, tp, a_tab, dd, wglu_bf, bglu)
    return out.reshape(n, d_s5)


def _s5_tables(a_re, a_im, b_re, b_im, c_re, c_im, log_dt, nb, tau):
    hp = lax.Precision.HIGHEST
    g, p = a_re.shape
    hch = b_re.shape[2]
    gpb = V7X_LANES // hch
    nblk = g // gpb
    lam_r, lam_i = a_re.astype(F32), a_im.astype(F32)
    dt = jnp.exp(log_dt.astype(F32))[:, None]
    mag = jnp.exp(lam_r * dt)
    ab_r = mag * jnp.cos(lam_i * dt)
    ab_i = mag * jnp.sin(lam_i * dt)
    den = lam_r * lam_r + lam_i * lam_i
    zr = ((ab_r - 1.0) * lam_r + ab_i * lam_i) / den
    zi = (ab_i * lam_r - (ab_r - 1.0) * lam_i) / den
    br_, bi_ = b_re.astype(F32), b_im.astype(F32)
    bb_r = zr[..., None] * br_ - zi[..., None] * bi_
    bb_i = zr[..., None] * bi_ + zi[..., None] * br_
    cr, ci = c_re.astype(F32), c_im.astype(F32)
    pr, pi = [jnp.ones_like(ab_r)], [jnp.zeros_like(ab_i)]
    for _ in range(tau):
        pr, pi = pr + [pr[-1] * ab_r - pi[-1] * ab_i], pi + [pr[-1] * ab_i + pi[-1] * ab_r]
    pw_r, pw_i = jnp.stack(pr), jnp.stack(pi)

    def blockdiag(x):
        r, c = x.shape[-2:]
        x = jnp.tile(x.reshape(tau, nblk, gpb * r, c), (1, 1, 1, gpb))
        same = (jnp.arange(gpb * r)[:, None] // r) == (jnp.arange(gpb * c)[None, :] // c)
        return jnp.where(same, x, 0.0)

    wr_ = jnp.stack([pr[tau - 1 - j] for j in range(tau)])[:, :, None, :]
    wi_ = jnp.stack([pi[tau - 1 - j] for j in range(tau)])[:, :, None, :]
    bt_r, bt_i = jnp.swapaxes(bb_r, 1, 2)[None], jnp.swapaxes(bb_i, 1, 2)[None]
    bp_r = blockdiag(wr_ * bt_r - wi_ * bt_i)
    bp_i = blockdiag(wr_ * bt_i + wi_ * bt_r)
    bp = jnp.concatenate([jnp.concatenate([bp_r[j], bp_i[j]], axis=-1) for j in range(tau)],
                         axis=1).astype(BF16)

    qr_, qi_ = pw_r[1:, :, :, None], pw_i[1:, :, :, None]
    ct_r, ct_i = jnp.swapaxes(cr, 1, 2)[None], jnp.swapaxes(ci, 1, 2)[None]
    cp_r = blockdiag(ct_r * qr_ - ct_i * qi_)
    cp_i = blockdiag(ct_r * qi_ + ct_i * qr_)
    cp = jnp.concatenate([jnp.concatenate([cp_r[i], -cp_i[i]], axis=1) for i in range(tau)],
                         axis=2).astype(BF16)

    ab_r_ = pw_r[:tau, :, :, None] * bb_r[None] - pw_i[:tau, :, :, None] * bb_i[None]
    ab_i_ = pw_r[:tau, :, :, None] * bb_i[None] + pw_i[:tau, :, :, None] * bb_r[None]
    kd = blockdiag(jnp.einsum('ghp,dgpe->dgeh', cr, ab_r_, precision=hp)
                   - jnp.einsum('ghp,dgpe->dgeh', ci, ab_i_, precision=hp))
    kzero = jnp.zeros_like(kd[0])
    tp = jnp.concatenate(
        [jnp.concatenate([kd[i - j] if i >= j else kzero for i in range(tau)], axis=2)
         for j in range(tau)], axis=1).astype(BF16)

    a_tab = jnp.stack([jnp.repeat(pw_r[tau].reshape(2, -1), nb, axis=0),
                       jnp.repeat(pw_i[tau].reshape(2, -1), nb, axis=0)])
    a_tab = a_tab.reshape(2, 2 * nb, -1, V7X_LANES).transpose(0, 2, 1, 3)
    return bp, cp, tp, a_tab


def _proj_ret_body(x_ref, n1_ref, w_ref, cos_ref, sin_ref, dec_ref, qdec_ref, kdect_ref,
                   cdec_ref, ms_ref, avg_ref, nw_ref, wg_ref, wu_ref, wd_ref,
                   us5_ref, y_ref, wg_out, wu_out, wd_out, u_even, u_odd, st_scr,
                   *, t_len, dh, scale):
    wg_out[...] = wg_ref[...].astype(wg_out.dtype)
    wu_out[...] = wu_ref[...].astype(wu_out.dtype)
    wd_out[...] = wd_ref[...].astype(wd_out.dtype)

    step = pl.program_id(0)
    nb = x_ref.shape[0]
    width = y_ref.shape[-1]
    d_s5 = us5_ref.shape[-1]

    @pl.when(step == 0)
    def _():
        u_odd[...] = jnp.zeros_like(u_odd)
        st_scr[...] = jnp.zeros_like(st_scr)

    def project(u_out):
        xn = _rms(x_ref[...].reshape(nb * t_len, x_ref.shape[-1]), n1_ref[...]).astype(BF16)
        u = jnp.dot(xn, w_ref[...], preferred_element_type=F32)
        us5_ref[...] = u[:, :d_s5].reshape(nb, t_len, d_s5).astype(us5_ref.dtype)
        u_out[...] = u[:, d_s5:].astype(u_out.dtype)

    for parity, (u_out, u_in) in enumerate(((u_even, u_odd), (u_odd, u_even))):
        @pl.when(step % 2 == parity)
        def _():
            project(u_out)
            _ret_chunk(u_in, cos_ref, sin_ref, dec_ref, qdec_ref, kdect_ref, cdec_ref, ms_ref, avg_ref,
                       nw_ref, y_ref, st_scr, t_len=t_len, dh=dh, scale=scale)


def _ret_chunk(u_ref, cos_ref, sin_ref, dec_ref, qdec_ref, kdect_ref, cdec_ref, ms_ref, avg_ref,
               nw_ref, y_ref, st_scr, *, t_len, dh, scale):
    nb = y_ref.shape[0]
    width = y_ref.shape[-1]
    heads = width // dh
    pairs = width // V7X_LANES
    reps = width // cos_ref.shape[-1]
    cos = jnp.concatenate([cos_ref[...]] * reps, axis=1)
    sin = jnp.concatenate([sin_ref[...]] * reps, axis=1)
    lane = lax.broadcasted_iota(jnp.int32, (t_len, width), 1)
    first = (lane % dh) < (dh // 2)

    def rot(x):
        x = x.astype(F32)
        partner = jnp.where(first, pltpu.roll(x, width - dh // 2, 1), pltpu.roll(x, dh // 2, 1))
        return x * cos + partner * sin

    def group_mean(x):
        return jnp.dot(x.astype(BF16), avg_ref[...], preferred_element_type=F32)

    zero_k = jnp.zeros((dh, t_len), BF16)
    zero_v = jnp.zeros((t_len, V7X_LANES), BF16)
    low_head = lax.broadcasted_iota(jnp.int32, (t_len, V7X_LANES), 1) < dh

    outs = []
    for b in range(nb):
        rows = slice(b * t_len, (b + 1) * t_len)
        q = rot(u_ref[rows, 0:width])
        k = rot(u_ref[rows, width:2 * width]) * scale
        vb = u_ref[rows, 2 * width:3 * width]
        kt = k.T
        ktb = kt.astype(BF16)
        qb = q.astype(BF16)
        inners = []
        for p in range(pairs):
            ps = slice(p * V7X_LANES, (p + 1) * V7X_LANES)
            k_lo, k_hi = ktb[p * V7X_LANES:p * V7X_LANES + dh], ktb[p * V7X_LANES + dh:(p + 1) * V7X_LANES]
            kbd = jnp.concatenate([jnp.concatenate([k_lo, zero_k], axis=0),
                                   jnp.concatenate([zero_k, k_hi], axis=0)], axis=1)
            sc = (jnp.dot(qb[:, ps], kbd, preferred_element_type=F32)
                  * dec_ref[:, 2 * p * t_len:2 * (p + 1) * t_len])
            vt = vb[:, ps]
            vbd = jnp.concatenate([jnp.where(low_head, vt, zero_v),
                                   jnp.where(low_head, zero_v, vt)], axis=0)
            inners.append(jnp.dot(sc.astype(BF16), vbd, preferred_element_type=F32))
        inner = jnp.concatenate(inners, axis=1)
        qd = (q * qdec_ref[...]).astype(BF16)
        kdt = (kt * kdect_ref[...]).astype(BF16)
        crosses = []
        for p in range(pairs):
            ps = slice(p * V7X_LANES, (p + 1) * V7X_LANES)
            state = st_scr[b, p]
            crosses.append(jnp.dot(qd[:, ps], state.astype(BF16), preferred_element_type=F32))
            kv = jnp.dot(kdt[ps, :], vb[:, ps], preferred_element_type=F32)
            st_scr[b, p] = state * cdec_ref[:, ps] + kv * ms_ref[...]
        outs.append(inner + jnp.concatenate(crosses, axis=1))

    o = jnp.concatenate(outs, axis=0)
    dlt = o - group_mean(o)
    on = dlt * lax.rsqrt(group_mean(dlt * dlt) + EPS) * nw_ref[...]
    gate = u_ref[:, 3 * width:4 * width].astype(F32)
    y_ref[...] = (jax.nn.silu(gate) * on).reshape(nb, t_len, width).astype(y_ref.dtype)


def _proj_retention(x3, n1, w_bf, d_s5, d_ret, norm_w, moe_w):
    nb, seq, d = x3.shape
    dh = d_ret // RET_HEADS
    half = dh // 2
    t_len = T_RET
    lg = jnp.log(1.0 - 2.0 ** (-5.0 - jnp.arange(RET_HEADS, dtype=F32)))
    t = jnp.arange(t_len, dtype=F32)
    diff = t[:, None] - t[None, :]
    dec = jnp.where(diff >= 0, jnp.exp(lg[:, None, None] * jnp.maximum(diff, 0.0)), 0.0)
    dec_all = dec.transpose(1, 0, 2).reshape(t_len, RET_HEADS * t_len)
    qdec = jnp.repeat(jnp.exp(lg[:, None] * (t + 1.0)[None, :]).T, dh, axis=1)
    kdect = jnp.repeat(jnp.exp(lg[:, None] * (t_len - 1 - t)[None, :]), dh, axis=0)
    cdec = jnp.repeat(jnp.exp(lg * t_len), dh)[None, :]
    head_of_lane = jnp.arange(d_ret) // dh
    pair_head = jnp.arange(V7X_LANES) // dh
    mask_s = (pair_head[:, None] == pair_head[None, :]).astype(F32)
    avg = (head_of_lane[:, None] == head_of_lane[None, :]).astype(F32) / dh
    assert dh & (dh - 1) == 0
    avg = avg.astype(BF16)
    inv = ROPE_BASE ** (-jnp.arange(half, dtype=F32) / half)
    ang = jnp.arange(seq, dtype=F32)[:, None] * inv[None, :]
    reps = V7X_LANES // dh
    cos_t = jnp.tile(jnp.cos(ang), (1, 2 * reps))
    sin_t = jnp.tile(jnp.concatenate([-jnp.sin(ang), jnp.sin(ang)], axis=1), (1, reps))

    def whole(a):
        return pl.BlockSpec(a.shape, lambda c: (0,) * a.ndim)

    nc = seq // t_len

    def proj_chunk(c):
        return jnp.minimum(c, nc - 1)

    def ret_chunk(c):
        return jnp.maximum(c - 1, 0)

    def sliced(w):
        e, rows, cols = w.shape
        if nc >= e:
            parts = nc // e
            assert nc % e == 0 and rows % (parts * V7X_SUBLANES) == 0
            return w.reshape(e * parts, rows // parts, cols), 1
        assert e % nc == 0
        return w, e // nc

    moe_in, moe_specs, moe_shapes = [], [], []
    for w in moe_w:
        ws, per_step = sliced(w)
        blk = (per_step,) + ws.shape[1:]
        moe_in.append(ws)
        moe_specs.append(pl.BlockSpec(blk, lambda c: (proj_chunk(c), 0, 0)))
        moe_shapes.append(jax.ShapeDtypeStruct(ws.shape, BF16))

    body = functools.partial(_proj_ret_body, t_len=t_len, dh=dh, scale=dh ** -0.5)
    outs = pl.pallas_call(
        body,
        grid=(nc + 1,),
        in_specs=[pl.BlockSpec((nb, t_len, d), lambda c: (0, proj_chunk(c), 0)),
                  whole(n1), whole(w_bf),
                  pl.BlockSpec((t_len, V7X_LANES), lambda c: (ret_chunk(c), 0)),
                  pl.BlockSpec((t_len, V7X_LANES), lambda c: (ret_chunk(c), 0)),
                  whole(dec_all), whole(qdec), whole(kdect), whole(cdec),
                  whole(mask_s), whole(avg), whole(norm_w)] + moe_specs,
        out_specs=[pl.BlockSpec((nb, t_len, d_s5), lambda c: (0, proj_chunk(c), 0)),
                   pl.BlockSpec((nb, t_len, d_ret), lambda c: (0, ret_chunk(c), 0))] + moe_specs,
        out_shape=[jax.ShapeDtypeStruct((nb, seq, d_s5), BF16),
                   jax.ShapeDtypeStruct((nb, seq, d_ret), BF16)] + moe_shapes,
        scratch_shapes=[pltpu.VMEM((nb * t_len, 4 * d_ret), BF16),
                        pltpu.VMEM((nb * t_len, 4 * d_ret), BF16),
                        pltpu.VMEM((nb, d_ret // V7X_LANES, V7X_LANES, V7X_LANES), F32)],
        compiler_params=_cparams(("arbitrary",)),
        name="inproj_retention",
    )(x3, n1, w_bf, cos_t, sin_t, dec_all, qdec, kdect, cdec, mask_s, avg, norm_w, *moe_in)
    u_s5, y_ret = outs[:2]
    moe_bf = [o.reshape(w.shape) for o, w in zip(outs[2:], moe_w)]
    return u_s5, y_ret, moe_bf


def _route_body(ys5_ref, yret_ref, x_ref, wo_ref, n2_ref, wr_ref, br_ref, tri_ref, ltri_ref,
                h_ref, route_ref, route_t_ref, cnt_ref, runs_ref, carry_scr):
    @pl.when(pl.program_id(0) == 0)
    def _():
        carry_scr[...] = jnp.zeros_like(carry_scr)

    d_s5 = ys5_ref.shape[1]
    h = (x_ref[...]
         + jnp.dot(ys5_ref[...], wo_ref[0:d_s5], preferred_element_type=F32)
         + jnp.dot(yret_ref[...], wo_ref[d_s5:], preferred_element_type=F32))
    h_ref[...] = h
    hn = _rms(h, n2_ref[...])
    hi = hn.astype(BF16)
    lo = (hn - hi.astype(F32)).astype(BF16)
    p_hi = jnp.dot(hi, wr_ref[...], preferred_element_type=F32)
    p_lo = jnp.dot(lo, wr_ref[:, 0:ROUTE_LANES], preferred_element_type=F32)
    logits = p_hi[:, 0:ROUTE_LANES] + p_hi[:, ROUTE_LANES:] + p_lo + br_ref[...]
    tm = logits.shape[0]
    lt = logits.T
    epg = EXPERTS_PER_GROUP
    row = lax.broadcasted_iota(jnp.int32, (epg, tm), 0)
    rowf = row.astype(F32)
    neg = -jnp.inf
    big = float(epg)
    gl = jnp.where(row < N_GROUPS, lt[0:epg], neg)
    gmax = jnp.max(gl, axis=0, keepdims=True)
    gidx = jnp.min(jnp.where(gl == gmax, rowf, big), axis=0, keepdims=True)
    g_w = 1.0 / jnp.sum(jnp.exp(gl - gmax), axis=0, keepdims=True)
    el = jnp.full((epg, tm), neg, F32)
    for g in range(N_GROUPS):
        el = jnp.where(gidx == g, lt[ROUTE_EXPERT_ROW + g * epg:ROUTE_EXPERT_ROW + (g + 1) * epg], el)
    v0 = jnp.max(el, axis=0, keepdims=True)
    i0 = jnp.min(jnp.where(el == v0, rowf, big), axis=0, keepdims=True)
    el2 = jnp.where(rowf == i0, neg, el)
    v1 = jnp.max(el2, axis=0, keepdims=True)
    i1 = jnp.min(jnp.where(el2 == v1, rowf, big), axis=0, keepdims=True)
    e = jnp.exp(v1 - v0)
    den = 1.0 + e
    w0 = (1.0 / den) * g_w
    w1 = (e / den) * g_w
    eid0 = gidx * epg + i0
    eid1 = gidx * epg + i1
    n_exp = N_GROUPS * epg
    erow = lax.broadcasted_iota(jnp.int32, (n_exp, tm), 0).astype(F32)
    sel0 = erow == eid0
    sel1 = erow == eid1
    onehot = jnp.where(sel0 | sel1, 1.0, 0.0)
    local = jnp.dot(onehot.astype(BF16), tri_ref[...], preferred_element_type=F32)
    seen = carry_scr[...]
    before = local + seen
    r0 = jnp.sum(jnp.where(sel0, before, 0.0), axis=0, keepdims=True)
    r1 = jnp.sum(jnp.where(sel1, before, 0.0), axis=0, keepdims=True)
    tile_cnt = jnp.sum(onehot, axis=1, keepdims=True)
    carry_scr[...] = seen + tile_cnt
    cnt_ref[...] = jnp.broadcast_to(seen + tile_cnt, cnt_ref.shape)
    run_off = jnp.dot(ltri_ref[...], jnp.broadcast_to(tile_cnt, (n_exp, ROUTE_LANES)),
                      precision=lax.Precision.HIGHEST, preferred_element_type=F32)[:, 0:1]
    lp0 = jnp.sum(jnp.where(sel0, local + run_off, 0.0), axis=0, keepdims=True)
    lp1 = jnp.sum(jnp.where(sel1, local + run_off, 0.0), axis=0, keepdims=True)
    lane = lax.broadcasted_iota(jnp.int32, (n_exp, ROUTE_LANES), 1)
    runs_ref[...] = jnp.where(lane == 0, tile_cnt, jnp.where(lane == 1, run_off,
                                                             jnp.where(lane == 2, seen, 0.0)))
    rec_t = jnp.zeros((ROUTE_FIELDS, tm), F32)
    for j, val in enumerate((eid0, eid1, w0, w1, r0, r1, lp0, lp1)):
        rec_t = jnp.where(row == j, val, rec_t)
    route_t_ref[...] = rec_t
    route_ref[...] = jnp.concatenate(
        [rec_t, jnp.zeros((ROUTE_LANES - ROUTE_FIELDS, tm), F32)], axis=0).T


def _route(ys5, yret, x2, wo_bf, n2, wr, br):
    n, d = x2.shape
    d_s5 = ys5.shape[1]
    d_ret = yret.shape[1]
    tm = TM_ROUTE
    n_exp = N_GROUPS * EXPERTS_PER_GROUP
    tri = (jnp.arange(tm)[:, None] < jnp.arange(tm)[None, :]).astype(BF16)
    ltri = (jnp.arange(n_exp)[:, None] > jnp.arange(n_exp)[None, :]).astype(F32)
    return pl.pallas_call(
        _route_body,
        grid=(n // tm,),
        in_specs=[pl.BlockSpec((tm, d_s5), lambda i: (i, 0)),
                  pl.BlockSpec((tm, d_ret), lambda i: (i, 0)),
                  pl.BlockSpec((tm, d), lambda i: (i, 0)),
                  pl.BlockSpec((d_s5 + d_ret, d), lambda i: (0, 0)),
                  pl.BlockSpec((1, d), lambda i: (0, 0)),
                  pl.BlockSpec((d, 2 * ROUTE_LANES), lambda i: (0, 0)),
                  pl.BlockSpec((1, ROUTE_LANES), lambda i: (0, 0)),
                  pl.BlockSpec((tm, tm), lambda i: (0, 0)),
                  pl.BlockSpec((n_exp, n_exp), lambda i: (0, 0))],
        out_specs=[pl.BlockSpec((tm, d), lambda i: (i, 0)),
                   pl.BlockSpec((tm, ROUTE_LANES), lambda i: (i, 0)),
                   pl.BlockSpec((ROUTE_FIELDS, tm), lambda i: (0, i)),
                   pl.BlockSpec((n_exp, ROUTE_LANES), lambda i: (0, 0)),
                   pl.BlockSpec((n_exp, ROUTE_LANES), lambda i: (i, 0))],
        out_shape=[jax.ShapeDtypeStruct((n, d), F32),
                   jax.ShapeDtypeStruct((n, ROUTE_LANES), F32),
                   jax.ShapeDtypeStruct((ROUTE_FIELDS, n), F32),
                   jax.ShapeDtypeStruct((n_exp, ROUTE_LANES), F32),
                   jax.ShapeDtypeStruct((n // tm * n_exp, ROUTE_LANES), F32)],
        scratch_shapes=[pltpu.VMEM((n_exp, 1), F32)],
        compiler_params=_cparams(("arbitrary",)),
        name="outproj_route",
    )(ys5, yret, x2, wo_bf, n2, wr, br, tri, ltri)


def _dispatch_body(pends_ref, padded_ref, runs_ref, h_ref, n2_ref, lp_ref, xs_ref,
                   srt_scr, zero_scr, sem, zsem):
    tm = h_ref.shape[0]
    n_exp = pends_ref.shape[0]

    @pl.when(pl.program_id(0) == 0)
    def _():
        zero_scr[...] = jnp.zeros_like(zero_scr)

        def zero_copy(e):
            first = pl.multiple_of((pends_ref[e] - R_BLK) * ROW_TILES, R_BLK * ROW_TILES)
            return pltpu.make_async_copy(zero_scr, xs_ref.at[pl.ds(first, R_BLK * ROW_TILES)], zsem)

        def zstart(e, carry):
            @pl.when(padded_ref[e] > 0)
            def _():
                zero_copy(e).start()
            return carry

        def zwait(e, carry):
            @pl.when(padded_ref[e] > 0)
            def _():
                zero_copy(e).wait()
            return carry

        lax.fori_loop(0, pends_ref.shape[0], zstart, 0)
        lax.fori_loop(0, pends_ref.shape[0], zwait, 0)

    step = pl.program_id(0)
    slot = step % 2
    buf = srt_scr.at[slot]
    hnb = _rms(h_ref[...], n2_ref[...]).astype(BF16)
    pos = lax.broadcasted_iota(jnp.int32, (TOP_K * tm, tm), 0).astype(F32)
    perm = jnp.where((pos == lp_ref[6:7, :]) | (pos == lp_ref[7:8, :]), 1.0, 0.0).astype(BF16)
    _rows_to_tiles(buf, _pack_rows(jnp.dot(perm, hnb, preferred_element_type=F32)))

    def copy(src_row, dst_row, rows, priority):
        src = buf.at[pl.ds(pl.multiple_of(src_row * ROW_TILES, ROW_TILES), rows * ROW_TILES)]
        dst = xs_ref.at[pl.ds(pl.multiple_of(dst_row * ROW_TILES, ROW_TILES), rows * ROW_TILES)]
        pltpu.make_async_copy(src, dst, sem.at[slot]).start(priority=priority)

    def run(e, carry):
        cnt, off, dst = runs_ref[e], runs_ref[n_exp + e], runs_ref[2 * n_exp + e]

        def chunk(c, cc):
            copy(off + c * RUN_CHUNK, dst + c * RUN_CHUNK, RUN_CHUNK, 0)
            return cc

        whole = lax.shift_right_logical(cnt, RUN_CHUNK.bit_length() - 1)
        lax.fori_loop(0, whole, chunk, 0)
        size = RUN_CHUNK // 2
        while size >= 1:
            done = cnt & ~(2 * size - 1)

            @pl.when((cnt & size) != 0)
            def _():
                copy(off + done, dst + done, size, 1)

            size //= 2
        return carry

    lax.fori_loop(0, n_exp, run, 0)

    def drain(s):
        pltpu.make_async_copy(srt_scr.at[s], xs_ref.at[pl.ds(0, TOP_K * tm * ROW_TILES)], sem.at[s]).wait()

    @pl.when(step > 0)
    def _():
        drain(1 - slot)

    @pl.when(step == pl.num_programs(0) - 1)
    def _():
        drain(slot)


def _dispatch(pends, padded, runs, h, n2, route_t, p_rows):
    n, d = h.shape
    tm = TM_ROUTE
    assert 3 * pends.shape[0] <= RUN_TABLE
    grid_spec = pltpu.PrefetchScalarGridSpec(
        num_scalar_prefetch=2,
        grid=(n // tm,),
        in_specs=[pl.BlockSpec((RUN_TABLE,), lambda i, pe, pa: (i,), memory_space=pltpu.SMEM),
                  pl.BlockSpec((tm, d), lambda i, pe, pa: (i, 0)),
                  pl.BlockSpec((1, d), lambda i, pe, pa: (0, 0)),
                  pl.BlockSpec((ROUTE_FIELDS, tm), lambda i, pe, pa: (0, i))],
        out_specs=pl.BlockSpec(memory_space=pl.ANY),
        scratch_shapes=[pltpu.VMEM((2, TOP_K * tm * ROW_TILES, V7X_LANES), U32),
                        pltpu.VMEM((R_BLK * ROW_TILES, V7X_LANES), U32),
                        pltpu.SemaphoreType.DMA((2,)), pltpu.SemaphoreType.DMA(())])
    assert d == 2 * ROW_TILES * V7X_LANES
    return pl.pallas_call(
        _dispatch_body,
        grid_spec=grid_spec,
        out_shape=jax.ShapeDtypeStruct((p_rows * ROW_TILES, V7X_LANES), U32),
        compiler_params=_cparams(("arbitrary",)),
        name="dispatch",
    )(pends, padded, runs, h, n2, route_t)


def _expert_body(be_ref, nu_ref, xs_ref, wg_ref, wu_ref, wd_ref, ys_ref):
    j = pl.program_id(0)

    @pl.when(j < nu_ref[0])
    def _():
        x = _unpack_rows(_tiles_to_rows(xs_ref, R_BLK)).astype(BF16)
        gate = jnp.dot(x, wg_ref[...], preferred_element_type=F32)
        up = jnp.dot(x, wu_ref[...], preferred_element_type=F32)
        hid = (jax.nn.silu(gate) * up).astype(BF16)
        _rows_to_tiles(ys_ref, _pack_rows(jnp.dot(hid, wd_ref[...], preferred_element_type=F32)))

    @pl.when(j >= nu_ref[0])
    def _():
        ys_ref[...] = jnp.zeros_like(ys_ref)


def _experts(block_e, n_used, xs, w_gate, w_up, w_down):
    d, d_e = w_gate.shape[1:]
    blk_rows = R_BLK * ROW_TILES
    nblk = xs.shape[0] // blk_rows

    def row_map(j, be, nu):
        return (jnp.maximum(jnp.minimum(j, nu[0] - 1), 0), 0)

    def w_map(j, be, nu):
        return (be[j], 0, 0)

    grid_spec = pltpu.PrefetchScalarGridSpec(
        num_scalar_prefetch=2,
        grid=(nblk,),
        in_specs=[pl.BlockSpec((blk_rows, V7X_LANES), row_map),
                  pl.BlockSpec((None, d, d_e), w_map),
                  pl.BlockSpec((None, d, d_e), w_map),
                  pl.BlockSpec((None, d_e, d), w_map)],
        out_specs=pl.BlockSpec((blk_rows, V7X_LANES), lambda j, be, nu: (j, 0)))
    return pl.pallas_call(
        _expert_body,
        grid_spec=grid_spec,
        out_shape=jax.ShapeDtypeStruct(xs.shape, xs.dtype),
        compiler_params=_cparams(("arbitrary",)),
        name="experts",
    )(block_e, n_used, xs, w_gate, w_up, w_down)


def _combine_body(dest0_ref, dest1_ref, dest2_ref, h_ref, route_ref, fw_ref, ys_ref, out_ref,
                  *scratch, last):
    bufs, sem = scratch[:GATHER_RING], scratch[GATHER_RING]
    tm = h_ref.shape[0]
    step = pl.program_id(0)

    def issue(d_ref, s):
        for r in range(tm):
            for k in range(TOP_K):
                pltpu.make_async_copy(_tile_rows(ys_ref, d_ref[k * tm + r]),
                                      _tile_rows(bufs[s].at[k], r), sem.at[s]).start(priority=k)

    def drain(s):
        for k in range(TOP_K):
            pltpu.make_async_copy(ys_ref.at[pl.ds(0, tm * ROW_TILES)], bufs[s].at[k], sem.at[s]).wait()

    @pl.when(step == 0)
    def _():
        issue(dest0_ref, 0)
        issue(dest1_ref, 1)

    for s in range(GATHER_RING):
        @pl.when(step % GATHER_RING == s)
        def _():
            drain(s)
            issue(dest2_ref, (s + 2) % GATHER_RING)
            rec = route_ref[...]
            hh = (h_ref[...] + rec[:, 2:3] * _unpack_rows(_tiles_to_rows(bufs[s].at[0], tm))
                  + rec[:, 3:4] * _unpack_rows(_tiles_to_rows(bufs[s].at[1], tm)))
            out_ref[...] = _rms(hh, fw_ref[...])

    @pl.when(step == last)
    def _():
        drain((last + 1) % GATHER_RING)
        drain((last + 2) % GATHER_RING)


def _combine(dest, h, route, fw, ys):
    n, d = h.shape
    tm = TM_ROWS
    steps = n // tm
    return pl.pallas_call(
        functools.partial(_combine_body, last=steps - 1),
        grid=(steps,),
        in_specs=[pl.BlockSpec((TOP_K * tm,), lambda i: (i,), memory_space=pltpu.SMEM),
                  pl.BlockSpec((TOP_K * tm,), lambda i: (jnp.minimum(i + 1, steps - 1),),
                               memory_space=pltpu.SMEM),
                  pl.BlockSpec((TOP_K * tm,), lambda i: (jnp.minimum(i + 2, steps - 1),),
                               memory_space=pltpu.SMEM),
                  pl.BlockSpec((tm, d), lambda i: (i, 0)),
                  pl.BlockSpec((tm, ROUTE_LANES), lambda i: (i, 0)),
                  pl.BlockSpec((1, d), lambda i: (0, 0)),
                  pl.BlockSpec(memory_space=pl.ANY)],
        out_specs=pl.BlockSpec((tm, d), lambda i: (i, 0)),
        out_shape=jax.ShapeDtypeStruct((n, d), F32),
        scratch_shapes=([pltpu.VMEM((TOP_K, tm * ROW_TILES, V7X_LANES), U32)] * GATHER_RING
                        + [pltpu.SemaphoreType.DMA((GATHER_RING,))]),
        compiler_params=_cparams(("arbitrary",)),
        name="combine",
    )(dest, dest, dest, h, route, fw, ys)


def _run_table(runs, pstart, n_experts):
    r = runs.reshape(-1, n_experts, ROUTE_LANES)[:, :, 0:3].astype(jnp.int32)
    first_xs = pstart[None, :] + r[:, :, 2]
    tab = jnp.concatenate([r[:, :, 0], r[:, :, 1], first_xs,
                           jnp.zeros((r.shape[0], RUN_TABLE - 3 * n_experts), jnp.int32)], axis=1)
    return tab.reshape(-1)


def _plan(route_t, counts, n_experts, n_blocks):
    eid = route_t[0:TOP_K].astype(jnp.int32).reshape(-1, V7X_LANES)
    rank = route_t[4:4 + TOP_K].astype(jnp.int32).reshape(-1, V7X_LANES)
    cnt = counts[:, 0].astype(jnp.int32)
    padded = (cnt + R_BLK - 1) // R_BLK * R_BLK
    pends = jnp.cumsum(padded)
    pstart = pends - padded
    dest = rank
    for e in range(n_experts):
        dest = dest + jnp.where(eid == e, pstart[e], 0)
    n_used = pends[-1] // R_BLK
    blk = jnp.minimum(jnp.arange(n_blocks, dtype=jnp.int32), n_used - 1)
    block_e = jnp.minimum(jnp.sum(pends[None, :] <= (blk * R_BLK)[:, None], axis=1), n_experts - 1)
    return (dest.astype(jnp.int32), block_e.astype(jnp.int32), n_used.reshape(1).astype(jnp.int32),
            pends.astype(jnp.int32), padded.astype(jnp.int32), pstart.astype(jnp.int32))


def _layer(h3, norm1_w, w_in, s5_a_re, s5_a_im, s5_b_re, s5_b_im, s5_c_re, s5_c_im, s5_d,
           s5_log_dt, s5_w_glu, s5_b_glu, ret_norm_w, w_out, norm2_w, router_group_w,
           router_group_b, router_expert_w, router_expert_b, moe_w_gate, moe_w_up, moe_w_down,
           out_norm_w):
    nb, seq, d = h3.shape
    n = nb * seq
    d_s5 = s5_d.shape[0]
    d_ret = ret_norm_w.shape[0]
    n_experts = moe_w_gate.shape[0]
    x2 = h3.reshape(n, d)

    u_s5, y_ret, moe_bf = _proj_retention(h3, norm1_w.reshape(1, d), w_in.astype(BF16), d_s5, d_ret,
                                          ret_norm_w.reshape(1, d_ret).astype(F32),
                                          (moe_w_gate, moe_w_up, moe_w_down))

    bp, cp, tp, a_tab = _s5_tables(s5_a_re, s5_a_im, s5_b_re, s5_b_im, s5_c_re, s5_c_im,
                                   s5_log_dt, nb, S5_TAU)
    y_s5 = _s5(u_s5.reshape(n, d_s5), nb, bp, cp, tp, a_tab, s5_d.reshape(1, d_s5).astype(F32),
               s5_w_glu.astype(BF16), s5_b_glu.reshape(1, d_s5).astype(F32))

    assert n_experts == N_GROUPS * EXPERTS_PER_GROUP
    wr = (jnp.zeros((d, ROUTE_LANES), F32).at[:, :N_GROUPS].set(router_group_w.astype(F32))
          .at[:, ROUTE_EXPERT_ROW:ROUTE_EXPERT_ROW + n_experts].set(router_expert_w.astype(F32)))
    br = (jnp.zeros((1, ROUTE_LANES), F32).at[0, :N_GROUPS].set(router_group_b.astype(F32))
          .at[0, ROUTE_EXPERT_ROW:ROUTE_EXPERT_ROW + n_experts].set(router_expert_b.astype(F32)))
    wr_hi = wr.astype(BF16)
    wr = jnp.concatenate([wr_hi, (wr - wr_hi.astype(F32)).astype(BF16)], axis=1)
    h, route, route_t, counts, runs = _route(y_s5.reshape(n, d_s5), y_ret.reshape(n, d_ret), x2,
                                             w_out.astype(BF16), norm2_w.reshape(1, d), wr, br)

    n_blocks = (n * TOP_K) // R_BLK + n_experts
    dest, block_e, n_used, pends, padded, pstart = _plan(route_t, counts, n_experts, n_blocks)
    dest = dest.reshape(TOP_K, n // TM_ROWS, TM_ROWS).transpose(1, 0, 2).reshape(-1)
    xs = _dispatch(pends, padded, _run_table(runs, pstart, n_experts), h, norm2_w.reshape(1, d),
                   route_t, n_blocks * R_BLK)
    ys = _experts(block_e, n_used, xs, *moe_bf)
    out = _combine(dest, h, route, out_norm_w.reshape(1, d), ys)
    return out.reshape(nb, seq, d)


def kernel(x, norm1_w, w_in, s5_a_re, s5_a_im, s5_b_re, s5_b_im, s5_c_re, s5_c_im, s5_d, s5_log_dt, s5_w_glu, s5_b_glu, ret_norm_w, w_out, norm2_w, router_group_w, router_group_b, router_expert_w, router_expert_b, moe_w_gate, moe_w_up, moe_w_down, final_norm_w):
    depth = norm1_w.shape[0]
    assert depth == 1, "the fused final norm assumes a single layer"
    l = 0
    return _layer(x, norm1_w[l], w_in[l], s5_a_re[l], s5_a_im[l], s5_b_re[l], s5_b_im[l],
                  s5_c_re[l], s5_c_im[l], s5_d[l], s5_log_dt[l], s5_w_glu[l], s5_b_glu[l],
                  ret_norm_w[l], w_out[l], norm2_w[l], router_group_w[l], router_group_b[l],
                  router_expert_w[l], router_expert_b[l], moe_w_gate[l], moe_w_up[l],
                  moe_w_down[l], final_norm_w)
```

```python
import functools

import jax
import jax.numpy as jnp
from jax import lax
from jax.experimental import pallas as pl
from jax.experimental.pallas import tpu as pltpu

F32 = jnp.float32
BF16 = jnp.bfloat16

EPS = 1e-6
ROPE_BASE = 10000.0
RET_HEADS = 8
TOP_K = 2
N_GROUPS = 4
EXPERTS_PER_GROUP = 8

V7X_LANES = 128
V7X_SUBLANES = 8
V7X_VMEM_LIMIT = 56 * 1024 * 1024

T_S5 = 512
S5_TAU = 8
S5_PAD = 8
T_RET = 128
TM_ROUTE = 512
TM_ROWS = 256
R_BLK = 512
GATHER_RING = 3
RUN_CHUNK = 32
RUN_TABLE = 128
ROUTE_LANES = 128
ROUTE_FIELDS = 8
ROUTE_EXPERT_ROW = 8


def _rms(x, w):
    return x * lax.rsqrt(jnp.mean(x * x, axis=-1, keepdims=True) + EPS) * w


def _cparams(sem):
    return pltpu.CompilerParams(dimension_semantics=sem, vmem_limit_bytes=V7X_VMEM_LIMIT)


ROW_TILES = V7X_SUBLANES // 2
U32 = jnp.uint32


def _pack_rows(x):
    half = x.shape[1] // 2

    def bf16_bits(v):
        return lax.bitcast_convert_type(v.astype(BF16).astype(F32), U32)

    return bf16_bits(x[:, half:]) | (bf16_bits(x[:, :half]) >> 16)


def _unpack_rows(w):
    lo = lax.bitcast_convert_type(w << 16, F32)
    hi = lax.bitcast_convert_type(w & U32(0xFFFF0000), F32)
    return jnp.concatenate([lo, hi], axis=1)


def _rows_to_tiles(ref, val):
    rows = val.shape[0]
    for s in range(ROW_TILES):
        ref[pl.ds(s, rows, stride=ROW_TILES), :] = val[:, s * V7X_LANES:(s + 1) * V7X_LANES]


def _tiles_to_rows(ref, rows):
    return jnp.concatenate(
        [ref[pl.ds(s, rows, stride=ROW_TILES), :] for s in range(ROW_TILES)], axis=1)


def _tile_rows(ref, row):
    return ref.at[pl.ds(pl.multiple_of(row * ROW_TILES, ROW_TILES), ROW_TILES)]


def _s5_body(u_ref, bp_ref, cp_ref, tp_ref, a_ref, dd_ref, wglu_ref, bglu_ref, y_ref,
             v_scr, sp_scr, st_scr, io_scr, *, nb, nk, seg, tau):
    @pl.when(pl.program_id(0) == 0)
    def _():
        st_scr[...] = jnp.zeros_like(st_scr)

    nblk = bp_ref.shape[0]
    cw = bp_ref.shape[1] // tau
    sw = bp_ref.shape[2]
    d_s5 = nblk * cw
    tiles = sw // V7X_LANES
    ht = tiles // 2
    ctiles = d_s5 // V7X_LANES
    srows = nb * nk

    u_all = u_ref[...].reshape(srows * tau, d_s5).astype(F32)
    for c in range(ctiles):
        io_scr[c] = u_all[:, c * V7X_LANES:(c + 1) * V7X_LANES]
    u_steps = [jnp.concatenate([io_scr[c, pl.ds(j, srows, stride=tau), :] for c in range(ctiles)],
                               axis=1) for j in range(tau)]
    u_blocks = [jnp.concatenate([u_steps[j][:, blk * cw:(blk + 1) * cw] for j in range(tau)],
                                axis=1).astype(BF16) for blk in range(nblk)]

    def block_inputs(blk):
        return u_blocks[blk]

    for blk in range(nblk):
        half, q = divmod(blk, 2)
        v = jnp.dot(block_inputs(blk), bp_ref[blk], preferred_element_type=F32)
        for b in range(nb):
            for j in range(tiles):
                v_scr[q * tiles + j, pl.ds((half * nb + b) * seg, nk), :] = (
                    v[b * nk:(b + 1) * nk, j * V7X_LANES:(j + 1) * V7X_LANES])

    rows = 2 * nb
    ar = [[a_ref[0, q * ht + i] for i in range(ht)] for q in range(2)]
    ai = [[a_ref[1, q * ht + i] for i in range(ht)] for q in range(2)]
    sr = [[st_scr[q * tiles + i] for i in range(ht)] for q in range(2)]
    si = [[st_scr[q * tiles + ht + i] for i in range(ht)] for q in range(2)]
    for k in range(nk):
        for q in range(2):
            for i in range(ht):
                jr = q * tiles + i
                ji = q * tiles + ht + i
                sp_scr[jr, pl.ds(k, rows, stride=seg), :] = sr[q][i]
                sp_scr[ji, pl.ds(k, rows, stride=seg), :] = si[q][i]
                vr = v_scr[jr, pl.ds(k, rows, stride=seg), :]
                vi = v_scr[ji, pl.ds(k, rows, stride=seg), :]
                nr = ar[q][i] * sr[q][i] - ai[q][i] * si[q][i] + vr
                ni = ar[q][i] * si[q][i] + ai[q][i] * sr[q][i] + vi
                sr[q][i], si[q][i] = nr, ni
    for q in range(2):
        for i in range(ht):
            st_scr[q * tiles + i] = sr[q][i]
            st_scr[q * tiles + ht + i] = si[q][i]

    yb = []
    for blk in range(nblk):
        half, q = divmod(blk, 2)
        sp = jnp.concatenate(
            [jnp.concatenate([sp_scr[q * tiles + j, pl.ds((half * nb + b) * seg, nk), :]
                              for j in range(tiles)], axis=1) for b in range(nb)],
            axis=0).astype(BF16)
        yb.append(jnp.dot(sp, cp_ref[blk], preferred_element_type=F32)
                  + jnp.dot(block_inputs(blk), tp_ref[blk], preferred_element_type=F32))
    for j in range(tau):
        y = jnp.concatenate([yb[blk][:, j * cw:(j + 1) * cw] for blk in range(nblk)], axis=1)
        y = y + dd_ref[...] * u_steps[j]
        y = jax.nn.gelu(y)
        z = jnp.dot(y.astype(BF16), wglu_ref[...], preferred_element_type=F32) + bglu_ref[...]
        out = y * jax.nn.sigmoid(z)
        for c in range(ctiles):
            io_scr[c, pl.ds(j, srows, stride=tau), :] = out[:, c * V7X_LANES:(c + 1) * V7X_LANES]
    y_all = jnp.concatenate([io_scr[c] for c in range(ctiles)], axis=1)
    y_ref[...] = y_all.reshape(nb, nk * tau, d_s5).astype(y_ref.dtype)


def _s5(u_s5, nb, bp, cp, tp, a_tab, dd, wglu_bf, bglu):
    n, d_s5 = u_s5.shape
    seq = n // nb
    tau = S5_TAU
    nk = T_S5 // tau
    seg = nk + S5_PAD
    nblk, _, sw = bp.shape
    rows = 2 * nb
    assert rows == V7X_SUBLANES and nblk == 4

    def whole(a):
        return pl.BlockSpec(a.shape, lambda c: (0,) * a.ndim)

    body = functools.partial(_s5_body, nb=nb, nk=nk, seg=seg, tau=tau)
    out = pl.pallas_call(
        body,
        grid=(seq // T_S5,),
        in_specs=[pl.BlockSpec((nb, T_S5, d_s5), lambda c: (0, c, 0)),
                  whole(bp), whole(cp), whole(tp), whole(a_tab), whole(dd), whole(wglu_bf),
                  whole(bglu)],
        out_specs=pl.BlockSpec((nb, T_S5, d_s5), lambda c: (0, c, 0)),
        out_shape=jax.ShapeDtypeStruct((nb, seq, d_s5), BF16),
        scratch_shapes=[pltpu.VMEM((2 * sw // V7X_LANES, rows * seg, V7X_LANES), F32),
                        pltpu.VMEM((2 * sw // V7X_LANES, rows * seg, V7X_LANES), F32),
                        pltpu.VMEM((2 * sw // V7X_LANES, rows, V7X_LANES), F32),
                        pltpu.VMEM((d_s5 // V7X_LANES, nb * T_S5, V7X_LANES), F32)],
        compiler_params=_cparams(("arbitrary",)),
        name="s5",
    )(u_s5.reshape(nb, seq, d_s5), bp, cp, tp, a_tab, dd, wglu_bf, bglu)
    return out.reshape(n, d_s5)


def _s5_tables(a_re, a_im, b_re, b_im, c_re, c_im, log_dt, nb, tau):
    hp = lax.Precision.HIGHEST
    g, p = a_re.shape
    hch = b_re.shape[2]
    gpb = V7X_LANES // hch
    nblk = g // gpb
    lam_r, lam_i = a_re.astype(F32), a_im.astype(F32)
    dt = jnp.exp(log_dt.astype(F32))[:, None]
    mag = jnp.exp(lam_r * dt)
    ab_r = mag * jnp.cos(lam_i * dt)
    ab_i = mag * jnp.sin(lam_i * dt)
    den = lam_r * lam_r + lam_i * lam_i
    zr = ((ab_r - 1.0) * lam_r + ab_i * lam_i) / den
    zi = (ab_i * lam_r - (ab_r - 1.0) * lam_i) / den
    br_, bi_ = b_re.astype(F32), b_im.astype(F32)
    bb_r = zr[..., None] * br_ - zi[..., None] * bi_
    bb_i = zr[..., None] * bi_ + zi[..., None] * br_
    cr, ci = c_re.astype(F32), c_im.astype(F32)
    pr, pi = [jnp.ones_like(ab_r)], [jnp.zeros_like(ab_i)]
    for _ in range(tau):
        pr, pi = pr + [pr[-1] * ab_r - pi[-1] * ab_i], pi + [pr[-1] * ab_i + pi[-1] * ab_r]
    pw_r, pw_i = jnp.stack(pr), jnp.stack(pi)

    def blockdiag(x):
        r, c = x.shape[-2:]
        x = jnp.tile(x.reshape(tau, nblk, gpb * r, c), (1, 1, 1, gpb))
        same = (jnp.arange(gpb * r)[:, None] // r) == (jnp.arange(gpb * c)[None, :] // c)
        return jnp.where(same, x, 0.0)

    wr_ = jnp.stack([pr[tau - 1 - j] for j in range(tau)])[:, :, None, :]
    wi_ = jnp.stack([pi[tau - 1 - j] for j in range(tau)])[:, :, None, :]
    bt_r, bt_i = jnp.swapaxes(bb_r, 1, 2)[None], jnp.swapaxes(bb_i, 1, 2)[None]
    bp_r = blockdiag(wr_ * bt_r - wi_ * bt_i)
    bp_i = blockdiag(wr_ * bt_i + wi_ * bt_r)
    bp = jnp.concatenate([jnp.concatenate([bp_r[j], bp_i[j]], axis=-1) for j in range(tau)],
                         axis=1).astype(BF16)

    qr_, qi_ = pw_r[1:, :, :, None], pw_i[1:, :, :, None]
    ct_r, ct_i = jnp.swapaxes(cr, 1, 2)[None], jnp.swapaxes(ci, 1, 2)[None]
    cp_r = blockdiag(ct_r * qr_ - ct_i * qi_)
    cp_i = blockdiag(ct_r * qi_ + ct_i * qr_)
    cp = jnp.concatenate([jnp.concatenate([cp_r[i], -cp_i[i]], axis=1) for i in range(tau)],
                         axis=2).astype(BF16)

    ab_r_ = pw_r[:tau, :, :, None] * bb_r[None] - pw_i[:tau, :, :, None] * bb_i[None]
    ab_i_ = pw_r[:tau, :, :, None] * bb_i[None] + pw_i[:tau, :, :, None] * bb_r[None]
    kd = blockdiag(jnp.einsum('ghp,dgpe->dgeh', cr, ab_r_, precision=hp)
                   - jnp.einsum('ghp,dgpe->dgeh', ci, ab_i_, precision=hp))
    kzero = jnp.zeros_like(kd[0])
    tp = jnp.concatenate(
        [jnp.concatenate([kd[i - j] if i >= j else kzero for i in range(tau)], axis=2)
         for j in range(tau)], axis=1).astype(BF16)

    a_tab = jnp.stack([jnp.repeat(pw_r[tau].reshape(2, -1), nb, axis=0),
                       jnp.repeat(pw_i[tau].reshape(2, -1), nb, axis=0)])
    a_tab = a_tab.reshape(2, 2 * nb, -1, V7X_LANES).transpose(0, 2, 1, 3)
    return bp, cp, tp, a_tab


def _proj_ret_body(x_ref, n1_ref, w_ref, cos_ref, sin_ref, dec_ref, qdec_ref, kdect_ref,
                   cdec_ref, ms_ref, avg_ref, nw_ref, wg_ref, wu_ref, wd_ref,
                   us5_ref, y_ref, wg_out, wu_out, wd_out, u_even, u_odd, st_scr,
                   *, t_len, dh, scale):
    wg_out[...] = wg_ref[...].astype(wg_out.dtype)
    wu_out[...] = wu_ref[...].astype(wu_out.dtype)
    wd_out[...] = wd_ref[...].astype(wd_out.dtype)

    step = pl.program_id(0)
    nb = x_ref.shape[0]
    width = y_ref.shape[-1]
    d_s5 = us5_ref.shape[-1]

    @pl.when(step == 0)
    def _():
        u_odd[...] = jnp.zeros_like(u_odd)
        st_scr[...] = jnp.zeros_like(st_scr)

    def project(u_out):
        xn = _rms(x_ref[...].reshape(nb * t_len, x_ref.shape[-1]), n1_ref[...]).astype(BF16)
        u = jnp.dot(xn, w_ref[...], preferred_element_type=F32)
        us5_ref[...] = u[:, :d_s5].reshape(nb, t_len, d_s5).astype(us5_ref.dtype)
        u_out[...] = u[:, d_s5:].astype(u_out.dtype)

    for parity, (u_out, u_in) in enumerate(((u_even, u_odd), (u_odd, u_even))):
        @pl.when(step % 2 == parity)
        def _():
            project(u_out)
            _ret_chunk(u_in, cos_ref, sin_ref, dec_ref, qdec_ref, kdect_ref, cdec_ref, ms_ref, avg_ref,
                       nw_ref, y_ref, st_scr, t_len=t_len, dh=dh, scale=scale)


def _ret_chunk(u_ref, cos_ref, sin_ref, dec_ref, qdec_ref, kdect_ref, cdec_ref, ms_ref, avg_ref,
               nw_ref, y_ref, st_scr, *, t_len, dh, scale):
    nb = y_ref.shape[0]
    width = y_ref.shape[-1]
    heads = width // dh
    pairs = width // V7X_LANES
    reps = width // cos_ref.shape[-1]
    cos = jnp.concatenate([cos_ref[...]] * reps, axis=1)
    sin = jnp.concatenate([sin_ref[...]] * reps, axis=1)
    lane = lax.broadcasted_iota(jnp.int32, (t_len, width), 1)
    first = (lane % dh) < (dh // 2)

    def rot(x):
        x = x.astype(F32)
        partner = jnp.where(first, pltpu.roll(x, width - dh // 2, 1), pltpu.roll(x, dh // 2, 1))
        return x * cos + partner * sin

    def group_mean(x):
        return jnp.dot(x.astype(BF16), avg_ref[...], preferred_element_type=F32)

    zero_k = jnp.zeros((dh, t_len), BF16)
    zero_v = jnp.zeros((t_len, V7X_LANES), BF16)
    low_head = lax.broadcasted_iota(jnp.int32, (t_len, V7X_LANES), 1) < dh

    outs = []
    for b in range(nb):
        rows = slice(b * t_len, (b + 1) * t_len)
        q = rot(u_ref[rows, 0:width])
        k = rot(u_ref[rows, width:2 * width]) * scale
        vb = u_ref[rows, 2 * width:3 * width]
        kt = k.T
        ktb = kt.astype(BF16)
        qb = q.astype(BF16)
        inners = []
        for p in range(pairs):
            ps = slice(p * V7X_LANES, (p + 1) * V7X_LANES)
            k_lo, k_hi = ktb[p * V7X_LANES:p * V7X_LANES + dh], ktb[p * V7X_LANES + dh:(p + 1) * V7X_LANES]
            kbd = jnp.concatenate([jnp.concatenate([k_lo, zero_k], axis=0),
                                   jnp.concatenate([zero_k, k_hi], axis=0)], axis=1)
            sc = (jnp.dot(qb[:, ps], kbd, preferred_element_type=F32)
                  * dec_ref[:, 2 * p * t_len:2 * (p + 1) * t_len])
            vt = vb[:, ps]
            vbd = jnp.concatenate([jnp.where(low_head, vt, zero_v),
                                   jnp.where(low_head, zero_v, vt)], axis=0)
            inners.append(jnp.dot(sc.astype(BF16), vbd, preferred_element_type=F32))
        inner = jnp.concatenate(inners, axis=1)
        qd = (q * qdec_ref[...]).astype(BF16)
        kdt = (kt * kdect_ref[...]).astype(BF16)
        crosses = []
        for p in range(pairs):
            ps = slice(p * V7X_LANES, (p + 1) * V7X_LANES)
            state = st_scr[b, p]
            crosses.append(jnp.dot(qd[:, ps], state.astype(BF16), preferred_element_type=F32))
            kv = jnp.dot(kdt[ps, :], vb[:, ps], preferred_element_type=F32)
            st_scr[b, p] = state * cdec_ref[:, ps] + kv * ms_ref[...]
        outs.append(inner + jnp.concatenate(crosses, axis=1))

    o = jnp.concatenate(outs, axis=0)
    dlt = o - group_mean(o)
    on = dlt * lax.rsqrt(group_mean(dlt * dlt) + EPS) * nw_ref[...]
    gate = u_ref[:, 3 * width:4 * width].astype(F32)
    y_ref[...] = (jax.nn.silu(gate) * on).reshape(nb, t_len, width).astype(y_ref.dtype)


def _proj_retention(x3, n1, w_bf, d_s5, d_ret, norm_w, moe_w):
    nb, seq, d = x3.shape
    dh = d_ret // RET_HEADS
    half = dh // 2
    t_len = T_RET
    lg = jnp.log(1.0 - 2.0 ** (-5.0 - jnp.arange(RET_HEADS, dtype=F32)))
    t = jnp.arange(t_len, dtype=F32)
    diff = t[:, None] - t[None, :]
    dec = jnp.where(diff >= 0, jnp.exp(lg[:, None, None] * jnp.maximum(diff, 0.0)), 0.0)
    dec_all = dec.transpose(1, 0, 2).reshape(t_len, RET_HEADS * t_len)
    qdec = jnp.repeat(jnp.exp(lg[:, None] * (t + 1.0)[None, :]).T, dh, axis=1)
    kdect = jnp.repeat(jnp.exp(lg[:, None] * (t_len - 1 - t)[None, :]), dh, axis=0)
    cdec = jnp.repeat(jnp.exp(lg * t_len), dh)[None, :]
    head_of_lane = jnp.arange(d_ret) // dh
    pair_head = jnp.arange(V7X_LANES) // dh
    mask_s = (pair_head[:, None] == pair_head[None, :]).astype(F32)
    avg = (head_of_lane[:, None] == head_of_lane[None, :]).astype(F32) / dh
    assert dh & (dh - 1) == 0
    avg = avg.astype(BF16)
    inv = ROPE_BASE ** (-jnp.arange(half, dtype=F32) / half)
    ang = jnp.arange(seq, dtype=F32)[:, None] * inv[None, :]
    reps = V7X_LANES // dh
    cos_t = jnp.tile(jnp.cos(ang), (1, 2 * reps))
    sin_t = jnp.tile(jnp.concatenate([-jnp.sin(ang), jnp.sin(ang)], axis=1), (1, reps))

    def whole(a):
        return pl.BlockSpec(a.shape, lambda c: (0,) * a.ndim)

    nc = seq // t_len

    def proj_chunk(c):
        return jnp.minimum(c, nc - 1)

    def ret_chunk(c):
        return jnp.maximum(c - 1, 0)

    def sliced(w):
        e, rows, cols = w.shape
        if nc >= e:
            parts = nc // e
            assert nc % e == 0 and rows % (parts * V7X_SUBLANES) == 0
            return w.reshape(e * parts, rows // parts, cols), 1
        assert e % nc == 0
        return w, e // nc

    moe_in, moe_specs, moe_shapes = [], [], []
    for w in moe_w:
        ws, per_step = sliced(w)
        blk = (per_step,) + ws.shape[1:]
        moe_in.append(ws)
        moe_specs.append(pl.BlockSpec(blk, lambda c: (proj_chunk(c), 0, 0)))
        moe_shapes.append(jax.ShapeDtypeStruct(ws.shape, BF16))

    body = functools.partial(_proj_ret_body, t_len=t_len, dh=dh, scale=dh ** -0.5)
    outs = pl.pallas_call(
        body,
        grid=(nc + 1,),
        in_specs=[pl.BlockSpec((nb, t_len, d), lambda c: (0, proj_chunk(c), 0)),
                  whole(n1), whole(w_bf),
                  pl.BlockSpec((t_len, V7X_LANES), lambda c: (ret_chunk(c), 0)),
                  pl.BlockSpec((t_len, V7X_LANES), lambda c: (ret_chunk(c), 0)),
                  whole(dec_all), whole(qdec), whole(kdect), whole(cdec),
                  whole(mask_s), whole(avg), whole(norm_w)] + moe_specs,
        out_specs=[pl.BlockSpec((nb, t_len, d_s5), lambda c: (0, proj_chunk(c), 0)),
                   pl.BlockSpec((nb, t_len, d_ret), lambda c: (0, ret_chunk(c), 0))] + moe_specs,
        out_shape=[jax.ShapeDtypeStruct((nb, seq, d_s5), BF16),
                   jax.ShapeDtypeStruct((nb, seq, d_ret), BF16)] + moe_shapes,
        scratch_shapes=[pltpu.VMEM((nb * t_len, 4 * d_ret), BF16),
                        pltpu.VMEM((nb * t_len, 4 * d_ret), BF16),
                        pltpu.VMEM((nb, d_ret // V7X_LANES, V7X_LANES, V7X_LANES), F32)],
        compiler_params=_cparams(("arbitrary",)),
        name="inproj_retention",
    )(x3, n1, w_bf, cos_t, sin_t, dec_all, qdec, kdect, cdec, mask_s, avg, norm_w, *moe_in)
    u_s5, y_ret = outs[:2]
    moe_bf = [o.reshape(w.shape) for o, w in zip(outs[2:], moe_w)]
    return u_s5, y_ret, moe_bf


def _route_body(ys5_ref, yret_ref, x_ref, wo_ref, n2_ref, wr_ref, br_ref, tri_ref, ltri_ref,
                h_ref, route_ref, route_t_ref, cnt_ref, runs_ref, carry_scr):
    @pl.when(pl.program_id(0) == 0)
    def _():
        carry_scr[...] = jnp.zeros_like(carry_scr)

    d_s5 = ys5_ref.shape[1]
    h = (x_ref[...]
         + jnp.dot(ys5_ref[...], wo_ref[0:d_s5], preferred_element_type=F32)
         + jnp.dot(yret_ref[...], wo_ref[d_s5:], preferred_element_type=F32))
    h_ref[...] = h
    hn = _rms(h, n2_ref[...])
    hi = hn.astype(BF16)
    lo = (hn - hi.astype(F32)).astype(BF16)
    p_hi = jnp.dot(hi, wr_ref[...], preferred_element_type=F32)
    p_lo = jnp.dot(lo, wr_ref[:, 0:ROUTE_LANES], preferred_element_type=F32)
    logits = p_hi[:, 0:ROUTE_LANES] + p_hi[:, ROUTE_LANES:] + p_lo + br_ref[...]
    tm = logits.shape[0]
    lt = logits.T
    epg = EXPERTS_PER_GROUP
    row = lax.broadcasted_iota(jnp.int32, (epg, tm), 0)
    rowf = row.astype(F32)
    neg = -jnp.inf
    big = float(epg)
    gl = jnp.where(row < N_GROUPS, lt[0:epg], neg)
    gmax = jnp.max(gl, axis=0, keepdims=True)
    gidx = jnp.min(jnp.where(gl == gmax, rowf, big), axis=0, keepdims=True)
    g_w = 1.0 / jnp.sum(jnp.exp(gl - gmax), axis=0, keepdims=True)
    el = jnp.full((epg, tm), neg, F32)
    for g in range(N_GROUPS):
        el = jnp.where(gidx == g, lt[ROUTE_EXPERT_ROW + g * epg:ROUTE_EXPERT_ROW + (g + 1) * epg], el)
    v0 = jnp.max(el, axis=0, keepdims=True)
    i0 = jnp.min(jnp.where(el == v0, rowf, big), axis=0, keepdims=True)
    el2 = jnp.where(rowf == i0, neg, el)
    v1 = jnp.max(el2, axis=0, keepdims=True)
    i1 = jnp.min(jnp.where(el2 == v1, rowf, big), axis=0, keepdims=True)
    e = jnp.exp(v1 - v0)
    den = 1.0 + e
    w0 = (1.0 / den) * g_w
    w1 = (e / den) * g_w
    eid0 = gidx * epg + i0
    eid1 = gidx * epg + i1
    n_exp = N_GROUPS * epg
    erow = lax.broadcasted_iota(jnp.int32, (n_exp, tm), 0).astype(F32)
    sel0 = erow == eid0
    sel1 = erow == eid1
    onehot = jnp.where(sel0 | sel1, 1.0, 0.0)
    local = jnp.dot(onehot.astype(BF16), tri_ref[...], preferred_element_type=F32)
    seen = carry_scr[...]
    before = local + seen
    r0 = jnp.sum(jnp.where(sel0, before, 0.0), axis=0, keepdims=True)
    r1 = jnp.sum(jnp.where(sel1, before, 0.0), axis=0, keepdims=True)
    tile_cnt = jnp.sum(onehot, axis=1, keepdims=True)
    carry_scr[...] = seen + tile_cnt
    cnt_ref[...] = jnp.broadcast_to(seen + tile_cnt, cnt_ref.shape)
    run_off = jnp.dot(ltri_ref[...], jnp.broadcast_to(tile_cnt, (n_exp, ROUTE_LANES)),
                      precision=lax.Precision.HIGHEST, preferred_element_type=F32)[:, 0:1]
    lp0 = jnp.sum(jnp.where(sel0, local + run_off, 0.0), axis=0, keepdims=True)
    lp1 = jnp.sum(jnp.where(sel1, local + run_off, 0.0), axis=0, keepdims=True)
    lane = lax.broadcasted_iota(jnp.int32, (n_exp, ROUTE_LANES), 1)
    runs_ref[...] = jnp.where(lane == 0, tile_cnt, jnp.where(lane == 1, run_off,
                                                             jnp.where(lane == 2, seen, 0.0)))
    rec_t = jnp.zeros((ROUTE_FIELDS, tm), F32)
    for j, val in enumerate((eid0, eid1, w0, w1, r0, r1, lp0, lp1)):
        rec_t = jnp.where(row == j, val, rec_t)
    route_t_ref[...] = rec_t
    route_ref[...] = jnp.concatenate(
        [rec_t, jnp.zeros((ROUTE_LANES - ROUTE_FIELDS, tm), F32)], axis=0).T


def _route(ys5, yret, x2, wo_bf, n2, wr, br):
    n, d = x2.shape
    d_s5 = ys5.shape[1]
    d_ret = yret.shape[1]
    tm = TM_ROUTE
    n_exp = N_GROUPS * EXPERTS_PER_GROUP
    tri = (jnp.arange(tm)[:, None] < jnp.arange(tm)[None, :]).astype(BF16)
    ltri = (jnp.arange(n_exp)[:, None] > jnp.arange(n_exp)[None, :]).astype(F32)
    return pl.pallas_call(
        _route_body,
        grid=(n // tm,),
        in_specs=[pl.BlockSpec((tm, d_s5), lambda i: (i, 0)),
                  pl.BlockSpec((tm, d_ret), lambda i: (i, 0)),
                  pl.BlockSpec((tm, d), lambda i: (i, 0)),
                  pl.BlockSpec((d_s5 + d_ret, d), lambda i: (0, 0)),
                  pl.BlockSpec((1, d), lambda i: (0, 0)),
                  pl.BlockSpec((d, 2 * ROUTE_LANES), lambda i: (0, 0)),
                  pl.BlockSpec((1, ROUTE_LANES), lambda i: (0, 0)),
                  pl.BlockSpec((tm, tm), lambda i: (0, 0)),
                  pl.BlockSpec((n_exp, n_exp), lambda i: (0, 0))],
        out_specs=[pl.BlockSpec((tm, d), lambda i: (i, 0)),
                   pl.BlockSpec((tm, ROUTE_LANES), lambda i: (i, 0)),
                   pl.BlockSpec((ROUTE_FIELDS, tm), lambda i: (0, i)),
                   pl.BlockSpec((n_exp, ROUTE_LANES), lambda i: (0, 0)),
                   pl.BlockSpec((n_exp, ROUTE_LANES), lambda i: (i, 0))],
        out_shape=[jax.ShapeDtypeStruct((n, d), F32),
                   jax.ShapeDtypeStruct((n, ROUTE_LANES), F32),
                   jax.ShapeDtypeStruct((ROUTE_FIELDS, n), F32),
                   jax.ShapeDtypeStruct((n_exp, ROUTE_LANES), F32),
                   jax.ShapeDtypeStruct((n // tm * n_exp, ROUTE_LANES), F32)],
        scratch_shapes=[pltpu.VMEM((n_exp, 1), F32)],
        compiler_params=_cparams(("arbitrary",)),
        name="outproj_route",
    )(ys5, yret, x2, wo_bf, n2, wr, br, tri, ltri)


def _dispatch_body(pends_ref, padded_ref, runs_ref, h_ref, n2_ref, lp_ref, xs_ref,
                   srt_scr, zero_scr, sem, zsem):
    tm = h_ref.shape[0]
    n_exp = pends_ref.shape[0]

    @pl.when(pl.program_id(0) == 0)
    def _():
        zero_scr[...] = jnp.zeros_like(zero_scr)
        srt_scr[:, pl.ds(TOP_K * tm * ROW_TILES, RUN_CHUNK * ROW_TILES), :] = jnp.zeros(
            (2, RUN_CHUNK * ROW_TILES, V7X_LANES), U32)

        def zero_copy(e, back):
            first = pl.multiple_of((pends_ref[e] - back * R_BLK) * ROW_TILES, R_BLK * ROW_TILES)
            return pltpu.make_async_copy(zero_scr, xs_ref.at[pl.ds(first, R_BLK * ROW_TILES)], zsem)

        def zstart(e, carry):
            for back in (1, 2):
                @pl.when(padded_ref[e] >= back * R_BLK)
                def _():
                    zero_copy(e, back).start()
            return carry

        def zwait(e, carry):
            for back in (1, 2):
                @pl.when(padded_ref[e] >= back * R_BLK)
                def _():
                    zero_copy(e, back).wait()
            return carry

        lax.fori_loop(0, pends_ref.shape[0], zstart, 0)
        lax.fori_loop(0, pends_ref.shape[0], zwait, 0)

    step = pl.program_id(0)
    slot = step % 2
    buf = srt_scr.at[slot]
    hnb = _rms(h_ref[...], n2_ref[...]).astype(BF16)
    pos = lax.broadcasted_iota(jnp.int32, (TOP_K * tm, tm), 0).astype(F32)
    perm = jnp.where((pos == lp_ref[6:7, :]) | (pos == lp_ref[7:8, :]), 1.0, 0.0).astype(BF16)
    _rows_to_tiles(buf, _pack_rows(jnp.dot(perm, hnb, preferred_element_type=F32)))

    def copy(src_row, dst_row, rows, priority):
        src = buf.at[pl.ds(pl.multiple_of(src_row * ROW_TILES, ROW_TILES), rows * ROW_TILES)]
        dst = xs_ref.at[pl.ds(pl.multiple_of(dst_row * ROW_TILES, ROW_TILES), rows * ROW_TILES)]
        pltpu.make_async_copy(src, dst, sem.at[slot]).start(priority=priority)

    def drain(s, chunks):
        def one(c, cc):
            pltpu.make_async_copy(srt_scr.at[s, pl.ds(0, RUN_CHUNK * ROW_TILES)],
                                  xs_ref.at[pl.ds(0, RUN_CHUNK * ROW_TILES)], sem.at[s]).wait()
            return cc

        lax.fori_loop(0, chunks, one, 0)

    @pl.when(step > 0)
    def _():
        drain(1 - slot, runs_ref[3 * n_exp + 1])

    def run(e, carry):
        cnt, off, dst = runs_ref[e], runs_ref[n_exp + e], runs_ref[2 * n_exp + e]

        def chunk(c, cc):
            copy(off + c * RUN_CHUNK, dst + c * RUN_CHUNK, RUN_CHUNK, 0)
            return cc

        lax.fori_loop(0, lax.shift_right_logical(cnt + (RUN_CHUNK - 1), RUN_CHUNK.bit_length() - 1),
                      chunk, 0)
        return carry

    lax.fori_loop(0, n_exp, run, 0)

    @pl.when(step == pl.num_programs(0) - 1)
    def _():
        drain(slot, runs_ref[3 * n_exp])


def _dispatch(pends, padded, runs, h, n2, route_t, p_rows):
    n, d = h.shape
    tm = TM_ROUTE
    assert 3 * pends.shape[0] + 2 <= RUN_TABLE
    grid_spec = pltpu.PrefetchScalarGridSpec(
        num_scalar_prefetch=2,
        grid=(n // tm,),
        in_specs=[pl.BlockSpec((RUN_TABLE,), lambda i, pe, pa: (i,), memory_space=pltpu.SMEM),
                  pl.BlockSpec((tm, d), lambda i, pe, pa: (i, 0)),
                  pl.BlockSpec((1, d), lambda i, pe, pa: (0, 0)),
                  pl.BlockSpec((ROUTE_FIELDS, tm), lambda i, pe, pa: (0, i))],
        out_specs=pl.BlockSpec(memory_space=pl.ANY),
        scratch_shapes=[pltpu.VMEM((2, (TOP_K * tm + RUN_CHUNK) * ROW_TILES, V7X_LANES), U32),
                        pltpu.VMEM((R_BLK * ROW_TILES, V7X_LANES), U32),
                        pltpu.SemaphoreType.DMA((2,)), pltpu.SemaphoreType.DMA(())])
    assert d == 2 * ROW_TILES * V7X_LANES
    return pl.pallas_call(
        _dispatch_body,
        grid_spec=grid_spec,
        out_shape=jax.ShapeDtypeStruct((p_rows * ROW_TILES, V7X_LANES), U32),
        compiler_params=_cparams(("arbitrary",)),
        name="dispatch",
    )(pends, padded, runs, h, n2, route_t)


def _expert_body(be_ref, nu_ref, xs_ref, wg_ref, wu_ref, wd_ref, ys_ref):
    j = pl.program_id(0)

    @pl.when(j < nu_ref[0])
    def _():
        x = _unpack_rows(_tiles_to_rows(xs_ref, R_BLK)).astype(BF16)
        gate = jnp.dot(x, wg_ref[...], preferred_element_type=F32)
        up = jnp.dot(x, wu_ref[...], preferred_element_type=F32)
        hid = (jax.nn.silu(gate) * up).astype(BF16)
        _rows_to_tiles(ys_ref, _pack_rows(jnp.dot(hid, wd_ref[...], preferred_element_type=F32)))

    @pl.when(j >= nu_ref[0])
    def _():
        ys_ref[...] = jnp.zeros_like(ys_ref)


def _experts(block_e, n_used, xs, w_gate, w_up, w_down):
    d, d_e = w_gate.shape[1:]
    blk_rows = R_BLK * ROW_TILES
    nblk = xs.shape[0] // blk_rows

    def row_map(j, be, nu):
        return (jnp.maximum(jnp.minimum(j, nu[0] - 1), 0), 0)

    def w_map(j, be, nu):
        return (be[j], 0, 0)

    grid_spec = pltpu.PrefetchScalarGridSpec(
        num_scalar_prefetch=2,
        grid=(nblk,),
        in_specs=[pl.BlockSpec((blk_rows, V7X_LANES), row_map),
                  pl.BlockSpec((None, d, d_e), w_map),
                  pl.BlockSpec((None, d, d_e), w_map),
                  pl.BlockSpec((None, d_e, d), w_map)],
        out_specs=pl.BlockSpec((blk_rows, V7X_LANES), lambda j, be, nu: (j, 0)))
    return pl.pallas_call(
        _expert_body,
        grid_spec=grid_spec,
        out_shape=jax.ShapeDtypeStruct(xs.shape, xs.dtype),
        compiler_params=_cparams(("arbitrary",)),
        name="experts",
    )(block_e, n_used, xs, w_gate, w_up, w_down)


def _combine_body(dest0_ref, dest1_ref, dest2_ref, h_ref, route_ref, fw_ref, ys_ref, out_ref,
                  *scratch, last):
    bufs, sem = scratch[:GATHER_RING], scratch[GATHER_RING]
    tm = h_ref.shape[0]
    step = pl.program_id(0)

    def issue(d_ref, s):
        for r in range(tm):
            for k in range(TOP_K):
                pltpu.make_async_copy(_tile_rows(ys_ref, d_ref[k * tm + r]),
                                      _tile_rows(bufs[s].at[k], r), sem.at[s]).start(priority=k)

    def drain(s):
        for k in range(TOP_K):
            pltpu.make_async_copy(ys_ref.at[pl.ds(0, tm * ROW_TILES)], bufs[s].at[k], sem.at[s]).wait()

    @pl.when(step == 0)
    def _():
        issue(dest0_ref, 0)
        issue(dest1_ref, 1)

    for s in range(GATHER_RING):
        @pl.when(step % GATHER_RING == s)
        def _():
            drain(s)
            issue(dest2_ref, (s + 2) % GATHER_RING)
            rec = route_ref[...]
            hh = (h_ref[...] + rec[:, 2:3] * _unpack_rows(_tiles_to_rows(bufs[s].at[0], tm))
                  + rec[:, 3:4] * _unpack_rows(_tiles_to_rows(bufs[s].at[1], tm)))
            out_ref[...] = _rms(hh, fw_ref[...])

    @pl.when(step == last)
    def _():
        drain((last + 1) % GATHER_RING)
        drain((last + 2) % GATHER_RING)


def _combine(dest, h, route, fw, ys):
    n, d = h.shape
    tm = TM_ROWS
    steps = n // tm
    return pl.pallas_call(
        functools.partial(_combine_body, last=steps - 1),
        grid=(steps,),
        in_specs=[pl.BlockSpec((TOP_K * tm,), lambda i: (i,), memory_space=pltpu.SMEM),
                  pl.BlockSpec((TOP_K * tm,), lambda i: (jnp.minimum(i + 1, steps - 1),),
                               memory_space=pltpu.SMEM),
                  pl.BlockSpec((TOP_K * tm,), lambda i: (jnp.minimum(i + 2, steps - 1),),
                               memory_space=pltpu.SMEM),
                  pl.BlockSpec((tm, d), lambda i: (i, 0)),
                  pl.BlockSpec((tm, ROUTE_LANES), lambda i: (i, 0)),
                  pl.BlockSpec((1, d), lambda i: (0, 0)),
                  pl.BlockSpec(memory_space=pl.ANY)],
        out_specs=pl.BlockSpec((tm, d), lambda i: (i, 0)),
        out_shape=jax.ShapeDtypeStruct((n, d), F32),
        scratch_shapes=([pltpu.VMEM((TOP_K, tm * ROW_TILES, V7X_LANES), U32)] * GATHER_RING
                        + [pltpu.SemaphoreType.DMA((GATHER_RING,))]),
        compiler_params=_cparams(("arbitrary",)),
        name="combine",
    )(dest, dest, dest, h, route, fw, ys)


def _run_table(runs, pstart, n_experts):
    r = runs.reshape(-1, n_experts, ROUTE_LANES)[:, :, 0:3].astype(jnp.int32)
    first_xs = pstart[None, :] + r[:, :, 2]
    chunks = jnp.sum((r[:, :, 0] + RUN_CHUNK - 1) // RUN_CHUNK, axis=1, keepdims=True)
    prev = jnp.concatenate([jnp.zeros_like(chunks[:1]), chunks[:-1]], axis=0)
    tab = jnp.concatenate([r[:, :, 0], r[:, :, 1], first_xs, chunks, prev,
                           jnp.zeros((r.shape[0], RUN_TABLE - 3 * n_experts - 2), jnp.int32)], axis=1)
    return tab.reshape(-1)


def _plan(route_t, counts, n_experts, n_blocks):
    eid = route_t[0:TOP_K].astype(jnp.int32).reshape(-1, V7X_LANES)
    rank = route_t[4:4 + TOP_K].astype(jnp.int32).reshape(-1, V7X_LANES)
    cnt = counts[:, 0].astype(jnp.int32)
    padded = jnp.where(cnt > 0, (cnt + RUN_CHUNK - 1 + R_BLK - 1) // R_BLK * R_BLK, 0)
    pends = jnp.cumsum(padded)
    pstart = pends - padded
    dest = rank
    for e in range(n_experts):
        dest = dest + jnp.where(eid == e, pstart[e], 0)
    n_used = pends[-1] // R_BLK
    blk = jnp.minimum(jnp.arange(n_blocks, dtype=jnp.int32), n_used - 1)
    block_e = jnp.minimum(jnp.sum(pends[None, :] <= (blk * R_BLK)[:, None], axis=1), n_experts - 1)
    return (dest.astype(jnp.int32), block_e.astype(jnp.int32), n_used.reshape(1).astype(jnp.int32),
            pends.astype(jnp.int32), padded.astype(jnp.int32), pstart.astype(jnp.int32))


def _layer(h3, norm1_w, w_in, s5_a_re, s5_a_im, s5_b_re, s5_b_im, s5_c_re, s5_c_im, s5_d,
           s5_log_dt, s5_w_glu, s5_b_glu, ret_norm_w, w_out, norm2_w, router_group_w,
           router_group_b, router_expert_w, router_expert_b, moe_w_gate, moe_w_up, moe_w_down,
           out_norm_w):
    nb, seq, d = h3.shape
    n = nb * seq
    d_s5 = s5_d.shape[0]
    d_ret = ret_norm_w.shape[0]
    n_experts = moe_w_gate.shape[0]
    x2 = h3.reshape(n, d)

    u_s5, y_ret, moe_bf = _proj_retention(h3, norm1_w.reshape(1, d), w_in.astype(BF16), d_s5, d_ret,
                                          ret_norm_w.reshape(1, d_ret).astype(F32),
                                          (moe_w_gate, moe_w_up, moe_w_down))

    bp, cp, tp, a_tab = _s5_tables(s5_a_re, s5_a_im, s5_b_re, s5_b_im, s5_c_re, s5_c_im,
                                   s5_log_dt, nb, S5_TAU)
    y_s5 = _s5(u_s5.reshape(n, d_s5), nb, bp, cp, tp, a_tab, s5_d.reshape(1, d_s5).astype(F32),
               s5_w_glu.astype(BF16), s5_b_glu.reshape(1, d_s5).astype(F32))

    assert n_experts == N_GROUPS * EXPERTS_PER_GROUP
    wr = (jnp.zeros((d, ROUTE_LANES), F32).at[:, :N_GROUPS].set(router_group_w.astype(F32))
          .at[:, ROUTE_EXPERT_ROW:ROUTE_EXPERT_ROW + n_experts].set(router_expert_w.astype(F32)))
    br = (jnp.zeros((1, ROUTE_LANES), F32).at[0, :N_GROUPS].set(router_group_b.astype(F32))
          .at[0, ROUTE_EXPERT_ROW:ROUTE_EXPERT_ROW + n_experts].set(router_expert_b.astype(F32)))
    wr_hi = wr.astype(BF16)
    wr = jnp.concatenate([wr_hi, (wr - wr_hi.astype(F32)).astype(BF16)], axis=1)
    h, route, route_t, counts, runs = _route(y_s5.reshape(n, d_s5), y_ret.reshape(n, d_ret), x2,
                                             w_out.astype(BF16), norm2_w.reshape(1, d), wr, br)

    n_blocks = (n * TOP_K) // R_BLK + 2 * n_experts
    dest, block_e, n_used, pends, padded, pstart = _plan(route_t, counts, n_experts, n_blocks)
    dest = dest.reshape(TOP_K, n // TM_ROWS, TM_ROWS).transpose(1, 0, 2).reshape(-1)
    xs = _dispatch(pends, padded, _run_table(runs, pstart, n_experts), h, norm2_w.reshape(1, d),
                   route_t, n_blocks * R_BLK)
    ys = _experts(block_e, n_used, xs, *moe_bf)
    out = _combine(dest, h, route, out_norm_w.reshape(1, d), ys)
    return out.reshape(nb, seq, d)


def kernel(x, norm1_w, w_in, s5_a_re, s5_a_im, s5_b_re, s5_b_im, s5_c_re, s5_c_im, s5_d, s5_log_dt, s5_w_glu, s5_b_glu, ret_norm_w, w_out, norm2_w, router_group_w, router_group_b, router_expert_w, router_expert_b, moe_w_gate, moe_w_up, moe_w_down, final_norm_w):
    depth = norm1_w.shape[0]
    assert depth == 1, "the fused final norm assumes a single layer"
    l = 0
    return _layer(x, norm1_w[l], w_in[l], s5_a_re[l], s5_a_im[l], s5_b_re[l], s5_b_im[l],
                  s5_c_re[l], s5_c_im[l], s5_d[l], s5_log_dt[l], s5_w_glu[l], s5_b_glu[l],
                  ret_norm_w[l], w_out[l], norm2_w[l], router_group_w[l], router_group_b[l],
                  router_expert_w[l], router_expert_b[l], moe_w_gate[l], moe_w_up[l],
                  moe_w_down[l], final_norm_w)
```

```python
import functools

import jax
import jax.numpy as jnp
from jax import lax
from jax.experimental import pallas as pl
from jax.experimental.pallas import tpu as pltpu

F32 = jnp.float32
BF16 = jnp.bfloat16

EPS = 1e-6
ROPE_BASE = 10000.0
RET_HEADS = 8
TOP_K = 2
N_GROUPS = 4
EXPERTS_PER_GROUP = 8

V7X_LANES = 128
V7X_SUBLANES = 8
V7X_VMEM_LIMIT = 56 * 1024 * 1024

T_S5 = 512
S5_TAU = 8
S5_PAD = 8
T_RET = 128
TM_ROUTE = 512
TM_ROWS = 256
R_BLK = 512
GATHER_RING = 3
ROUTE_LANES = 128
ROUTE_FIELDS = 8
ROUTE_EXPERT_ROW = 8


def _rms(x, w):
    return x * lax.rsqrt(jnp.mean(x * x, axis=-1, keepdims=True) + EPS) * w


def _cparams(sem):
    return pltpu.CompilerParams(dimension_semantics=sem, vmem_limit_bytes=V7X_VMEM_LIMIT)


ROW_TILES = V7X_SUBLANES // 2
U32 = jnp.uint32


def _pack_rows(x):
    half = x.shape[1] // 2

    def bf16_bits(v):
        return lax.bitcast_convert_type(v.astype(BF16).astype(F32), U32)

    return bf16_bits(x[:, half:]) | (bf16_bits(x[:, :half]) >> 16)


def _unpack_rows(w):
    lo = lax.bitcast_convert_type(w << 16, F32)
    hi = lax.bitcast_convert_type(w & U32(0xFFFF0000), F32)
    return jnp.concatenate([lo, hi], axis=1)


def _rows_to_tiles(ref, val):
    rows = val.shape[0]
    for s in range(ROW_TILES):
        ref[pl.ds(s, rows, stride=ROW_TILES), :] = val[:, s * V7X_LANES:(s + 1) * V7X_LANES]


def _tiles_to_rows(ref, rows):
    return jnp.concatenate(
        [ref[pl.ds(s, rows, stride=ROW_TILES), :] for s in range(ROW_TILES)], axis=1)


def _tile_rows(ref, row):
    return ref.at[pl.ds(pl.multiple_of(row * ROW_TILES, ROW_TILES), ROW_TILES)]


def _s5_body(u_ref, bp_ref, cp_ref, tp_ref, a_ref, dd_ref, wglu_ref, bglu_ref, y_ref,
             v_scr, sp_scr, st_scr, io_scr, *, nb, nk, seg, tau):
    @pl.when(pl.program_id(0) == 0)
    def _():
        st_scr[...] = jnp.zeros_like(st_scr)

    nblk = bp_ref.shape[0]
    cw = bp_ref.shape[1] // tau
    sw = bp_ref.shape[2]
    d_s5 = nblk * cw
    tiles = sw // V7X_LANES
    ht = tiles // 2
    ctiles = d_s5 // V7X_LANES
    srows = nb * nk

    u_all = u_ref[...].reshape(srows * tau, d_s5).astype(F32)
    for c in range(ctiles):
        io_scr[c] = u_all[:, c * V7X_LANES:(c + 1) * V7X_LANES]
    u_steps = [jnp.concatenate([io_scr[c, pl.ds(j, srows, stride=tau), :] for c in range(ctiles)],
                               axis=1) for j in range(tau)]
    u_blocks = [jnp.concatenate([u_steps[j][:, blk * cw:(blk + 1) * cw] for j in range(tau)],
                                axis=1).astype(BF16) for blk in range(nblk)]

    for blk in range(nblk):
        half, q = divmod(blk, 2)
        v = jnp.dot(u_blocks[blk], bp_ref[blk], preferred_element_type=F32)
        for b in range(nb):
            for j in range(tiles):
                v_scr[q * tiles + j, pl.ds((half * nb + b) * seg, nk), :] = (
                    v[b * nk:(b + 1) * nk, j * V7X_LANES:(j + 1) * V7X_LANES])

    rows = 2 * nb
    ar = [[a_ref[0, q * ht + i] for i in range(ht)] for q in range(2)]
    ai = [[a_ref[1, q * ht + i] for i in range(ht)] for q in range(2)]
    sr = [[st_scr[q * tiles + i] for i in range(ht)] for q in range(2)]
    si = [[st_scr[q * tiles + ht + i] for i in range(ht)] for q in range(2)]
    for k in range(nk):
        for q in range(2):
            for i in range(ht):
                jr = q * tiles + i
                ji = q * tiles + ht + i
                sp_scr[jr, pl.ds(k, rows, stride=seg), :] = sr[q][i]
                sp_scr[ji, pl.ds(k, rows, stride=seg), :] = si[q][i]
                vr = v_scr[jr, pl.ds(k, rows, stride=seg), :]
                vi = v_scr[ji, pl.ds(k, rows, stride=seg), :]
                nr = ar[q][i] * sr[q][i] - ai[q][i] * si[q][i] + vr
                ni = ar[q][i] * si[q][i] + ai[q][i] * sr[q][i] + vi
                sr[q][i], si[q][i] = nr, ni
    for q in range(2):
        for i in range(ht):
            st_scr[q * tiles + i] = sr[q][i]
            st_scr[q * tiles + ht + i] = si[q][i]

    yb = []
    for blk in range(nblk):
        half, q = divmod(blk, 2)
        sp = jnp.concatenate(
            [jnp.concatenate([sp_scr[q * tiles + j, pl.ds((half * nb + b) * seg, nk), :]
                              for j in range(tiles)], axis=1) for b in range(nb)],
            axis=0).astype(BF16)
        yb.append(jnp.dot(sp, cp_ref[blk], preferred_element_type=F32)
                  + jnp.dot(u_blocks[blk], tp_ref[blk], preferred_element_type=F32))
    for j in range(tau):
        y = jnp.concatenate([yb[blk][:, j * cw:(j + 1) * cw] for blk in range(nblk)], axis=1)
        y = y + dd_ref[...] * u_steps[j]
        y = jax.nn.gelu(y)
        z = jnp.dot(y.astype(BF16), wglu_ref[...], preferred_element_type=F32) + bglu_ref[...]
        out = y * jax.nn.sigmoid(z)
        for c in range(ctiles):
            io_scr[c, pl.ds(j, srows, stride=tau), :] = out[:, c * V7X_LANES:(c + 1) * V7X_LANES]
    y_all = jnp.concatenate([io_scr[c] for c in range(ctiles)], axis=1)
    y_ref[...] = y_all.reshape(nb, nk * tau, d_s5).astype(y_ref.dtype)


def _s5(u_s5, nb, bp, cp, tp, a_tab, dd, wglu_bf, bglu):
    n, d_s5 = u_s5.shape
    seq = n // nb
    tau = S5_TAU
    nk = T_S5 // tau
    seg = nk + S5_PAD
    nblk, _, sw = bp.shape
    rows = 2 * nb
    assert rows == V7X_SUBLANES and nblk == 4

    def whole(a):
        return pl.BlockSpec(a.shape, lambda c: (0,) * a.ndim)

    body = functools.partial(_s5_body, nb=nb, nk=nk, seg=seg, tau=tau)
    out = pl.pallas_call(
        body,
        grid=(seq // T_S5,),
        in_specs=[pl.BlockSpec((nb, T_S5, d_s5), lambda c: (0, c, 0)),
                  whole(bp), whole(cp), whole(tp), whole(a_tab), whole(dd), whole(wglu_bf),
                  whole(bglu)],
        out_specs=pl.BlockSpec((nb, T_S5, d_s5), lambda c: (0, c, 0)),
        out_shape=jax.ShapeDtypeStruct((nb, seq, d_s5), BF16),
        scratch_shapes=[pltpu.VMEM((2 * sw // V7X_LANES, rows * seg, V7X_LANES), F32),
                        pltpu.VMEM((2 * sw // V7X_LANES, rows * seg, V7X_LANES), F32),
                        pltpu.VMEM((2 * sw // V7X_LANES, rows, V7X_LANES), F32),
                        pltpu.VMEM((d_s5 // V7X_LANES, nb * T_S5, V7X_LANES), F32)],
        compiler_params=_cparams(("arbitrary",)),
        name="s5",
    )(u_s5.reshape(nb, seq, d_s5), bp, cp, tp, a_tab, dd, wglu_bf, bglu)
    return out.reshape(n, d_s5)


def _s5_tables(a_re, a_im, b_re, b_im, c_re, c_im, log_dt, nb, tau):
    hp = lax.Precision.HIGHEST
    g, p = a_re.shape
    hch = b_re.shape[2]
    gpb = V7X_LANES // hch
    nblk = g // gpb
    lam_r, lam_i = a_re.astype(F32), a_im.astype(F32)
    dt = jnp.exp(log_dt.astype(F32))[:, None]
    mag = jnp.exp(lam_r * dt)
    ab_r = mag * jnp.cos(lam_i * dt)
    ab_i = mag * jnp.sin(lam_i * dt)
    den = lam_r * lam_r + lam_i * lam_i
    zr = ((ab_r - 1.0) * lam_r + ab_i * lam_i) / den
    zi = (ab_i * lam_r - (ab_r - 1.0) * lam_i) / den
    br_, bi_ = b_re.astype(F32), b_im.astype(F32)
    bb_r = zr[..., None] * br_ - zi[..., None] * bi_
    bb_i = zr[..., None] * bi_ + zi[..., None] * br_
    cr, ci = c_re.astype(F32), c_im.astype(F32)
    pr, pi = [jnp.ones_like(ab_r)], [jnp.zeros_like(ab_i)]
    for _ in range(tau):
        pr, pi = pr + [pr[-1] * ab_r - pi[-1] * ab_i], pi + [pr[-1] * ab_i + pi[-1] * ab_r]
    pw_r, pw_i = jnp.stack(pr), jnp.stack(pi)

    def blockdiag(x):
        r, c = x.shape[-2:]
        x = jnp.tile(x.reshape(tau, nblk, gpb * r, c), (1, 1, 1, gpb))
        same = (jnp.arange(gpb * r)[:, None] // r) == (jnp.arange(gpb * c)[None, :] // c)
        return jnp.where(same, x, 0.0)

    wr_ = jnp.stack([pr[tau - 1 - j] for j in range(tau)])[:, :, None, :]
    wi_ = jnp.stack([pi[tau - 1 - j] for j in range(tau)])[:, :, None, :]
    bt_r, bt_i = jnp.swapaxes(bb_r, 1, 2)[None], jnp.swapaxes(bb_i, 1, 2)[None]
    bp_r = blockdiag(wr_ * bt_r - wi_ * bt_i)
    bp_i = blockdiag(wr_ * bt_i + wi_ * bt_r)
    bp = jnp.concatenate([jnp.concatenate([bp_r[j], bp_i[j]], axis=-1) for j in range(tau)],
                         axis=1).astype(BF16)

    qr_, qi_ = pw_r[1:, :, :, None], pw_i[1:, :, :, None]
    ct_r, ct_i = jnp.swapaxes(cr, 1, 2)[None], jnp.swapaxes(ci, 1, 2)[None]
    cp_r = blockdiag(ct_r * qr_ - ct_i * qi_)
    cp_i = blockdiag(ct_r * qi_ + ct_i * qr_)
    cp = jnp.concatenate([jnp.concatenate([cp_r[i], -cp_i[i]], axis=1) for i in range(tau)],
                         axis=2).astype(BF16)

    ab_r_ = pw_r[:tau, :, :, None] * bb_r[None] - pw_i[:tau, :, :, None] * bb_i[None]
    ab_i_ = pw_r[:tau, :, :, None] * bb_i[None] + pw_i[:tau, :, :, None] * bb_r[None]
    kd = blockdiag(jnp.einsum('ghp,dgpe->dgeh', cr, ab_r_, precision=hp)
                   - jnp.einsum('ghp,dgpe->dgeh', ci, ab_i_, precision=hp))
    kzero = jnp.zeros_like(kd[0])
    tp = jnp.concatenate(
        [jnp.concatenate([kd[i - j] if i >= j else kzero for i in range(tau)], axis=2)
         for j in range(tau)], axis=1).astype(BF16)

    a_tab = jnp.stack([jnp.repeat(pw_r[tau].reshape(2, -1), nb, axis=0),
                       jnp.repeat(pw_i[tau].reshape(2, -1), nb, axis=0)])
    a_tab = a_tab.reshape(2, 2 * nb, -1, V7X_LANES).transpose(0, 2, 1, 3)
    return bp, cp, tp, a_tab


def _proj_ret_body(x_ref, n1_ref, w_ref, cos_ref, sin_ref, dec_ref, qdec_ref, kdect_ref,
                   cdec_ref, ms_ref, avg_ref, nw_ref, wg_ref, wu_ref, wd_ref,
                   us5_ref, y_ref, wg_out, wu_out, wd_out, u_even, u_odd, st_scr,
                   *, t_len, dh, scale):
    wg_out[...] = wg_ref[...].astype(wg_out.dtype)
    wu_out[...] = wu_ref[...].astype(wu_out.dtype)
    wd_out[...] = wd_ref[...].astype(wd_out.dtype)

    step = pl.program_id(0)
    nb = x_ref.shape[0]
    d_s5 = us5_ref.shape[-1]

    @pl.when(step == 0)
    def _():
        u_odd[...] = jnp.zeros_like(u_odd)
        st_scr[...] = jnp.zeros_like(st_scr)

    def project(u_out):
        xn = _rms(x_ref[...].reshape(nb * t_len, x_ref.shape[-1]), n1_ref[...]).astype(BF16)
        u = jnp.dot(xn, w_ref[...], preferred_element_type=F32)
        us5_ref[...] = u[:, :d_s5].reshape(nb, t_len, d_s5).astype(us5_ref.dtype)
        u_out[...] = u[:, d_s5:].astype(u_out.dtype)

    for parity, (u_out, u_in) in enumerate(((u_even, u_odd), (u_odd, u_even))):
        @pl.when(step % 2 == parity)
        def _():
            project(u_out)
            _ret_chunk(u_in, cos_ref, sin_ref, dec_ref, qdec_ref, kdect_ref, cdec_ref, ms_ref, avg_ref,
                       nw_ref, y_ref, st_scr, t_len=t_len, dh=dh, scale=scale)


def _ret_chunk(u_ref, cos_ref, sin_ref, dec_ref, qdec_ref, kdect_ref, cdec_ref, ms_ref, avg_ref,
               nw_ref, y_ref, st_scr, *, t_len, dh, scale):
    nb = y_ref.shape[0]
    width = y_ref.shape[-1]
    pairs = width // V7X_LANES
    reps = width // cos_ref.shape[-1]
    cos = jnp.concatenate([cos_ref[...]] * reps, axis=1)
    sin = jnp.concatenate([sin_ref[...]] * reps, axis=1)
    lane = lax.broadcasted_iota(jnp.int32, (t_len, width), 1)
    first = (lane % dh) < (dh // 2)

    def rot(x):
        x = x.astype(F32)
        partner = jnp.where(first, pltpu.roll(x, width - dh // 2, 1), pltpu.roll(x, dh // 2, 1))
        return x * cos + partner * sin

    def group_mean(x):
        return jnp.dot(x.astype(BF16), avg_ref[...], preferred_element_type=F32)

    zero_k = jnp.zeros((dh, t_len), BF16)
    zero_v = jnp.zeros((t_len, V7X_LANES), BF16)
    low_head = lax.broadcasted_iota(jnp.int32, (t_len, V7X_LANES), 1) < dh

    outs = []
    for b in range(nb):
        rows = slice(b * t_len, (b + 1) * t_len)
        q = rot(u_ref[rows, 0:width])
        k = rot(u_ref[rows, width:2 * width]) * scale
        vb = u_ref[rows, 2 * width:3 * width]
        kt = k.T
        ktb = kt.astype(BF16)
        qb = q.astype(BF16)
        inners = []
        for p in range(pairs):
            ps = slice(p * V7X_LANES, (p + 1) * V7X_LANES)
            k_lo, k_hi = ktb[p * V7X_LANES:p * V7X_LANES + dh], ktb[p * V7X_LANES + dh:(p + 1) * V7X_LANES]
            kbd = jnp.concatenate([jnp.concatenate([k_lo, zero_k], axis=0),
                                   jnp.concatenate([zero_k, k_hi], axis=0)], axis=1)
            sc = (jnp.dot(qb[:, ps], kbd, preferred_element_type=F32)
                  * dec_ref[:, 2 * p * t_len:2 * (p + 1) * t_len])
            vt = vb[:, ps]
            vbd = jnp.concatenate([jnp.where(low_head, vt, zero_v),
                                   jnp.where(low_head, zero_v, vt)], axis=0)
            inners.append(jnp.dot(sc.astype(BF16), vbd, preferred_element_type=F32))
        inner = jnp.concatenate(inners, axis=1)
        qd = (q * qdec_ref[...]).astype(BF16)
        kdt = (kt * kdect_ref[...]).astype(BF16)
        crosses = []
        for p in range(pairs):
            ps = slice(p * V7X_LANES, (p + 1) * V7X_LANES)
            state = st_scr[b, p]
            crosses.append(jnp.dot(qd[:, ps], state.astype(BF16), preferred_element_type=F32))
            kv = jnp.dot(kdt[ps, :], vb[:, ps], preferred_element_type=F32)
            st_scr[b, p] = state * cdec_ref[:, ps] + kv * ms_ref[...]
        outs.append(inner + jnp.concatenate(crosses, axis=1))

    o = jnp.concatenate(outs, axis=0)
    dlt = o - group_mean(o)
    on = dlt * lax.rsqrt(group_mean(dlt * dlt) + EPS) * nw_ref[...]
    gate = u_ref[:, 3 * width:4 * width].astype(F32)
    y_ref[...] = (jax.nn.silu(gate) * on).reshape(nb, t_len, width).astype(y_ref.dtype)


def _proj_retention(x3, n1, w_bf, d_s5, d_ret, norm_w, moe_w):
    nb, seq, d = x3.shape
    dh = d_ret // RET_HEADS
    half = dh // 2
    t_len = T_RET
    assert V7X_LANES == 2 * dh
    lg = jnp.log(1.0 - 2.0 ** (-5.0 - jnp.arange(RET_HEADS, dtype=F32)))
    t = jnp.arange(t_len, dtype=F32)
    diff = t[:, None] - t[None, :]
    dec = jnp.where(diff >= 0, jnp.exp(lg[:, None, None] * jnp.maximum(diff, 0.0)), 0.0)
    dec_all = dec.transpose(1, 0, 2).reshape(t_len, RET_HEADS * t_len)
    qdec = jnp.repeat(jnp.exp(lg[:, None] * (t + 1.0)[None, :]).T, dh, axis=1)
    kdect = jnp.repeat(jnp.exp(lg[:, None] * (t_len - 1 - t)[None, :]), dh, axis=0)
    cdec = jnp.repeat(jnp.exp(lg * t_len), dh)[None, :]
    head_of_lane = jnp.arange(d_ret) // dh
    pair_head = jnp.arange(V7X_LANES) // dh
    mask_s = (pair_head[:, None] == pair_head[None, :]).astype(F32)
    avg = (head_of_lane[:, None] == head_of_lane[None, :]).astype(F32) / dh
    assert dh & (dh - 1) == 0
    avg = avg.astype(BF16)
    inv = ROPE_BASE ** (-jnp.arange(half, dtype=F32) / half)
    ang = jnp.arange(seq, dtype=F32)[:, None] * inv[None, :]
    reps = V7X_LANES // dh
    cos_t = jnp.tile(jnp.cos(ang), (1, 2 * reps))
    sin_t = jnp.tile(jnp.concatenate([-jnp.sin(ang), jnp.sin(ang)], axis=1), (1, reps))

    def whole(a):
        return pl.BlockSpec(a.shape, lambda c: (0,) * a.ndim)

    nc = seq // t_len

    def proj_chunk(c):
        return jnp.minimum(c, nc - 1)

    def ret_chunk(c):
        return jnp.maximum(c - 1, 0)

    def sliced(w):
        e, rows, cols = w.shape
        if nc >= e:
            parts = nc // e
            assert nc % e == 0 and rows % (parts * V7X_SUBLANES) == 0
            return w.reshape(e * parts, rows // parts, cols), 1
        assert e % nc == 0
        return w, e // nc

    moe_in, moe_specs, moe_shapes = [], [], []
    for w in moe_w:
        ws, per_step = sliced(w)
        blk = (per_step,) + ws.shape[1:]
        moe_in.append(ws)
        moe_specs.append(pl.BlockSpec(blk, lambda c: (proj_chunk(c), 0, 0)))
        moe_shapes.append(jax.ShapeDtypeStruct(ws.shape, BF16))

    body = functools.partial(_proj_ret_body, t_len=t_len, dh=dh, scale=dh ** -0.5)
    outs = pl.pallas_call(
        body,
        grid=(nc + 1,),
        in_specs=[pl.BlockSpec((nb, t_len, d), lambda c: (0, proj_chunk(c), 0)),
                  whole(n1), whole(w_bf),
                  pl.BlockSpec((t_len, V7X_LANES), lambda c: (ret_chunk(c), 0)),
                  pl.BlockSpec((t_len, V7X_LANES), lambda c: (ret_chunk(c), 0)),
                  whole(dec_all), whole(qdec), whole(kdect), whole(cdec),
                  whole(mask_s), whole(avg), whole(norm_w)] + moe_specs,
        out_specs=[pl.BlockSpec((nb, t_len, d_s5), lambda c: (0, proj_chunk(c), 0)),
                   pl.BlockSpec((nb, t_len, d_ret), lambda c: (0, ret_chunk(c), 0))] + moe_specs,
        out_shape=[jax.ShapeDtypeStruct((nb, seq, d_s5), BF16),
                   jax.ShapeDtypeStruct((nb, seq, d_ret), BF16)] + moe_shapes,
        scratch_shapes=[pltpu.VMEM((nb * t_len, 4 * d_ret), BF16),
                        pltpu.VMEM((nb * t_len, 4 * d_ret), BF16),
                        pltpu.VMEM((nb, d_ret // V7X_LANES, V7X_LANES, V7X_LANES), F32)],
        compiler_params=_cparams(("arbitrary",)),
        name="inproj_retention",
    )(x3, n1, w_bf, cos_t, sin_t, dec_all, qdec, kdect, cdec, mask_s, avg, norm_w, *moe_in)
    u_s5, y_ret = outs[:2]
    moe_bf = [o.reshape(w.shape) for o, w in zip(outs[2:], moe_w)]
    return u_s5, y_ret, moe_bf


def _route_body(ys5_ref, yret_ref, x_ref, wo_ref, n2_ref, wr_ref, br_ref, tri_ref,
                h_ref, route_ref, route_t_ref, cnt_ref, carry_scr):
    @pl.when(pl.program_id(0) == 0)
    def _():
        carry_scr[...] = jnp.zeros_like(carry_scr)

    d_s5 = ys5_ref.shape[1]
    h = (x_ref[...]
         + jnp.dot(ys5_ref[...], wo_ref[0:d_s5], preferred_element_type=F32)
         + jnp.dot(yret_ref[...], wo_ref[d_s5:], preferred_element_type=F32))
    h_ref[...] = h
    hn = _rms(h, n2_ref[...])
    hi = hn.astype(BF16)
    lo = (hn - hi.astype(F32)).astype(BF16)
    p_hi = jnp.dot(hi, wr_ref[...], preferred_element_type=F32)
    p_lo = jnp.dot(lo, wr_ref[:, 0:ROUTE_LANES], preferred_element_type=F32)
    logits = p_hi[:, 0:ROUTE_LANES] + p_hi[:, ROUTE_LANES:] + p_lo + br_ref[...]
    tm = logits.shape[0]
    lt = logits.T
    epg = EXPERTS_PER_GROUP
    row = lax.broadcasted_iota(jnp.int32, (epg, tm), 0)
    rowf = row.astype(F32)
    neg = -jnp.inf
    big = float(epg)
    gl = jnp.where(row < N_GROUPS, lt[0:epg], neg)
    gmax = jnp.max(gl, axis=0, keepdims=True)
    gidx = jnp.min(jnp.where(gl == gmax, rowf, big), axis=0, keepdims=True)
    g_w = 1.0 / jnp.sum(jnp.exp(gl - gmax), axis=0, keepdims=True)
    el = jnp.full((epg, tm), neg, F32)
    for g in range(N_GROUPS):
        el = jnp.where(gidx == g, lt[ROUTE_EXPERT_ROW + g * epg:ROUTE_EXPERT_ROW + (g + 1) * epg], el)
    v0 = jnp.max(el, axis=0, keepdims=True)
    i0 = jnp.min(jnp.where(el == v0, rowf, big), axis=0, keepdims=True)
    el2 = jnp.where(rowf == i0, neg, el)
    v1 = jnp.max(el2, axis=0, keepdims=True)
    i1 = jnp.min(jnp.where(el2 == v1, rowf, big), axis=0, keepdims=True)
    e = jnp.exp(v1 - v0)
    den = 1.0 + e
    w0 = (1.0 / den) * g_w
    w1 = (e / den) * g_w
    eid0 = gidx * epg + i0
    eid1 = gidx * epg + i1
    n_exp = N_GROUPS * epg
    erow = lax.broadcasted_iota(jnp.int32, (n_exp, tm), 0).astype(F32)
    sel0 = erow == eid0
    sel1 = erow == eid1
    onehot = jnp.where(sel0 | sel1, 1.0, 0.0)
    before = jnp.dot(onehot.astype(BF16), tri_ref[...], preferred_element_type=F32) + carry_scr[...]
    r0 = jnp.sum(jnp.where(sel0, before, 0.0), axis=0, keepdims=True)
    r1 = jnp.sum(jnp.where(sel1, before, 0.0), axis=0, keepdims=True)
    carry_scr[...] += jnp.sum(onehot, axis=1, keepdims=True)
    cnt_ref[...] = jnp.broadcast_to(carry_scr[...], cnt_ref.shape)
    rec_t = jnp.zeros((ROUTE_FIELDS, tm), F32)
    for j, val in enumerate((eid0, eid1, w0, w1, r0, r1)):
        rec_t = jnp.where(row == j, val, rec_t)
    route_t_ref[...] = rec_t
    route_ref[...] = jnp.concatenate(
        [rec_t, jnp.zeros((ROUTE_LANES - ROUTE_FIELDS, tm), F32)], axis=0).T


def _route(ys5, yret, x2, wo_bf, n2, wr, br):
    n, d = x2.shape
    d_s5 = ys5.shape[1]
    d_ret = yret.shape[1]
    tm = TM_ROUTE
    n_exp = N_GROUPS * EXPERTS_PER_GROUP
    tri = (jnp.arange(tm)[:, None] < jnp.arange(tm)[None, :]).astype(BF16)
    return pl.pallas_call(
        _route_body,
        grid=(n // tm,),
        in_specs=[pl.BlockSpec((tm, d_s5), lambda i: (i, 0)),
                  pl.BlockSpec((tm, d_ret), lambda i: (i, 0)),
                  pl.BlockSpec((tm, d), lambda i: (i, 0)),
                  pl.BlockSpec((d_s5 + d_ret, d), lambda i: (0, 0)),
                  pl.BlockSpec((1, d), lambda i: (0, 0)),
                  pl.BlockSpec((d, 2 * ROUTE_LANES), lambda i: (0, 0)),
                  pl.BlockSpec((1, ROUTE_LANES), lambda i: (0, 0)),
                  pl.BlockSpec((tm, tm), lambda i: (0, 0))],
        out_specs=[pl.BlockSpec((tm, d), lambda i: (i, 0)),
                   pl.BlockSpec((tm, ROUTE_LANES), lambda i: (i, 0)),
                   pl.BlockSpec((ROUTE_FIELDS, tm), lambda i: (0, i)),
                   pl.BlockSpec((n_exp, ROUTE_LANES), lambda i: (0, 0))],
        out_shape=[jax.ShapeDtypeStruct((n, d), F32),
                   jax.ShapeDtypeStruct((n, ROUTE_LANES), F32),
                   jax.ShapeDtypeStruct((ROUTE_FIELDS, n), F32),
                   jax.ShapeDtypeStruct((n_exp, ROUTE_LANES), F32)],
        scratch_shapes=[pltpu.VMEM((n_exp, 1), F32)],
        compiler_params=_cparams(("arbitrary",)),
        name="outproj_route",
    )(ys5, yret, x2, wo_bf, n2, wr, br, tri)


def _dispatch_body(pends_ref, padded_ref, dest_ref, h_ref, n2_ref, xs_ref,
                   hn_scr, zero_scr, sem, zsem):
    tm = h_ref.shape[0]

    @pl.when(pl.program_id(0) == 0)
    def _():
        zero_scr[...] = jnp.zeros_like(zero_scr)

        def zero_copy(e):
            first = pl.multiple_of((pends_ref[e] - R_BLK) * ROW_TILES, R_BLK * ROW_TILES)
            return pltpu.make_async_copy(zero_scr, xs_ref.at[pl.ds(first, R_BLK * ROW_TILES)], zsem)

        def zstart(e, carry):
            @pl.when(padded_ref[e] > 0)
            def _():
                zero_copy(e).start()
            return carry

        def zwait(e, carry):
            @pl.when(padded_ref[e] > 0)
            def _():
                zero_copy(e).wait()
            return carry

        lax.fori_loop(0, pends_ref.shape[0], zstart, 0)
        lax.fori_loop(0, pends_ref.shape[0], zwait, 0)

    step = pl.program_id(0)
    slot = step % 2
    buf = hn_scr.at[slot]
    _rows_to_tiles(buf, _pack_rows(_rms(h_ref[...], n2_ref[...])))

    for r in range(tm):
        for k in range(TOP_K):
            pltpu.make_async_copy(_tile_rows(buf, r), _tile_rows(xs_ref, dest_ref[k * tm + r]),
                                  sem.at[slot]).start(priority=k)

    def drain(s):
        for k in range(TOP_K):
            pltpu.make_async_copy(hn_scr.at[s], xs_ref.at[pl.ds(0, tm * ROW_TILES)], sem.at[s]).wait()

    @pl.when(step > 0)
    def _():
        drain(1 - slot)

    @pl.when(step == pl.num_programs(0) - 1)
    def _():
        drain(slot)


def _dispatch(pends, padded, dest, h, n2, p_rows):
    n, d = h.shape
    tm = TM_ROWS
    grid_spec = pltpu.PrefetchScalarGridSpec(
        num_scalar_prefetch=2,
        grid=(n // tm,),
        in_specs=[pl.BlockSpec((TOP_K * tm,), lambda i, pe, pa: (i,), memory_space=pltpu.SMEM),
                  pl.BlockSpec((tm, d), lambda i, pe, pa: (i, 0)),
                  pl.BlockSpec((1, d), lambda i, pe, pa: (0, 0))],
        out_specs=pl.BlockSpec(memory_space=pl.ANY),
        scratch_shapes=[pltpu.VMEM((2, tm * ROW_TILES, V7X_LANES), U32),
                        pltpu.VMEM((R_BLK * ROW_TILES, V7X_LANES), U32),
                        pltpu.SemaphoreType.DMA((2,)), pltpu.SemaphoreType.DMA(())])
    assert d == 2 * ROW_TILES * V7X_LANES
    return pl.pallas_call(
        _dispatch_body,
        grid_spec=grid_spec,
        out_shape=jax.ShapeDtypeStruct((p_rows * ROW_TILES, V7X_LANES), U32),
        compiler_params=_cparams(("arbitrary",)),
        name="dispatch",
    )(pends, padded, dest, h, n2)


def _expert_body(be_ref, nu_ref, xs_ref, wg_ref, wu_ref, wd_ref, ys_ref):
    j = pl.program_id(0)

    @pl.when(j < nu_ref[0])
    def _():
        x = _unpack_rows(_tiles_to_rows(xs_ref, R_BLK)).astype(BF16)
        gate = jnp.dot(x, wg_ref[...], preferred_element_type=F32)
        up = jnp.dot(x, wu_ref[...], preferred_element_type=F32)
        hid = (jax.nn.silu(gate) * up).astype(BF16)
        _rows_to_tiles(ys_ref, _pack_rows(jnp.dot(hid, wd_ref[...], preferred_element_type=F32)))

    @pl.when(j >= nu_ref[0])
    def _():
        ys_ref[...] = jnp.zeros_like(ys_ref)


def _experts(block_e, n_used, xs, w_gate, w_up, w_down):
    d, d_e = w_gate.shape[1:]
    blk_rows = R_BLK * ROW_TILES
    nblk = xs.shape[0] // blk_rows

    def row_map(j, be, nu):
        return (jnp.maximum(jnp.minimum(j, nu[0] - 1), 0), 0)

    def w_map(j, be, nu):
        return (be[j], 0, 0)

    grid_spec = pltpu.PrefetchScalarGridSpec(
        num_scalar_prefetch=2,
        grid=(nblk,),
        in_specs=[pl.BlockSpec((blk_rows, V7X_LANES), row_map),
                  pl.BlockSpec((None, d, d_e), w_map),
                  pl.BlockSpec((None, d, d_e), w_map),
                  pl.BlockSpec((None, d_e, d), w_map)],
        out_specs=pl.BlockSpec((blk_rows, V7X_LANES), lambda j, be, nu: (j, 0)))
    return pl.pallas_call(
        _expert_body,
        grid_spec=grid_spec,
        out_shape=jax.ShapeDtypeStruct(xs.shape, xs.dtype),
        compiler_params=_cparams(("arbitrary",)),
        name="experts",
    )(block_e, n_used, xs, w_gate, w_up, w_down)


def _combine_body(dest0_ref, dest1_ref, dest2_ref, h_ref, route_ref, fw_ref, ys_ref, out_ref,
                  *scratch, last):
    bufs, sem = scratch[:GATHER_RING], scratch[GATHER_RING]
    tm = h_ref.shape[0]
    step = pl.program_id(0)

    def issue(d_ref, s):
        for r in range(tm):
            for k in range(TOP_K):
                pltpu.make_async_copy(_tile_rows(ys_ref, d_ref[k * tm + r]),
                                      _tile_rows(bufs[s].at[k], r), sem.at[s]).start(priority=k)

    def drain(s):
        for k in range(TOP_K):
            pltpu.make_async_copy(ys_ref.at[pl.ds(0, tm * ROW_TILES)], bufs[s].at[k], sem.at[s]).wait()

    @pl.when(step == 0)
    def _():
        issue(dest0_ref, 0)
        issue(dest1_ref, 1)

    for s in range(GATHER_RING):
        @pl.when(step % GATHER_RING == s)
        def _():
            drain(s)
            issue(dest2_ref, (s + 2) % GATHER_RING)
            rec = route_ref[...]
            hh = (h_ref[...] + rec[:, 2:3] * _unpack_rows(_tiles_to_rows(bufs[s].at[0], tm))
                  + rec[:, 3:4] * _unpack_rows(_tiles_to_rows(bufs[s].at[1], tm)))
            out_ref[...] = _rms(hh, fw_ref[...])

    @pl.when(step == last)
    def _():
        drain((last + 1) % GATHER_RING)
        drain((last + 2) % GATHER_RING)


def _combine(dest, h, route, fw, ys):
    n, d = h.shape
    tm = TM_ROWS
    steps = n // tm
    return pl.pallas_call(
        functools.partial(_combine_body, last=steps - 1),
        grid=(steps,),
        in_specs=[pl.BlockSpec((TOP_K * tm,), lambda i: (i,), memory_space=pltpu.SMEM),
                  pl.BlockSpec((TOP_K * tm,), lambda i: (jnp.minimum(i + 1, steps - 1),),
                               memory_space=pltpu.SMEM),
                  pl.BlockSpec((TOP_K * tm,), lambda i: (jnp.minimum(i + 2, steps - 1),),
                               memory_space=pltpu.SMEM),
                  pl.BlockSpec((tm, d), lambda i: (i, 0)),
                  pl.BlockSpec((tm, ROUTE_LANES), lambda i: (i, 0)),
                  pl.BlockSpec((1, d), lambda i: (0, 0)),
                  pl.BlockSpec(memory_space=pl.ANY)],
        out_specs=pl.BlockSpec((tm, d), lambda i: (i, 0)),
        out_shape=jax.ShapeDtypeStruct((n, d), F32),
        scratch_shapes=([pltpu.VMEM((TOP_K, tm * ROW_TILES, V7X_LANES), U32)] * GATHER_RING
                        + [pltpu.SemaphoreType.DMA((GATHER_RING,))]),
        compiler_params=_cparams(("arbitrary",)),
        name="combine",
    )(dest, dest, dest, h, route, fw, ys)


def _plan(route_t, counts, n_experts, n_blocks):
    eid = route_t[0:TOP_K].astype(jnp.int32).reshape(-1, V7X_LANES)
    rank = route_t[4:4 + TOP_K].astype(jnp.int32).reshape(-1, V7X_LANES)
    cnt = counts[:, 0].astype(jnp.int32)
    padded = (cnt + R_BLK - 1) // R_BLK * R_BLK
    pends = jnp.cumsum(padded)
    pstart = pends - padded
    dest = rank
    for e in range(n_experts):
        dest = dest + jnp.where(eid == e, pstart[e], 0)
    n_used = pends[-1] // R_BLK
    blk = jnp.minimum(jnp.arange(n_blocks, dtype=jnp.int32), n_used - 1)
    block_e = jnp.minimum(jnp.sum(pends[None, :] <= (blk * R_BLK)[:, None], axis=1), n_experts - 1)
    return (dest.astype(jnp.int32), block_e.astype(jnp.int32), n_used.reshape(1).astype(jnp.int32),
            pends.astype(jnp.int32), padded.astype(jnp.int32))


def _layer(h3, norm1_w, w_in, s5_a_re, s5_a_im, s5_b_re, s5_b_im, s5_c_re, s5_c_im, s5_d,
           s5_log_dt, s5_w_glu, s5_b_glu, ret_norm_w, w_out, norm2_w, router_group_w,
           router_group_b, router_expert_w, router_expert_b, moe_w_gate, moe_w_up, moe_w_down,
           out_norm_w):
    nb, seq, d = h3.shape
    n = nb * seq
    d_s5 = s5_d.shape[0]
    d_ret = ret_norm_w.shape[0]
    n_experts = moe_w_gate.shape[0]
    x2 = h3.reshape(n, d)

    u_s5, y_ret, moe_bf = _proj_retention(h3, norm1_w.reshape(1, d), w_in.astype(BF16), d_s5, d_ret,
                                          ret_norm_w.reshape(1, d_ret).astype(F32),
                                          (moe_w_gate, moe_w_up, moe_w_down))

    bp, cp, tp, a_tab = _s5_tables(s5_a_re, s5_a_im, s5_b_re, s5_b_im, s5_c_re, s5_c_im,
                                   s5_log_dt, nb, S5_TAU)
    y_s5 = _s5(u_s5.reshape(n, d_s5), nb, bp, cp, tp, a_tab, s5_d.reshape(1, d_s5).astype(F32),
               s5_w_glu.astype(BF16), s5_b_glu.reshape(1, d_s5).astype(F32))

    assert n_experts == N_GROUPS * EXPERTS_PER_GROUP
    wr = (jnp.zeros((d, ROUTE_LANES), F32).at[:, :N_GROUPS].set(router_group_w.astype(F32))
          .at[:, ROUTE_EXPERT_ROW:ROUTE_EXPERT_ROW + n_experts].set(router_expert_w.astype(F32)))
    br = (jnp.zeros((1, ROUTE_LANES), F32).at[0, :N_GROUPS].set(router_group_b.astype(F32))
          .at[0, ROUTE_EXPERT_ROW:ROUTE_EXPERT_ROW + n_experts].set(router_expert_b.astype(F32)))
    wr_hi = wr.astype(BF16)
    wr = jnp.concatenate([wr_hi, (wr - wr_hi.astype(F32)).astype(BF16)], axis=1)
    h, route, route_t, counts = _route(y_s5.reshape(n, d_s5), y_ret.reshape(n, d_ret), x2,
                                       w_out.astype(BF16), norm2_w.reshape(1, d), wr, br)

    n_blocks = (n * TOP_K) // R_BLK + n_experts
    dest, block_e, n_used, pends, padded = _plan(route_t, counts, n_experts, n_blocks)
    dest = dest.reshape(TOP_K, n // TM_ROWS, TM_ROWS).transpose(1, 0, 2).reshape(-1)
    xs = _dispatch(pends, padded, dest, h, norm2_w.reshape(1, d), n_blocks * R_BLK)
    ys = _experts(block_e, n_used, xs, *moe_bf)
    out = _combine(dest, h, route, out_norm_w.reshape(1, d), ys)
    return out.reshape(nb, seq, d)


def kernel(x, norm1_w, w_in, s5_a_re, s5_a_im, s5_b_re, s5_b_im, s5_c_re, s5_c_im, s5_d, s5_log_dt, s5_w_glu, s5_b_glu, ret_norm_w, w_out, norm2_w, router_group_w, router_group_b, router_expert_w, router_expert_b, moe_w_gate, moe_w_up, moe_w_down, final_norm_w):
    depth = norm1_w.shape[0]
    assert depth == 1, "the fused final norm assumes a single layer"
    l = 0
    return _layer(x, norm1_w[l], w_in[l], s5_a_re[l], s5_a_im[l], s5_b_re[l], s5_b_im[l],
                  s5_c_re[l], s5_c_im[l], s5_d[l], s5_log_dt[l], s5_w_glu[l], s5_b_glu[l],
                  ret_norm_w[l], w_out[l], norm2_w[l], router_group_w[l], router_group_b[l],
                  router_expert_w[l], router_expert_b[l], moe_w_gate[l], moe_w_up[l],
                  moe_w_down[l], final_norm_w)
```

```python
import functools

import jax
import jax.numpy as jnp
from jax import lax
from jax.experimental import pallas as pl
from jax.experimental.pallas import tpu as pltpu

F32 = jnp.float32
BF16 = jnp.bfloat16

EPS = 1e-6
ROPE_BASE = 10000.0
RET_HEADS = 8
TOP_K = 2
N_GROUPS = 4
EXPERTS_PER_GROUP = 8

V7X_LANES = 128
V7X_SUBLANES = 8
V7X_VMEM_LIMIT = 56 * 1024 * 1024

T_S5 = 512
S5_TAU = 8
S5_PAD = 8
T_RET = 128
TM_ROUTE = 512
TM_ROWS = 512
R_BLK = 512
GATHER_RING = 3
ROUTE_LANES = 128
ROUTE_FIELDS = 8
ROUTE_EXPERT_ROW = 8


def _rms(x, w):
    return x * lax.rsqrt(jnp.mean(x * x, axis=-1, keepdims=True) + EPS) * w


def _cparams(sem):
    return pltpu.CompilerParams(dimension_semantics=sem, vmem_limit_bytes=V7X_VMEM_LIMIT)


ROW_TILES = V7X_SUBLANES // 2
U32 = jnp.uint32


def _pack_rows(x):
    half = x.shape[1] // 2

    def bf16_bits(v):
        return lax.bitcast_convert_type(v.astype(BF16).astype(F32), U32)

    return bf16_bits(x[:, half:]) | (bf16_bits(x[:, :half]) >> 16)


def _unpack_rows(w):
    lo = lax.bitcast_convert_type(w << 16, F32)
    hi = lax.bitcast_convert_type(w & U32(0xFFFF0000), F32)
    return jnp.concatenate([lo, hi], axis=1)


def _rows_to_tiles(ref, val):
    rows = val.shape[0]
    for s in range(ROW_TILES):
        ref[pl.ds(s, rows, stride=ROW_TILES), :] = val[:, s * V7X_LANES:(s + 1) * V7X_LANES]


def _tiles_to_rows(ref, rows):
    return jnp.concatenate(
        [ref[pl.ds(s, rows, stride=ROW_TILES), :] for s in range(ROW_TILES)], axis=1)


def _tile_rows(ref, row):
    return ref.at[pl.ds(pl.multiple_of(row * ROW_TILES, ROW_TILES), ROW_TILES)]


def _s5_body(u_ref, bp_ref, cp_ref, tp_ref, a_ref, dd_ref, wglu_ref, bglu_ref, y_ref,
             v_scr, sp_scr, st_scr, io_scr, *, nb, nk, seg, tau):
    @pl.when(pl.program_id(0) == 0)
    def _():
        st_scr[...] = jnp.zeros_like(st_scr)

    nblk = bp_ref.shape[0]
    cw = bp_ref.shape[1] // tau
    sw = bp_ref.shape[2]
    d_s5 = nblk * cw
    tiles = sw // V7X_LANES
    ht = tiles // 2
    ctiles = d_s5 // V7X_LANES
    srows = nb * nk

    u_all = u_ref[...].reshape(srows * tau, d_s5).astype(F32)
    for c in range(ctiles):
        io_scr[c] = u_all[:, c * V7X_LANES:(c + 1) * V7X_LANES]
    u_steps = [jnp.concatenate([io_scr[c, pl.ds(j, srows, stride=tau), :] for c in range(ctiles)],
                               axis=1) for j in range(tau)]
    u_blocks = [jnp.concatenate([u_steps[j][:, blk * cw:(blk + 1) * cw] for j in range(tau)],
                                axis=1).astype(BF16) for blk in range(nblk)]

    for blk in range(nblk):
        half, q = divmod(blk, 2)
        v = jnp.dot(u_blocks[blk], bp_ref[blk], preferred_element_type=F32)
        for b in range(nb):
            for j in range(tiles):
                v_scr[q * tiles + j, pl.ds((half * nb + b) * seg, nk), :] = (
                    v[b * nk:(b + 1) * nk, j * V7X_LANES:(j + 1) * V7X_LANES])

    rows = 2 * nb
    ar = [[a_ref[0, q * ht + i] for i in range(ht)] for q in range(2)]
    ai = [[a_ref[1, q * ht + i] for i in range(ht)] for q in range(2)]
    sr = [[st_scr[q * tiles + i] for i in range(ht)] for q in range(2)]
    si = [[st_scr[q * tiles + ht + i] for i in range(ht)] for q in range(2)]
    for k in range(nk):
        for q in range(2):
            for i in range(ht):
                jr = q * tiles + i
                ji = q * tiles + ht + i
                sp_scr[jr, pl.ds(k, rows, stride=seg), :] = sr[q][i]
                sp_scr[ji, pl.ds(k, rows, stride=seg), :] = si[q][i]
                vr = v_scr[jr, pl.ds(k, rows, stride=seg), :]
                vi = v_scr[ji, pl.ds(k, rows, stride=seg), :]
                nr = ar[q][i] * sr[q][i] - ai[q][i] * si[q][i] + vr
                ni = ar[q][i] * si[q][i] + ai[q][i] * sr[q][i] + vi
                sr[q][i], si[q][i] = nr, ni
    for q in range(2):
        for i in range(ht):
            st_scr[q * tiles + i] = sr[q][i]
            st_scr[q * tiles + ht + i] = si[q][i]

    yb = []
    for blk in range(nblk):
        half, q = divmod(blk, 2)
        sp = jnp.concatenate(
            [jnp.concatenate([sp_scr[q * tiles + j, pl.ds((half * nb + b) * seg, nk), :]
                              for j in range(tiles)], axis=1) for b in range(nb)],
            axis=0).astype(BF16)
        yb.append(jnp.dot(sp, cp_ref[blk], preferred_element_type=F32)
                  + jnp.dot(u_blocks[blk], tp_ref[blk], preferred_element_type=F32))
    for j in range(tau):
        y = jnp.concatenate([yb[blk][:, j * cw:(j + 1) * cw] for blk in range(nblk)], axis=1)
        y = y + dd_ref[...] * u_steps[j]
        y = jax.nn.gelu(y)
        z = jnp.dot(y.astype(BF16), wglu_ref[...], preferred_element_type=F32) + bglu_ref[...]
        out = y * jax.nn.sigmoid(z)
        for c in range(ctiles):
            io_scr[c, pl.ds(j, srows, stride=tau), :] = out[:, c * V7X_LANES:(c + 1) * V7X_LANES]
    y_all = jnp.concatenate([io_scr[c] for c in range(ctiles)], axis=1)
    y_ref[...] = y_all.reshape(nb, nk * tau, d_s5).astype(y_ref.dtype)


def _s5(u_s5, nb, bp, cp, tp, a_tab, dd, wglu_bf, bglu):
    n, d_s5 = u_s5.shape
    seq = n // nb
    tau = S5_TAU
    nk = T_S5 // tau
    seg = nk + S5_PAD
    nblk, _, sw = bp.shape
    rows = 2 * nb
    assert rows == V7X_SUBLANES and nblk == 4

    def whole(a):
        return pl.BlockSpec(a.shape, lambda c: (0,) * a.ndim)

    body = functools.partial(_s5_body, nb=nb, nk=nk, seg=seg, tau=tau)
    out = pl.pallas_call(
        body,
        grid=(seq // T_S5,),
        in_specs=[pl.BlockSpec((nb, T_S5, d_s5), lambda c: (0, c, 0)),
                  whole(bp), whole(cp), whole(tp), whole(a_tab), whole(dd), whole(wglu_bf),
                  whole(bglu)],
        out_specs=pl.BlockSpec((nb, T_S5, d_s5), lambda c: (0, c, 0)),
        out_shape=jax.ShapeDtypeStruct((nb, seq, d_s5), BF16),
        scratch_shapes=[pltpu.VMEM((2 * sw // V7X_LANES, rows * seg, V7X_LANES), F32),
                        pltpu.VMEM((2 * sw // V7X_LANES, rows * seg, V7X_LANES), F32),
                        pltpu.VMEM((2 * sw // V7X_LANES, rows, V7X_LANES), F32),
                        pltpu.VMEM((d_s5 // V7X_LANES, nb * T_S5, V7X_LANES), F32)],
        compiler_params=_cparams(("arbitrary",)),
        name="s5",
    )(u_s5.reshape(nb, seq, d_s5), bp, cp, tp, a_tab, dd, wglu_bf, bglu)
    return out.reshape(n, d_s5)


def _s5_tables(a_re, a_im, b_re, b_im, c_re, c_im, log_dt, nb, tau):
    hp = lax.Precision.HIGHEST
    g, p = a_re.shape
    hch = b_re.shape[2]
    gpb = V7X_LANES // hch
    nblk = g // gpb
    lam_r, lam_i = a_re.astype(F32), a_im.astype(F32)
    dt = jnp.exp(log_dt.astype(F32))[:, None]
    mag = jnp.exp(lam_r * dt)
    ab_r = mag * jnp.cos(lam_i * dt)
    ab_i = mag * jnp.sin(lam_i * dt)
    den = lam_r * lam_r + lam_i * lam_i
    zr = ((ab_r - 1.0) * lam_r + ab_i * lam_i) / den
    zi = (ab_i * lam_r - (ab_r - 1.0) * lam_i) / den
    br_, bi_ = b_re.astype(F32), b_im.astype(F32)
    bb_r = zr[..., None] * br_ - zi[..., None] * bi_
    bb_i = zr[..., None] * bi_ + zi[..., None] * br_
    cr, ci = c_re.astype(F32), c_im.astype(F32)
    pr, pi = [jnp.ones_like(ab_r)], [jnp.zeros_like(ab_i)]
    for _ in range(tau):
        pr, pi = pr + [pr[-1] * ab_r - pi[-1] * ab_i], pi + [pr[-1] * ab_i + pi[-1] * ab_r]
    pw_r, pw_i = jnp.stack(pr), jnp.stack(pi)

    def blockdiag(x):
        r, c = x.shape[-2:]
        x = jnp.tile(x.reshape(tau, nblk, gpb * r, c), (1, 1, 1, gpb))
        same = (jnp.arange(gpb * r)[:, None] // r) == (jnp.arange(gpb * c)[None, :] // c)
        return jnp.where(same, x, 0.0)

    wr_ = jnp.stack([pr[tau - 1 - j] for j in range(tau)])[:, :, None, :]
    wi_ = jnp.stack([pi[tau - 1 - j] for j in range(tau)])[:, :, None, :]
    bt_r, bt_i = jnp.swapaxes(bb_r, 1, 2)[None], jnp.swapaxes(bb_i, 1, 2)[None]
    bp_r = blockdiag(wr_ * bt_r - wi_ * bt_i)
    bp_i = blockdiag(wr_ * bt_i + wi_ * bt_r)
    bp = jnp.concatenate([jnp.concatenate([bp_r[j], bp_i[j]], axis=-1) for j in range(tau)],
                         axis=1).astype(BF16)

    qr_, qi_ = pw_r[1:, :, :, None], pw_i[1:, :, :, None]
    ct_r, ct_i = jnp.swapaxes(cr, 1, 2)[None], jnp.swapaxes(ci, 1, 2)[None]
    cp_r = blockdiag(ct_r * qr_ - ct_i * qi_)
    cp_i = blockdiag(ct_r * qi_ + ct_i * qr_)
    cp = jnp.concatenate([jnp.concatenate([cp_r[i], -cp_i[i]], axis=1) for i in range(tau)],
                         axis=2).astype(BF16)

    ab_r_ = pw_r[:tau, :, :, None] * bb_r[None] - pw_i[:tau, :, :, None] * bb_i[None]
    ab_i_ = pw_r[:tau, :, :, None] * bb_i[None] + pw_i[:tau, :, :, None] * bb_r[None]
    kd = blockdiag(jnp.einsum('ghp,dgpe->dgeh', cr, ab_r_, precision=hp)
                   - jnp.einsum('ghp,dgpe->dgeh', ci, ab_i_, precision=hp))
    kzero = jnp.zeros_like(kd[0])
    tp = jnp.concatenate(
        [jnp.concatenate([kd[i - j] if i >= j else kzero for i in range(tau)], axis=2)
         for j in range(tau)], axis=1).astype(BF16)

    a_tab = jnp.stack([jnp.repeat(pw_r[tau].reshape(2, -1), nb, axis=0),
                       jnp.repeat(pw_i[tau].reshape(2, -1), nb, axis=0)])
    a_tab = a_tab.reshape(2, 2 * nb, -1, V7X_LANES).transpose(0, 2, 1, 3)
    return bp, cp, tp, a_tab


def _proj_ret_body(x_ref, n1_ref, w_ref, cos_ref, sin_ref, dec_ref, qdec_ref, kdect_ref,
                   cdec_ref, ms_ref, avg_ref, nw_ref, wg_ref, wu_ref, wd_ref,
                   us5_ref, y_ref, wg_out, wu_out, wd_out, u_even, u_odd, st_scr,
                   *, t_len, dh, scale):
    wg_out[...] = wg_ref[...].astype(wg_out.dtype)
    wu_out[...] = wu_ref[...].astype(wu_out.dtype)
    wd_out[...] = wd_ref[...].astype(wd_out.dtype)

    step = pl.program_id(0)
    nb = x_ref.shape[0]
    d_s5 = us5_ref.shape[-1]

    @pl.when(step == 0)
    def _():
        u_odd[...] = jnp.zeros_like(u_odd)
        st_scr[...] = jnp.zeros_like(st_scr)

    def project(u_out):
        xn = _rms(x_ref[...].reshape(nb * t_len, x_ref.shape[-1]), n1_ref[...]).astype(BF16)
        u = jnp.dot(xn, w_ref[...], preferred_element_type=F32)
        us5_ref[...] = u[:, :d_s5].reshape(nb, t_len, d_s5).astype(us5_ref.dtype)
        u_out[...] = u[:, d_s5:].astype(u_out.dtype)

    for parity, (u_out, u_in) in enumerate(((u_even, u_odd), (u_odd, u_even))):
        @pl.when(step % 2 == parity)
        def _():
            project(u_out)
            _ret_chunk(u_in, cos_ref, sin_ref, dec_ref, qdec_ref, kdect_ref, cdec_ref, ms_ref, avg_ref,
                       nw_ref, y_ref, st_scr, t_len=t_len, dh=dh, scale=scale)


def _ret_chunk(u_ref, cos_ref, sin_ref, dec_ref, qdec_ref, kdect_ref, cdec_ref, ms_ref, avg_ref,
               nw_ref, y_ref, st_scr, *, t_len, dh, scale):
    nb = y_ref.shape[0]
    width = y_ref.shape[-1]
    pairs = width // V7X_LANES
    reps = width // cos_ref.shape[-1]
    cos = jnp.concatenate([cos_ref[...]] * reps, axis=1)
    sin = jnp.concatenate([sin_ref[...]] * reps, axis=1)
    lane = lax.broadcasted_iota(jnp.int32, (t_len, width), 1)
    first = (lane % dh) < (dh // 2)

    def rot(x):
        x = x.astype(F32)
        partner = jnp.where(first, pltpu.roll(x, width - dh // 2, 1), pltpu.roll(x, dh // 2, 1))
        return x * cos + partner * sin

    def group_mean(x):
        return jnp.dot(x.astype(BF16), avg_ref[...], preferred_element_type=F32)

    zero_k = jnp.zeros((dh, t_len), BF16)
    zero_v = jnp.zeros((t_len, V7X_LANES), BF16)
    low_head = lax.broadcasted_iota(jnp.int32, (t_len, V7X_LANES), 1) < dh

    outs = []
    for b in range(nb):
        rows = slice(b * t_len, (b + 1) * t_len)
        q = rot(u_ref[rows, 0:width])
        k = rot(u_ref[rows, width:2 * width]) * scale
        vb = u_ref[rows, 2 * width:3 * width]
        kt = k.T
        ktb = kt.astype(BF16)
        qb = q.astype(BF16)
        inners = []
        for p in range(pairs):
            ps = slice(p * V7X_LANES, (p + 1) * V7X_LANES)
            k_lo, k_hi = ktb[p * V7X_LANES:p * V7X_LANES + dh], ktb[p * V7X_LANES + dh:(p + 1) * V7X_LANES]
            kbd = jnp.concatenate([jnp.concatenate([k_lo, zero_k], axis=0),
                                   jnp.concatenate([zero_k, k_hi], axis=0)], axis=1)
            sc = (jnp.dot(qb[:, ps], kbd, preferred_element_type=F32)
                  * dec_ref[:, 2 * p * t_len:2 * (p + 1) * t_len])
            vt = vb[:, ps]
            vbd = jnp.concatenate([jnp.where(low_head, vt, zero_v),
                                   jnp.where(low_head, zero_v, vt)], axis=0)
            inners.append(jnp.dot(sc.astype(BF16), vbd, preferred_element_type=F32))
        inner = jnp.concatenate(inners, axis=1)
        qd = (q * qdec_ref[...]).astype(BF16)
        kdt = (kt * kdect_ref[...]).astype(BF16)
        crosses = []
        for p in range(pairs):
            ps = slice(p * V7X_LANES, (p + 1) * V7X_LANES)
            state = st_scr[b, p]
            crosses.append(jnp.dot(qd[:, ps], state.astype(BF16), preferred_element_type=F32))
            kv = jnp.dot(kdt[ps, :], vb[:, ps], preferred_element_type=F32)
            st_scr[b, p] = state * cdec_ref[:, ps] + kv * ms_ref[...]
        outs.append(inner + jnp.concatenate(crosses, axis=1))

    o = jnp.concatenate(outs, axis=0)
    dlt = o - group_mean(o)
    on = dlt * lax.rsqrt(group_mean(dlt * dlt) + EPS) * nw_ref[...]
    gate = u_ref[:, 3 * width:4 * width].astype(F32)
    y_ref[...] = (jax.nn.silu(gate) * on).reshape(nb, t_len, width).astype(y_ref.dtype)


def _proj_retention(x3, n1, w_bf, d_s5, d_ret, norm_w, moe_w):
    nb, seq, d = x3.shape
    dh = d_ret // RET_HEADS
    half = dh // 2
    t_len = T_RET
    assert V7X_LANES == 2 * dh
    lg = jnp.log(1.0 - 2.0 ** (-5.0 - jnp.arange(RET_HEADS, dtype=F32)))
    t = jnp.arange(t_len, dtype=F32)
    diff = t[:, None] - t[None, :]
    dec = jnp.where(diff >= 0, jnp.exp(lg[:, None, None] * jnp.maximum(diff, 0.0)), 0.0)
    dec_all = dec.transpose(1, 0, 2).reshape(t_len, RET_HEADS * t_len)
    qdec = jnp.repeat(jnp.exp(lg[:, None] * (t + 1.0)[None, :]).T, dh, axis=1)
    kdect = jnp.repeat(jnp.exp(lg[:, None] * (t_len - 1 - t)[None, :]), dh, axis=0)
    cdec = jnp.repeat(jnp.exp(lg * t_len), dh)[None, :]
    head_of_lane = jnp.arange(d_ret) // dh
    pair_head = jnp.arange(V7X_LANES) // dh
    mask_s = (pair_head[:, None] == pair_head[None, :]).astype(F32)
    avg = (head_of_lane[:, None] == head_of_lane[None, :]).astype(F32) / dh
    assert dh & (dh - 1) == 0
    avg = avg.astype(BF16)
    inv = ROPE_BASE ** (-jnp.arange(half, dtype=F32) / half)
    ang = jnp.arange(seq, dtype=F32)[:, None] * inv[None, :]
    reps = V7X_LANES // dh
    cos_t = jnp.tile(jnp.cos(ang), (1, 2 * reps))
    sin_t = jnp.tile(jnp.concatenate([-jnp.sin(ang), jnp.sin(ang)], axis=1), (1, reps))

    def whole(a):
        return pl.BlockSpec(a.shape, lambda c: (0,) * a.ndim)

    nc = seq // t_len

    def proj_chunk(c):
        return jnp.minimum(c, nc - 1)

    def ret_chunk(c):
        return jnp.maximum(c - 1, 0)

    def sliced(w):
        e, rows, cols = w.shape
        if nc >= e:
            parts = nc // e
            assert nc % e == 0 and rows % (parts * V7X_SUBLANES) == 0
            return w.reshape(e * parts, rows // parts, cols), 1
        assert e % nc == 0
        return w, e // nc

    moe_in, moe_specs, moe_shapes = [], [], []
    for w in moe_w:
        ws, per_step = sliced(w)
        blk = (per_step,) + ws.shape[1:]
        moe_in.append(ws)
        moe_specs.append(pl.BlockSpec(blk, lambda c: (proj_chunk(c), 0, 0)))
        moe_shapes.append(jax.ShapeDtypeStruct(ws.shape, BF16))

    body = functools.partial(_proj_ret_body, t_len=t_len, dh=dh, scale=dh ** -0.5)
    outs = pl.pallas_call(
        body,
        grid=(nc + 1,),
        in_specs=[pl.BlockSpec((nb, t_len, d), lambda c: (0, proj_chunk(c), 0)),
                  whole(n1), whole(w_bf),
                  pl.BlockSpec((t_len, V7X_LANES), lambda c: (ret_chunk(c), 0)),
                  pl.BlockSpec((t_len, V7X_LANES), lambda c: (ret_chunk(c), 0)),
                  whole(dec_all), whole(qdec), whole(kdect), whole(cdec),
                  whole(mask_s), whole(avg), whole(norm_w)] + moe_specs,
        out_specs=[pl.BlockSpec((nb, t_len, d_s5), lambda c: (0, proj_chunk(c), 0)),
                   pl.BlockSpec((nb, t_len, d_ret), lambda c: (0, ret_chunk(c), 0))] + moe_specs,
        out_shape=[jax.ShapeDtypeStruct((nb, seq, d_s5), BF16),
                   jax.ShapeDtypeStruct((nb, seq, d_ret), BF16)] + moe_shapes,
        scratch_shapes=[pltpu.VMEM((nb * t_len, 4 * d_ret), BF16),
                        pltpu.VMEM((nb * t_len, 4 * d_ret), BF16),
                        pltpu.VMEM((nb, d_ret // V7X_LANES, V7X_LANES, V7X_LANES), F32)],
        compiler_params=_cparams(("arbitrary",)),
        name="inproj_retention",
    )(x3, n1, w_bf, cos_t, sin_t, dec_all, qdec, kdect, cdec, mask_s, avg, norm_w, *moe_in)
    u_s5, y_ret = outs[:2]
    moe_bf = [o.reshape(w.shape) for o, w in zip(outs[2:], moe_w)]
    return u_s5, y_ret, moe_bf


def _route_body(ys5_ref, yret_ref, x_ref, wo_ref, n2_ref, wr_ref, br_ref, tri_ref,
                h_ref, route_ref, route_t_ref, cnt_ref, carry_scr):
    @pl.when(pl.program_id(0) == 0)
    def _():
        carry_scr[...] = jnp.zeros_like(carry_scr)

    d_s5 = ys5_ref.shape[1]
    h = (x_ref[...]
         + jnp.dot(ys5_ref[...], wo_ref[0:d_s5], preferred_element_type=F32)
         + jnp.dot(yret_ref[...], wo_ref[d_s5:], preferred_element_type=F32))
    h_ref[...] = h
    hn = _rms(h, n2_ref[...])
    hi = hn.astype(BF16)
    lo = (hn - hi.astype(F32)).astype(BF16)
    p_hi = jnp.dot(hi, wr_ref[...], preferred_element_type=F32)
    p_lo = jnp.dot(lo, wr_ref[:, 0:ROUTE_LANES], preferred_element_type=F32)
    logits = p_hi[:, 0:ROUTE_LANES] + p_hi[:, ROUTE_LANES:] + p_lo + br_ref[...]
    tm = logits.shape[0]
    lt = logits.T
    epg = EXPERTS_PER_GROUP
    row = lax.broadcasted_iota(jnp.int32, (epg, tm), 0)
    rowf = row.astype(F32)
    neg = -jnp.inf
    big = float(epg)
    gl = jnp.where(row < N_GROUPS, lt[0:epg], neg)
    gmax = jnp.max(gl, axis=0, keepdims=True)
    gidx = jnp.min(jnp.where(gl == gmax, rowf, big), axis=0, keepdims=True)
    g_w = 1.0 / jnp.sum(jnp.exp(gl - gmax), axis=0, keepdims=True)
    el = jnp.full((epg, tm), neg, F32)
    for g in range(N_GROUPS):
        el = jnp.where(gidx == g, lt[ROUTE_EXPERT_ROW + g * epg:ROUTE_EXPERT_ROW + (g + 1) * epg], el)
    v0 = jnp.max(el, axis=0, keepdims=True)
    i0 = jnp.min(jnp.where(el == v0, rowf, big), axis=0, keepdims=True)
    el2 = jnp.where(rowf == i0, neg, el)
    v1 = jnp.max(el2, axis=0, keepdims=True)
    i1 = jnp.min(jnp.where(el2 == v1, rowf, big), axis=0, keepdims=True)
    e = jnp.exp(v1 - v0)
    den = 1.0 + e
    w0 = (1.0 / den) * g_w
    w1 = (e / den) * g_w
    eid0 = gidx * epg + i0
    eid1 = gidx * epg + i1
    n_exp = N_GROUPS * epg
    erow = lax.broadcasted_iota(jnp.int32, (n_exp, tm), 0).astype(F32)
    sel0 = erow == eid0
    sel1 = erow == eid1
    onehot = jnp.where(sel0 | sel1, 1.0, 0.0)
    before = jnp.dot(onehot.astype(BF16), tri_ref[...], preferred_element_type=F32) + carry_scr[...]
    r0 = jnp.sum(jnp.where(sel0, before, 0.0), axis=0, keepdims=True)
    r1 = jnp.sum(jnp.where(sel1, before, 0.0), axis=0, keepdims=True)
    carry_scr[...] += jnp.sum(onehot, axis=1, keepdims=True)
    cnt_ref[...] = jnp.broadcast_to(carry_scr[...], cnt_ref.shape)
    rec_t = jnp.zeros((ROUTE_FIELDS, tm), F32)
    for j, val in enumerate((eid0, eid1, w0, w1, r0, r1)):
        rec_t = jnp.where(row == j, val, rec_t)
    route_t_ref[...] = rec_t
    route_ref[...] = jnp.concatenate(
        [rec_t, jnp.zeros((ROUTE_LANES - ROUTE_FIELDS, tm), F32)], axis=0).T


def _route(ys5, yret, x2, wo_bf, n2, wr, br):
    n, d = x2.shape
    d_s5 = ys5.shape[1]
    d_ret = yret.shape[1]
    tm = TM_ROUTE
    n_exp = N_GROUPS * EXPERTS_PER_GROUP
    tri = (jnp.arange(tm)[:, None] < jnp.arange(tm)[None, :]).astype(BF16)
    return pl.pallas_call(
        _route_body,
        grid=(n // tm,),
        in_specs=[pl.BlockSpec((tm, d_s5), lambda i: (i, 0)),
                  pl.BlockSpec((tm, d_ret), lambda i: (i, 0)),
                  pl.BlockSpec((tm, d), lambda i: (i, 0)),
                  pl.BlockSpec((d_s5 + d_ret, d), lambda i: (0, 0)),
                  pl.BlockSpec((1, d), lambda i: (0, 0)),
                  pl.BlockSpec((d, 2 * ROUTE_LANES), lambda i: (0, 0)),
                  pl.BlockSpec((1, ROUTE_LANES), lambda i: (0, 0)),
                  pl.BlockSpec((tm, tm), lambda i: (0, 0))],
        out_specs=[pl.BlockSpec((tm, d), lambda i: (i, 0)),
                   pl.BlockSpec((tm, ROUTE_LANES), lambda i: (i, 0)),
                   pl.BlockSpec((ROUTE_FIELDS, tm), lambda i: (0, i)),
                   pl.BlockSpec((n_exp, ROUTE_LANES), lambda i: (0, 0))],
        out_shape=[jax.ShapeDtypeStruct((n, d), F32),
                   jax.ShapeDtypeStruct((n, ROUTE_LANES), F32),
                   jax.ShapeDtypeStruct((ROUTE_FIELDS, n), F32),
                   jax.ShapeDtypeStruct((n_exp, ROUTE_LANES), F32)],
        scratch_shapes=[pltpu.VMEM((n_exp, 1), F32)],
        compiler_params=_cparams(("arbitrary",)),
        name="outproj_route",
    )(ys5, yret, x2, wo_bf, n2, wr, br, tri)


def _dispatch_body(pends_ref, padded_ref, dest_ref, h_ref, n2_ref, xs_ref,
                   hn_scr, zero_scr, sem, zsem):
    tm = h_ref.shape[0]

    @pl.when(pl.program_id(0) == 0)
    def _():
        zero_scr[...] = jnp.zeros_like(zero_scr)

        def zero_copy(e):
            first = pl.multiple_of((pends_ref[e] - R_BLK) * ROW_TILES, R_BLK * ROW_TILES)
            return pltpu.make_async_copy(zero_scr, xs_ref.at[pl.ds(first, R_BLK * ROW_TILES)], zsem)

        def zstart(e, carry):
            @pl.when(padded_ref[e] > 0)
            def _():
                zero_copy(e).start()
            return carry

        def zwait(e, carry):
            @pl.when(padded_ref[e] > 0)
            def _():
                zero_copy(e).wait()
            return carry

        lax.fori_loop(0, pends_ref.shape[0], zstart, 0)
        lax.fori_loop(0, pends_ref.shape[0], zwait, 0)

    step = pl.program_id(0)
    slot = step % 2
    buf = hn_scr.at[slot]
    _rows_to_tiles(buf, _pack_rows(_rms(h_ref[...], n2_ref[...])))

    for r in range(tm):
        for k in range(TOP_K):
            pltpu.make_async_copy(_tile_rows(buf, r), _tile_rows(xs_ref, dest_ref[k * tm + r]),
                                  sem.at[slot]).start(priority=k)

    def drain(s):
        for k in range(TOP_K):
            pltpu.make_async_copy(hn_scr.at[s], xs_ref.at[pl.ds(0, tm * ROW_TILES)], sem.at[s]).wait()

    @pl.when(step > 0)
    def _():
        drain(1 - slot)

    @pl.when(step == pl.num_programs(0) - 1)
    def _():
        drain(slot)


def _dispatch(pends, padded, dest, h, n2, p_rows):
    n, d = h.shape
    tm = TM_ROWS
    grid_spec = pltpu.PrefetchScalarGridSpec(
        num_scalar_prefetch=2,
        grid=(n // tm,),
        in_specs=[pl.BlockSpec((TOP_K * tm,), lambda i, pe, pa: (i,), memory_space=pltpu.SMEM),
                  pl.BlockSpec((tm, d), lambda i, pe, pa: (i, 0)),
                  pl.BlockSpec((1, d), lambda i, pe, pa: (0, 0))],
        out_specs=pl.BlockSpec(memory_space=pl.ANY),
        scratch_shapes=[pltpu.VMEM((2, tm * ROW_TILES, V7X_LANES), U32),
                        pltpu.VMEM((R_BLK * ROW_TILES, V7X_LANES), U32),
                        pltpu.SemaphoreType.DMA((2,)), pltpu.SemaphoreType.DMA(())])
    assert d == 2 * ROW_TILES * V7X_LANES
    return pl.pallas_call(
        _dispatch_body,
        grid_spec=grid_spec,
        out_shape=jax.ShapeDtypeStruct((p_rows * ROW_TILES, V7X_LANES), U32),
        compiler_params=_cparams(("arbitrary",)),
        name="dispatch",
    )(pends, padded, dest, h, n2)


def _expert_body(be_ref, nu_ref, xs_ref, wg_ref, wu_ref, wd_ref, ys_ref):
    j = pl.program_id(0)

    @pl.when(j < nu_ref[0])
    def _():
        x = _unpack_rows(_tiles_to_rows(xs_ref, R_BLK)).astype(BF16)
        gate = jnp.dot(x, wg_ref[...], preferred_element_type=F32)
        up = jnp.dot(x, wu_ref[...], preferred_element_type=F32)
        hid = (jax.nn.silu(gate) * up).astype(BF16)
        _rows_to_tiles(ys_ref, _pack_rows(jnp.dot(hid, wd_ref[...], preferred_element_type=F32)))

    @pl.when(j >= nu_ref[0])
    def _():
        ys_ref[...] = jnp.zeros_like(ys_ref)


def _experts(block_e, n_used, xs, w_gate, w_up, w_down):
    d, d_e = w_gate.shape[1:]
    blk_rows = R_BLK * ROW_TILES
    nblk = xs.shape[0] // blk_rows

    def row_map(j, be, nu):
        return (jnp.maximum(jnp.minimum(j, nu[0] - 1), 0), 0)

    def w_map(j, be, nu):
        return (be[j], 0, 0)

    grid_spec = pltpu.PrefetchScalarGridSpec(
        num_scalar_prefetch=2,
        grid=(nblk,),
        in_specs=[pl.BlockSpec((blk_rows, V7X_LANES), row_map),
                  pl.BlockSpec((None, d, d_e), w_map),
                  pl.BlockSpec((None, d, d_e), w_map),
                  pl.BlockSpec((None, d_e, d), w_map)],
        out_specs=pl.BlockSpec((blk_rows, V7X_LANES), lambda j, be, nu: (j, 0)))
    return pl.pallas_call(
        _expert_body,
        grid_spec=grid_spec,
        out_shape=jax.ShapeDtypeStruct(xs.shape, xs.dtype),
        compiler_params=_cparams(("arbitrary",)),
        name="experts",
    )(block_e, n_used, xs, w_gate, w_up, w_down)


def _combine_body(dest0_ref, dest1_ref, dest2_ref, h_ref, route_ref, fw_ref, ys_ref, out_ref,
                  *scratch, last):
    bufs, sem = scratch[:GATHER_RING], scratch[GATHER_RING]
    tm = h_ref.shape[0]
    step = pl.program_id(0)

    def issue(d_ref, s):
        for r in range(tm):
            for k in range(TOP_K):
                pltpu.make_async_copy(_tile_rows(ys_ref, d_ref[k * tm + r]),
                                      _tile_rows(bufs[s].at[k], r), sem.at[s]).start(priority=k)

    def drain(s):
        for k in range(TOP_K):
            pltpu.make_async_copy(ys_ref.at[pl.ds(0, tm * ROW_TILES)], bufs[s].at[k], sem.at[s]).wait()

    @pl.when(step == 0)
    def _():
        issue(dest0_ref, 0)
        issue(dest1_ref, 1)

    for s in range(GATHER_RING):
        @pl.when(step % GATHER_RING == s)
        def _():
            drain(s)
            issue(dest2_ref, (s + 2) % GATHER_RING)
            rec = route_ref[...]
            hh = (h_ref[...] + rec[:, 2:3] * _unpack_rows(_tiles_to_rows(bufs[s].at[0], tm))
                  + rec[:, 3:4] * _unpack_rows(_tiles_to_rows(bufs[s].at[1], tm)))
            out_ref[...] = _rms(hh, fw_ref[...])

    @pl.when(step == last)
    def _():
        drain((last + 1) % GATHER_RING)
        drain((last + 2) % GATHER_RING)


def _combine(dest, h, route, fw, ys):
    n, d = h.shape
    tm = TM_ROWS
    steps = n // tm
    return pl.pallas_call(
        functools.partial(_combine_body, last=steps - 1),
        grid=(steps,),
        in_specs=[pl.BlockSpec((TOP_K * tm,), lambda i: (i,), memory_space=pltpu.SMEM),
                  pl.BlockSpec((TOP_K * tm,), lambda i: (jnp.minimum(i + 1, steps - 1),),
                               memory_space=pltpu.SMEM),
                  pl.BlockSpec((TOP_K * tm,), lambda i: (jnp.minimum(i + 2, steps - 1),),
                               memory_space=pltpu.SMEM),
                  pl.BlockSpec((tm, d), lambda i: (i, 0)),
                  pl.BlockSpec((tm, ROUTE_LANES), lambda i: (i, 0)),
                  pl.BlockSpec((1, d), lambda i: (0, 0)),
                  pl.BlockSpec(memory_space=pl.ANY)],
        out_specs=pl.BlockSpec((tm, d), lambda i: (i, 0)),
        out_shape=jax.ShapeDtypeStruct((n, d), F32),
        scratch_shapes=([pltpu.VMEM((TOP_K, tm * ROW_TILES, V7X_LANES), U32)] * GATHER_RING
                        + [pltpu.SemaphoreType.DMA((GATHER_RING,))]),
        compiler_params=_cparams(("arbitrary",)),
        name="combine",
    )(dest, dest, dest, h, route, fw, ys)


def _plan(route_t, counts, n_experts, n_blocks):
    eid = route_t[0:TOP_K].astype(jnp.int32).reshape(-1, V7X_LANES)
    rank = route_t[4:4 + TOP_K].astype(jnp.int32).reshape(-1, V7X_LANES)
    cnt = counts[:, 0].astype(jnp.int32)
    padded = (cnt + R_BLK - 1) // R_BLK * R_BLK
    pends = jnp.cumsum(padded)
    pstart = pends - padded
    dest = rank
    for e in range(n_experts):
        dest = dest + jnp.where(eid == e, pstart[e], 0)
    n_used = pends[-1] // R_BLK
    blk = jnp.minimum(jnp.arange(n_blocks, dtype=jnp.int32), n_used - 1)
    block_e = jnp.minimum(jnp.sum(pends[None, :] <= (blk * R_BLK)[:, None], axis=1), n_experts - 1)
    return (dest.astype(jnp.int32), block_e.astype(jnp.int32), n_used.reshape(1).astype(jnp.int32),
            pends.astype(jnp.int32), padded.astype(jnp.int32))


def _layer(h3, norm1_w, w_in, s5_a_re, s5_a_im, s5_b_re, s5_b_im, s5_c_re, s5_c_im, s5_d,
           s5_log_dt, s5_w_glu, s5_b_glu, ret_norm_w, w_out, norm2_w, router_group_w,
           router_group_b, router_expert_w, router_expert_b, moe_w_gate, moe_w_up, moe_w_down,
           out_norm_w):
    nb, seq, d = h3.shape
    n = nb * seq
    d_s5 = s5_d.shape[0]
    d_ret = ret_norm_w.shape[0]
    n_experts = moe_w_gate.shape[0]
    x2 = h3.reshape(n, d)

    u_s5, y_ret, moe_bf = _proj_retention(h3, norm1_w.reshape(1, d), w_in.astype(BF16), d_s5, d_ret,
                                          ret_norm_w.reshape(1, d_ret).astype(F32),
                                          (moe_w_gate, moe_w_up, moe_w_down))

    bp, cp, tp, a_tab = _s5_tables(s5_a_re, s5_a_im, s5_b_re, s5_b_im, s5_c_re, s5_c_im,
                                   s5_log_dt, nb, S5_TAU)
    y_s5 = _s5(u_s5.reshape(n, d_s5), nb, bp, cp, tp, a_tab, s5_d.reshape(1, d_s5).astype(F32),
               s5_w_glu.astype(BF16), s5_b_glu.reshape(1, d_s5).astype(F32))

    assert n_experts == N_GROUPS * EXPERTS_PER_GROUP
    wr = (jnp.zeros((d, ROUTE_LANES), F32).at[:, :N_GROUPS].set(router_group_w.astype(F32))
          .at[:, ROUTE_EXPERT_ROW:ROUTE_EXPERT_ROW + n_experts].set(router_expert_w.astype(F32)))
    br = (jnp.zeros((1, ROUTE_LANES), F32).at[0, :N_GROUPS].set(router_group_b.astype(F32))
          .at[0, ROUTE_EXPERT_ROW:ROUTE_EXPERT_ROW + n_experts].set(router_expert_b.astype(F32)))
    wr_hi = wr.astype(BF16)
    wr = jnp.concatenate([wr_hi, (wr - wr_hi.astype(F32)).astype(BF16)], axis=1)
    h, route, route_t, counts = _route(y_s5.reshape(n, d_s5), y_ret.reshape(n, d_ret), x2,
                                       w_out.astype(BF16), norm2_w.reshape(1, d), wr, br)

    n_blocks = (n * TOP_K) // R_BLK + n_experts
    dest, block_e, n_used, pends, padded = _plan(route_t, counts, n_experts, n_blocks)
    dest = dest.reshape(TOP_K, n // TM_ROWS, TM_ROWS).transpose(1, 0, 2).reshape(-1)
    xs = _dispatch(pends, padded, dest, h, norm2_w.reshape(1, d), n_blocks * R_BLK)
    ys = _experts(block_e, n_used, xs, *moe_bf)
    out = _combine(dest, h, route, out_norm_w.reshape(1, d), ys)
    return out.reshape(nb, seq, d)


def kernel(x, norm1_w, w_in, s5_a_re, s5_a_im, s5_b_re, s5_b_im, s5_c_re, s5_c_im, s5_d, s5_log_dt, s5_w_glu, s5_b_glu, ret_norm_w, w_out, norm2_w, router_group_w, router_group_b, router_expert_w, router_expert_b, moe_w_gate, moe_w_up, moe_w_down, final_norm_w):
    depth = norm1_w.shape[0]
    assert depth == 1, "the fused final norm assumes a single layer"
    l = 0
    return _layer(x, norm1_w[l], w_in[l], s5_a_re[l], s5_a_im[l], s5_b_re[l], s5_b_im[l],
                  s5_c_re[l], s5_c_im[l], s5_d[l], s5_log_dt[l], s5_w_glu[l], s5_b_glu[l],
                  ret_norm_w[l], w_out[l], norm2_w[l], router_group_w[l], router_group_b[l],
                  router_expert_w[l], router_expert_b[l], moe_w_gate[l], moe_w_up[l],
                  moe_w_down[l], final_norm_w)
```

```python
import functools

import jax
import jax.numpy as jnp
from jax import lax
from jax.experimental import pallas as pl
from jax.experimental.pallas import tpu as pltpu

F32 = jnp.float32
BF16 = jnp.bfloat16

EPS = 1e-6
ROPE_BASE = 10000.0
RET_HEADS = 8
TOP_K = 2
N_GROUPS = 4
EXPERTS_PER_GROUP = 8

V7X_LANES = 128
V7X_SUBLANES = 8
V7X_VMEM_LIMIT = 56 * 1024 * 1024

T_S5 = 512
S5_TAU = 8
S5_PAD = 8
T_RET = 128
TM_ROUTE = 512
TM_ROWS = 1024
R_BLK = 512
GATHER_RING = 3
ROUTE_LANES = 128
ROUTE_FIELDS = 8
ROUTE_EXPERT_ROW = 8


def _rms(x, w):
    return x * lax.rsqrt(jnp.mean(x * x, axis=-1, keepdims=True) + EPS) * w


def _cparams(sem):
    return pltpu.CompilerParams(dimension_semantics=sem, vmem_limit_bytes=V7X_VMEM_LIMIT)


ROW_TILES = V7X_SUBLANES // 2
U32 = jnp.uint32


def _pack_rows(x):
    half = x.shape[1] // 2

    def bf16_bits(v):
        return lax.bitcast_convert_type(v.astype(BF16).astype(F32), U32)

    return bf16_bits(x[:, half:]) | (bf16_bits(x[:, :half]) >> 16)


def _unpack_rows(w):
    lo = lax.bitcast_convert_type(w << 16, F32)
    hi = lax.bitcast_convert_type(w & U32(0xFFFF0000), F32)
    return jnp.concatenate([lo, hi], axis=1)


def _rows_to_tiles(ref, val):
    rows = val.shape[0]
    for s in range(ROW_TILES):
        ref[pl.ds(s, rows, stride=ROW_TILES), :] = val[:, s * V7X_LANES:(s + 1) * V7X_LANES]


def _tiles_to_rows(ref, rows):
    return jnp.concatenate(
        [ref[pl.ds(s, rows, stride=ROW_TILES), :] for s in range(ROW_TILES)], axis=1)


def _tile_rows(ref, row):
    return ref.at[pl.ds(pl.multiple_of(row * ROW_TILES, ROW_TILES), ROW_TILES)]


def _s5_body(u_ref, bp_ref, cp_ref, tp_ref, a_ref, dd_ref, wglu_ref, bglu_ref, y_ref,
             v_scr, sp_scr, st_scr, io_scr, *, nb, nk, seg, tau):
    @pl.when(pl.program_id(0) == 0)
    def _():
        st_scr[...] = jnp.zeros_like(st_scr)

    nblk = bp_ref.shape[0]
    cw = bp_ref.shape[1] // tau
    sw = bp_ref.shape[2]
    d_s5 = nblk * cw
    tiles = sw // V7X_LANES
    ht = tiles // 2
    ctiles = d_s5 // V7X_LANES
    srows = nb * nk

    u_all = u_ref[...].reshape(srows * tau, d_s5).astype(F32)
    for c in range(ctiles):
        io_scr[c] = u_all[:, c * V7X_LANES:(c + 1) * V7X_LANES]
    u_steps = [jnp.concatenate([io_scr[c, pl.ds(j, srows, stride=tau), :] for c in range(ctiles)],
                               axis=1) for j in range(tau)]
    u_blocks = [jnp.concatenate([u_steps[j][:, blk * cw:(blk + 1) * cw] for j in range(tau)],
                                axis=1).astype(BF16) for blk in range(nblk)]

    for blk in range(nblk):
        half, q = divmod(blk, 2)
        v = jnp.dot(u_blocks[blk], bp_ref[blk], preferred_element_type=F32)
        for b in range(nb):
            for j in range(tiles):
                v_scr[q * tiles + j, pl.ds((half * nb + b) * seg, nk), :] = (
                    v[b * nk:(b + 1) * nk, j * V7X_LANES:(j + 1) * V7X_LANES])

    rows = 2 * nb
    ar = [[a_ref[0, q * ht + i] for i in range(ht)] for q in range(2)]
    ai = [[a_ref[1, q * ht + i] for i in range(ht)] for q in range(2)]
    sr = [[st_scr[q * tiles + i] for i in range(ht)] for q in range(2)]
    si = [[st_scr[q * tiles + ht + i] for i in range(ht)] for q in range(2)]
    for k in range(nk):
        for q in range(2):
            for i in range(ht):
                jr = q * tiles + i
                ji = q * tiles + ht + i
                sp_scr[jr, pl.ds(k, rows, stride=seg), :] = sr[q][i]
                sp_scr[ji, pl.ds(k, rows, stride=seg), :] = si[q][i]
                vr = v_scr[jr, pl.ds(k, rows, stride=seg), :]
                vi = v_scr[ji, pl.ds(k, rows, stride=seg), :]
                nr = ar[q][i] * sr[q][i] - ai[q][i] * si[q][i] + vr
                ni = ar[q][i] * si[q][i] + ai[q][i] * sr[q][i] + vi
                sr[q][i], si[q][i] = nr, ni
    for q in range(2):
        for i in range(ht):
            st_scr[q * tiles + i] = sr[q][i]
            st_scr[q * tiles + ht + i] = si[q][i]

    yb = []
    for blk in range(nblk):
        half, q = divmod(blk, 2)
        sp = jnp.concatenate(
            [jnp.concatenate([sp_scr[q * tiles + j, pl.ds((half * nb + b) * seg, nk), :]
                              for j in range(tiles)], axis=1) for b in range(nb)],
            axis=0).astype(BF16)
        yb.append(jnp.dot(sp, cp_ref[blk], preferred_element_type=F32)
                  + jnp.dot(u_blocks[blk], tp_ref[blk], preferred_element_type=F32))
    for j in range(tau):
        y = jnp.concatenate([yb[blk][:, j * cw:(j + 1) * cw] for blk in range(nblk)], axis=1)
        y = y + dd_ref[...] * u_steps[j]
        y = jax.nn.gelu(y)
        z = jnp.dot(y.astype(BF16), wglu_ref[...], preferred_element_type=F32) + bglu_ref[...]
        out = y * jax.nn.sigmoid(z)
        for c in range(ctiles):
            io_scr[c, pl.ds(j, srows, stride=tau), :] = out[:, c * V7X_LANES:(c + 1) * V7X_LANES]
    y_all = jnp.concatenate([io_scr[c] for c in range(ctiles)], axis=1)
    y_ref[...] = y_all.reshape(nb, nk * tau, d_s5).astype(y_ref.dtype)


def _s5(u_s5, nb, bp, cp, tp, a_tab, dd, wglu_bf, bglu):
    n, d_s5 = u_s5.shape
    seq = n // nb
    tau = S5_TAU
    nk = T_S5 // tau
    seg = nk + S5_PAD
    nblk, _, sw = bp.shape
    rows = 2 * nb
    assert rows == V7X_SUBLANES and nblk == 4

    def whole(a):
        return pl.BlockSpec(a.shape, lambda c: (0,) * a.ndim)

    body = functools.partial(_s5_body, nb=nb, nk=nk, seg=seg, tau=tau)
    out = pl.pallas_call(
        body,
        grid=(seq // T_S5,),
        in_specs=[pl.BlockSpec((nb, T_S5, d_s5), lambda c: (0, c, 0)),
                  whole(bp), whole(cp), whole(tp), whole(a_tab), whole(dd), whole(wglu_bf),
                  whole(bglu)],
        out_specs=pl.BlockSpec((nb, T_S5, d_s5), lambda c: (0, c, 0)),
        out_shape=jax.ShapeDtypeStruct((nb, seq, d_s5), BF16),
        scratch_shapes=[pltpu.VMEM((2 * sw // V7X_LANES, rows * seg, V7X_LANES), F32),
                        pltpu.VMEM((2 * sw // V7X_LANES, rows * seg, V7X_LANES), F32),
                        pltpu.VMEM((2 * sw // V7X_LANES, rows, V7X_LANES), F32),
                        pltpu.VMEM((d_s5 // V7X_LANES, nb * T_S5, V7X_LANES), F32)],
        compiler_params=_cparams(("arbitrary",)),
        name="s5",
    )(u_s5.reshape(nb, seq, d_s5), bp, cp, tp, a_tab, dd, wglu_bf, bglu)
    return out.reshape(n, d_s5)


def _s5_tables(a_re, a_im, b_re, b_im, c_re, c_im, log_dt, nb, tau):
    hp = lax.Precision.HIGHEST
    g, p = a_re.shape
    hch = b_re.shape[2]
    gpb = V7X_LANES // hch
    nblk = g // gpb
    lam_r, lam_i = a_re.astype(F32), a_im.astype(F32)
    dt = jnp.exp(log_dt.astype(F32))[:, None]
    mag = jnp.exp(lam_r * dt)
    ab_r = mag * jnp.cos(lam_i * dt)
    ab_i = mag * jnp.sin(lam_i * dt)
    den = lam_r * lam_r + lam_i * lam_i
    zr = ((ab_r - 1.0) * lam_r + ab_i * lam_i) / den
    zi = (ab_i * lam_r - (ab_r - 1.0) * lam_i) / den
    br_, bi_ = b_re.astype(F32), b_im.astype(F32)
    bb_r = zr[..., None] * br_ - zi[..., None] * bi_
    bb_i = zr[..., None] * bi_ + zi[..., None] * br_
    cr, ci = c_re.astype(F32), c_im.astype(F32)
    pr, pi = [jnp.ones_like(ab_r)], [jnp.zeros_like(ab_i)]
    for _ in range(tau):
        pr, pi = pr + [pr[-1] * ab_r - pi[-1] * ab_i], pi + [pr[-1] * ab_i + pi[-1] * ab_r]
    pw_r, pw_i = jnp.stack(pr), jnp.stack(pi)

    def blockdiag(x):
        r, c = x.shape[-2:]
        x = jnp.tile(x.reshape(tau, nblk, gpb * r, c), (1, 1, 1, gpb))
        same = (jnp.arange(gpb * r)[:, None] // r) == (jnp.arange(gpb * c)[None, :] // c)
        return jnp.where(same, x, 0.0)

    wr_ = jnp.stack([pr[tau - 1 - j] for j in range(tau)])[:, :, None, :]
    wi_ = jnp.stack([pi[tau - 1 - j] for j in range(tau)])[:, :, None, :]
    bt_r, bt_i = jnp.swapaxes(bb_r, 1, 2)[None], jnp.swapaxes(bb_i, 1, 2)[None]
    bp_r = blockdiag(wr_ * bt_r - wi_ * bt_i)
    bp_i = blockdiag(wr_ * bt_i + wi_ * bt_r)
    bp = jnp.concatenate([jnp.concatenate([bp_r[j], bp_i[j]], axis=-1) for j in range(tau)],
                         axis=1).astype(BF16)

    qr_, qi_ = pw_r[1:, :, :, None], pw_i[1:, :, :, None]
    ct_r, ct_i = jnp.swapaxes(cr, 1, 2)[None], jnp.swapaxes(ci, 1, 2)[None]
    cp_r = blockdiag(ct_r * qr_ - ct_i * qi_)
    cp_i = blockdiag(ct_r * qi_ + ct_i * qr_)
    cp = jnp.concatenate([jnp.concatenate([cp_r[i], -cp_i[i]], axis=1) for i in range(tau)],
                         axis=2).astype(BF16)

    ab_r_ = pw_r[:tau, :, :, None] * bb_r[None] - pw_i[:tau, :, :, None] * bb_i[None]
    ab_i_ = pw_r[:tau, :, :, None] * bb_i[None] + pw_i[:tau, :, :, None] * bb_r[None]
    kd = blockdiag(jnp.einsum('ghp,dgpe->dgeh', cr, ab_r_, precision=hp)
                   - jnp.einsum('ghp,dgpe->dgeh', ci, ab_i_, precision=hp))
    kzero = jnp.zeros_like(kd[0])
    tp = jnp.concatenate(
        [jnp.concatenate([kd[i - j] if i >= j else kzero for i in range(tau)], axis=2)
         for j in range(tau)], axis=1).astype(BF16)

    a_tab = jnp.stack([jnp.repeat(pw_r[tau].reshape(2, -1), nb, axis=0),
                       jnp.repeat(pw_i[tau].reshape(2, -1), nb, axis=0)])
    a_tab = a_tab.reshape(2, 2 * nb, -1, V7X_LANES).transpose(0, 2, 1, 3)
    return bp, cp, tp, a_tab


def _proj_ret_body(x_ref, n1_ref, w_ref, cos_ref, sin_ref, dec_ref, qdec_ref, kdect_ref,
                   cdec_ref, ms_ref, avg_ref, nw_ref, wg_ref, wu_ref, wd_ref,
                   us5_ref, y_ref, wg_out, wu_out, wd_out, u_even, u_odd, st_scr,
                   *, t_len, dh, scale):
    wg_out[...] = wg_ref[...].astype(wg_out.dtype)
    wu_out[...] = wu_ref[...].astype(wu_out.dtype)
    wd_out[...] = wd_ref[...].astype(wd_out.dtype)

    step = pl.program_id(0)
    nb = x_ref.shape[0]
    d_s5 = us5_ref.shape[-1]

    @pl.when(step == 0)
    def _():
        u_odd[...] = jnp.zeros_like(u_odd)
        st_scr[...] = jnp.zeros_like(st_scr)

    def project(u_out):
        xn = _rms(x_ref[...].reshape(nb * t_len, x_ref.shape[-1]), n1_ref[...]).astype(BF16)
        u = jnp.dot(xn, w_ref[...], preferred_element_type=F32)
        us5_ref[...] = u[:, :d_s5].reshape(nb, t_len, d_s5).astype(us5_ref.dtype)
        u_out[...] = u[:, d_s5:].astype(u_out.dtype)

    for parity, (u_out, u_in) in enumerate(((u_even, u_odd), (u_odd, u_even))):
        @pl.when(step % 2 == parity)
        def _():
            project(u_out)
            _ret_chunk(u_in, cos_ref, sin_ref, dec_ref, qdec_ref, kdect_ref, cdec_ref, ms_ref, avg_ref,
                       nw_ref, y_ref, st_scr, t_len=t_len, dh=dh, scale=scale)


def _ret_chunk(u_ref, cos_ref, sin_ref, dec_ref, qdec_ref, kdect_ref, cdec_ref, ms_ref, avg_ref,
               nw_ref, y_ref, st_scr, *, t_len, dh, scale):
    nb = y_ref.shape[0]
    width = y_ref.shape[-1]
    pairs = width // V7X_LANES
    reps = width // cos_ref.shape[-1]
    cos = jnp.concatenate([cos_ref[...]] * reps, axis=1)
    sin = jnp.concatenate([sin_ref[...]] * reps, axis=1)
    lane = lax.broadcasted_iota(jnp.int32, (t_len, width), 1)
    first = (lane % dh) < (dh // 2)

    def rot(x):
        x = x.astype(F32)
        partner = jnp.where(first, pltpu.roll(x, width - dh // 2, 1), pltpu.roll(x, dh // 2, 1))
        return x * cos + partner * sin

    def group_mean(x):
        return jnp.dot(x.astype(BF16), avg_ref[...], preferred_element_type=F32)

    zero_k = jnp.zeros((dh, t_len), BF16)
    zero_v = jnp.zeros((t_len, V7X_LANES), BF16)
    low_head = lax.broadcasted_iota(jnp.int32, (t_len, V7X_LANES), 1) < dh

    outs = []
    for b in range(nb):
        rows = slice(b * t_len, (b + 1) * t_len)
        q = rot(u_ref[rows, 0:width])
        k = rot(u_ref[rows, width:2 * width]) * scale
        vb = u_ref[rows, 2 * width:3 * width]
        kt = k.T
        ktb = kt.astype(BF16)
        qb = q.astype(BF16)
        inners = []
        for p in range(pairs):
            ps = slice(p * V7X_LANES, (p + 1) * V7X_LANES)
            k_lo, k_hi = ktb[p * V7X_LANES:p * V7X_LANES + dh], ktb[p * V7X_LANES + dh:(p + 1) * V7X_LANES]
            kbd = jnp.concatenate([jnp.concatenate([k_lo, zero_k], axis=0),
                                   jnp.concatenate([zero_k, k_hi], axis=0)], axis=1)
            sc = (jnp.dot(qb[:, ps], kbd, preferred_element_type=F32)
                  * dec_ref[:, 2 * p * t_len:2 * (p + 1) * t_len])
            vt = vb[:, ps]
            vbd = jnp.concatenate([jnp.where(low_head, vt, zero_v),
                                   jnp.where(low_head, zero_v, vt)], axis=0)
            inners.append(jnp.dot(sc.astype(BF16), vbd, preferred_element_type=F32))
        inner = jnp.concatenate(inners, axis=1)
        qd = (q * qdec_ref[...]).astype(BF16)
        kdt = (kt * kdect_ref[...]).astype(BF16)
        crosses = []
        for p in range(pairs):
            ps = slice(p * V7X_LANES, (p + 1) * V7X_LANES)
            state = st_scr[b, p]
            crosses.append(jnp.dot(qd[:, ps], state.astype(BF16), preferred_element_type=F32))
            kv = jnp.dot(kdt[ps, :], vb[:, ps], preferred_element_type=F32)
            st_scr[b, p] = state * cdec_ref[:, ps] + kv * ms_ref[...]
        outs.append(inner + jnp.concatenate(crosses, axis=1))

    o = jnp.concatenate(outs, axis=0)
    dlt = o - group_mean(o)
    on = dlt * lax.rsqrt(group_mean(dlt * dlt) + EPS) * nw_ref[...]
    gate = u_ref[:, 3 * width:4 * width].astype(F32)
    y_ref[...] = (jax.nn.silu(gate) * on).reshape(nb, t_len, width).astype(y_ref.dtype)


def _proj_retention(x3, n1, w_bf, d_s5, d_ret, norm_w, moe_w):
    nb, seq, d = x3.shape
    dh = d_ret // RET_HEADS
    half = dh // 2
    t_len = T_RET
    assert V7X_LANES == 2 * dh
    lg = jnp.log(1.0 - 2.0 ** (-5.0 - jnp.arange(RET_HEADS, dtype=F32)))
    t = jnp.arange(t_len, dtype=F32)
    diff = t[:, None] - t[None, :]
    dec = jnp.where(diff >= 0, jnp.exp(lg[:, None, None] * jnp.maximum(diff, 0.0)), 0.0)
    dec_all = dec.transpose(1, 0, 2).reshape(t_len, RET_HEADS * t_len)
    qdec = jnp.repeat(jnp.exp(lg[:, None] * (t + 1.0)[None, :]).T, dh, axis=1)
    kdect = jnp.repeat(jnp.exp(lg[:, None] * (t_len - 1 - t)[None, :]), dh, axis=0)
    cdec = jnp.repeat(jnp.exp(lg * t_len), dh)[None, :]
    head_of_lane = jnp.arange(d_ret) // dh
    pair_head = jnp.arange(V7X_LANES) // dh
    mask_s = (pair_head[:, None] == pair_head[None, :]).astype(F32)
    avg = (head_of_lane[:, None] == head_of_lane[None, :]).astype(F32) / dh
    assert dh & (dh - 1) == 0
    avg = avg.astype(BF16)
    inv = ROPE_BASE ** (-jnp.arange(half, dtype=F32) / half)
    ang = jnp.arange(seq, dtype=F32)[:, None] * inv[None, :]
    reps = V7X_LANES // dh
    cos_t = jnp.tile(jnp.cos(ang), (1, 2 * reps))
    sin_t = jnp.tile(jnp.concatenate([-jnp.sin(ang), jnp.sin(ang)], axis=1), (1, reps))

    def whole(a):
        return pl.BlockSpec(a.shape, lambda c: (0,) * a.ndim)

    nc = seq // t_len

    def proj_chunk(c):
        return jnp.minimum(c, nc - 1)

    def ret_chunk(c):
        return jnp.maximum(c - 1, 0)

    def sliced(w):
        e, rows, cols = w.shape
        if nc >= e:
            parts = nc // e
            assert nc % e == 0 and rows % (parts * V7X_SUBLANES) == 0
            return w.reshape(e * parts, rows // parts, cols), 1
        assert e % nc == 0
        return w, e // nc

    moe_in, moe_specs, moe_shapes = [], [], []
    for w in moe_w:
        ws, per_step = sliced(w)
        blk = (per_step,) + ws.shape[1:]
        moe_in.append(ws)
        moe_specs.append(pl.BlockSpec(blk, lambda c: (proj_chunk(c), 0, 0)))
        moe_shapes.append(jax.ShapeDtypeStruct(ws.shape, BF16))

    body = functools.partial(_proj_ret_body, t_len=t_len, dh=dh, scale=dh ** -0.5)
    outs = pl.pallas_call(
        body,
        grid=(nc + 1,),
        in_specs=[pl.BlockSpec((nb, t_len, d), lambda c: (0, proj_chunk(c), 0)),
                  whole(n1), whole(w_bf),
                  pl.BlockSpec((t_len, V7X_LANES), lambda c: (ret_chunk(c), 0)),
                  pl.BlockSpec((t_len, V7X_LANES), lambda c: (ret_chunk(c), 0)),
                  whole(dec_all), whole(qdec), whole(kdect), whole(cdec),
                  whole(mask_s), whole(avg), whole(norm_w)] + moe_specs,
        out_specs=[pl.BlockSpec((nb, t_len, d_s5), lambda c: (0, proj_chunk(c), 0)),
                   pl.BlockSpec((nb, t_len, d_ret), lambda c: (0, ret_chunk(c), 0))] + moe_specs,
        out_shape=[jax.ShapeDtypeStruct((nb, seq, d_s5), BF16),
                   jax.ShapeDtypeStruct((nb, seq, d_ret), BF16)] + moe_shapes,
        scratch_shapes=[pltpu.VMEM((nb * t_len, 4 * d_ret), BF16),
                        pltpu.VMEM((nb * t_len, 4 * d_ret), BF16),
                        pltpu.VMEM((nb, d_ret // V7X_LANES, V7X_LANES, V7X_LANES), F32)],
        compiler_params=_cparams(("arbitrary",)),
        name="inproj_retention",
    )(x3, n1, w_bf, cos_t, sin_t, dec_all, qdec, kdect, cdec, mask_s, avg, norm_w, *moe_in)
    u_s5, y_ret = outs[:2]
    moe_bf = [o.reshape(w.shape) for o, w in zip(outs[2:], moe_w)]
    return u_s5, y_ret, moe_bf


def _route_body(ys5_ref, yret_ref, x_ref, wo_ref, n2_ref, wr_ref, br_ref, tri_ref,
                h_ref, route_ref, route_t_ref, cnt_ref, carry_scr):
    @pl.when(pl.program_id(0) == 0)
    def _():
        carry_scr[...] = jnp.zeros_like(carry_scr)

    d_s5 = ys5_ref.shape[1]
    h = (x_ref[...]
         + jnp.dot(ys5_ref[...], wo_ref[0:d_s5], preferred_element_type=F32)
         + jnp.dot(yret_ref[...], wo_ref[d_s5:], preferred_element_type=F32))
    h_ref[...] = h
    hn = _rms(h, n2_ref[...])
    hi = hn.astype(BF16)
    lo = (hn - hi.astype(F32)).astype(BF16)
    p_hi = jnp.dot(hi, wr_ref[...], preferred_element_type=F32)
    p_lo = jnp.dot(lo, wr_ref[:, 0:ROUTE_LANES], preferred_element_type=F32)
    logits = p_hi[:, 0:ROUTE_LANES] + p_hi[:, ROUTE_LANES:] + p_lo + br_ref[...]
    tm = logits.shape[0]
    lt = logits.T
    epg = EXPERTS_PER_GROUP
    row = lax.broadcasted_iota(jnp.int32, (epg, tm), 0)
    rowf = row.astype(F32)
    neg = -jnp.inf
    big = float(epg)
    gl = jnp.where(row < N_GROUPS, lt[0:epg], neg)
    gmax = jnp.max(gl, axis=0, keepdims=True)
    gidx = jnp.min(jnp.where(gl == gmax, rowf, big), axis=0, keepdims=True)
    g_w = 1.0 / jnp.sum(jnp.exp(gl - gmax), axis=0, keepdims=True)
    el = jnp.full((epg, tm), neg, F32)
    for g in range(N_GROUPS):
        el = jnp.where(gidx == g, lt[ROUTE_EXPERT_ROW + g * epg:ROUTE_EXPERT_ROW + (g + 1) * epg], el)
    v0 = jnp.max(el, axis=0, keepdims=True)
    i0 = jnp.min(jnp.where(el == v0, rowf, big), axis=0, keepdims=True)
    el2 = jnp.where(rowf == i0, neg, el)
    v1 = jnp.max(el2, axis=0, keepdims=True)
    i1 = jnp.min(jnp.where(el2 == v1, rowf, big), axis=0, keepdims=True)
    e = jnp.exp(v1 - v0)
    den = 1.0 + e
    w0 = (1.0 / den) * g_w
    w1 = (e / den) * g_w
    eid0 = gidx * epg + i0
    eid1 = gidx * epg + i1
    n_exp = N_GROUPS * epg
    erow = lax.broadcasted_iota(jnp.int32, (n_exp, tm), 0).astype(F32)
    sel0 = erow == eid0
    sel1 = erow == eid1
    onehot = jnp.where(sel0 | sel1, 1.0, 0.0)
    before = jnp.dot(onehot.astype(BF16), tri_ref[...], preferred_element_type=F32) + carry_scr[...]
    r0 = jnp.sum(jnp.where(sel0, before, 0.0), axis=0, keepdims=True)
    r1 = jnp.sum(jnp.where(sel1, before, 0.0), axis=0, keepdims=True)
    carry_scr[...] += jnp.sum(onehot, axis=1, keepdims=True)
    cnt_ref[...] = jnp.broadcast_to(carry_scr[...], cnt_ref.shape)
    rec_t = jnp.zeros((ROUTE_FIELDS, tm), F32)
    for j, val in enumerate((eid0, eid1, w0, w1, r0, r1)):
        rec_t = jnp.where(row == j, val, rec_t)
    route_t_ref[...] = rec_t
    route_ref[...] = jnp.concatenate(
        [rec_t, jnp.zeros((ROUTE_LANES - ROUTE_FIELDS, tm), F32)], axis=0).T


def _route(ys5, yret, x2, wo_bf, n2, wr, br):
    n, d = x2.shape
    d_s5 = ys5.shape[1]
    d_ret = yret.shape[1]
    tm = TM_ROUTE
    n_exp = N_GROUPS * EXPERTS_PER_GROUP
    tri = (jnp.arange(tm)[:, None] < jnp.arange(tm)[None, :]).astype(BF16)
    return pl.pallas_call(
        _route_body,
        grid=(n // tm,),
        in_specs=[pl.BlockSpec((tm, d_s5), lambda i: (i, 0)),
                  pl.BlockSpec((tm, d_ret), lambda i: (i, 0)),
                  pl.BlockSpec((tm, d), lambda i: (i, 0)),
                  pl.BlockSpec((d_s5 + d_ret, d), lambda i: (0, 0)),
                  pl.BlockSpec((1, d), lambda i: (0, 0)),
                  pl.BlockSpec((d, 2 * ROUTE_LANES), lambda i: (0, 0)),
                  pl.BlockSpec((1, ROUTE_LANES), lambda i: (0, 0)),
                  pl.BlockSpec((tm, tm), lambda i: (0, 0))],
        out_specs=[pl.BlockSpec((tm, d), lambda i: (i, 0)),
                   pl.BlockSpec((tm, ROUTE_LANES), lambda i: (i, 0)),
                   pl.BlockSpec((ROUTE_FIELDS, tm), lambda i: (0, i)),
                   pl.BlockSpec((n_exp, ROUTE_LANES), lambda i: (0, 0))],
        out_shape=[jax.ShapeDtypeStruct((n, d), F32),
                   jax.ShapeDtypeStruct((n, ROUTE_LANES), F32),
                   jax.ShapeDtypeStruct((ROUTE_FIELDS, n), F32),
                   jax.ShapeDtypeStruct((n_exp, ROUTE_LANES), F32)],
        scratch_shapes=[pltpu.VMEM((n_exp, 1), F32)],
        compiler_params=_cparams(("arbitrary",)),
        name="outproj_route",
    )(ys5, yret, x2, wo_bf, n2, wr, br, tri)


def _dispatch_body(pends_ref, padded_ref, dest_ref, h_ref, n2_ref, xs_ref,
                   hn_scr, zero_scr, sem, zsem):
    tm = h_ref.shape[0]

    @pl.when(pl.program_id(0) == 0)
    def _():
        zero_scr[...] = jnp.zeros_like(zero_scr)

        def zero_copy(e):
            first = pl.multiple_of((pends_ref[e] - R_BLK) * ROW_TILES, R_BLK * ROW_TILES)
            return pltpu.make_async_copy(zero_scr, xs_ref.at[pl.ds(first, R_BLK * ROW_TILES)], zsem)

        def zstart(e, carry):
            @pl.when(padded_ref[e] > 0)
            def _():
                zero_copy(e).start()
            return carry

        def zwait(e, carry):
            @pl.when(padded_ref[e] > 0)
            def _():
                zero_copy(e).wait()
            return carry

        lax.fori_loop(0, pends_ref.shape[0], zstart, 0)
        lax.fori_loop(0, pends_ref.shape[0], zwait, 0)

    step = pl.program_id(0)
    slot = step % 2
    buf = hn_scr.at[slot]
    _rows_to_tiles(buf, _pack_rows(_rms(h_ref[...], n2_ref[...])))

    for r in range(tm):
        for k in range(TOP_K):
            pltpu.make_async_copy(_tile_rows(buf, r), _tile_rows(xs_ref, dest_ref[k * tm + r]),
                                  sem.at[slot]).start(priority=k)

    def drain(s):
        for k in range(TOP_K):
            pltpu.make_async_copy(hn_scr.at[s], xs_ref.at[pl.ds(0, tm * ROW_TILES)], sem.at[s]).wait()

    @pl.when(step > 0)
    def _():
        drain(1 - slot)

    @pl.when(step == pl.num_programs(0) - 1)
    def _():
        drain(slot)


def _dispatch(pends, padded, dest, h, n2, p_rows):
    n, d = h.shape
    tm = TM_ROWS
    grid_spec = pltpu.PrefetchScalarGridSpec(
        num_scalar_prefetch=2,
        grid=(n // tm,),
        in_specs=[pl.BlockSpec((TOP_K * tm,), lambda i, pe, pa: (i,), memory_space=pltpu.SMEM),
                  pl.BlockSpec((tm, d), lambda i, pe, pa: (i, 0)),
                  pl.BlockSpec((1, d), lambda i, pe, pa: (0, 0))],
        out_specs=pl.BlockSpec(memory_space=pl.ANY),
        scratch_shapes=[pltpu.VMEM((2, tm * ROW_TILES, V7X_LANES), U32),
                        pltpu.VMEM((R_BLK * ROW_TILES, V7X_LANES), U32),
                        pltpu.SemaphoreType.DMA((2,)), pltpu.SemaphoreType.DMA(())])
    assert d == 2 * ROW_TILES * V7X_LANES
    return pl.pallas_call(
        _dispatch_body,
        grid_spec=grid_spec,
        out_shape=jax.ShapeDtypeStruct((p_rows * ROW_TILES, V7X_LANES), U32),
        compiler_params=_cparams(("arbitrary",)),
        name="dispatch",
    )(pends, padded, dest, h, n2)


def _expert_body(be_ref, nu_ref, xs_ref, wg_ref, wu_ref, wd_ref, ys_ref):
    j = pl.program_id(0)

    @pl.when(j < nu_ref[0])
    def _():
        x = _unpack_rows(_tiles_to_rows(xs_ref, R_BLK)).astype(BF16)
        gate = jnp.dot(x, wg_ref[...], preferred_element_type=F32)
        up = jnp.dot(x, wu_ref[...], preferred_element_type=F32)
        hid = (jax.nn.silu(gate) * up).astype(BF16)
        _rows_to_tiles(ys_ref, _pack_rows(jnp.dot(hid, wd_ref[...], preferred_element_type=F32)))

    @pl.when(j >= nu_ref[0])
    def _():
        ys_ref[...] = jnp.zeros_like(ys_ref)


def _experts(block_e, n_used, xs, w_gate, w_up, w_down):
    d, d_e = w_gate.shape[1:]
    blk_rows = R_BLK * ROW_TILES
    nblk = xs.shape[0] // blk_rows

    def row_map(j, be, nu):
        return (jnp.maximum(jnp.minimum(j, nu[0] - 1), 0), 0)

    def w_map(j, be, nu):
        return (be[j], 0, 0)

    grid_spec = pltpu.PrefetchScalarGridSpec(
        num_scalar_prefetch=2,
        grid=(nblk,),
        in_specs=[pl.BlockSpec((blk_rows, V7X_LANES), row_map),
                  pl.BlockSpec((None, d, d_e), w_map),
                  pl.BlockSpec((None, d, d_e), w_map),
                  pl.BlockSpec((None, d_e, d), w_map)],
        out_specs=pl.BlockSpec((blk_rows, V7X_LANES), lambda j, be, nu: (j, 0)))
    return pl.pallas_call(
        _expert_body,
        grid_spec=grid_spec,
        out_shape=jax.ShapeDtypeStruct(xs.shape, xs.dtype),
        compiler_params=_cparams(("arbitrary",)),
        name="experts",
    )(block_e, n_used, xs, w_gate, w_up, w_down)


def _combine_body(dest0_ref, dest1_ref, dest2_ref, h_ref, route_ref, fw_ref, ys_ref, out_ref,
                  *scratch, last):
    bufs, sem = scratch[:GATHER_RING], scratch[GATHER_RING]
    tm = h_ref.shape[0]
    step = pl.program_id(0)

    def issue(d_ref, s):
        for r in range(tm):
            for k in range(TOP_K):
                pltpu.make_async_copy(_tile_rows(ys_ref, d_ref[k * tm + r]),
                                      _tile_rows(bufs[s].at[k], r), sem.at[s]).start(priority=k)

    def drain(s):
        for k in range(TOP_K):
            pltpu.make_async_copy(ys_ref.at[pl.ds(0, tm * ROW_TILES)], bufs[s].at[k], sem.at[s]).wait()

    @pl.when(step == 0)
    def _():
        issue(dest0_ref, 0)
        issue(dest1_ref, 1)

    for s in range(GATHER_RING):
        @pl.when(step % GATHER_RING == s)
        def _():
            drain(s)
            issue(dest2_ref, (s + 2) % GATHER_RING)
            rec = route_ref[...]
            hh = (h_ref[...] + rec[:, 2:3] * _unpack_rows(_tiles_to_rows(bufs[s].at[0], tm))
                  + rec[:, 3:4] * _unpack_rows(_tiles_to_rows(bufs[s].at[1], tm)))
            out_ref[...] = _rms(hh, fw_ref[...])

    @pl.when(step == last)
    def _():
        drain((last + 1) % GATHER_RING)
        drain((last + 2) % GATHER_RING)


def _combine(dest, h, route, fw, ys):
    n, d = h.shape
    tm = TM_ROWS
    steps = n // tm
    return pl.pallas_call(
        functools.partial(_combine_body, last=steps - 1),
        grid=(steps,),
        in_specs=[pl.BlockSpec((TOP_K * tm,), lambda i: (i,), memory_space=pltpu.SMEM),
                  pl.BlockSpec((TOP_K * tm,), lambda i: (jnp.minimum(i + 1, steps - 1),),
                               memory_space=pltpu.SMEM),
                  pl.BlockSpec((TOP_K * tm,), lambda i: (jnp.minimum(i + 2, steps - 1),),
                               memory_space=pltpu.SMEM),
                  pl.BlockSpec((tm, d), lambda i: (i, 0)),
                  pl.BlockSpec((tm, ROUTE_LANES), lambda i: (i, 0)),
                  pl.BlockSpec((1, d), lambda i: (0, 0)),
                  pl.BlockSpec(memory_space=pl.ANY)],
        out_specs=pl.BlockSpec((tm, d), lambda i: (i, 0)),
        out_shape=jax.ShapeDtypeStruct((n, d), F32),
        scratch_shapes=([pltpu.VMEM((TOP_K, tm * ROW_TILES, V7X_LANES), U32)] * GATHER_RING
                        + [pltpu.SemaphoreType.DMA((GATHER_RING,))]),
        compiler_params=_cparams(("arbitrary",)),
        name="combine",
    )(dest, dest, dest, h, route, fw, ys)


def _plan(route_t, counts, n_experts, n_blocks):
    eid = route_t[0:TOP_K].astype(jnp.int32).reshape(-1, V7X_LANES)
    rank = route_t[4:4 + TOP_K].astype(jnp.int32).reshape(-1, V7X_LANES)
    cnt = counts[:, 0].astype(jnp.int32)
    padded = (cnt + R_BLK - 1) // R_BLK * R_BLK
    pends = jnp.cumsum(padded)
    pstart = pends - padded
    dest = rank
    for e in range(n_experts):
        dest = dest + jnp.where(eid == e, pstart[e], 0)
    n_used = pends[-1] // R_BLK
    blk = jnp.minimum(jnp.arange(n_blocks, dtype=jnp.int32), n_used - 1)
    block_e = jnp.minimum(jnp.sum(pends[None, :] <= (blk * R_BLK)[:, None], axis=1), n_experts - 1)
    return (dest.astype(jnp.int32), block_e.astype(jnp.int32), n_used.reshape(1).astype(jnp.int32),
            pends.astype(jnp.int32), padded.astype(jnp.int32))


def _layer(h3, norm1_w, w_in, s5_a_re, s5_a_im, s5_b_re, s5_b_im, s5_c_re, s5_c_im, s5_d,
           s5_log_dt, s5_w_glu, s5_b_glu, ret_norm_w, w_out, norm2_w, router_group_w,
           router_group_b, router_expert_w, router_expert_b, moe_w_gate, moe_w_up, moe_w_down,
           out_norm_w):
    nb, seq, d = h3.shape
    n = nb * seq
    d_s5 = s5_d.shape[0]
    d_ret = ret_norm_w.shape[0]
    n_experts = moe_w_gate.shape[0]
    x2 = h3.reshape(n, d)

    u_s5, y_ret, moe_bf = _proj_retention(h3, norm1_w.reshape(1, d), w_in.astype(BF16), d_s5, d_ret,
                                          ret_norm_w.reshape(1, d_ret).astype(F32),
                                          (moe_w_gate, moe_w_up, moe_w_down))

    bp, cp, tp, a_tab = _s5_tables(s5_a_re, s5_a_im, s5_b_re, s5_b_im, s5_c_re, s5_c_im,
                                   s5_log_dt, nb, S5_TAU)
    y_s5 = _s5(u_s5.reshape(n, d_s5), nb, bp, cp, tp, a_tab, s5_d.reshape(1, d_s5).astype(F32),
               s5_w_glu.astype(BF16), s5_b_glu.reshape(1, d_s5).astype(F32))

    assert n_experts == N_GROUPS * EXPERTS_PER_GROUP
    wr = (jnp.zeros((d, ROUTE_LANES), F32).at[:, :N_GROUPS].set(router_group_w.astype(F32))
          .at[:, ROUTE_EXPERT_ROW:ROUTE_EXPERT_ROW + n_experts].set(router_expert_w.astype(F32)))
    br = (jnp.zeros((1, ROUTE_LANES), F32).at[0, :N_GROUPS].set(router_group_b.astype(F32))
          .at[0, ROUTE_EXPERT_ROW:ROUTE_EXPERT_ROW + n_experts].set(router_expert_b.astype(F32)))
    wr_hi = wr.astype(BF16)
    wr = jnp.concatenate([wr_hi, (wr - wr_hi.astype(F32)).astype(BF16)], axis=1)
    h, route, route_t, counts = _route(y_s5.reshape(n, d_s5), y_ret.reshape(n, d_ret), x2,
                                       w_out.astype(BF16), norm2_w.reshape(1, d), wr, br)

    n_blocks = (n * TOP_K) // R_BLK + n_experts
    dest, block_e, n_used, pends, padded = _plan(route_t, counts, n_experts, n_blocks)
    dest = dest.reshape(TOP_K, n // TM_ROWS, TM_ROWS).transpose(1, 0, 2).reshape(-1)
    xs = _dispatch(pends, padded, dest, h, norm2_w.reshape(1, d), n_blocks * R_BLK)
    ys = _experts(block_e, n_used, xs, *moe_bf)
    out = _combine(dest, h, route, out_norm_w.reshape(1, d), ys)
    return out.reshape(nb, seq, d)


def kernel(x, norm1_w, w_in, s5_a_re, s5_a_im, s5_b_re, s5_b_im, s5_c_re, s5_c_im, s5_d, s5_log_dt, s5_w_glu, s5_b_glu, ret_norm_w, w_out, norm2_w, router_group_w, router_group_b, router_expert_w, router_expert_b, moe_w_gate, moe_w_up, moe_w_down, final_norm_w):
    depth = norm1_w.shape[0]
    assert depth == 1, "the fused final norm assumes a single layer"
    l = 0
    return _layer(x, norm1_w[l], w_in[l], s5_a_re[l], s5_a_im[l], s5_b_re[l], s5_b_im[l],
                  s5_c_re[l], s5_c_im[l], s5_d[l], s5_log_dt[l], s5_w_glu[l], s5_b_glu[l],
                  ret_norm_w[l], w_out[l], norm2_w[l], router_group_w[l], router_group_b[l],
                  router_expert_w[l], router_expert_b[l], moe_w_gate[l], moe_w_up[l],
                  moe_w_down[l], final_norm_w)
```

```python
import functools

import jax
import jax.numpy as jnp
from jax import lax
from jax.experimental import pallas as pl
from jax.experimental.pallas import tpu as pltpu

F32 = jnp.float32
BF16 = jnp.bfloat16

EPS = 1e-6
ROPE_BASE = 10000.0
RET_HEADS = 8
TOP_K = 2
N_GROUPS = 4
EXPERTS_PER_GROUP = 8

V7X_LANES = 128
V7X_SUBLANES = 8
V7X_VMEM_LIMIT = 56 * 1024 * 1024

T_S5 = 512
S5_TAU = 8
S5_PAD = 8
T_RET = 128
TM_ROUTE = 1024
TM_ROWS = 512
R_BLK = 512
GATHER_RING = 3
ROUTE_LANES = 128
ROUTE_FIELDS = 8
ROUTE_EXPERT_ROW = 8


def _rms(x, w):
    return x * lax.rsqrt(jnp.mean(x * x, axis=-1, keepdims=True) + EPS) * w


def _cparams(sem):
    return pltpu.CompilerParams(dimension_semantics=sem, vmem_limit_bytes=V7X_VMEM_LIMIT)


ROW_TILES = V7X_SUBLANES // 2
U32 = jnp.uint32


def _pack_rows(x):
    half = x.shape[1] // 2

    def bf16_bits(v):
        return lax.bitcast_convert_type(v.astype(BF16).astype(F32), U32)

    return bf16_bits(x[:, half:]) | (bf16_bits(x[:, :half]) >> 16)


def _unpack_rows(w):
    lo = lax.bitcast_convert_type(w << 16, F32)
    hi = lax.bitcast_convert_type(w & U32(0xFFFF0000), F32)
    return jnp.concatenate([lo, hi], axis=1)


def _rows_to_tiles(ref, val):
    rows = val.shape[0]
    for s in range(ROW_TILES):
        ref[pl.ds(s, rows, stride=ROW_TILES), :] = val[:, s * V7X_LANES:(s + 1) * V7X_LANES]


def _tiles_to_rows(ref, rows):
    return jnp.concatenate(
        [ref[pl.ds(s, rows, stride=ROW_TILES), :] for s in range(ROW_TILES)], axis=1)


def _tile_rows(ref, row):
    return ref.at[pl.ds(pl.multiple_of(row * ROW_TILES, ROW_TILES), ROW_TILES)]


def _s5_body(u_ref, bp_ref, cp_ref, tp_ref, a_ref, dd_ref, wglu_ref, bglu_ref, y_ref,
             v_scr, sp_scr, st_scr, io_scr, *, nb, nk, seg, tau):
    @pl.when(pl.program_id(0) == 0)
    def _():
        st_scr[...] = jnp.zeros_like(st_scr)

    nblk = bp_ref.shape[0]
    cw = bp_ref.shape[1] // tau
    sw = bp_ref.shape[2]
    d_s5 = nblk * cw
    tiles = sw // V7X_LANES
    ht = tiles // 2
    ctiles = d_s5 // V7X_LANES
    srows = nb * nk

    u_all = u_ref[...].reshape(srows * tau, d_s5).astype(F32)
    for c in range(ctiles):
        io_scr[c] = u_all[:, c * V7X_LANES:(c + 1) * V7X_LANES]
    u_steps = [jnp.concatenate([io_scr[c, pl.ds(j, srows, stride=tau), :] for c in range(ctiles)],
                               axis=1) for j in range(tau)]
    u_blocks = [jnp.concatenate([u_steps[j][:, blk * cw:(blk + 1) * cw] for j in range(tau)],
                                axis=1).astype(BF16) for blk in range(nblk)]

    for blk in range(nblk):
        half, q = divmod(blk, 2)
        v = jnp.dot(u_blocks[blk], bp_ref[blk], preferred_element_type=F32)
        for b in range(nb):
            for j in range(tiles):
                v_scr[q * tiles + j, pl.ds((half * nb + b) * seg, nk), :] = (
                    v[b * nk:(b + 1) * nk, j * V7X_LANES:(j + 1) * V7X_LANES])

    rows = 2 * nb
    ar = [[a_ref[0, q * ht + i] for i in range(ht)] for q in range(2)]
    ai = [[a_ref[1, q * ht + i] for i in range(ht)] for q in range(2)]
    sr = [[st_scr[q * tiles + i] for i in range(ht)] for q in range(2)]
    si = [[st_scr[q * tiles + ht + i] for i in range(ht)] for q in range(2)]
    for k in range(nk):
        for q in range(2):
            for i in range(ht):
                jr = q * tiles + i
                ji = q * tiles + ht + i
                sp_scr[jr, pl.ds(k, rows, stride=seg), :] = sr[q][i]
                sp_scr[ji, pl.ds(k, rows, stride=seg), :] = si[q][i]
                vr = v_scr[jr, pl.ds(k, rows, stride=seg), :]
                vi = v_scr[ji, pl.ds(k, rows, stride=seg), :]
                nr = ar[q][i] * sr[q][i] - ai[q][i] * si[q][i] + vr
                ni = ar[q][i] * si[q][i] + ai[q][i] * sr[q][i] + vi
                sr[q][i], si[q][i] = nr, ni
    for q in range(2):
        for i in range(ht):
            st_scr[q * tiles + i] = sr[q][i]
            st_scr[q * tiles + ht + i] = si[q][i]

    yb = []
    for blk in range(nblk):
        half, q = divmod(blk, 2)
        sp = jnp.concatenate(
            [jnp.concatenate([sp_scr[q * tiles + j, pl.ds((half * nb + b) * seg, nk), :]
                              for j in range(tiles)], axis=1) for b in range(nb)],
            axis=0).astype(BF16)
        yb.append(jnp.dot(sp, cp_ref[blk], preferred_element_type=F32)
                  + jnp.dot(u_blocks[blk], tp_ref[blk], preferred_element_type=F32))
    for j in range(tau):
        y = jnp.concatenate([yb[blk][:, j * cw:(j + 1) * cw] for blk in range(nblk)], axis=1)
        y = y + dd_ref[...] * u_steps[j]
        y = jax.nn.gelu(y)
        z = jnp.dot(y.astype(BF16), wglu_ref[...], preferred_element_type=F32) + bglu_ref[...]
        out = y * jax.nn.sigmoid(z)
        for c in range(ctiles):
            io_scr[c, pl.ds(j, srows, stride=tau), :] = out[:, c * V7X_LANES:(c + 1) * V7X_LANES]
    y_all = jnp.concatenate([io_scr[c] for c in range(ctiles)], axis=1)
    y_ref[...] = y_all.reshape(nb, nk * tau, d_s5).astype(y_ref.dtype)


def _s5(u_s5, nb, bp, cp, tp, a_tab, dd, wglu_bf, bglu):
    n, d_s5 = u_s5.shape
    seq = n // nb
    tau = S5_TAU
    nk = T_S5 // tau
    seg = nk + S5_PAD
    nblk, _, sw = bp.shape
    rows = 2 * nb
    assert rows == V7X_SUBLANES and nblk == 4

    def whole(a):
        return pl.BlockSpec(a.shape, lambda c: (0,) * a.ndim)

    body = functools.partial(_s5_body, nb=nb, nk=nk, seg=seg, tau=tau)
    out = pl.pallas_call(
        body,
        grid=(seq // T_S5,),
        in_specs=[pl.BlockSpec((nb, T_S5, d_s5), lambda c: (0, c, 0)),
                  whole(bp), whole(cp), whole(tp), whole(a_tab), whole(dd), whole(wglu_bf),
                  whole(bglu)],
        out_specs=pl.BlockSpec((nb, T_S5, d_s5), lambda c: (0, c, 0)),
        out_shape=jax.ShapeDtypeStruct((nb, seq, d_s5), BF16),
        scratch_shapes=[pltpu.VMEM((2 * sw // V7X_LANES, rows * seg, V7X_LANES), F32),
                        pltpu.VMEM((2 * sw // V7X_LANES, rows * seg, V7X_LANES), F32),
                        pltpu.VMEM((2 * sw // V7X_LANES, rows, V7X_LANES), F32),
                        pltpu.VMEM((d_s5 // V7X_LANES, nb * T_S5, V7X_LANES), F32)],
        compiler_params=_cparams(("arbitrary",)),
        name="s5",
    )(u_s5.reshape(nb, seq, d_s5), bp, cp, tp, a_tab, dd, wglu_bf, bglu)
    return out.reshape(n, d_s5)


def _s5_tables(a_re, a_im, b_re, b_im, c_re, c_im, log_dt, nb, tau):
    hp = lax.Precision.HIGHEST
    g, p = a_re.shape
    hch = b_re.shape[2]
    gpb = V7X_LANES // hch
    nblk = g // gpb
    lam_r, lam_i = a_re.astype(F32), a_im.astype(F32)
    dt = jnp.exp(log_dt.astype(F32))[:, None]
    mag = jnp.exp(lam_r * dt)
    ab_r = mag * jnp.cos(lam_i * dt)
    ab_i = mag * jnp.sin(lam_i * dt)
    den = lam_r * lam_r + lam_i * lam_i
    zr = ((ab_r - 1.0) * lam_r + ab_i * lam_i) / den
    zi = (ab_i * lam_r - (ab_r - 1.0) * lam_i) / den
    br_, bi_ = b_re.astype(F32), b_im.astype(F32)
    bb_r = zr[..., None] * br_ - zi[..., None] * bi_
    bb_i = zr[..., None] * bi_ + zi[..., None] * br_
    cr, ci = c_re.astype(F32), c_im.astype(F32)
    pr, pi = [jnp.ones_like(ab_r)], [jnp.zeros_like(ab_i)]
    for _ in range(tau):
        pr, pi = pr + [pr[-1] * ab_r - pi[-1] * ab_i], pi + [pr[-1] * ab_i + pi[-1] * ab_r]
    pw_r, pw_i = jnp.stack(pr), jnp.stack(pi)

    def blockdiag(x):
        r, c = x.shape[-2:]
        x = jnp.tile(x.reshape(tau, nblk, gpb * r, c), (1, 1, 1, gpb))
        same = (jnp.arange(gpb * r)[:, None] // r) == (jnp.arange(gpb * c)[None, :] // c)
        return jnp.where(same, x, 0.0)

    wr_ = jnp.stack([pr[tau - 1 - j] for j in range(tau)])[:, :, None, :]
    wi_ = jnp.stack([pi[tau - 1 - j] for j in range(tau)])[:, :, None, :]
    bt_r, bt_i = jnp.swapaxes(bb_r, 1, 2)[None], jnp.swapaxes(bb_i, 1, 2)[None]
    bp_r = blockdiag(wr_ * bt_r - wi_ * bt_i)
    bp_i = blockdiag(wr_ * bt_i + wi_ * bt_r)
    bp = jnp.concatenate([jnp.concatenate([bp_r[j], bp_i[j]], axis=-1) for j in range(tau)],
                         axis=1).astype(BF16)

    qr_, qi_ = pw_r[1:, :, :, None], pw_i[1:, :, :, None]
    ct_r, ct_i = jnp.swapaxes(cr, 1, 2)[None], jnp.swapaxes(ci, 1, 2)[None]
    cp_r = blockdiag(ct_r * qr_ - ct_i * qi_)
    cp_i = blockdiag(ct_r * qi_ + ct_i * qr_)
    cp = jnp.concatenate([jnp.concatenate([cp_r[i], -cp_i[i]], axis=1) for i in range(tau)],
                         axis=2).astype(BF16)

    ab_r_ = pw_r[:tau, :, :, None] * bb_r[None] - pw_i[:tau, :, :, None] * bb_i[None]
    ab_i_ = pw_r[:tau, :, :, None] * bb_i[None] + pw_i[:tau, :, :, None] * bb_r[None]
    kd = blockdiag(jnp.einsum('ghp,dgpe->dgeh', cr, ab_r_, precision=hp)
                   - jnp.einsum('ghp,dgpe->dgeh', ci, ab_i_, precision=hp))
    kzero = jnp.zeros_like(kd[0])
    tp = jnp.concatenate(
        [jnp.concatenate([kd[i - j] if i >= j else kzero for i in range(tau)], axis=2)
         for j in range(tau)], axis=1).astype(BF16)

    a_tab = jnp.stack([jnp.repeat(pw_r[tau].reshape(2, -1), nb, axis=0),
                       jnp.repeat(pw_i[tau].reshape(2, -1), nb, axis=0)])
    a_tab = a_tab.reshape(2, 2 * nb, -1, V7X_LANES).transpose(0, 2, 1, 3)
    return bp, cp, tp, a_tab


def _proj_ret_body(x_ref, n1_ref, w_ref, cos_ref, sin_ref, dec_ref, qdec_ref, kdect_ref,
                   cdec_ref, ms_ref, avg_ref, nw_ref, wg_ref, wu_ref, wd_ref,
                   us5_ref, y_ref, wg_out, wu_out, wd_out, u_even, u_odd, st_scr,
                   *, t_len, dh, scale):
    wg_out[...] = wg_ref[...].astype(wg_out.dtype)
    wu_out[...] = wu_ref[...].astype(wu_out.dtype)
    wd_out[...] = wd_ref[...].astype(wd_out.dtype)

    step = pl.program_id(0)
    nb = x_ref.shape[0]
    d_s5 = us5_ref.shape[-1]

    @pl.when(step == 0)
    def _():
        u_odd[...] = jnp.zeros_like(u_odd)
        st_scr[...] = jnp.zeros_like(st_scr)

    def project(u_out):
        xn = _rms(x_ref[...].reshape(nb * t_len, x_ref.shape[-1]), n1_ref[...]).astype(BF16)
        u = jnp.dot(xn, w_ref[...], preferred_element_type=F32)
        us5_ref[...] = u[:, :d_s5].reshape(nb, t_len, d_s5).astype(us5_ref.dtype)
        u_out[...] = u[:, d_s5:].astype(u_out.dtype)

    for parity, (u_out, u_in) in enumerate(((u_even, u_odd), (u_odd, u_even))):
        @pl.when(step % 2 == parity)
        def _():
            project(u_out)
            _ret_chunk(u_in, cos_ref, sin_ref, dec_ref, qdec_ref, kdect_ref, cdec_ref, ms_ref, avg_ref,
                       nw_ref, y_ref, st_scr, t_len=t_len, dh=dh, scale=scale)


def _ret_chunk(u_ref, cos_ref, sin_ref, dec_ref, qdec_ref, kdect_ref, cdec_ref, ms_ref, avg_ref,
               nw_ref, y_ref, st_scr, *, t_len, dh, scale):
    nb = y_ref.shape[0]
    width = y_ref.shape[-1]
    pairs = width // V7X_LANES
    reps = width // cos_ref.shape[-1]
    cos = jnp.concatenate([cos_ref[...]] * reps, axis=1)
    sin = jnp.concatenate([sin_ref[...]] * reps, axis=1)
    lane = lax.broadcasted_iota(jnp.int32, (t_len, width), 1)
    first = (lane % dh) < (dh // 2)

    def rot(x):
        x = x.astype(F32)
        partner = jnp.where(first, pltpu.roll(x, width - dh // 2, 1), pltpu.roll(x, dh // 2, 1))
        return x * cos + partner * sin

    def group_mean(x):
        return jnp.dot(x.astype(BF16), avg_ref[...], preferred_element_type=F32)

    zero_k = jnp.zeros((dh, t_len), BF16)
    zero_v = jnp.zeros((t_len, V7X_LANES), BF16)
    low_head = lax.broadcasted_iota(jnp.int32, (t_len, V7X_LANES), 1) < dh

    outs = []
    for b in range(nb):
        rows = slice(b * t_len, (b + 1) * t_len)
        q = rot(u_ref[rows, 0:width])
        k = rot(u_ref[rows, width:2 * width]) * scale
        vb = u_ref[rows, 2 * width:3 * width]
        kt = k.T
        ktb = kt.astype(BF16)
        qb = q.astype(BF16)
        inners = []
        for p in range(pairs):
            ps = slice(p * V7X_LANES, (p + 1) * V7X_LANES)
            k_lo, k_hi = ktb[p * V7X_LANES:p * V7X_LANES + dh], ktb[p * V7X_LANES + dh:(p + 1) * V7X_LANES]
            kbd = jnp.concatenate([jnp.concatenate([k_lo, zero_k], axis=0),
                                   jnp.concatenate([zero_k, k_hi], axis=0)], axis=1)
            sc = (jnp.dot(qb[:, ps], kbd, preferred_element_type=F32)
                  * dec_ref[:, 2 * p * t_len:2 * (p + 1) * t_len])
            vt = vb[:, ps]
            vbd = jnp.concatenate([jnp.where(low_head, vt, zero_v),
                                   jnp.where(low_head, zero_v, vt)], axis=0)
            inners.append(jnp.dot(sc.astype(BF16), vbd, preferred_element_type=F32))
        inner = jnp.concatenate(inners, axis=1)
        qd = (q * qdec_ref[...]).astype(BF16)
        kdt = (kt * kdect_ref[...]).astype(BF16)
        crosses = []
        for p in range(pairs):
            ps = slice(p * V7X_LANES, (p + 1) * V7X_LANES)
            state = st_scr[b, p]
            crosses.append(jnp.dot(qd[:, ps], state.astype(BF16), preferred_element_type=F32))
            kv = jnp.dot(kdt[ps, :], vb[:, ps], preferred_element_type=F32)
            st_scr[b, p] = state * cdec_ref[:, ps] + kv * ms_ref[...]
        outs.append(inner + jnp.concatenate(crosses, axis=1))

    o = jnp.concatenate(outs, axis=0)
    dlt = o - group_mean(o)
    on = dlt * lax.rsqrt(group_mean(dlt * dlt) + EPS) * nw_ref[...]
    gate = u_ref[:, 3 * width:4 * width].astype(F32)
    y_ref[...] = (jax.nn.silu(gate) * on).reshape(nb, t_len, width).astype(y_ref.dtype)


def _proj_retention(x3, n1, w_bf, d_s5, d_ret, norm_w, moe_w):
    nb, seq, d = x3.shape
    dh = d_ret // RET_HEADS
    half = dh // 2
    t_len = T_RET
    assert V7X_LANES == 2 * dh
    lg = jnp.log(1.0 - 2.0 ** (-5.0 - jnp.arange(RET_HEADS, dtype=F32)))
    t = jnp.arange(t_len, dtype=F32)
    diff = t[:, None] - t[None, :]
    dec = jnp.where(diff >= 0, jnp.exp(lg[:, None, None] * jnp.maximum(diff, 0.0)), 0.0)
    dec_all = dec.transpose(1, 0, 2).reshape(t_len, RET_HEADS * t_len)
    qdec = jnp.repeat(jnp.exp(lg[:, None] * (t + 1.0)[None, :]).T, dh, axis=1)
    kdect = jnp.repeat(jnp.exp(lg[:, None] * (t_len - 1 - t)[None, :]), dh, axis=0)
    cdec = jnp.repeat(jnp.exp(lg * t_len), dh)[None, :]
    head_of_lane = jnp.arange(d_ret) // dh
    pair_head = jnp.arange(V7X_LANES) // dh
    mask_s = (pair_head[:, None] == pair_head[None, :]).astype(F32)
    avg = (head_of_lane[:, None] == head_of_lane[None, :]).astype(F32) / dh
    assert dh & (dh - 1) == 0
    avg = avg.astype(BF16)
    inv = ROPE_BASE ** (-jnp.arange(half, dtype=F32) / half)
    ang = jnp.arange(seq, dtype=F32)[:, None] * inv[None, :]
    reps = V7X_LANES // dh
    cos_t = jnp.tile(jnp.cos(ang), (1, 2 * reps))
    sin_t = jnp.tile(jnp.concatenate([-jnp.sin(ang), jnp.sin(ang)], axis=1), (1, reps))

    def whole(a):
        return pl.BlockSpec(a.shape, lambda c: (0,) * a.ndim)

    nc = seq // t_len

    def proj_chunk(c):
        return jnp.minimum(c, nc - 1)

    def ret_chunk(c):
        return jnp.maximum(c - 1, 0)

    def sliced(w):
        e, rows, cols = w.shape
        if nc >= e:
            parts = nc // e
            assert nc % e == 0 and rows % (parts * V7X_SUBLANES) == 0
            return w.reshape(e * parts, rows // parts, cols), 1
        assert e % nc == 0
        return w, e // nc

    moe_in, moe_specs, moe_shapes = [], [], []
    for w in moe_w:
        ws, per_step = sliced(w)
        blk = (per_step,) + ws.shape[1:]
        moe_in.append(ws)
        moe_specs.append(pl.BlockSpec(blk, lambda c: (proj_chunk(c), 0, 0)))
        moe_shapes.append(jax.ShapeDtypeStruct(ws.shape, BF16))

    body = functools.partial(_proj_ret_body, t_len=t_len, dh=dh, scale=dh ** -0.5)
    outs = pl.pallas_call(
        body,
        grid=(nc + 1,),
        in_specs=[pl.BlockSpec((nb, t_len, d), lambda c: (0, proj_chunk(c), 0)),
                  whole(n1), whole(w_bf),
                  pl.BlockSpec((t_len, V7X_LANES), lambda c: (ret_chunk(c), 0)),
                  pl.BlockSpec((t_len, V7X_LANES), lambda c: (ret_chunk(c), 0)),
                  whole(dec_all), whole(qdec), whole(kdect), whole(cdec),
                  whole(mask_s), whole(avg), whole(norm_w)] + moe_specs,
        out_specs=[pl.BlockSpec((nb, t_len, d_s5), lambda c: (0, proj_chunk(c), 0)),
                   pl.BlockSpec((nb, t_len, d_ret), lambda c: (0, ret_chunk(c), 0))] + moe_specs,
        out_shape=[jax.ShapeDtypeStruct((nb, seq, d_s5), BF16),
                   jax.ShapeDtypeStruct((nb, seq, d_ret), BF16)] + moe_shapes,
        scratch_shapes=[pltpu.VMEM((nb * t_len, 4 * d_ret), BF16),
                        pltpu.VMEM((nb * t_len, 4 * d_ret), BF16),
                        pltpu.VMEM((nb, d_ret // V7X_LANES, V7X_LANES, V7X_LANES), F32)],
        compiler_params=_cparams(("arbitrary",)),
        name="inproj_retention",
    )(x3, n1, w_bf, cos_t, sin_t, dec_all, qdec, kdect, cdec, mask_s, avg, norm_w, *moe_in)
    u_s5, y_ret = outs[:2]
    moe_bf = [o.reshape(w.shape) for o, w in zip(outs[2:], moe_w)]
    return u_s5, y_ret, moe_bf


def _route_body(ys5_ref, yret_ref, x_ref, wo_ref, n2_ref, wr_ref, br_ref, tri_ref,
                h_ref, route_ref, route_t_ref, cnt_ref, carry_scr):
    @pl.when(pl.program_id(0) == 0)
    def _():
        carry_scr[...] = jnp.zeros_like(carry_scr)

    d_s5 = ys5_ref.shape[1]
    h = (x_ref[...]
         + jnp.dot(ys5_ref[...], wo_ref[0:d_s5], preferred_element_type=F32)
         + jnp.dot(yret_ref[...], wo_ref[d_s5:], preferred_element_type=F32))
    h_ref[...] = h
    hn = _rms(h, n2_ref[...])
    hi = hn.astype(BF16)
    lo = (hn - hi.astype(F32)).astype(BF16)
    p_hi = jnp.dot(hi, wr_ref[...], preferred_element_type=F32)
    p_lo = jnp.dot(lo, wr_ref[:, 0:ROUTE_LANES], preferred_element_type=F32)
    logits = p_hi[:, 0:ROUTE_LANES] + p_hi[:, ROUTE_LANES:] + p_lo + br_ref[...]
    tm = logits.shape[0]
    lt = logits.T
    epg = EXPERTS_PER_GROUP
    row = lax.broadcasted_iota(jnp.int32, (epg, tm), 0)
    rowf = row.astype(F32)
    neg = -jnp.inf
    big = float(epg)
    gl = jnp.where(row < N_GROUPS, lt[0:epg], neg)
    gmax = jnp.max(gl, axis=0, keepdims=True)
    gidx = jnp.min(jnp.where(gl == gmax, rowf, big), axis=0, keepdims=True)
    g_w = 1.0 / jnp.sum(jnp.exp(gl - gmax), axis=0, keepdims=True)
    el = jnp.full((epg, tm), neg, F32)
    for g in range(N_GROUPS):
        el = jnp.where(gidx == g, lt[ROUTE_EXPERT_ROW + g * epg:ROUTE_EXPERT_ROW + (g + 1) * epg], el)
    v0 = jnp.max(el, axis=0, keepdims=True)
    i0 = jnp.min(jnp.where(el == v0, rowf, big), axis=0, keepdims=True)
    el2 = jnp.where(rowf == i0, neg, el)
    v1 = jnp.max(el2, axis=0, keepdims=True)
    i1 = jnp.min(jnp.where(el2 == v1, rowf, big), axis=0, keepdims=True)
    e = jnp.exp(v1 - v0)
    den = 1.0 + e
    w0 = (1.0 / den) * g_w
    w1 = (e / den) * g_w
    eid0 = gidx * epg + i0
    eid1 = gidx * epg + i1
    n_exp = N_GROUPS * epg
    erow = lax.broadcasted_iota(jnp.int32, (n_exp, tm), 0).astype(F32)
    sel0 = erow == eid0
    sel1 = erow == eid1
    onehot = jnp.where(sel0 | sel1, 1.0, 0.0)
    before = jnp.dot(onehot.astype(BF16), tri_ref[...], preferred_element_type=F32) + carry_scr[...]
    r0 = jnp.sum(jnp.where(sel0, before, 0.0), axis=0, keepdims=True)
    r1 = jnp.sum(jnp.where(sel1, before, 0.0), axis=0, keepdims=True)
    carry_scr[...] += jnp.sum(onehot, axis=1, keepdims=True)
    cnt_ref[...] = jnp.broadcast_to(carry_scr[...], cnt_ref.shape)
    rec_t = jnp.zeros((ROUTE_FIELDS, tm), F32)
    for j, val in enumerate((eid0, eid1, w0, w1, r0, r1)):
        rec_t = jnp.where(row == j, val, rec_t)
    route_t_ref[...] = rec_t
    route_ref[...] = jnp.concatenate(
        [rec_t, jnp.zeros((ROUTE_LANES - ROUTE_FIELDS, tm), F32)], axis=0).T


def _route(ys5, yret, x2, wo_bf, n2, wr, br):
    n, d = x2.shape
    d_s5 = ys5.shape[1]
    d_ret = yret.shape[1]
    tm = TM_ROUTE
    n_exp = N_GROUPS * EXPERTS_PER_GROUP
    tri = (jnp.arange(tm)[:, None] < jnp.arange(tm)[None, :]).astype(BF16)
    return pl.pallas_call(
        _route_body,
        grid=(n // tm,),
        in_specs=[pl.BlockSpec((tm, d_s5), lambda i: (i, 0)),
                  pl.BlockSpec((tm, d_ret), lambda i: (i, 0)),
                  pl.BlockSpec((tm, d), lambda i: (i, 0)),
                  pl.BlockSpec((d_s5 + d_ret, d), lambda i: (0, 0)),
                  pl.BlockSpec((1, d), lambda i: (0, 0)),
                  pl.BlockSpec((d, 2 * ROUTE_LANES), lambda i: (0, 0)),
                  pl.BlockSpec((1, ROUTE_LANES), lambda i: (0, 0)),
                  pl.BlockSpec((tm, tm), lambda i: (0, 0))],
        out_specs=[pl.BlockSpec((tm, d), lambda i: (i, 0)),
                   pl.BlockSpec((tm, ROUTE_LANES), lambda i: (i, 0)),
                   pl.BlockSpec((ROUTE_FIELDS, tm), lambda i: (0, i)),
                   pl.BlockSpec((n_exp, ROUTE_LANES), lambda i: (0, 0))],
        out_shape=[jax.ShapeDtypeStruct((n, d), F32),
                   jax.ShapeDtypeStruct((n, ROUTE_LANES), F32),
                   jax.ShapeDtypeStruct((ROUTE_FIELDS, n), F32),
                   jax.ShapeDtypeStruct((n_exp, ROUTE_LANES), F32)],
        scratch_shapes=[pltpu.VMEM((n_exp, 1), F32)],
        compiler_params=_cparams(("arbitrary",)),
        name="outproj_route",
    )(ys5, yret, x2, wo_bf, n2, wr, br, tri)


def _dispatch_body(pends_ref, padded_ref, dest_ref, h_ref, n2_ref, xs_ref,
                   hn_scr, zero_scr, sem, zsem):
    tm = h_ref.shape[0]

    @pl.when(pl.program_id(0) == 0)
    def _():
        zero_scr[...] = jnp.zeros_like(zero_scr)

        def zero_copy(e):
            first = pl.multiple_of((pends_ref[e] - R_BLK) * ROW_TILES, R_BLK * ROW_TILES)
            return pltpu.make_async_copy(zero_scr, xs_ref.at[pl.ds(first, R_BLK * ROW_TILES)], zsem)

        def zstart(e, carry):
            @pl.when(padded_ref[e] > 0)
            def _():
                zero_copy(e).start()
            return carry

        def zwait(e, carry):
            @pl.when(padded_ref[e] > 0)
            def _():
                zero_copy(e).wait()
            return carry

        lax.fori_loop(0, pends_ref.shape[0], zstart, 0)
        lax.fori_loop(0, pends_ref.shape[0], zwait, 0)

    step = pl.program_id(0)
    slot = step % 2
    buf = hn_scr.at[slot]
    _rows_to_tiles(buf, _pack_rows(_rms(h_ref[...], n2_ref[...])))

    for r in range(tm):
        for k in range(TOP_K):
            pltpu.make_async_copy(_tile_rows(buf, r), _tile_rows(xs_ref, dest_ref[k * tm + r]),
                                  sem.at[slot]).start(priority=k)

    def drain(s):
        for k in range(TOP_K):
            pltpu.make_async_copy(hn_scr.at[s], xs_ref.at[pl.ds(0, tm * ROW_TILES)], sem.at[s]).wait()

    @pl.when(step > 0)
    def _():
        drain(1 - slot)

    @pl.when(step == pl.num_programs(0) - 1)
    def _():
        drain(slot)


def _dispatch(pends, padded, dest, h, n2, p_rows):
    n, d = h.shape
    tm = TM_ROWS
    grid_spec = pltpu.PrefetchScalarGridSpec(
        num_scalar_prefetch=2,
        grid=(n // tm,),
        in_specs=[pl.BlockSpec((TOP_K * tm,), lambda i, pe, pa: (i,), memory_space=pltpu.SMEM),
                  pl.BlockSpec((tm, d), lambda i, pe, pa: (i, 0)),
                  pl.BlockSpec((1, d), lambda i, pe, pa: (0, 0))],
        out_specs=pl.BlockSpec(memory_space=pl.ANY),
        scratch_shapes=[pltpu.VMEM((2, tm * ROW_TILES, V7X_LANES), U32),
                        pltpu.VMEM((R_BLK * ROW_TILES, V7X_LANES), U32),
                        pltpu.SemaphoreType.DMA((2,)), pltpu.SemaphoreType.DMA(())])
    assert d == 2 * ROW_TILES * V7X_LANES
    return pl.pallas_call(
        _dispatch_body,
        grid_spec=grid_spec,
        out_shape=jax.ShapeDtypeStruct((p_rows * ROW_TILES, V7X_LANES), U32),
        compiler_params=_cparams(("arbitrary",)),
        name="dispatch",
    )(pends, padded, dest, h, n2)


def _expert_body(be_ref, nu_ref, xs_ref, wg_ref, wu_ref, wd_ref, ys_ref):
    j = pl.program_id(0)

    @pl.when(j < nu_ref[0])
    def _():
        x = _unpack_rows(_tiles_to_rows(xs_ref, R_BLK)).astype(BF16)
        gate = jnp.dot(x, wg_ref[...], preferred_element_type=F32)
        up = jnp.dot(x, wu_ref[...], preferred_element_type=F32)
        hid = (jax.nn.silu(gate) * up).astype(BF16)
        _rows_to_tiles(ys_ref, _pack_rows(jnp.dot(hid, wd_ref[...], preferred_element_type=F32)))

    @pl.when(j >= nu_ref[0])
    def _():
        ys_ref[...] = jnp.zeros_like(ys_ref)


def _experts(block_e, n_used, xs, w_gate, w_up, w_down):
    d, d_e = w_gate.shape[1:]
    blk_rows = R_BLK * ROW_TILES
    nblk = xs.shape[0] // blk_rows

    def row_map(j, be, nu):
        return (jnp.maximum(jnp.minimum(j, nu[0] - 1), 0), 0)

    def w_map(j, be, nu):
        return (be[j], 0, 0)

    grid_spec = pltpu.PrefetchScalarGridSpec(
        num_scalar_prefetch=2,
        grid=(nblk,),
        in_specs=[pl.BlockSpec((blk_rows, V7X_LANES), row_map),
                  pl.BlockSpec((None, d, d_e), w_map),
                  pl.BlockSpec((None, d, d_e), w_map),
                  pl.BlockSpec((None, d_e, d), w_map)],
        out_specs=pl.BlockSpec((blk_rows, V7X_LANES), lambda j, be, nu: (j, 0)))
    return pl.pallas_call(
        _expert_body,
        grid_spec=grid_spec,
        out_shape=jax.ShapeDtypeStruct(xs.shape, xs.dtype),
        compiler_params=_cparams(("arbitrary",)),
        name="experts",
    )(block_e, n_used, xs, w_gate, w_up, w_down)


def _combine_body(dest0_ref, dest1_ref, dest2_ref, h_ref, route_ref, fw_ref, ys_ref, out_ref,
                  *scratch, last):
    bufs, sem = scratch[:GATHER_RING], scratch[GATHER_RING]
    tm = h_ref.shape[0]
    step = pl.program_id(0)

    def issue(d_ref, s):
        for r in range(tm):
            for k in range(TOP_K):
                pltpu.make_async_copy(_tile_rows(ys_ref, d_ref[k * tm + r]),
                                      _tile_rows(bufs[s].at[k], r), sem.at[s]).start(priority=k)

    def drain(s):
        for k in range(TOP_K):
            pltpu.make_async_copy(ys_ref.at[pl.ds(0, tm * ROW_TILES)], bufs[s].at[k], sem.at[s]).wait()

    @pl.when(step == 0)
    def _():
        issue(dest0_ref, 0)
        issue(dest1_ref, 1)

    for s in range(GATHER_RING):
        @pl.when(step % GATHER_RING == s)
        def _():
            drain(s)
            issue(dest2_ref, (s + 2) % GATHER_RING)
            rec = route_ref[...]
            hh = (h_ref[...] + rec[:, 2:3] * _unpack_rows(_tiles_to_rows(bufs[s].at[0], tm))
                  + rec[:, 3:4] * _unpack_rows(_tiles_to_rows(bufs[s].at[1], tm)))
            out_ref[...] = _rms(hh, fw_ref[...])

    @pl.when(step == last)
    def _():
        drain((last + 1) % GATHER_RING)
        drain((last + 2) % GATHER_RING)


def _combine(dest, h, route, fw, ys):
    n, d = h.shape
    tm = TM_ROWS
    steps = n // tm
    return pl.pallas_call(
        functools.partial(_combine_body, last=steps - 1),
        grid=(steps,),
        in_specs=[pl.BlockSpec((TOP_K * tm,), lambda i: (i,), memory_space=pltpu.SMEM),
                  pl.BlockSpec((TOP_K * tm,), lambda i: (jnp.minimum(i + 1, steps - 1),),
                               memory_space=pltpu.SMEM),
                  pl.BlockSpec((TOP_K * tm,), lambda i: (jnp.minimum(i + 2, steps - 1),),
                               memory_space=pltpu.SMEM),
                  pl.BlockSpec((tm, d), lambda i: (i, 0)),
                  pl.BlockSpec((tm, ROUTE_LANES), lambda i: (i, 0)),
                  pl.BlockSpec((1, d), lambda i: (0, 0)),
                  pl.BlockSpec(memory_space=pl.ANY)],
        out_specs=pl.BlockSpec((tm, d), lambda i: (i, 0)),
        out_shape=jax.ShapeDtypeStruct((n, d), F32),
        scratch_shapes=([pltpu.VMEM((TOP_K, tm * ROW_TILES, V7X_LANES), U32)] * GATHER_RING
                        + [pltpu.SemaphoreType.DMA((GATHER_RING,))]),
        compiler_params=_cparams(("arbitrary",)),
        name="combine",
    )(dest, dest, dest, h, route, fw, ys)


def _plan(route_t, counts, n_experts, n_blocks):
    eid = route_t[0:TOP_K].astype(jnp.int32).reshape(-1, V7X_LANES)
    rank = route_t[4:4 + TOP_K].astype(jnp.int32).reshape(-1, V7X_LANES)
    cnt = counts[:, 0].astype(jnp.int32)
    padded = (cnt + R_BLK - 1) // R_BLK * R_BLK
    pends = jnp.cumsum(padded)
    pstart = pends - padded
    dest = rank
    for e in range(n_experts):
        dest = dest + jnp.where(eid == e, pstart[e], 0)
    n_used = pends[-1] // R_BLK
    blk = jnp.minimum(jnp.arange(n_blocks, dtype=jnp.int32), n_used - 1)
    block_e = jnp.minimum(jnp.sum(pends[None, :] <= (blk * R_BLK)[:, None], axis=1), n_experts - 1)
    return (dest.astype(jnp.int32), block_e.astype(jnp.int32), n_used.reshape(1).astype(jnp.int32),
            pends.astype(jnp.int32), padded.astype(jnp.int32))


def _layer(h3, norm1_w, w_in, s5_a_re, s5_a_im, s5_b_re, s5_b_im, s5_c_re, s5_c_im, s5_d,
           s5_log_dt, s5_w_glu, s5_b_glu, ret_norm_w, w_out, norm2_w, router_group_w,
           router_group_b, router_expert_w, router_expert_b, moe_w_gate, moe_w_up, moe_w_down,
           out_norm_w):
    nb, seq, d = h3.shape
    n = nb * seq
    d_s5 = s5_d.shape[0]
    d_ret = ret_norm_w.shape[0]
    n_experts = moe_w_gate.shape[0]
    x2 = h3.reshape(n, d)

    u_s5, y_ret, moe_bf = _proj_retention(h3, norm1_w.reshape(1, d), w_in.astype(BF16), d_s5, d_ret,
                                          ret_norm_w.reshape(1, d_ret).astype(F32),
                                          (moe_w_gate, moe_w_up, moe_w_down))

    bp, cp, tp, a_tab = _s5_tables(s5_a_re, s5_a_im, s5_b_re, s5_b_im, s5_c_re, s5_c_im,
                                   s5_log_dt, nb, S5_TAU)
    y_s5 = _s5(u_s5.reshape(n, d_s5), nb, bp, cp, tp, a_tab, s5_d.reshape(1, d_s5).astype(F32),
               s5_w_glu.astype(BF16), s5_b_glu.reshape(1, d_s5).astype(F32))

    assert n_experts == N_GROUPS * EXPERTS_PER_GROUP
    wr = (jnp.zeros((d, ROUTE_LANES), F32).at[:, :N_GROUPS].set(router_group_w.astype(F32))
          .at[:, ROUTE_EXPERT_ROW:ROUTE_EXPERT_ROW + n_experts].set(router_expert_w.astype(F32)))
    br = (jnp.zeros((1, ROUTE_LANES), F32).at[0, :N_GROUPS].set(router_group_b.astype(F32))
          .at[0, ROUTE_EXPERT_ROW:ROUTE_EXPERT_ROW + n_experts].set(router_expert_b.astype(F32)))
    wr_hi = wr.astype(BF16)
    wr = jnp.concatenate([wr_hi, (wr - wr_hi.astype(F32)).astype(BF16)], axis=1)
    h, route, route_t, counts = _route(y_s5.reshape(n, d_s5), y_ret.reshape(n, d_ret), x2,
                                       w_out.astype(BF16), norm2_w.reshape(1, d), wr, br)

    n_blocks = (n * TOP_K) // R_BLK + n_experts
    dest, block_e, n_used, pends, padded = _plan(route_t, counts, n_experts, n_blocks)
    dest = dest.reshape(TOP_K, n // TM_ROWS, TM_ROWS).transpose(1, 0, 2).reshape(-1)
    xs = _dispatch(pends, padded, dest, h, norm2_w.reshape(1, d), n_blocks * R_BLK)
    ys = _experts(block_e, n_used, xs, *moe_bf)
    out = _combine(dest, h, route, out_norm_w.reshape(1, d), ys)
    return out.reshape(nb, seq, d)


def kernel(x, norm1_w, w_in, s5_a_re, s5_a_im, s5_b_re, s5_b_im, s5_c_re, s5_c_im, s5_d, s5_log_dt, s5_w_glu, s5_b_glu, ret_norm_w, w_out, norm2_w, router_group_w, router_group_b, router_expert_w, router_expert_b, moe_w_gate, moe_w_up, moe_w_down, final_norm_w):
    depth = norm1_w.shape[0]
    assert depth == 1, "the fused final norm assumes a single layer"
    l = 0
    return _layer(x, norm1_w[l], w_in[l], s5_a_re[l], s5_a_im[l], s5_b_re[l], s5_b_im[l],
                  s5_c_re[l], s5_c_im[l], s5_d[l], s5_log_dt[l], s5_w_glu[l], s5_b_glu[l],
                  ret_norm_w[l], w_out[l], norm2_w[l], router_group_w[l], router_group_b[l],
                  router_expert_w[l], router_expert_b[l], moe_w_gate[l], moe_w_up[l],
                  moe_w_down[l], final_norm_w)
```
